```python
import math
import numpy as np
import jax
import jax.numpy as jnp
from jax import lax


D_MODEL = 1024
BATCH = 8
SEQ = 2048
DEPTH = 2

SB_HEADS = 8
SB_HEAD_DIM = 64
SB_WIDTH = SB_HEADS * SB_HEAD_DIM
SSM_WIDTH = D_MODEL - SB_WIDTH
SSM_GROUP = 16
SSM_GROUPS = SSM_WIDTH // SSM_GROUP
SSM_STATE = 64
Q_BLOCK = 128

NSA_HEADS = 16
NSA_KV_HEADS = 4
NSA_HEAD_DIM = 64
CMP_LEN = 32
CMP_STRIDE = 16
CMP_HIDDEN = 256
SEL_LEN = 64
SEL_TOP = 8
WINDOW = 512
SEL_Q_CHUNK = 32
ROPE_THETA = 10000.0
FORCE_BONUS = 1e4
NEG = -1e30

N_EXPERTS = 16
N_GROUPS = 4
EXPERTS_PER_GROUP = N_EXPERTS // N_GROUPS
TOP_K = 2
D_EXPERT = 512

ALPHA = (2 * DEPTH) ** 0.25
BETA = (8 * DEPTH) ** -0.25
LN_EPS = 1e-5

kernel_name = 'hybrid_stickbreak_s5_nsa_grouped_moe_deepnorm'


def layer_norm(x, g, b):
    xf = x.astype(jnp.float32)
    mu = jnp.mean(xf, axis=-1, keepdims=True)
    var = jnp.mean(jnp.square(xf - mu), axis=-1, keepdims=True)
    return ((xf - mu) * lax.rsqrt(var + LN_EPS) * g + b).astype(x.dtype)


def rope_tables(pos, dim):
    inv = ROPE_THETA ** (-jnp.arange(0, dim, 2, dtype=jnp.float32) / dim)
    ang = pos.astype(jnp.float32)[:, None] * inv[None, :]
    return jnp.cos(ang), jnp.sin(ang)


def apply_rope(x, cos, sin):
    x1, x2 = jnp.split(x.astype(jnp.float32), 2, axis=-1)
    c = cos[None, :, None, :]
    s = sin[None, :, None, :]
    return jnp.concatenate([x1 * c - x2 * s, x1 * s + x2 * c], axis=-1).astype(x.dtype)


def stick_breaking_attention(q, k, v):
    B, L, H, d = q.shape
    scale = d ** -0.5
    outs = []
    for i in range(L // Q_BLOCK):
        q0 = i * Q_BLOCK
        kend = q0 + Q_BLOCK
        z = jnp.einsum('bqhd,bkhd->bhqk', q[:, q0:kend], k[:, :kend]).astype(jnp.float32) * scale
        t = q0 + jnp.arange(Q_BLOCK)
        s = jnp.arange(kend)
        mask = s[None, :] < t[:, None]
        log_1mb = jnp.where(mask, -jax.nn.softplus(z), 0.0)
        after = lax.cumsum(log_1mb, axis=3, reverse=True) - log_1mb
        w = jnp.where(mask, jnp.exp(jax.nn.log_sigmoid(z) + after), 0.0)
        outs.append(jnp.einsum('bhqk,bkhd->bqhd', w.astype(v.dtype), v[:, :kend]))
    return jnp.concatenate(outs, axis=1)


def _ssm_combine(e1, e2):
    a1r, a1i, b1r, b1i = e1
    a2r, a2i, b2r, b2i = e2
    return (a2r * a1r - a2i * a1i,
            a2r * a1i + a2i * a1r,
            a2r * b1r - a2i * b1i + b2r,
            a2r * b1i + a2i * b1r + b2i)


def s5_ssm(u, lam_re, lam_im, log_dt, b_re, b_im, c_re, c_im, d_skip):
    Bsz, L, W = u.shape
    uf = u.astype(jnp.float32).reshape(Bsz, L, SSM_GROUPS, SSM_GROUP)
    dt = jnp.exp(log_dt.astype(jnp.float32))[:, None]
    lr = lam_re.astype(jnp.float32)
    li = lam_im.astype(jnp.float32)
    mag = jnp.exp(lr * dt)
    a_re = mag * jnp.cos(li * dt)
    a_im = mag * jnp.sin(li * dt)
    den = lr * lr + li * li
    nr = a_re - 1.0
    f_re = (nr * lr + a_im * li) / den
    f_im = (a_im * lr - nr * li) / den
    br = b_re.astype(jnp.float32)
    bi = b_im.astype(jnp.float32)
    bb_re = f_re[..., None] * br - f_im[..., None] * bi
    bb_im = f_re[..., None] * bi + f_im[..., None] * br
    bu_re = jnp.einsum('blgh,gph->lbgp', uf, bb_re)
    bu_im = jnp.einsum('blgh,gph->lbgp', uf, bb_im)
    shp = (L, 1, SSM_GROUPS, SSM_STATE)
    elems = (jnp.broadcast_to(a_re[None, None], shp), jnp.broadcast_to(a_im[None, None], shp), bu_re, bu_im)
    _, _, xr, xi = lax.associative_scan(_ssm_combine, elems, axis=0)
    y = (jnp.einsum('ghp,lbgp->blgh', c_re.astype(jnp.float32), xr)
         - jnp.einsum('ghp,lbgp->blgh', c_im.astype(jnp.float32), xi))
    y = y.reshape(Bsz, L, W) + d_skip.astype(jnp.float32) * u.astype(jnp.float32)
    return y.astype(u.dtype)


def even_mixer(x, w_in, lam_re, lam_im, log_dt, b_re, b_im, c_re, c_im, d_skip, w_glu, w_out):
    B, L, _ = x.shape
    proj = x @ w_in
    q, k, v, u = jnp.split(proj, [SB_WIDTH, 2 * SB_WIDTH, 3 * SB_WIDTH], axis=-1)
    shp = (B, L, SB_HEADS, SB_HEAD_DIM)
    o_a = stick_breaking_attention(q.reshape(shp), k.reshape(shp), v.reshape(shp)).reshape(B, L, SB_WIDTH)
    y = s5_ssm(u, lam_re, lam_im, log_dt, b_re, b_im, c_re, c_im, d_skip)
    h = jax.nn.gelu(y)
    o_b = h * jax.nn.sigmoid(h @ w_glu)
    return jnp.concatenate([o_a, o_b], axis=-1) @ w_out


def compress_blocks(t, pos_emb, w1, w2, starts):
    B, L, G, d = t.shape
    idx = jnp.asarray((starts[:, None] + np.arange(CMP_LEN)[None, :]).astype(np.int32))
    blk = t[:, idx] + pos_emb[:, None, :]
    flat = jnp.moveaxis(blk, 3, 2).reshape(B, idx.shape[0], G, CMP_LEN * d)
    return jax.nn.gelu(flat @ w1) @ w2


def nsa_mixer(x, w_in, cmp_pos_k, cmp_w1_k, cmp_w2_k, cmp_pos_v, cmp_w1_v, cmp_w2_v, w_out):
    B, L, _ = x.shape
    H, G, d = NSA_HEADS, NSA_KV_HEADS, NSA_HEAD_DIM
    R = H // G
    scale = d ** -0.5
    kvw = G * d
    proj = x @ w_in
    cuts = [H * d + i * kvw for i in range(7)]
    q, kc, vc, ks, vs, kw, vw, gate = jnp.split(proj, cuts, axis=-1)
    pos = jnp.arange(L)
    cos, sin = rope_tables(pos, d)
    q = apply_rope(q.reshape(B, L, H, d), cos, sin).reshape(B, L, G, R, d)
    kc, vc, ks, vs, kw, vw = [a.reshape(B, L, G, d) for a in (kc, vc, ks, vs, kw, vw)]
    ks = apply_rope(ks, cos, sin)
    kw = apply_rope(kw, cos, sin)

    M = (L - CMP_LEN) // CMP_STRIDE + 1
    cmp_start = np.arange(M) * CMP_STRIDE
    cmp_end = cmp_start + CMP_LEN - 1
    k_cmp = compress_blocks(kc, cmp_pos_k, cmp_w1_k, cmp_w2_k, cmp_start)
    c_cos, c_sin = rope_tables(jnp.asarray(cmp_end.astype(np.int32)), d)
    k_cmp = apply_rope(k_cmp, c_cos, c_sin)
    v_cmp = compress_blocks(vc, cmp_pos_v, cmp_w1_v, cmp_w2_v, cmp_start)
    s_c = jnp.einsum('blgrd,bmgd->blgrm', q, k_cmp).astype(jnp.float32) * scale
    valid_c = (jnp.asarray(cmp_end.astype(np.int32))[None, :] <= pos[:, None])[None, :, None, None, :]
    p_c = jnp.where(valid_c, jax.nn.softmax(jnp.where(valid_c, s_c, NEG), axis=-1), 0.0)
    o_c = jnp.einsum('blgrm,bmgd->blgrd', p_c.astype(v_cmp.dtype), v_cmp)

    NB = L // SEL_LEN
    n_sel = min(SEL_TOP, NB)
    sel_start = np.arange(NB) * SEL_LEN
    ov = ((cmp_start[:, None] < sel_start[None, :] + SEL_LEN)
          & (cmp_start[:, None] + CMP_LEN > sel_start[None, :])).astype(np.float32)
    imp = jnp.einsum('blgrm,mn->blgn', p_c, jnp.asarray(ov))
    blk = jnp.arange(NB)
    cur = pos // SEL_LEN
    valid_s = blk[None, :] * SEL_LEN <= pos[:, None]
    forced = (blk[None, :] == 0) | (blk[None, :] == cur[:, None]) | (blk[None, :] == cur[:, None] - 1)
    score = jnp.where(valid_s[None, :, None, :],
                      imp + jnp.where(forced, FORCE_BONUS, 0.0)[None, :, None, :], NEG)
    _, sel_idx = lax.top_k(score, n_sel)

    ks_t = ks.reshape(B, NB, SEL_LEN, G, d).transpose(0, 3, 1, 2, 4)
    vs_t = vs.reshape(B, NB, SEL_LEN, G, d).transpose(0, 3, 1, 2, 4)
    b_ix = jnp.arange(B)[:, None, None, None]
    g_ix = jnp.arange(G)[None, None, :, None]

    def sel_chunk(c):
        q0 = c * SEL_Q_CHUNK
        qc = lax.dynamic_slice_in_dim(q, q0, SEL_Q_CHUNK, axis=1)
        ic = lax.dynamic_slice_in_dim(sel_idx, q0, SEL_Q_CHUNK, axis=1)
        kg = ks_t[b_ix, g_ix, ic]
        vg = vs_t[b_ix, g_ix, ic]
        s = jnp.einsum('bcgrd,bcgnsd->bcgrns', qc, kg).astype(jnp.float32) * scale
        kpos = ic[..., None] * SEL_LEN + jnp.arange(SEL_LEN)
        tq = q0 + jnp.arange(SEL_Q_CHUNK)
        mask = (kpos <= tq[None, :, None, None, None])[:, :, :, None]
        s = jnp.where(mask, s, NEG).reshape(B, SEL_Q_CHUNK, G, R, n_sel * SEL_LEN)
        p = jax.nn.softmax(s, axis=-1).reshape(B, SEL_Q_CHUNK, G, R, n_sel, SEL_LEN)
        return jnp.einsum('bcgrns,bcgnsd->bcgrd', p.astype(vg.dtype), vg)

    o_s = lax.map(sel_chunk, jnp.arange(L // SEL_Q_CHUNK))
    o_s = jnp.moveaxis(o_s, 0, 1).reshape(B, L, G, R, d)

    span = WINDOW + Q_BLOCK
    kw_pad = jnp.pad(kw, ((0, 0), (WINDOW, 0), (0, 0), (0, 0)))
    vw_pad = jnp.pad(vw, ((0, 0), (WINDOW, 0), (0, 0), (0, 0)))

    def win_block(i):
        q0 = i * Q_BLOCK
        qb = lax.dynamic_slice_in_dim(q, q0, Q_BLOCK, axis=1)
        kb = lax.dynamic_slice_in_dim(kw_pad, q0, span, axis=1)
        vb = lax.dynamic_slice_in_dim(vw_pad, q0, span, axis=1)
        s = jnp.einsum('bqgrd,bkgd->bqgrk', qb, kb).astype(jnp.float32) * scale
        kpos = q0 - WINDOW + jnp.arange(span)
        tq = q0 + jnp.arange(Q_BLOCK)
        mask = ((kpos[None, :] <= tq[:, None]) & (kpos[None, :] > tq[:, None] - WINDOW)
                & (kpos[None, :] >= 0))
        s = jnp.where(mask[None, :, None, None, :], s, NEG)
        p = jax.nn.softmax(s, axis=-1)
        return jnp.einsum('bqgrk,bkgd->bqgrd', p.astype(vb.dtype), vb)

    o_w = lax.map(win_block, jnp.arange(L // Q_BLOCK))
    o_w = jnp.moveaxis(o_w, 0, 1).reshape(B, L, G, R, d)

    g = jax.nn.sigmoid(gate.astype(jnp.float32)).reshape(B, L, G, R, 3).astype(x.dtype)
    o = g[..., 0:1] * o_c + g[..., 1:2] * o_s + g[..., 2:3] * o_w
    return o.reshape(B, L, H * d) @ w_out


def moe_ffn(x, w_router, b_router, w1, w3, w2):
    B, L, D = x.shape
    xt = x.reshape(B * L, D)
    n_tok = xt.shape[0]
    aff = jax.nn.sigmoid((xt @ w_router).astype(jnp.float32))
    sel = aff + b_router.astype(jnp.float32)
    grp = sel.reshape(n_tok, N_GROUPS, EXPERTS_PER_GROUP)
    grp_score = lax.top_k(grp, 2)[0].sum(-1)
    g_idx = jnp.argmax(grp_score, axis=-1)
    in_grp = jnp.take_along_axis(grp, g_idx[:, None, None], axis=1)[:, 0]
    _, loc = lax.top_k(in_grp, TOP_K)
    e_idx = g_idx[:, None] * EXPERTS_PER_GROUP + loc
    w_sel = jnp.take_along_axis(aff, e_idx, axis=1)
    w_sel = w_sel / jnp.sum(w_sel, axis=-1, keepdims=True)
    gates = jnp.sum(jax.nn.one_hot(e_idx, N_EXPERTS, dtype=jnp.float32) * w_sel[..., None], axis=1).astype(x.dtype)
    y = jnp.zeros_like(xt)
    for e in range(N_EXPERTS):
        h = jax.nn.silu(xt @ w1[e]) * (xt @ w3[e])
        y = y + gates[:, e:e + 1] * (h @ w2[e])
    return y.reshape(B, L, D)


def setup_inputs(seed: int = 0) -> dict:
    key = jax.random.key(seed)
    keys = iter(jax.random.split(key, 48))

    def nrm(shape, scale):
        return jax.random.normal(next(keys), shape, jnp.float32) * scale

    def gain():
        return 1.0 + nrm((D_MODEL,), 0.02)

    def bias():
        return nrm((D_MODEL,), 0.02)

    D = D_MODEL
    nsa_in = NSA_HEADS * NSA_HEAD_DIM + 6 * NSA_KV_HEADS * NSA_HEAD_DIM + 3 * NSA_HEADS
    inp = {}
    inp['x'] = nrm((BATCH, SEQ, D), 1.0)
    inp['w_in_0'] = nrm((D, 3 * SB_WIDTH + SSM_WIDTH), D ** -0.5)
    inp['ssm_lam_re'] = -0.5 + nrm((SSM_GROUPS, SSM_STATE), 0.01)
    inp['ssm_lam_im'] = math.pi * jnp.arange(SSM_STATE, dtype=jnp.float32)[None, :] + nrm((SSM_GROUPS, SSM_STATE), 0.01)
    inp['ssm_log_dt'] = jax.random.uniform(next(keys), (SSM_GROUPS,), jnp.float32, math.log(1e-3), math.log(1e-1))
    inp['ssm_b_re'] = nrm((SSM_GROUPS, SSM_STATE, SSM_GROUP), (2 * SSM_GROUP) ** -0.5)
    inp['ssm_b_im'] = nrm((SSM_GROUPS, SSM_STATE, SSM_GROUP), (2 * SSM_GROUP) ** -0.5)
    inp['ssm_c_re'] = nrm((SSM_GROUPS, SSM_GROUP, SSM_STATE), SSM_STATE ** -0.5)
    inp['ssm_c_im'] = nrm((SSM_GROUPS, SSM_GROUP, SSM_STATE), SSM_STATE ** -0.5)
    inp['ssm_d'] = 1.0 + nrm((SSM_WIDTH,), 0.1)
    inp['w_glu'] = nrm((SSM_WIDTH, SSM_WIDTH), SSM_WIDTH ** -0.5)
    inp['w_out_0'] = nrm((D, D), BETA * D ** -0.5)
    inp['ln_mix_g_0'] = gain()
    inp['ln_mix_b_0'] = bias()
    inp['ln_ffn_g_0'] = gain()
    inp['ln_ffn_b_0'] = bias()
    inp['w1_0'] = nrm((N_EXPERTS, D, D_EXPERT), D ** -0.5)
    inp['w3_0'] = nrm((N_EXPERTS, D, D_EXPERT), D ** -0.5)
    inp['w2_0'] = nrm((N_EXPERTS, D_EXPERT, D), BETA * D_EXPERT ** -0.5)
    inp['w_in_1'] = nrm((D, nsa_in), D ** -0.5)
    inp['cmp_pos_k'] = nrm((CMP_LEN, NSA_HEAD_DIM), 0.1)
    inp['cmp_w1_k'] = nrm((CMP_LEN * NSA_HEAD_DIM, CMP_HIDDEN), (CMP_LEN * NSA_HEAD_DIM) ** -0.5)
    inp['cmp_w2_k'] = nrm((CMP_HIDDEN, NSA_HEAD_DIM), CMP_HIDDEN ** -0.5)
    inp['cmp_pos_v'] = nrm((CMP_LEN, NSA_HEAD_DIM), 0.1)
    inp['cmp_w1_v'] = nrm((CMP_LEN * NSA_HEAD_DIM, CMP_HIDDEN), (CMP_LEN * NSA_HEAD_DIM) ** -0.5)
    inp['cmp_w2_v'] = nrm((CMP_HIDDEN, NSA_HEAD_DIM), CMP_HIDDEN ** -0.5)
    inp['w_out_1'] = nrm((NSA_HEADS * NSA_HEAD_DIM, D), BETA * (NSA_HEADS * NSA_HEAD_DIM) ** -0.5)
    inp['ln_mix_g_1'] = gain()
    inp['ln_mix_b_1'] = bias()
    inp['ln_ffn_g_1'] = gain()
    inp['ln_ffn_b_1'] = bias()
    inp['w1_1'] = nrm((N_EXPERTS, D, D_EXPERT), D ** -0.5)
    inp['w3_1'] = nrm((N_EXPERTS, D, D_EXPERT), D ** -0.5)
    inp['w2_1'] = nrm((N_EXPERTS, D_EXPERT, D), BETA * D_EXPERT ** -0.5)
    inp['w_router'] = nrm((D, N_EXPERTS), D ** -0.5)
    inp['b_router'] = nrm((N_EXPERTS,), 0.01)
    return inp


def reference(x, w_in_0, ssm_lam_re, ssm_lam_im, ssm_log_dt, ssm_b_re, ssm_b_im, ssm_c_re, ssm_c_im, ssm_d,
              w_glu, w_out_0, ln_mix_g_0, ln_mix_b_0, ln_ffn_g_0, ln_ffn_b_0, w1_0, w3_0, w2_0,
              w_in_1, cmp_pos_k, cmp_w1_k, cmp_w2_k, cmp_pos_v, cmp_w1_v, cmp_w2_v, w_out_1,
              ln_mix_g_1, ln_mix_b_1, ln_ffn_g_1, ln_ffn_b_1, w1_1, w3_1, w2_1,
              w_router, b_router):
    even_params = (w_in_0, ssm_lam_re, ssm_lam_im, ssm_log_dt, ssm_b_re, ssm_b_im, ssm_c_re, ssm_c_im,
                   ssm_d, w_glu, w_out_0)
    odd_params = (w_in_1, cmp_pos_k, cmp_w1_k, cmp_w2_k, cmp_pos_v, cmp_w1_v, cmp_w2_v, w_out_1)
    ln_params = ((ln_mix_g_0, ln_mix_b_0, ln_ffn_g_0, ln_ffn_b_0),
                 (ln_mix_g_1, ln_mix_b_1, ln_ffn_g_1, ln_ffn_b_1))
    experts = ((w1_0, w3_0, w2_0), (w1_1, w3_1, w2_1))
    h = x
    for layer in range(DEPTH):
        if layer % 2 == 0:
            m = even_mixer(h, *even_params)
        else:
            m = nsa_mixer(h, *odd_params)
        g_m, b_m, g_f, b_f = ln_params[layer]
        h = layer_norm(ALPHA * h + m, g_m, b_m)
        h = layer_norm(ALPHA * h + moe_ffn(h, w_router, b_router, *experts[layer]), g_f, b_f)
    return h
```

```python
import functools
import math

import numpy as np
import jax
import jax.numpy as jnp
from jax import lax
from jax.experimental import pallas as pl
from jax.experimental.pallas import tpu as pltpu

F32 = jnp.float32
BF16 = jnp.bfloat16

D_MODEL = 1024
DEPTH = 2
SB_HEADS = 8
HEAD_DIM = 64
SB_WIDTH = SB_HEADS * HEAD_DIM
SSM_WIDTH = D_MODEL - SB_WIDTH
SSM_GROUP = 16
SSM_GROUPS = SSM_WIDTH // SSM_GROUP
SSM_STATE = 64
SSM_CHUNK = 16
NSA_HEADS = 16
NSA_KV_HEADS = 4
NSA_REP = NSA_HEADS // NSA_KV_HEADS
CMP_LEN = 32
CMP_STRIDE = 16
CMP_HIDDEN = 256
SEL_LEN = 64
SEL_TOP = 8
WINDOW = 512
ROPE_THETA = 10000.0
FORCE_BONUS = 1e4
NEG = -1e30
N_EXPERTS = 16
N_GROUPS = 4
EXPERTS_PER_GROUP = N_EXPERTS // N_GROUPS
D_EXPERT = 512
ALPHA = (2 * DEPTH) ** 0.25
LN_EPS = 1e-5
ATTN_SCALE = HEAD_DIM ** -0.5
ATT_BLOCK = 128
GELU_C = math.sqrt(2.0 / math.pi)


def _params(*sem):
    return pltpu.CompilerParams(dimension_semantics=sem, vmem_limit_bytes=56 * 1024 * 1024)


def _sigmoid(x):
    return 1.0 / (1.0 + jnp.exp(-x))


def _gelu(x):
    return 0.5 * x * (1.0 + jnp.tanh(GELU_C * (x + 0.044715 * (x * x * x))))


def _layer_norm(r, g, b):
    mu = jnp.mean(r, axis=-1, keepdims=True)
    d = r - mu
    var = jnp.mean(d * d, axis=-1, keepdims=True)
    return d * lax.rsqrt(var + LN_EPS) * g + b


def _dot(a, b):
    return jnp.dot(a, b, preferred_element_type=F32)


def _dot_nt(a, b):
    return lax.dot_general(a, b, (((1,), (1,)), ((), ())), preferred_element_type=F32)


def _split_bf16(x):
    hi = x.astype(BF16)
    lo = (x - hi.astype(F32)).astype(BF16)
    return hi, lo


def _inproj0_body(x_ref, w_ref, q_ref, k_ref, v_ref, u_ref):
    p = _dot(x_ref[0].astype(BF16), w_ref[...])
    for h in range(SB_HEADS):
        c = h * HEAD_DIM
        q_ref[0, h] = (p[:, c:c + HEAD_DIM] * ATTN_SCALE).astype(BF16)
        k_ref[0, h] = p[:, SB_WIDTH + c:SB_WIDTH + c + HEAD_DIM].astype(BF16)
        v_ref[0, h] = p[:, 2 * SB_WIDTH + c:2 * SB_WIDTH + c + HEAD_DIM].astype(BF16)
    u_ref[0] = p[:, 3 * SB_WIDTH:]


def _inproj0(x, w_bf16, tl=512):
    B, L, D = x.shape
    tl = min(tl, L)
    nout = w_bf16.shape[1]
    head = jax.ShapeDtypeStruct((B, SB_HEADS, L, HEAD_DIM), BF16)
    head_spec = pl.BlockSpec((1, SB_HEADS, tl, HEAD_DIM), lambda b, i: (b, 0, i, 0))
    return pl.pallas_call(
        _inproj0_body,
        grid=(B, L // tl),
        in_specs=[pl.BlockSpec((1, tl, D), lambda b, i: (b, i, 0)),
                  pl.BlockSpec((D, nout), lambda b, i: (0, 0))],
        out_specs=[head_spec, head_spec, head_spec,
                   pl.BlockSpec((1, tl, SSM_WIDTH), lambda b, i: (b, i, 0))],
        out_shape=[head, head, head, jax.ShapeDtypeStruct((B, L, SSM_WIDTH), F32)],
        compiler_params=_params("parallel", "parallel"),
        name="inproj0",
    )(x, w_bf16)


def _sb_attn_body(q_ref, k_ref, v_ref, o_ref):
    i = pl.program_id(2)
    tq = ATT_BLOCK
    row = lax.broadcasted_iota(jnp.int32, (tq, tq), 0)
    col = lax.broadcasted_iota(jnp.int32, (tq, tq), 1)
    suffix = jnp.where(row > col, 1.0, 0.0).astype(BF16)
    outs = []
    for hh in range(2):
        q = q_ref[0, hh]

        def body(jj, carry, hh=hh, q=q):
            acc, cs = carry
            j = i - jj
            k0 = pl.multiple_of(j * tq, tq)
            kb = k_ref[0, hh, pl.ds(k0, tq), :]
            vb = v_ref[0, hh, pl.ds(k0, tq), :]
            z = _dot_nt(q, kb)
            sp = jnp.maximum(z, 0.0) + jnp.log(1.0 + jnp.exp(-jnp.abs(z)))
            mask = (k0 + col) < (i * tq + row)
            l1 = jnp.where(mask, -sp, 0.0)
            hi, lo = _split_bf16(l1)
            within = _dot(hi, suffix) + _dot(lo, suffix)
            w = jnp.where(mask, jnp.exp(z - sp + within + cs), 0.0)
            acc = acc + _dot(w.astype(BF16), vb)
            return acc, cs + jnp.sum(l1, axis=1, keepdims=True)

        acc, _ = lax.fori_loop(0, i + 1, body,
                               (jnp.zeros((tq, HEAD_DIM), F32), jnp.zeros((tq, 1), F32)))
        outs.append(acc)
    o_ref[0] = jnp.concatenate(outs, axis=-1).astype(o_ref.dtype)


def _sb_attention(q, k, v):
    B, H, L, d = q.shape
    tq = ATT_BLOCK
    return pl.pallas_call(
        _sb_attn_body,
        grid=(B, H // 2, L // tq),
        in_specs=[pl.BlockSpec((1, 2, tq, d), lambda b, h, i: (b, h, i, 0)),
                  pl.BlockSpec((1, 2, L, d), lambda b, h, i: (b, h, 0, 0)),
                  pl.BlockSpec((1, 2, L, d), lambda b, h, i: (b, h, 0, 0))],
        out_specs=pl.BlockSpec((1, tq, 2 * d), lambda b, h, i: (b, i, h)),
        out_shape=jax.ShapeDtypeStruct((B, L, H * d), BF16),
        compiler_params=_params("parallel", "parallel", "arbitrary"),
        name="sb_attention",
    )(q, k, v)


def _ssm_tables(lam_re, lam_im, log_dt, b_re, b_im, c_re, c_im, d_skip):
    T, G, P, H = SSM_CHUNK, SSM_GROUPS, SSM_STATE, SSM_GROUP
    hp = lax.Precision.HIGHEST
    dt = jnp.exp(log_dt.astype(F32))[:, None]
    lr = lam_re.astype(F32)
    li = lam_im.astype(F32)
    mag = jnp.exp(lr * dt)
    a_re = mag * jnp.cos(li * dt)
    a_im = mag * jnp.sin(li * dt)
    den = lr * lr + li * li
    nr = a_re - 1.0
    f_re = (nr * lr + a_im * li) / den
    f_im = (a_im * lr - nr * li) / den
    br = b_re.astype(F32)
    bi = b_im.astype(F32)
    bb_re = f_re[..., None] * br - f_im[..., None] * bi
    bb_im = f_re[..., None] * bi + f_im[..., None] * br
    tau = jnp.arange(T + 1, dtype=F32)[None, :, None]
    pmag = jnp.exp(tau * (lr * dt)[:, None, :])
    pw_re = pmag * jnp.cos(tau * (li * dt)[:, None, :])
    pw_im = pmag * jnp.sin(tau * (li * dt)[:, None, :])
    cr = c_re.astype(F32)[:, None]
    ci = c_im.astype(F32)[:, None]
    cpw_re = cr * pw_re[:, :, None, :] - ci * pw_im[:, :, None, :]
    cpw_im = cr * pw_im[:, :, None, :] + ci * pw_re[:, :, None, :]
    kern = (jnp.einsum('gthp,gpi->gthi', cpw_re, bb_re, precision=hp)
            - jnp.einsum('gthp,gpi->gthi', cpw_im, bb_im, precision=hp))
    s_idx = np.arange(T)[:, None]
    t_idx = np.arange(T)[None, :]
    lag = np.clip(t_idx - s_idx, 0, T)
    toe = kern[:, lag]
    toe = jnp.where(jnp.asarray(t_idx >= s_idx)[None, :, :, None, None], toe, 0.0)
    intra = toe.transpose(0, 1, 4, 2, 3).reshape(G, T * H, T * H)
    rev = np.arange(T - 1, -1, -1)
    qr = pw_re[:, rev][:, :, None, :]
    qi = pw_im[:, rev][:, :, None, :]
    bbr = bb_re.transpose(0, 2, 1)[:, None]
    bbi = bb_im.transpose(0, 2, 1)[:, None]
    st_re = (qr * bbr - qi * bbi).reshape(G, T * H, P)
    st_im = (qr * bbi + qi * bbr).reshape(G, T * H, P)
    to_st = jnp.concatenate([st_re, st_im, st_im, st_re], axis=-1)
    top = cpw_re[:, 1:].transpose(0, 3, 1, 2).reshape(G, P, T * H)
    bot = (-cpw_im[:, 1:]).transpose(0, 3, 1, 2).reshape(G, P, T * H)
    from_st = jnp.concatenate([top, bot], axis=1)
    at_re = pw_re[:, T]
    at_im = pw_im[:, T]
    zeros = jnp.zeros_like(at_re)
    adv = jnp.stack([jnp.concatenate([at_re, at_re], -1),
                     jnp.concatenate([-at_im, at_im], -1),
                     jnp.concatenate([at_im, -at_im], -1)]
                    + [jnp.concatenate([zeros, zeros], -1)] * 5, axis=1)
    dvec = jnp.tile(d_skip.astype(F32).reshape(G, 1, H), (1, 1, T))
    return intra.astype(BF16), to_st.astype(BF16), from_st.astype(BF16), adv, dvec


def _ssm_body(u_ref, intra_ref, tost_ref, fromst_ref, adv_ref, dvec_ref, y_ref, s_scr, xin_scr, *, bsz, chunks):
    P2 = 2 * SSM_STATE
    u = u_ref[0]
    ub = u.astype(BF16)
    s_scr[...] = _dot(ub, tost_ref[0])
    a1 = adv_ref[0, 0:1, :]
    a2 = adv_ref[0, 1:2, :]
    a3 = adv_ref[0, 2:3, :]

    def step(c, carry):
        x1, x2 = carry
        r0 = pl.multiple_of(c * bsz, bsz)
        xin_scr[pl.ds(r0, bsz), :] = x1
        s = s_scr[pl.ds(r0, bsz), :]
        return (a1 * x1 + a2 * x2 + s[:, :P2], a1 * x2 + a3 * x1 + s[:, P2:])

    zero = jnp.zeros((bsz, P2), F32)
    lax.fori_loop(0, chunks, step, (zero, zero))
    y_ref[0] = (_dot(ub, intra_ref[0]) + _dot(xin_scr[...].astype(BF16), fromst_ref[0]) + dvec_ref[0] * u)


def _ssm(u, tables):
    intra, to_st, from_st, adv, dvec = tables
    B, L, W = u.shape
    T, G, H, P = SSM_CHUNK, SSM_GROUPS, SSM_GROUP, SSM_STATE
    C = L // T
    ut = u.reshape(B, C, T, G, H).transpose(3, 1, 0, 2, 4).reshape(G, C * B, T * H)
    yt = pl.pallas_call(
        functools.partial(_ssm_body, bsz=B, chunks=C),
        grid=(G,),
        in_specs=[pl.BlockSpec((1, C * B, T * H), lambda g: (g, 0, 0)),
                  pl.BlockSpec((1, T * H, T * H), lambda g: (g, 0, 0)),
                  pl.BlockSpec((1, T * H, 4 * P), lambda g: (g, 0, 0)),
                  pl.BlockSpec((1, 2 * P, T * H), lambda g: (g, 0, 0)),
                  pl.BlockSpec((1, 8, 2 * P), lambda g: (g, 0, 0)),
                  pl.BlockSpec((1, 1, T * H), lambda g: (g, 0, 0))],
        out_specs=pl.BlockSpec((1, C * B, T * H), lambda g: (g, 0, 0)),
        out_shape=jax.ShapeDtypeStruct((G, C * B, T * H), F32),
        scratch_shapes=[pltpu.VMEM((C * B, 4 * P), F32), pltpu.VMEM((C * B, 2 * P), F32)],
        compiler_params=_params("parallel"),
        name="s5_scan",
    )(ut, intra, to_st, from_st, adv, dvec)
    return yt.reshape(G, C, B, T, H).transpose(2, 1, 3, 0, 4).reshape(B, L, W)


def _outproj0_body(x_ref, oa_ref, y_ref, wglu_ref, woa_ref, wob_ref, g_ref, b_ref, o_ref):
    h = _gelu(y_ref[...])
    ob = h * _sigmoid(_dot(h.astype(BF16), wglu_ref[...]))
    m = _dot(oa_ref[...], woa_ref[...]) + _dot(ob.astype(BF16), wob_ref[...])
    o_ref[...] = _layer_norm(ALPHA * x_ref[...] + m, g_ref[...], b_ref[...])


def _outproj0(x, oa, y, wglu, woa, wob, g, b, tm=512):
    N, D = x.shape
    tm = min(tm, N)
    row = lambda i: (i, 0)
    fix = lambda i: (0, 0)
    return pl.pallas_call(
        _outproj0_body,
        grid=(N // tm,),
        in_specs=[pl.BlockSpec((tm, D), row), pl.BlockSpec((tm, SB_WIDTH), row), pl.BlockSpec((tm, SSM_WIDTH), row),
                  pl.BlockSpec((SSM_WIDTH, SSM_WIDTH), fix), pl.BlockSpec((SB_WIDTH, D), fix),
                  pl.BlockSpec((SSM_WIDTH, D), fix), pl.BlockSpec((1, D), fix), pl.BlockSpec((1, D), fix)],
        out_specs=pl.BlockSpec((tm, D), row),
        out_shape=jax.ShapeDtypeStruct((N, D), F32),
        compiler_params=_params("parallel"),
        name="outproj0",
    )(x, oa, y, wglu, woa, wob, g, b)


def _outproj1_body(x_ref, o_ref_in, w_ref, g_ref, b_ref, o_ref):
    m = _dot(o_ref_in[...].astype(BF16), w_ref[...])
    o_ref[...] = _layer_norm(ALPHA * x_ref[...] + m, g_ref[...], b_ref[...])


def _outproj1(x, o, w, g, b, tm=512):
    N, D = x.shape
    tm = min(tm, N)
    row = lambda i: (i, 0)
    fix = lambda i: (0, 0)
    return pl.pallas_call(
        _outproj1_body,
        grid=(N // tm,),
        in_specs=[pl.BlockSpec((tm, D), row), pl.BlockSpec((tm, o.shape[1]), row),
                  pl.BlockSpec(w.shape, fix), pl.BlockSpec((1, D), fix), pl.BlockSpec((1, D), fix)],
        out_specs=pl.BlockSpec((tm, D), row),
        out_shape=jax.ShapeDtypeStruct((N, D), F32),
        compiler_params=_params("parallel"),
        name="outproj1",
    )(x, o, w, g, b)


def _router_body(x_ref, wh_ref, wl_ref, b_ref, g_ref):
    xh, xl = _split_bf16(x_ref[...])
    wh = wh_ref[...]
    logits = _dot_nt(wh, xh) + _dot_nt(wh, xl) + _dot_nt(wl_ref[...], xh)
    aff = _sigmoid(logits)
    sel = aff + b_ref[...]
    E, K = EXPERTS_PER_GROUP, N_GROUPS
    s = [sel[e:e + 1, :] for e in range(N_EXPERTS)]
    a = [aff[e:e + 1, :] for e in range(N_EXPERTS)]
    gscore = []
    for k in range(K):
        v = s[k * E:(k + 1) * E]
        best = None
        for x in range(E):
            for y in range(x + 1, E):
                pair = v[x] + v[y]
                best = pair if best is None else jnp.maximum(best, pair)
        gscore.append(best)
    top = functools.reduce(jnp.maximum, gscore)
    is_g = []
    taken = None
    for k in range(K):
        hit = gscore[k] == top
        if taken is None:
            is_g.append(hit)
            taken = hit
        else:
            is_g.append(jnp.logical_and(hit, jnp.logical_not(taken)))
            taken = jnp.logical_or(taken, hit)

    def pick(rows, j):
        out = rows[(K - 1) * E + j]
        for k in range(K - 2, -1, -1):
            out = jnp.where(is_g[k], rows[k * E + j], out)
        return out

    v = [pick(s, j) for j in range(E)]
    av = [pick(a, j) for j in range(E)]
    chosen = []
    for j in range(E):
        r = jnp.zeros_like(v[j])
        for j2 in range(E):
            if j2 == j:
                continue
            ahead = (v[j2] >= v[j]) if j2 < j else (v[j2] > v[j])
            r = r + jnp.where(ahead, 1.0, 0.0)
        chosen.append(r < 2.0)
    wj = [jnp.where(chosen[j], av[j], 0.0) for j in range(E)]
    tot = wj[0] + wj[1] + wj[2] + wj[3]
    gj = [w / tot for w in wj]
    rows = [jnp.where(is_g[e // E], gj[e % E], 0.0) for e in range(N_EXPERTS)]
    g_ref[...] = jnp.concatenate(rows, axis=0)


def _router(x, w_router, b_router, tm=512):
    N, D = x.shape
    tm = min(tm, N)
    wt = w_router.astype(F32).T
    wh, wl = _split_bf16(wt)
    fix = lambda i: (0, 0)
    return pl.pallas_call(
        _router_body,
        grid=(N // tm,),
        in_specs=[pl.BlockSpec((tm, D), lambda i: (i, 0)), pl.BlockSpec((N_EXPERTS, D), fix),
                  pl.BlockSpec((N_EXPERTS, D), fix), pl.BlockSpec((N_EXPERTS, 1), fix)],
        out_specs=pl.BlockSpec((N_EXPERTS, tm), lambda i: (0, i)),
        out_shape=jax.ShapeDtypeStruct((N_EXPERTS, N), F32),
        compiler_params=_params("parallel"),
        name="router",
    )(x, wh, wl, b_router.astype(F32).reshape(N_EXPERTS, 1))


def _moe_body(x_ref, gates_ref, w1_ref, w3_ref, w2_ref, g_ref, b_ref, o_ref, xb_scr, acc_scr):
    e = pl.program_id(1)

    @pl.when(e == 0)
    def _():
        xb_scr[...] = x_ref[...].astype(BF16)
        acc_scr[...] = jnp.zeros_like(acc_scr)

    xb = xb_scr[...]
    a = _dot(xb, w1_ref[0])
    hm = (a * _sigmoid(a)) * _dot(xb, w3_ref[0])
    gates = gates_ref[...]
    lane = lax.broadcasted_iota(jnp.int32, gates.shape, 1)
    gcol = jnp.sum(jnp.where(lane == e, gates, 0.0), axis=1, keepdims=True)
    acc_scr[...] += gcol * _dot(hm.astype(BF16), w2_ref[0])

    @pl.when(e == pl.num_programs(1) - 1)
    def _():
        o_ref[...] = _layer_norm(ALPHA * x_ref[...] + acc_scr[...], g_ref[...], b_ref[...])


def _moe(x, gates, w1, w3, w2, g, b, tm=512):
    N, D = x.shape
    tm = min(tm, N)
    fix = lambda i, e: (0, 0)
    return pl.pallas_call(
        _moe_body,
        grid=(N // tm, N_EXPERTS),
        in_specs=[pl.BlockSpec((tm, D), lambda i, e: (i, 0)), pl.BlockSpec((tm, N_EXPERTS), lambda i, e: (i, 0)),
                  pl.BlockSpec((1, D, D_EXPERT), lambda i, e: (e, 0, 0)),
                  pl.BlockSpec((1, D, D_EXPERT), lambda i, e: (e, 0, 0)),
                  pl.BlockSpec((1, D_EXPERT, D), lambda i, e: (e, 0, 0)),
                  pl.BlockSpec((1, D), fix), pl.BlockSpec((1, D), fix)],
        out_specs=pl.BlockSpec((tm, D), lambda i, e: (i, 0)),
        out_shape=jax.ShapeDtypeStruct((N, D), F32),
        scratch_shapes=[pltpu.VMEM((tm, D), BF16), pltpu.VMEM((tm, D), F32)],
        compiler_params=_params("parallel", "arbitrary"),
        name="moe",
    )(x, gates, w1, w3, w2, g, b)


NSA_KVW = NSA_KV_HEADS * HEAD_DIM
NSA_ROPE_W = NSA_HEADS * HEAD_DIM + 2 * NSA_KVW


def _rot_cols(w):
    K, n = w.shape
    w3 = w.reshape(K, n // HEAD_DIM, 2, HEAD_DIM // 2)
    return jnp.stack([-w3[:, :, 1], w3[:, :, 0]], axis=2).reshape(K, n)


def _inproj1_weights(w_in):
    H, G, d = NSA_HEADS, NSA_KV_HEADS, HEAD_DIM
    cuts = [H * d + i * NSA_KVW for i in range(7)]
    q, kc, vc, ks, vs, kw, vw, gate = jnp.split(w_in.astype(F32), cuts, axis=1)
    rope = jnp.concatenate([q, ks, kw], axis=1)
    gate = jnp.pad(gate, ((0, 0), (0, 128 - gate.shape[1])))
    return jnp.concatenate([rope, _rot_cols(rope), kc, vc, vs, vw, gate], axis=1).astype(BF16)


def _rope_tables(pos):
    inv = ROPE_THETA ** (-jnp.arange(0, HEAD_DIM, 2, dtype=F32) / HEAD_DIM)
    ang = pos.astype(F32)[:, None] * inv[None, :]
    c, s = jnp.cos(ang), jnp.sin(ang)
    return jnp.concatenate([c, c], axis=-1), jnp.concatenate([s, s], axis=-1)


def _inproj1_body(x_ref, w_ref, cos_ref, sin_ref, q_ref, ks_ref, kw_ref, kc_ref, vc_ref, vs_ref, vw_ref, gate_ref):
    p = _dot(x_ref[0].astype(BF16), w_ref[...])
    R = NSA_ROPE_W
    cos = cos_ref[...]
    sin = sin_ref[...]
    roped = []
    for c in range(R // 128):
        roped.append(p[:, c * 128:(c + 1) * 128] * cos + p[:, R + c * 128:R + (c + 1) * 128] * sin)
    d = HEAD_DIM

    def head(chunks, h):
        blk = chunks[h // 2]
        return blk[:, (h % 2) * d:(h % 2 + 1) * d]

    for h in range(NSA_HEADS):
        q_ref[0, h] = (head(roped, h) * ATTN_SCALE).astype(BF16)
    nq = NSA_HEADS // 2
    for g in range(NSA_KV_HEADS):
        ks_ref[0, g] = head(roped[nq:], g).astype(BF16)
        kw_ref[0, g] = head(roped[nq + 2:], g).astype(BF16)
    base = 2 * R
    for n, ref in enumerate((kc_ref, vc_ref, vs_ref, vw_ref)):
        for g in range(NSA_KV_HEADS):
            c0 = base + n * NSA_KVW + g * d
            ref[0, g] = p[:, c0:c0 + d].astype(BF16)
    gate_ref[0] = p[:, base + 4 * NSA_KVW:]


def _inproj1(x, w_all, cos2, sin2, tl=256):
    B, L, D = x.shape
    tl = min(tl, L)
    nout = w_all.shape[1]
    qh = jax.ShapeDtypeStruct((B, NSA_HEADS, L, HEAD_DIM), BF16)
    kvh = jax.ShapeDtypeStruct((B, NSA_KV_HEADS, L, HEAD_DIM), BF16)
    q_spec = pl.BlockSpec((1, NSA_HEADS, tl, HEAD_DIM), lambda b, i: (b, 0, i, 0))
    kv_spec = pl.BlockSpec((1, NSA_KV_HEADS, tl, HEAD_DIM), lambda b, i: (b, 0, i, 0))
    return pl.pallas_call(
        _inproj1_body,
        grid=(B, L // tl),
        in_specs=[pl.BlockSpec((1, tl, D), lambda b, i: (b, i, 0)),
                  pl.BlockSpec((D, nout), lambda b, i: (0, 0)),
                  pl.BlockSpec((tl, 128), lambda b, i: (i, 0)),
                  pl.BlockSpec((tl, 128), lambda b, i: (i, 0))],
        out_specs=[q_spec] + [kv_spec] * 6 + [pl.BlockSpec((1, tl, 128), lambda b, i: (b, i, 0))],
        out_shape=[qh] + [kvh] * 6 + [jax.ShapeDtypeStruct((B, L, 128), F32)],
        compiler_params=_params("parallel", "parallel"),
        name="inproj1",
    )(x, w_all, cos2, sin2)


def _compress_body(kc_ref, vc_ref, posk_ref, posv_ref, w1k_ref, w1v_ref, w2k_ref, w2kr_ref, w2v_ref,
                   cos_ref, sin_ref, kcmp_ref, vcmp_ref):
    def hidden(a_ref, pos_ref, w1_ref):
        a = a_ref[0, 0].astype(F32)
        nrow = a.shape[0]
        lo = _dot((a + pos_ref[0:1, :]).astype(BF16), w1_ref[0])
        hi = _dot((a + pos_ref[1:2, :]).astype(BF16), w1_ref[1])
        hi_next = pltpu.roll(hi, nrow - 1, 0)
        return _gelu(lo + hi_next).astype(BF16)

    hk = hidden(kc_ref, posk_ref, w1k_ref)
    kcmp = _dot(hk, w2k_ref[...]) * cos_ref[...] + _dot(hk, w2kr_ref[...]) * sin_ref[...]
    kcmp_ref[0, 0] = kcmp.astype(BF16)
    hv = hidden(vc_ref, posv_ref, w1v_ref)
    vcmp_ref[0, 0] = _dot(hv, w2v_ref[...]).astype(BF16)


def _compress(kc, vc, pos_k, w1_k, w2_k, pos_v, w1_v, w2_v):
    B, G, L, d = kc.shape
    half = CMP_STRIDE * d
    nb = L // CMP_STRIDE
    kc2 = kc.reshape(B, G, nb, half)
    vc2 = vc.reshape(B, G, nb, half)
    posk = pos_k.astype(F32).reshape(2, half)
    posv = pos_v.astype(F32).reshape(2, half)
    w1k = w1_k.astype(BF16).reshape(2, half, CMP_HIDDEN)
    w1v = w1_v.astype(BF16).reshape(2, half, CMP_HIDDEN)
    w2k = w2_k.astype(F32)
    cos, sin = _rope_tables(jnp.arange(nb) * CMP_STRIDE + CMP_LEN - 1)
    blk = pl.BlockSpec((1, 1, nb, half), lambda b, g: (b, g, 0, 0))
    out = pl.BlockSpec((1, 1, nb, d), lambda b, g: (b, g, 0, 0))
    fix2 = lambda b, g: (0, 0)
    fix3 = lambda b, g: (0, 0, 0)
    return pl.pallas_call(
        _compress_body,
        grid=(B, G),
        in_specs=[blk, blk, pl.BlockSpec((2, half), fix2), pl.BlockSpec((2, half), fix2),
                  pl.BlockSpec((2, half, CMP_HIDDEN), fix3), pl.BlockSpec((2, half, CMP_HIDDEN), fix3),
                  pl.BlockSpec((CMP_HIDDEN, d), fix2), pl.BlockSpec((CMP_HIDDEN, d), fix2),
                  pl.BlockSpec((CMP_HIDDEN, d), fix2), pl.BlockSpec((nb, d), fix2), pl.BlockSpec((nb, d), fix2)],
        out_specs=[out, out],
        out_shape=[jax.ShapeDtypeStruct((B, G, nb, d), BF16)] * 2,
        compiler_params=_params("parallel", "parallel"),
        name="compress",
    )(kc2, vc2, posk, posv, w1k, w1v, w2k.astype(BF16), _rot_cols(w2k).astype(BF16), w2_v.astype(BF16), cos, sin)


def _nsa_body(q_ref, kcmp_ref, vcmp_ref, ks_ref, vs_ref, kw_ref, vw_ref, gate_ref, o_ref, *, seq):
    i = pl.program_id(2)
    tq = ATT_BLOCK
    R, d = NSA_REP, HEAD_DIM
    nb = seq // SEL_LEN
    mc = seq // CMP_STRIDE
    q = q_ref[0].reshape(R * tq, d)
    t_col = i * tq + lax.broadcasted_iota(jnp.int32, (tq, 1), 0)
    t_row = i * tq + lax.broadcasted_iota(jnp.int32, (1, tq), 1)

    m_row = lax.broadcasted_iota(jnp.int32, (1, mc), 1)
    valid_c = (m_row * CMP_STRIDE + (CMP_LEN - 1)) <= t_col
    s = _dot_nt(q, kcmp_ref[0, 0]).reshape(R, tq, mc)
    s = jnp.where(valid_c[None], s, NEG)
    e = jnp.where(valid_c[None], jnp.exp(s - jnp.max(s, axis=-1, keepdims=True)), 0.0)
    den = jnp.sum(e, axis=-1, keepdims=True)
    p = e / jnp.where(den > 0.0, den, 1.0)
    o_c = _dot(p.reshape(R * tq, mc).astype(BF16), vcmp_ref[0, 0])

    psum = p[0]
    for r in range(1, R):
        psum = psum + p[r]
    n_col = lax.broadcasted_iota(jnp.int32, (nb, 1), 0)
    ovl = jnp.logical_and(m_row * CMP_STRIDE < (n_col + 1) * SEL_LEN,
                          m_row * CMP_STRIDE + CMP_LEN > n_col * SEL_LEN)
    ovl = jnp.where(ovl, 1.0, 0.0).astype(BF16)
    ph, plo = _split_bf16(psum)
    imp = _dot_nt(ovl, ph) + _dot_nt(ovl, plo)
    cur = t_row // SEL_LEN
    forced = jnp.logical_or(n_col == 0, jnp.logical_or(n_col == cur, n_col == cur - 1))
    score = jnp.where(n_col * SEL_LEN <= t_row, imp + jnp.where(forced, FORCE_BONUS, 0.0), NEG)
    rank = jnp.zeros((nb, tq), F32)
    for n2 in range(nb):
        other = score[n2:n2 + 1, :]
        tie = jnp.where(n_col > n2, 1.0, 0.0)
        rank = rank + jnp.where(other > score, 1.0, jnp.where(other == score, tie, 0.0))
    sel = jnp.where(rank < float(SEL_TOP), 1.0, 0.0).T.astype(BF16)

    k_lane = lax.broadcasted_iota(jnp.int32, (1, tq), 1)

    def sweep(k_ref, v_ref, j_lo, mask_fn):
        def body(j, carry):
            m_run, l_run, acc = carry
            k0 = pl.multiple_of(j * tq, tq)
            kb = k_ref[0, 0, pl.ds(k0, tq), :]
            vb = v_ref[0, 0, pl.ds(k0, tq), :]
            mask = mask_fn(j, k0 + k_lane)[None]
            sc = jnp.where(mask, _dot_nt(q, kb).reshape(R, tq, tq), NEG)
            m_new = jnp.maximum(m_run, jnp.max(sc, axis=-1, keepdims=True))
            pr = jnp.where(mask, jnp.exp(sc - m_new), 0.0)
            scale = jnp.exp(m_run - m_new)
            l_new = scale * l_run + jnp.sum(pr, axis=-1, keepdims=True)
            pv = _dot(pr.reshape(R * tq, tq).astype(BF16), vb).reshape(R, tq, d)
            return m_new, l_new, scale * acc + pv

        init = (jnp.full((R, tq, 1), NEG, F32), jnp.zeros((R, tq, 1), F32), jnp.zeros((R, tq, d), F32))
        _, l_fin, acc = lax.fori_loop(j_lo, i + 1, body, init)
        return acc / l_fin

    def sel_mask(j, kpos):
        pick = jnp.where(n_col == 2 * j + (k_lane >= SEL_LEN).astype(jnp.int32), 1.0, 0.0).astype(BF16)
        return jnp.logical_and(_dot(sel, pick) > 0.5, kpos <= t_col)

    o_s = sweep(ks_ref, vs_ref, 0, sel_mask)

    def win_mask(j, kpos):
        return jnp.logical_and(kpos <= t_col, kpos > t_col - WINDOW)

    o_w = sweep(kw_ref, vw_ref, jnp.maximum(i - WINDOW // tq, 0), win_mask)

    sg = _sigmoid(gate_ref[0, 0])
    o_c = o_c.reshape(R, tq, d)
    outs = []
    for r in range(R):
        outs.append(sg[:, 3 * r:3 * r + 1] * o_c[r] + sg[:, 3 * r + 1:3 * r + 2] * o_s[r]
                    + sg[:, 3 * r + 2:3 * r + 3] * o_w[r])
    o_ref[0] = jnp.concatenate(outs, axis=-1)


def _nsa_attention(q, kcmp, vcmp, ks, vs, kw, vw, gate):
    B, H, L, d = q.shape
    G, R = NSA_KV_HEADS, NSA_REP
    tq = ATT_BLOCK
    mc = L // CMP_STRIDE
    full = pl.BlockSpec((1, 1, L, d), lambda b, g, i: (b, g, 0, 0))
    cmp_spec = pl.BlockSpec((1, 1, mc, d), lambda b, g, i: (b, g, 0, 0))
    return pl.pallas_call(
        functools.partial(_nsa_body, seq=L),
        grid=(B, G, L // tq),
        in_specs=[pl.BlockSpec((1, R, tq, d), lambda b, g, i: (b, g, i, 0)), cmp_spec, cmp_spec,
                  full, full, full, full,
                  pl.BlockSpec((1, 1, tq, 3 * R), lambda b, g, i: (b, g, i, 0))],
        out_specs=pl.BlockSpec((1, tq, R * d), lambda b, g, i: (b, i, g)),
        out_shape=jax.ShapeDtypeStruct((B, L, H * d), F32),
        compiler_params=_params("parallel", "parallel", "arbitrary"),
        name="nsa_attention",
    )(q, kcmp, vcmp, ks, vs, kw, vw, gate)


def kernel(x, w_in_0, ssm_lam_re, ssm_lam_im, ssm_log_dt, ssm_b_re, ssm_b_im, ssm_c_re, ssm_c_im, ssm_d, w_glu, w_out_0, ln_mix_g_0, ln_mix_b_0, ln_ffn_g_0, ln_ffn_b_0, w1_0, w3_0, w2_0, w_in_1, cmp_pos_k, cmp_w1_k, cmp_w2_k, cmp_pos_v, cmp_w1_v, cmp_w2_v, w_out_1, ln_mix_g_1, ln_mix_b_1, ln_ffn_g_1, ln_ffn_b_1, w1_1, w3_1, w2_1, w_router, b_router):
    B, L, D = x.shape
    N = B * L
    vec = lambda a: a.astype(F32).reshape(1, D)

    def ffn(h, w1, w3, w2, g, b):
        gates = _router(h, w_router, b_router).T
        return _moe(h, gates, w1.astype(BF16), w3.astype(BF16), w2.astype(BF16), vec(g), vec(b))

    q, k, v, u = _inproj0(x, w_in_0.astype(BF16))
    o_a = _sb_attention(q, k, v).reshape(N, SB_WIDTH)
    y = _ssm(u, _ssm_tables(ssm_lam_re, ssm_lam_im, ssm_log_dt, ssm_b_re, ssm_b_im, ssm_c_re, ssm_c_im, ssm_d))
    w_out_0b = w_out_0.astype(BF16)
    h = _outproj0(x.reshape(N, D), o_a, y.reshape(N, SSM_WIDTH), w_glu.astype(BF16),
                  w_out_0b[:SB_WIDTH], w_out_0b[SB_WIDTH:], vec(ln_mix_g_0), vec(ln_mix_b_0))
    h = ffn(h, w1_0, w3_0, w2_0, ln_ffn_g_0, ln_ffn_b_0)

    cos, sin = _rope_tables(jnp.arange(L))
    cos2 = jnp.concatenate([cos, cos], axis=-1)
    sin2 = jnp.concatenate([sin, sin], axis=-1)
    q, ks, kw, kc, vc, vs, vw, gate = _inproj1(h.reshape(B, L, D), _inproj1_weights(w_in_1), cos2, sin2)
    kcmp, vcmp = _compress(kc, vc, cmp_pos_k, cmp_w1_k, cmp_w2_k, cmp_pos_v, cmp_w1_v, cmp_w2_v)
    gate = gate[:, :, :3 * NSA_HEADS].reshape(B, L, NSA_KV_HEADS, 3 * NSA_REP).transpose(0, 2, 1, 3)
    o = _nsa_attention(q, kcmp, vcmp, ks, vs, kw, vw, gate).reshape(N, NSA_HEADS * HEAD_DIM)
    h = _outproj1(h, o, w_out_1.astype(BF16), vec(ln_mix_g_1), vec(ln_mix_b_1))
    h = ffn(h, w1_1, w3_1, w2_1, ln_ffn_g_1, ln_ffn_b_1)
    return h.reshape(B, L, D)
```

```python
import functools
import math

import numpy as np
import jax
import jax.numpy as jnp
from jax import lax
from jax.experimental import pallas as pl
from jax.experimental.pallas import tpu as pltpu

F32 = jnp.float32
BF16 = jnp.bfloat16

D_MODEL = 1024
DEPTH = 2
SB_HEADS = 8
HEAD_DIM = 64
SB_WIDTH = SB_HEADS * HEAD_DIM
SSM_WIDTH = D_MODEL - SB_WIDTH
SSM_GROUP = 16
SSM_GROUPS = SSM_WIDTH // SSM_GROUP
SSM_STATE = 64
SSM_CHUNK = 16
NSA_HEADS = 16
NSA_KV_HEADS = 4
NSA_REP = NSA_HEADS // NSA_KV_HEADS
CMP_LEN = 32
CMP_STRIDE = 16
CMP_HIDDEN = 256
SEL_LEN = 64
SEL_TOP = 8
WINDOW = 512
ROPE_THETA = 10000.0
FORCE_BONUS = 1e4
NEG = -1e30
N_EXPERTS = 16
N_GROUPS = 4
EXPERTS_PER_GROUP = N_EXPERTS // N_GROUPS
D_EXPERT = 512
ALPHA = (2 * DEPTH) ** 0.25
LN_EPS = 1e-5
ATTN_SCALE = HEAD_DIM ** -0.5
ATT_BLOCK = 128
GELU_C = math.sqrt(2.0 / math.pi)


def _params(*sem):
    return pltpu.CompilerParams(dimension_semantics=sem, vmem_limit_bytes=56 * 1024 * 1024)


def _sigmoid(x):
    return 1.0 / (1.0 + jnp.exp(-x))


def _gelu(x):
    return 0.5 * x * (1.0 + jnp.tanh(GELU_C * (x + 0.044715 * (x * x * x))))


def _layer_norm(r, g, b):
    mu = jnp.mean(r, axis=-1, keepdims=True)
    d = r - mu
    var = jnp.mean(d * d, axis=-1, keepdims=True)
    return d * lax.rsqrt(var + LN_EPS) * g + b


def _dot(a, b):
    return jnp.dot(a, b, preferred_element_type=F32)


def _dot_nt(a, b):
    return lax.dot_general(a, b, (((1,), (1,)), ((), ())), preferred_element_type=F32)


def _split_bf16(x):
    hi = x.astype(BF16)
    lo = (x - hi.astype(F32)).astype(BF16)
    return hi, lo


def _inproj0_body(x_ref, w_ref, q_ref, k_ref, v_ref, u_ref):
    p = _dot(x_ref[0].astype(BF16), w_ref[...])
    for h in range(SB_HEADS):
        c = h * HEAD_DIM
        q_ref[0, h] = (p[:, c:c + HEAD_DIM] * ATTN_SCALE).astype(BF16)
        k_ref[0, h] = p[:, SB_WIDTH + c:SB_WIDTH + c + HEAD_DIM].astype(BF16)
        v_ref[0, h] = p[:, 2 * SB_WIDTH + c:2 * SB_WIDTH + c + HEAD_DIM].astype(BF16)
    u_ref[0] = p[:, 3 * SB_WIDTH:]


def _inproj0(x, w_bf16, tl=512):
    B, L, D = x.shape
    tl = min(tl, L)
    nout = w_bf16.shape[1]
    head = jax.ShapeDtypeStruct((B, SB_HEADS, L, HEAD_DIM), BF16)
    head_spec = pl.BlockSpec((1, SB_HEADS, tl, HEAD_DIM), lambda b, i: (b, 0, i, 0))
    return pl.pallas_call(
        _inproj0_body,
        grid=(B, L // tl),
        in_specs=[pl.BlockSpec((1, tl, D), lambda b, i: (b, i, 0)),
                  pl.BlockSpec((D, nout), lambda b, i: (0, 0))],
        out_specs=[head_spec, head_spec, head_spec,
                   pl.BlockSpec((1, tl, SSM_WIDTH), lambda b, i: (b, i, 0))],
        out_shape=[head, head, head, jax.ShapeDtypeStruct((B, L, SSM_WIDTH), F32)],
        compiler_params=_params("parallel", "parallel"),
        name="inproj0",
    )(x, w_bf16)


SB_BLOCK = 256
SB_HEADS_PER_STEP = 4
EXP_UNDERFLOW = -104.0


def _sb_attn_body(q_ref, k_ref, v_ref, o_ref, acc_scr, cs_scr):
    i = pl.program_id(2)
    t = q_ref.shape[2]
    nh = q_ref.shape[1]
    row = lax.broadcasted_iota(jnp.int32, (t, t), 0)
    col = lax.broadcasted_iota(jnp.int32, (t, t), 1)
    suffix = jnp.where(row > col, 1.0, 0.0).astype(BF16)
    below = col < row

    def block(hh, k0, diagonal):
        q = q_ref[0, hh]
        kb = k_ref[0, hh, pl.ds(k0, t), :]
        vb = v_ref[0, hh, pl.ds(k0, t), :]
        z = _dot_nt(q, kb)
        sp = jnp.maximum(z, 0.0) + jnp.log(1.0 + jnp.exp(-jnp.abs(z)))
        l1 = jnp.where(below, -sp, 0.0) if diagonal else -sp
        hi, lo = _split_bf16(l1)
        within = _dot(hi, suffix) + _dot(lo, suffix)
        if diagonal:
            w = jnp.where(below, jnp.exp(z - sp + within), 0.0)
            acc_scr[hh] = _dot(w.astype(BF16), vb)
            cs = jnp.sum(l1, axis=1, keepdims=True)
        else:
            w = jnp.exp(z - sp + within + cs_scr[hh])
            acc_scr[hh] += _dot(w.astype(BF16), vb)
            cs = cs_scr[hh] + jnp.sum(l1, axis=1, keepdims=True)
        cs_scr[hh] = cs
        return cs

    def alive(css):
        return (jnp.max(functools.reduce(jnp.maximum, css)) > EXP_UNDERFLOW).astype(jnp.int32)

    first = alive([block(hh, pl.multiple_of(i * t, t), True) for hh in range(nh)])

    def cond(c):
        return jnp.logical_and(c[0] <= i, c[1] > 0)

    def body(c):
        k0 = pl.multiple_of((i - c[0]) * t, t)
        return c[0] + 1, alive([block(hh, k0, False) for hh in range(nh)])

    lax.while_loop(cond, body, (jnp.int32(1), first))
    o_ref[0] = jnp.concatenate([acc_scr[hh] for hh in range(nh)], axis=-1).astype(o_ref.dtype)


def _sb_attention(q, k, v):
    B, H, L, d = q.shape
    t = min(SB_BLOCK, L)
    nh = SB_HEADS_PER_STEP
    return pl.pallas_call(
        _sb_attn_body,
        grid=(B, H // nh, L // t),
        in_specs=[pl.BlockSpec((1, nh, t, d), lambda b, h, i: (b, h, i, 0)),
                  pl.BlockSpec((1, nh, L, d), lambda b, h, i: (b, h, 0, 0)),
                  pl.BlockSpec((1, nh, L, d), lambda b, h, i: (b, h, 0, 0))],
        out_specs=pl.BlockSpec((1, t, nh * d), lambda b, h, i: (b, i, h)),
        out_shape=jax.ShapeDtypeStruct((B, L, H * d), BF16),
        scratch_shapes=[pltpu.VMEM((nh, t, d), F32), pltpu.VMEM((nh, t, 1), F32)],
        compiler_params=_params("parallel", "parallel", "arbitrary"),
        name="sb_attention",
    )(q, k, v)


def _ssm_tables(lam_re, lam_im, log_dt, b_re, b_im, c_re, c_im, d_skip):
    T, G, P, H = SSM_CHUNK, SSM_GROUPS, SSM_STATE, SSM_GROUP
    hp = lax.Precision.HIGHEST
    dt = jnp.exp(log_dt.astype(F32))[:, None]
    lr = lam_re.astype(F32)
    li = lam_im.astype(F32)
    mag = jnp.exp(lr * dt)
    a_re = mag * jnp.cos(li * dt)
    a_im = mag * jnp.sin(li * dt)
    den = lr * lr + li * li
    nr = a_re - 1.0
    f_re = (nr * lr + a_im * li) / den
    f_im = (a_im * lr - nr * li) / den
    br = b_re.astype(F32)
    bi = b_im.astype(F32)
    bb_re = f_re[..., None] * br - f_im[..., None] * bi
    bb_im = f_re[..., None] * bi + f_im[..., None] * br
    tau = jnp.arange(T + 1, dtype=F32)[None, :, None]
    pmag = jnp.exp(tau * (lr * dt)[:, None, :])
    pw_re = pmag * jnp.cos(tau * (li * dt)[:, None, :])
    pw_im = pmag * jnp.sin(tau * (li * dt)[:, None, :])
    cr = c_re.astype(F32)[:, None]
    ci = c_im.astype(F32)[:, None]
    cpw_re = cr * pw_re[:, :, None, :] - ci * pw_im[:, :, None, :]
    cpw_im = cr * pw_im[:, :, None, :] + ci * pw_re[:, :, None, :]
    kern = (jnp.einsum('gthp,gpi->gthi', cpw_re, bb_re, precision=hp)
            - jnp.einsum('gthp,gpi->gthi', cpw_im, bb_im, precision=hp))
    s_idx = np.arange(T)[:, None]
    t_idx = np.arange(T)[None, :]
    lag = np.clip(t_idx - s_idx, 0, T)
    toe = kern[:, lag]
    toe = jnp.where(jnp.asarray(t_idx >= s_idx)[None, :, :, None, None], toe, 0.0)
    intra = toe.transpose(0, 1, 4, 2, 3).reshape(G, T * H, T * H)
    rev = np.arange(T - 1, -1, -1)
    qr = pw_re[:, rev][:, :, None, :]
    qi = pw_im[:, rev][:, :, None, :]
    bbr = bb_re.transpose(0, 2, 1)[:, None]
    bbi = bb_im.transpose(0, 2, 1)[:, None]
    st_re = (qr * bbr - qi * bbi).reshape(G, T * H, P)
    st_im = (qr * bbi + qi * bbr).reshape(G, T * H, P)
    to_st = jnp.concatenate([st_re, st_im, st_im, st_re], axis=-1)
    top = cpw_re[:, 1:].transpose(0, 3, 1, 2).reshape(G, P, T * H)
    bot = (-cpw_im[:, 1:]).transpose(0, 3, 1, 2).reshape(G, P, T * H)
    from_st = jnp.concatenate([top, bot], axis=1)
    at_re = pw_re[:, T]
    at_im = pw_im[:, T]
    zeros = jnp.zeros_like(at_re)
    adv = jnp.stack([jnp.concatenate([at_re, at_re], -1),
                     jnp.concatenate([-at_im, at_im], -1),
                     jnp.concatenate([at_im, -at_im], -1)]
                    + [jnp.concatenate([zeros, zeros], -1)] * 5, axis=1)
    dvec = jnp.tile(d_skip.astype(F32).reshape(G, 1, H), (1, 1, T))
    return intra.astype(BF16), to_st.astype(BF16), from_st.astype(BF16), adv, dvec


def _ssm_body(u_ref, intra_ref, tost_ref, fromst_ref, adv_ref, dvec_ref, y_ref, s_scr, xin_scr, *, bsz, chunks):
    P2 = 2 * SSM_STATE
    u = u_ref[0]
    ub = u.astype(BF16)
    s_scr[...] = _dot(ub, tost_ref[0])
    a1 = adv_ref[0, 0:1, :]
    a2 = adv_ref[0, 1:2, :]
    a3 = adv_ref[0, 2:3, :]

    def step(c, carry):
        x1, x2 = carry
        r0 = pl.multiple_of(c * bsz, bsz)
        xin_scr[pl.ds(r0, bsz), :] = x1
        s = s_scr[pl.ds(r0, bsz), :]
        return (a1 * x1 + a2 * x2 + s[:, :P2], a1 * x2 + a3 * x1 + s[:, P2:])

    zero = jnp.zeros((bsz, P2), F32)
    lax.fori_loop(0, chunks, step, (zero, zero))
    y_ref[0] = (_dot(ub, intra_ref[0]) + _dot(xin_scr[...].astype(BF16), fromst_ref[0]) + dvec_ref[0] * u)


def _ssm(u, tables):
    intra, to_st, from_st, adv, dvec = tables
    B, L, W = u.shape
    T, G, H, P = SSM_CHUNK, SSM_GROUPS, SSM_GROUP, SSM_STATE
    C = L // T
    ut = u.reshape(B, C, T, G, H).transpose(3, 1, 0, 2, 4).reshape(G, C * B, T * H)
    yt = pl.pallas_call(
        functools.partial(_ssm_body, bsz=B, chunks=C),
        grid=(G,),
        in_specs=[pl.BlockSpec((1, C * B, T * H), lambda g: (g, 0, 0)),
                  pl.BlockSpec((1, T * H, T * H), lambda g: (g, 0, 0)),
                  pl.BlockSpec((1, T * H, 4 * P), lambda g: (g, 0, 0)),
                  pl.BlockSpec((1, 2 * P, T * H), lambda g: (g, 0, 0)),
                  pl.BlockSpec((1, 8, 2 * P), lambda g: (g, 0, 0)),
                  pl.BlockSpec((1, 1, T * H), lambda g: (g, 0, 0))],
        out_specs=pl.BlockSpec((1, C * B, T * H), lambda g: (g, 0, 0)),
        out_shape=jax.ShapeDtypeStruct((G, C * B, T * H), F32),
        scratch_shapes=[pltpu.VMEM((C * B, 4 * P), F32), pltpu.VMEM((C * B, 2 * P), F32)],
        compiler_params=_params("parallel"),
        name="s5_scan",
    )(ut, intra, to_st, from_st, adv, dvec)
    return yt.reshape(G, C, B, T, H).transpose(2, 1, 3, 0, 4).reshape(B, L, W)


def _outproj0_body(x_ref, oa_ref, y_ref, wglu_ref, woa_ref, wob_ref, g_ref, b_ref, o_ref):
    h = _gelu(y_ref[...])
    ob = h * _sigmoid(_dot(h.astype(BF16), wglu_ref[...]))
    m = _dot(oa_ref[...], woa_ref[...]) + _dot(ob.astype(BF16), wob_ref[...])
    o_ref[...] = _layer_norm(ALPHA * x_ref[...] + m, g_ref[...], b_ref[...])


def _outproj0(x, oa, y, wglu, woa, wob, g, b, tm=512):
    N, D = x.shape
    tm = min(tm, N)
    row = lambda i: (i, 0)
    fix = lambda i: (0, 0)
    return pl.pallas_call(
        _outproj0_body,
        grid=(N // tm,),
        in_specs=[pl.BlockSpec((tm, D), row), pl.BlockSpec((tm, SB_WIDTH), row), pl.BlockSpec((tm, SSM_WIDTH), row),
                  pl.BlockSpec((SSM_WIDTH, SSM_WIDTH), fix), pl.BlockSpec((SB_WIDTH, D), fix),
                  pl.BlockSpec((SSM_WIDTH, D), fix), pl.BlockSpec((1, D), fix), pl.BlockSpec((1, D), fix)],
        out_specs=pl.BlockSpec((tm, D), row),
        out_shape=jax.ShapeDtypeStruct((N, D), F32),
        compiler_params=_params("parallel"),
        name="outproj0",
    )(x, oa, y, wglu, woa, wob, g, b)


def _outproj1_body(x_ref, o_ref_in, w_ref, g_ref, b_ref, o_ref):
    m = _dot(o_ref_in[...].astype(BF16), w_ref[...])
    o_ref[...] = _layer_norm(ALPHA * x_ref[...] + m, g_ref[...], b_ref[...])


def _outproj1(x, o, w, g, b, tm=512):
    N, D = x.shape
    tm = min(tm, N)
    row = lambda i: (i, 0)
    fix = lambda i: (0, 0)
    return pl.pallas_call(
        _outproj1_body,
        grid=(N // tm,),
        in_specs=[pl.BlockSpec((tm, D), row), pl.BlockSpec((tm, o.shape[1]), row),
                  pl.BlockSpec(w.shape, fix), pl.BlockSpec((1, D), fix), pl.BlockSpec((1, D), fix)],
        out_specs=pl.BlockSpec((tm, D), row),
        out_shape=jax.ShapeDtypeStruct((N, D), F32),
        compiler_params=_params("parallel"),
        name="outproj1",
    )(x, o, w, g, b)


def _router_body(x_ref, wh_ref, wl_ref, b_ref, g_ref):
    xh, xl = _split_bf16(x_ref[...])
    wh = wh_ref[...]
    logits = _dot_nt(wh, xh) + _dot_nt(wh, xl) + _dot_nt(wl_ref[...], xh)
    aff = _sigmoid(logits)
    sel = aff + b_ref[...]
    E, K = EXPERTS_PER_GROUP, N_GROUPS
    s = [sel[e:e + 1, :] for e in range(N_EXPERTS)]
    a = [aff[e:e + 1, :] for e in range(N_EXPERTS)]
    gscore = []
    for k in range(K):
        v = s[k * E:(k + 1) * E]
        best = None
        for x in range(E):
            for y in range(x + 1, E):
                pair = v[x] + v[y]
                best = pair if best is None else jnp.maximum(best, pair)
        gscore.append(best)
    top = functools.reduce(jnp.maximum, gscore)
    is_g = []
    taken = None
    for k in range(K):
        hit = gscore[k] == top
        if taken is None:
            is_g.append(hit)
            taken = hit
        else:
            is_g.append(jnp.logical_and(hit, jnp.logical_not(taken)))
            taken = jnp.logical_or(taken, hit)

    def pick(rows, j):
        out = rows[(K - 1) * E + j]
        for k in range(K - 2, -1, -1):
            out = jnp.where(is_g[k], rows[k * E + j], out)
        return out

    v = [pick(s, j) for j in range(E)]
    av = [pick(a, j) for j in range(E)]
    chosen = []
    for j in range(E):
        r = jnp.zeros_like(v[j])
        for j2 in range(E):
            if j2 == j:
                continue
            ahead = (v[j2] >= v[j]) if j2 < j else (v[j2] > v[j])
            r = r + jnp.where(ahead, 1.0, 0.0)
        chosen.append(r < 2.0)
    wj = [jnp.where(chosen[j], av[j], 0.0) for j in range(E)]
    tot = wj[0] + wj[1] + wj[2] + wj[3]
    gj = [w / tot for w in wj]
    rows = [jnp.where(is_g[e // E], gj[e % E], 0.0) for e in range(N_EXPERTS)]
    g_ref[...] = jnp.concatenate(rows, axis=0)


def _router(x, w_router, b_router, tm=512):
    N, D = x.shape
    tm = min(tm, N)
    wt = w_router.astype(F32).T
    wh, wl = _split_bf16(wt)
    fix = lambda i: (0, 0)
    return pl.pallas_call(
        _router_body,
        grid=(N // tm,),
        in_specs=[pl.BlockSpec((tm, D), lambda i: (i, 0)), pl.BlockSpec((N_EXPERTS, D), fix),
                  pl.BlockSpec((N_EXPERTS, D), fix), pl.BlockSpec((N_EXPERTS, 1), fix)],
        out_specs=pl.BlockSpec((N_EXPERTS, tm), lambda i: (0, i)),
        out_shape=jax.ShapeDtypeStruct((N_EXPERTS, N), F32),
        compiler_params=_params("parallel"),
        name="router",
    )(x, wh, wl, b_router.astype(F32).reshape(N_EXPERTS, 1))


def _moe_body(x_ref, gates_ref, w1_ref, w3_ref, w2_ref, g_ref, b_ref, o_ref, xb_scr, acc_scr):
    e = pl.program_id(1)

    @pl.when(e == 0)
    def _():
        xb_scr[...] = x_ref[...].astype(BF16)
        acc_scr[...] = jnp.zeros_like(acc_scr)

    xb = xb_scr[...]
    a = _dot(xb, w1_ref[0])
    hm = (a * _sigmoid(a)) * _dot(xb, w3_ref[0])
    gates = gates_ref[...]
    lane = lax.broadcasted_iota(jnp.int32, gates.shape, 1)
    gcol = jnp.sum(jnp.where(lane == e, gates, 0.0), axis=1, keepdims=True)
    acc_scr[...] += gcol * _dot(hm.astype(BF16), w2_ref[0])

    @pl.when(e == pl.num_programs(1) - 1)
    def _():
        o_ref[...] = _layer_norm(ALPHA * x_ref[...] + acc_scr[...], g_ref[...], b_ref[...])


def _moe(x, gates, w1, w3, w2, g, b, tm=512):
    N, D = x.shape
    tm = min(tm, N)
    fix = lambda i, e: (0, 0)
    return pl.pallas_call(
        _moe_body,
        grid=(N // tm, N_EXPERTS),
        in_specs=[pl.BlockSpec((tm, D), lambda i, e: (i, 0)), pl.BlockSpec((tm, N_EXPERTS), lambda i, e: (i, 0)),
                  pl.BlockSpec((1, D, D_EXPERT), lambda i, e: (e, 0, 0)),
                  pl.BlockSpec((1, D, D_EXPERT), lambda i, e: (e, 0, 0)),
                  pl.BlockSpec((1, D_EXPERT, D), lambda i, e: (e, 0, 0)),
                  pl.BlockSpec((1, D), fix), pl.BlockSpec((1, D), fix)],
        out_specs=pl.BlockSpec((tm, D), lambda i, e: (i, 0)),
        out_shape=jax.ShapeDtypeStruct((N, D), F32),
        scratch_shapes=[pltpu.VMEM((tm, D), BF16), pltpu.VMEM((tm, D), F32)],
        compiler_params=_params("parallel", "arbitrary"),
        name="moe",
    )(x, gates, w1, w3, w2, g, b)


NSA_KVW = NSA_KV_HEADS * HEAD_DIM
NSA_ROPE_W = NSA_HEADS * HEAD_DIM + 2 * NSA_KVW


def _rot_cols(w):
    K, n = w.shape
    w3 = w.reshape(K, n // HEAD_DIM, 2, HEAD_DIM // 2)
    return jnp.stack([-w3[:, :, 1], w3[:, :, 0]], axis=2).reshape(K, n)


def _inproj1_weights(w_in):
    H, G, d = NSA_HEADS, NSA_KV_HEADS, HEAD_DIM
    cuts = [H * d + i * NSA_KVW for i in range(7)]
    q, kc, vc, ks, vs, kw, vw, gate = jnp.split(w_in.astype(F32), cuts, axis=1)
    rope = jnp.concatenate([q, ks, kw], axis=1)
    gate = jnp.pad(gate, ((0, 0), (0, 128 - gate.shape[1])))
    return jnp.concatenate([rope, _rot_cols(rope), kc, vc, vs, vw, gate], axis=1).astype(BF16)


def _rope_tables(pos):
    inv = ROPE_THETA ** (-jnp.arange(0, HEAD_DIM, 2, dtype=F32) / HEAD_DIM)
    ang = pos.astype(F32)[:, None] * inv[None, :]
    c, s = jnp.cos(ang), jnp.sin(ang)
    return jnp.concatenate([c, c], axis=-1), jnp.concatenate([s, s], axis=-1)


def _inproj1_body(x_ref, w_ref, cos_ref, sin_ref, q_ref, ks_ref, kw_ref, kc_ref, vc_ref, vs_ref, vw_ref, gate_ref):
    p = _dot(x_ref[0].astype(BF16), w_ref[...])
    R = NSA_ROPE_W
    cos = cos_ref[...]
    sin = sin_ref[...]
    roped = []
    for c in range(R // 128):
        roped.append(p[:, c * 128:(c + 1) * 128] * cos + p[:, R + c * 128:R + (c + 1) * 128] * sin)
    d = HEAD_DIM

    def head(chunks, h):
        blk = chunks[h // 2]
        return blk[:, (h % 2) * d:(h % 2 + 1) * d]

    for h in range(NSA_HEADS):
        q_ref[0, h] = (head(roped, h) * ATTN_SCALE).astype(BF16)
    nq = NSA_HEADS // 2
    for g in range(NSA_KV_HEADS):
        ks_ref[0, g] = head(roped[nq:], g).astype(BF16)
        kw_ref[0, g] = head(roped[nq + 2:], g).astype(BF16)
    base = 2 * R
    for n, ref in enumerate((kc_ref, vc_ref, vs_ref, vw_ref)):
        for g in range(NSA_KV_HEADS):
            c0 = base + n * NSA_KVW + g * d
            ref[0, g] = p[:, c0:c0 + d].astype(BF16)
    gate_ref[0] = p[:, base + 4 * NSA_KVW:]


def _inproj1(x, w_all, cos2, sin2, tl=256):
    B, L, D = x.shape
    tl = min(tl, L)
    nout = w_all.shape[1]
    qh = jax.ShapeDtypeStruct((B, NSA_HEADS, L, HEAD_DIM), BF16)
    kvh = jax.ShapeDtypeStruct((B, NSA_KV_HEADS, L, HEAD_DIM), BF16)
    q_spec = pl.BlockSpec((1, NSA_HEADS, tl, HEAD_DIM), lambda b, i: (b, 0, i, 0))
    kv_spec = pl.BlockSpec((1, NSA_KV_HEADS, tl, HEAD_DIM), lambda b, i: (b, 0, i, 0))
    return pl.pallas_call(
        _inproj1_body,
        grid=(B, L // tl),
        in_specs=[pl.BlockSpec((1, tl, D), lambda b, i: (b, i, 0)),
                  pl.BlockSpec((D, nout), lambda b, i: (0, 0)),
                  pl.BlockSpec((tl, 128), lambda b, i: (i, 0)),
                  pl.BlockSpec((tl, 128), lambda b, i: (i, 0))],
        out_specs=[q_spec] + [kv_spec] * 6 + [pl.BlockSpec((1, tl, 128), lambda b, i: (b, i, 0))],
        out_shape=[qh] + [kvh] * 6 + [jax.ShapeDtypeStruct((B, L, 128), F32)],
        compiler_params=_params("parallel", "parallel"),
        name="inproj1",
    )(x, w_all, cos2, sin2)


def _compress_body(kc_ref, vc_ref, posk_ref, posv_ref, w1k_ref, w1v_ref, w2k_ref, w2kr_ref, w2v_ref,
                   cos_ref, sin_ref, kcmp_ref, vcmp_ref):
    def hidden(a_ref, pos_ref, w1_ref):
        a = a_ref[0, 0].astype(F32)
        nrow = a.shape[0]
        lo = _dot((a + pos_ref[0:1, :]).astype(BF16), w1_ref[0])
        hi = _dot((a + pos_ref[1:2, :]).astype(BF16), w1_ref[1])
        hi_next = pltpu.roll(hi, nrow - 1, 0)
        return _gelu(lo + hi_next).astype(BF16)

    hk = hidden(kc_ref, posk_ref, w1k_ref)
    kcmp = _dot(hk, w2k_ref[...]) * cos_ref[...] + _dot(hk, w2kr_ref[...]) * sin_ref[...]
    kcmp_ref[0, 0] = kcmp.astype(BF16)
    hv = hidden(vc_ref, posv_ref, w1v_ref)
    vcmp_ref[0, 0] = _dot(hv, w2v_ref[...]).astype(BF16)


def _compress(kc, vc, pos_k, w1_k, w2_k, pos_v, w1_v, w2_v):
    B, G, L, d = kc.shape
    half = CMP_STRIDE * d
    nb = L // CMP_STRIDE
    kc2 = kc.reshape(B, G, nb, half)
    vc2 = vc.reshape(B, G, nb, half)
    posk = pos_k.astype(F32).reshape(2, half)
    posv = pos_v.astype(F32).reshape(2, half)
    w1k = w1_k.astype(BF16).reshape(2, half, CMP_HIDDEN)
    w1v = w1_v.astype(BF16).reshape(2, half, CMP_HIDDEN)
    w2k = w2_k.astype(F32)
    cos, sin = _rope_tables(jnp.arange(nb) * CMP_STRIDE + CMP_LEN - 1)
    blk = pl.BlockSpec((1, 1, nb, half), lambda b, g: (b, g, 0, 0))
    out = pl.BlockSpec((1, 1, nb, d), lambda b, g: (b, g, 0, 0))
    fix2 = lambda b, g: (0, 0)
    fix3 = lambda b, g: (0, 0, 0)
    return pl.pallas_call(
        _compress_body,
        grid=(B, G),
        in_specs=[blk, blk, pl.BlockSpec((2, half), fix2), pl.BlockSpec((2, half), fix2),
                  pl.BlockSpec((2, half, CMP_HIDDEN), fix3), pl.BlockSpec((2, half, CMP_HIDDEN), fix3),
                  pl.BlockSpec((CMP_HIDDEN, d), fix2), pl.BlockSpec((CMP_HIDDEN, d), fix2),
                  pl.BlockSpec((CMP_HIDDEN, d), fix2), pl.BlockSpec((nb, d), fix2), pl.BlockSpec((nb, d), fix2)],
        out_specs=[out, out],
        out_shape=[jax.ShapeDtypeStruct((B, G, nb, d), BF16)] * 2,
        compiler_params=_params("parallel", "parallel"),
        name="compress",
    )(kc2, vc2, posk, posv, w1k, w1v, w2k.astype(BF16), _rot_cols(w2k).astype(BF16), w2_v.astype(BF16), cos, sin)


NSA_KBLOCK = 256
NSA_GROUPS_PER_STEP = 4


def _nsa_body(q_ref, kcmp_ref, vcmp_ref, ks_ref, vs_ref, kw_ref, vw_ref, gate_ref, o_ref,
              m_scr, l_scr, acc_scr, *, seq):
    i = pl.program_id(2)
    tq = q_ref.shape[2]
    gb = kcmp_ref.shape[1]
    tk = min(NSA_KBLOCK, seq)
    R, d = NSA_REP, HEAD_DIM
    nb = seq // SEL_LEN
    mc = seq // CMP_STRIDE
    t_col = i * tq + lax.broadcasted_iota(jnp.int32, (tq, 1), 0)
    t_row = i * tq + lax.broadcasted_iota(jnp.int32, (1, tq), 1)
    m_row = lax.broadcasted_iota(jnp.int32, (1, mc), 1)
    n_col = lax.broadcasted_iota(jnp.int32, (nb, 1), 0)

    def q_rows(g):
        return q_ref[0, g * R:(g + 1) * R].reshape(R * tq, d)

    valid_c = (m_row * CMP_STRIDE + (CMP_LEN - 1)) <= t_col
    ovl = jnp.logical_and(m_row * CMP_STRIDE < (n_col + 1) * SEL_LEN,
                          m_row * CMP_STRIDE + CMP_LEN > n_col * SEL_LEN)
    ovl = jnp.where(ovl, 1.0, 0.0).astype(BF16)
    cur = t_row // SEL_LEN
    forced = jnp.logical_or(n_col == 0, jnp.logical_or(n_col == cur, n_col == cur - 1))
    bonus = jnp.where(forced, FORCE_BONUS, 0.0)
    valid_s = n_col * SEL_LEN <= t_row
    o_c, sel = [], []
    for g in range(gb):
        s = _dot_nt(q_rows(g), kcmp_ref[0, g]).reshape(R, tq, mc)
        s = jnp.where(valid_c[None], s, NEG)
        e = jnp.where(valid_c[None], jnp.exp(s - jnp.max(s, axis=-1, keepdims=True)), 0.0)
        den = jnp.sum(e, axis=-1, keepdims=True)
        p = e / jnp.where(den > 0.0, den, 1.0)
        o_c.append(_dot(p.reshape(R * tq, mc).astype(BF16), vcmp_ref[0, g]).reshape(R, tq, d))
        psum = p[0]
        for r in range(1, R):
            psum = psum + p[r]
        ph, plo = _split_bf16(psum)
        score = jnp.where(valid_s, _dot_nt(ovl, ph) + _dot_nt(ovl, plo) + bonus, NEG)
        rank = jnp.zeros((nb, tq), F32)
        for n2 in range(nb):
            other = score[n2:n2 + 1, :]
            tie = jnp.where(n_col > n2, 1.0, 0.0)
            rank = rank + jnp.where(other > score, 1.0, jnp.where(other == score, tie, 0.0))
        sel.append(jnp.where(rank < float(SEL_TOP), 1.0, 0.0).T.astype(BF16))

    k_lane = lax.broadcasted_iota(jnp.int32, (1, tk), 1)
    j_hi = (i * tq + tq - 1) // tk + 1

    def sweep(k_ref, v_ref, j_lo, masks_fn):
        for g in range(gb):
            m_scr[g] = jnp.full((R, tq, 1), NEG, F32)
            l_scr[g] = jnp.zeros((R, tq, 1), F32)
            acc_scr[g] = jnp.zeros((R, tq, d), F32)

        def body(j, c):
            k0 = pl.multiple_of(j * tk, tk)
            masks = masks_fn(k0 + k_lane)
            for g in range(gb):
                kb = k_ref[0, g, pl.ds(k0, tk), :]
                vb = v_ref[0, g, pl.ds(k0, tk), :]
                mask = masks[g][None]
                sc = jnp.where(mask, _dot_nt(q_rows(g), kb).reshape(R, tq, tk), NEG)
                m_run = m_scr[g]
                m_new = jnp.maximum(m_run, jnp.max(sc, axis=-1, keepdims=True))
                pr = jnp.where(mask, jnp.exp(sc - m_new), 0.0)
                scale = jnp.exp(m_run - m_new)
                l_scr[g] = scale * l_scr[g] + jnp.sum(pr, axis=-1, keepdims=True)
                pv = _dot(pr.reshape(R * tq, tk).astype(BF16), vb).reshape(R, tq, d)
                acc_scr[g] = scale * acc_scr[g] + pv
                m_scr[g] = m_new
            return c

        lax.fori_loop(j_lo, j_hi, body, 0)
        return [acc_scr[g] / l_scr[g] for g in range(gb)]

    def sel_masks(kpos):
        pick = jnp.where(n_col == kpos // SEL_LEN, 1.0, 0.0).astype(BF16)
        causal = kpos <= t_col
        return [jnp.logical_and(_dot(sel[g], pick) > 0.5, causal) for g in range(gb)]

    o_s = sweep(ks_ref, vs_ref, 0, sel_masks)

    def win_masks(kpos):
        return [jnp.logical_and(kpos <= t_col, kpos > t_col - WINDOW)] * gb

    o_w = sweep(kw_ref, vw_ref, jnp.maximum(i * tq - (WINDOW - 1), 0) // tk, win_masks)

    outs = []
    for g in range(gb):
        sg = _sigmoid(gate_ref[0, g])
        for r in range(R):
            outs.append(sg[:, 3 * r:3 * r + 1] * o_c[g][r] + sg[:, 3 * r + 1:3 * r + 2] * o_s[g][r]
                        + sg[:, 3 * r + 2:3 * r + 3] * o_w[g][r])
    o_ref[0] = jnp.concatenate(outs, axis=-1)


def _nsa_attention(q, kcmp, vcmp, ks, vs, kw, vw, gate):
    B, H, L, d = q.shape
    G, R = NSA_KV_HEADS, NSA_REP
    gb = NSA_GROUPS_PER_STEP
    tq = ATT_BLOCK
    mc = L // CMP_STRIDE
    full = pl.BlockSpec((1, gb, L, d), lambda b, g, i: (b, g, 0, 0))
    cmp_spec = pl.BlockSpec((1, gb, mc, d), lambda b, g, i: (b, g, 0, 0))
    return pl.pallas_call(
        functools.partial(_nsa_body, seq=L),
        grid=(B, G // gb, L // tq),
        in_specs=[pl.BlockSpec((1, gb * R, tq, d), lambda b, g, i: (b, g, i, 0)), cmp_spec, cmp_spec,
                  full, full, full, full,
                  pl.BlockSpec((1, gb, tq, 3 * R), lambda b, g, i: (b, g, i, 0))],
        out_specs=pl.BlockSpec((1, tq, gb * R * d), lambda b, g, i: (b, i, g)),
        out_shape=jax.ShapeDtypeStruct((B, L, H * d), F32),
        scratch_shapes=[pltpu.VMEM((gb, R, tq, 1), F32), pltpu.VMEM((gb, R, tq, 1), F32),
                        pltpu.VMEM((gb, R, tq, d), F32)],
        compiler_params=_params("parallel", "parallel", "arbitrary"),
        name="nsa_attention",
    )(q, kcmp, vcmp, ks, vs, kw, vw, gate)


def kernel(x, w_in_0, ssm_lam_re, ssm_lam_im, ssm_log_dt, ssm_b_re, ssm_b_im, ssm_c_re, ssm_c_im, ssm_d, w_glu, w_out_0, ln_mix_g_0, ln_mix_b_0, ln_ffn_g_0, ln_ffn_b_0, w1_0, w3_0, w2_0, w_in_1, cmp_pos_k, cmp_w1_k, cmp_w2_k, cmp_pos_v, cmp_w1_v, cmp_w2_v, w_out_1, ln_mix_g_1, ln_mix_b_1, ln_ffn_g_1, ln_ffn_b_1, w1_1, w3_1, w2_1, w_router, b_router):
    B, L, D = x.shape
    N = B * L
    vec = lambda a: a.astype(F32).reshape(1, D)

    def ffn(h, w1, w3, w2, g, b):
        gates = _router(h, w_router, b_router).T
        return _moe(h, gates, w1.astype(BF16), w3.astype(BF16), w2.astype(BF16), vec(g), vec(b))

    q, k, v, u = _inproj0(x, w_in_0.astype(BF16))
    o_a = _sb_attention(q, k, v).reshape(N, SB_WIDTH)
    y = _ssm(u, _ssm_tables(ssm_lam_re, ssm_lam_im, ssm_log_dt, ssm_b_re, ssm_b_im, ssm_c_re, ssm_c_im, ssm_d))
    w_out_0b = w_out_0.astype(BF16)
    h = _outproj0(x.reshape(N, D), o_a, y.reshape(N, SSM_WIDTH), w_glu.astype(BF16),
                  w_out_0b[:SB_WIDTH], w_out_0b[SB_WIDTH:], vec(ln_mix_g_0), vec(ln_mix_b_0))
    h = ffn(h, w1_0, w3_0, w2_0, ln_ffn_g_0, ln_ffn_b_0)

    cos, sin = _rope_tables(jnp.arange(L))
    cos2 = jnp.concatenate([cos, cos], axis=-1)
    sin2 = jnp.concatenate([sin, sin], axis=-1)
    q, ks, kw, kc, vc, vs, vw, gate = _inproj1(h.reshape(B, L, D), _inproj1_weights(w_in_1), cos2, sin2)
    kcmp, vcmp = _compress(kc, vc, cmp_pos_k, cmp_w1_k, cmp_w2_k, cmp_pos_v, cmp_w1_v, cmp_w2_v)
    gate = gate[:, :, :3 * NSA_HEADS].reshape(B, L, NSA_KV_HEADS, 3 * NSA_REP).transpose(0, 2, 1, 3)
    o = _nsa_attention(q, kcmp, vcmp, ks, vs, kw, vw, gate).reshape(N, NSA_HEADS * HEAD_DIM)
    h = _outproj1(h, o, w_out_1.astype(BF16), vec(ln_mix_g_1), vec(ln_mix_b_1))
    h = ffn(h, w1_1, w3_1, w2_1, ln_ffn_g_1, ln_ffn_b_1)
    return h.reshape(B, L, D)
```

```python
import functools
import math

import numpy as np
import jax
import jax.numpy as jnp
from jax import lax
from jax.experimental import pallas as pl
from jax.experimental.pallas import tpu as pltpu

F32 = jnp.float32
BF16 = jnp.bfloat16

D_MODEL = 1024
DEPTH = 2
SB_HEADS = 8
HEAD_DIM = 64
SB_WIDTH = SB_HEADS * HEAD_DIM
SSM_WIDTH = D_MODEL - SB_WIDTH
SSM_GROUP = 16
SSM_GROUPS = SSM_WIDTH // SSM_GROUP
SSM_STATE = 64
SSM_CHUNK = 16
NSA_HEADS = 16
NSA_KV_HEADS = 4
NSA_REP = NSA_HEADS // NSA_KV_HEADS
CMP_LEN = 32
CMP_STRIDE = 16
CMP_HIDDEN = 256
SEL_LEN = 64
SEL_TOP = 8
WINDOW = 512
ROPE_THETA = 10000.0
FORCE_BONUS = 1e4
NEG = -1e30
N_EXPERTS = 16
N_GROUPS = 4
EXPERTS_PER_GROUP = N_EXPERTS // N_GROUPS
D_EXPERT = 512
ALPHA = (2 * DEPTH) ** 0.25
LN_EPS = 1e-5
ATTN_SCALE = HEAD_DIM ** -0.5
ATT_BLOCK = 128
GELU_C = math.sqrt(2.0 / math.pi)


def _params(*sem):
    return pltpu.CompilerParams(dimension_semantics=sem, vmem_limit_bytes=56 * 1024 * 1024)


def _sigmoid(x):
    return 1.0 / (1.0 + jnp.exp(-x))


def _gelu(x):
    return 0.5 * x * (1.0 + jnp.tanh(GELU_C * (x + 0.044715 * (x * x * x))))


def _layer_norm(r, g, b):
    mu = jnp.mean(r, axis=-1, keepdims=True)
    d = r - mu
    var = jnp.mean(d * d, axis=-1, keepdims=True)
    return d * lax.rsqrt(var + LN_EPS) * g + b


def _dot(a, b):
    return jnp.dot(a, b, preferred_element_type=F32)


def _dot_nt(a, b):
    return lax.dot_general(a, b, (((1,), (1,)), ((), ())), preferred_element_type=F32)


def _split_bf16(x):
    hi = x.astype(BF16)
    lo = (x - hi.astype(F32)).astype(BF16)
    return hi, lo


def _inproj0_body(x_ref, w_ref, q_ref, k_ref, v_ref, u_ref):
    p = _dot(x_ref[0].astype(BF16), w_ref[...])
    for h in range(SB_HEADS):
        c = h * HEAD_DIM
        q_ref[0, h] = (p[:, c:c + HEAD_DIM] * ATTN_SCALE).astype(BF16)
        k_ref[0, h] = p[:, SB_WIDTH + c:SB_WIDTH + c + HEAD_DIM].astype(BF16)
        v_ref[0, h] = p[:, 2 * SB_WIDTH + c:2 * SB_WIDTH + c + HEAD_DIM].astype(BF16)
    u_ref[0] = p[:, 3 * SB_WIDTH:]


def _inproj0(x, w_bf16, tl=512):
    B, L, D = x.shape
    tl = min(tl, L)
    nout = w_bf16.shape[1]
    head = jax.ShapeDtypeStruct((B, SB_HEADS, L, HEAD_DIM), BF16)
    head_spec = pl.BlockSpec((1, SB_HEADS, tl, HEAD_DIM), lambda b, i: (b, 0, i, 0))
    return pl.pallas_call(
        _inproj0_body,
        grid=(B, L // tl),
        in_specs=[pl.BlockSpec((1, tl, D), lambda b, i: (b, i, 0)),
                  pl.BlockSpec((D, nout), lambda b, i: (0, 0))],
        out_specs=[head_spec, head_spec, head_spec,
                   pl.BlockSpec((1, tl, SSM_WIDTH), lambda b, i: (b, i, 0))],
        out_shape=[head, head, head, jax.ShapeDtypeStruct((B, L, SSM_WIDTH), F32)],
        compiler_params=_params("parallel", "parallel"),
        name="inproj0",
    )(x, w_bf16)


SB_BLOCK = 256
SB_HEADS_PER_STEP = 4
EXP_UNDERFLOW = -104.0


def _sb_attn_body(q_ref, k_ref, v_ref, o_ref, acc_scr, cs_scr):
    i = pl.program_id(2)
    t = q_ref.shape[2]
    nh = q_ref.shape[1]
    row = lax.broadcasted_iota(jnp.int32, (t, t), 0)
    col = lax.broadcasted_iota(jnp.int32, (t, t), 1)
    suffix = jnp.where(row > col, 1.0, 0.0).astype(BF16)
    below = col < row

    def block(hh, k0, diagonal):
        q = q_ref[0, hh]
        kb = k_ref[0, hh, pl.ds(k0, t), :]
        vb = v_ref[0, hh, pl.ds(k0, t), :]
        z = _dot_nt(q, kb)
        sp = jnp.maximum(z, 0.0) + jnp.log(1.0 + jnp.exp(-jnp.abs(z)))
        l1 = jnp.where(below, -sp, 0.0) if diagonal else -sp
        hi, lo = _split_bf16(l1)
        within = _dot(hi, suffix) + _dot(lo, suffix)
        if diagonal:
            w = jnp.where(below, jnp.exp(z - sp + within), 0.0)
            acc_scr[hh] = _dot(w.astype(BF16), vb)
            cs = jnp.sum(l1, axis=1, keepdims=True)
        else:
            w = jnp.exp(z - sp + within + cs_scr[hh])
            acc_scr[hh] += _dot(w.astype(BF16), vb)
            cs = cs_scr[hh] + jnp.sum(l1, axis=1, keepdims=True)
        cs_scr[hh] = cs
        return cs

    def alive(css):
        return (jnp.max(functools.reduce(jnp.maximum, css)) > EXP_UNDERFLOW).astype(jnp.int32)

    first = alive([block(hh, pl.multiple_of(i * t, t), True) for hh in range(nh)])

    def cond(c):
        return jnp.logical_and(c[0] <= i, c[1] > 0)

    def body(c):
        k0 = pl.multiple_of((i - c[0]) * t, t)
        return c[0] + 1, alive([block(hh, k0, False) for hh in range(nh)])

    lax.while_loop(cond, body, (jnp.int32(1), first))
    o_ref[0] = jnp.concatenate([acc_scr[hh] for hh in range(nh)], axis=-1).astype(o_ref.dtype)


def _sb_attention(q, k, v):
    B, H, L, d = q.shape
    t = min(SB_BLOCK, L)
    nh = SB_HEADS_PER_STEP
    return pl.pallas_call(
        _sb_attn_body,
        grid=(B, H // nh, L // t),
        in_specs=[pl.BlockSpec((1, nh, t, d), lambda b, h, i: (b, h, i, 0)),
                  pl.BlockSpec((1, nh, L, d), lambda b, h, i: (b, h, 0, 0)),
                  pl.BlockSpec((1, nh, L, d), lambda b, h, i: (b, h, 0, 0))],
        out_specs=pl.BlockSpec((1, t, nh * d), lambda b, h, i: (b, i, h)),
        out_shape=jax.ShapeDtypeStruct((B, L, H * d), BF16),
        scratch_shapes=[pltpu.VMEM((nh, t, d), F32), pltpu.VMEM((nh, t, 1), F32)],
        compiler_params=_params("parallel", "parallel", "arbitrary"),
        name="sb_attention",
    )(q, k, v)


def _ssm_tables(lam_re, lam_im, log_dt, b_re, b_im, c_re, c_im, d_skip):
    T, G, P, H = SSM_CHUNK, SSM_GROUPS, SSM_STATE, SSM_GROUP
    hp = lax.Precision.HIGHEST
    dt = jnp.exp(log_dt.astype(F32))[:, None]
    lr = lam_re.astype(F32)
    li = lam_im.astype(F32)
    mag = jnp.exp(lr * dt)
    a_re = mag * jnp.cos(li * dt)
    a_im = mag * jnp.sin(li * dt)
    den = lr * lr + li * li
    nr = a_re - 1.0
    f_re = (nr * lr + a_im * li) / den
    f_im = (a_im * lr - nr * li) / den
    br = b_re.astype(F32)
    bi = b_im.astype(F32)
    bb_re = f_re[..., None] * br - f_im[..., None] * bi
    bb_im = f_re[..., None] * bi + f_im[..., None] * br
    tau = jnp.arange(T + 1, dtype=F32)[None, :, None]
    pmag = jnp.exp(tau * (lr * dt)[:, None, :])
    pw_re = pmag * jnp.cos(tau * (li * dt)[:, None, :])
    pw_im = pmag * jnp.sin(tau * (li * dt)[:, None, :])
    cr = c_re.astype(F32)[:, None]
    ci = c_im.astype(F32)[:, None]
    cpw_re = cr * pw_re[:, :, None, :] - ci * pw_im[:, :, None, :]
    cpw_im = cr * pw_im[:, :, None, :] + ci * pw_re[:, :, None, :]
    kern = (jnp.einsum('gthp,gpi->gthi', cpw_re, bb_re, precision=hp)
            - jnp.einsum('gthp,gpi->gthi', cpw_im, bb_im, precision=hp))
    s_idx = np.arange(T)[:, None]
    t_idx = np.arange(T)[None, :]
    lag = np.clip(t_idx - s_idx, 0, T)
    toe = kern[:, lag]
    toe = jnp.where(jnp.asarray(t_idx >= s_idx)[None, :, :, None, None], toe, 0.0)
    intra = toe.transpose(0, 1, 4, 2, 3).reshape(G, T * H, T * H)
    rev = np.arange(T - 1, -1, -1)
    qr = pw_re[:, rev][:, :, None, :]
    qi = pw_im[:, rev][:, :, None, :]
    bbr = bb_re.transpose(0, 2, 1)[:, None]
    bbi = bb_im.transpose(0, 2, 1)[:, None]
    st_re = (qr * bbr - qi * bbi).reshape(G, T * H, P)
    st_im = (qr * bbi + qi * bbr).reshape(G, T * H, P)
    to_st = jnp.concatenate([st_re, st_im, st_im, st_re], axis=-1)
    top = cpw_re[:, 1:].transpose(0, 3, 1, 2).reshape(G, P, T * H)
    bot = (-cpw_im[:, 1:]).transpose(0, 3, 1, 2).reshape(G, P, T * H)
    from_st = jnp.concatenate([top, bot], axis=1)
    at_re = pw_re[:, T]
    at_im = pw_im[:, T]
    zeros = jnp.zeros_like(at_re)
    adv = jnp.stack([jnp.concatenate([at_re, at_re], -1),
                     jnp.concatenate([-at_im, at_im], -1),
                     jnp.concatenate([at_im, -at_im], -1)]
                    + [jnp.concatenate([zeros, zeros], -1)] * 5, axis=1)
    dvec = jnp.tile(d_skip.astype(F32).reshape(G, 1, H), (1, 1, T))
    return intra.astype(BF16), to_st.astype(BF16), from_st.astype(BF16), adv, dvec


def _ssm_body(u_ref, intra_ref, tost_ref, fromst_ref, adv_ref, dvec_ref, y_ref, s_scr, xin_scr, *, bsz, chunks):
    P2 = 2 * SSM_STATE
    u = u_ref[0]
    ub = u.astype(BF16)
    s_scr[...] = _dot(ub, tost_ref[0])
    a1 = adv_ref[0, 0:1, :]
    a2 = adv_ref[0, 1:2, :]
    a3 = adv_ref[0, 2:3, :]

    def step(c, carry):
        x1, x2 = carry
        r0 = pl.multiple_of(c * bsz, bsz)
        xin_scr[pl.ds(r0, bsz), :] = x1
        s = s_scr[pl.ds(r0, bsz), :]
        return (a1 * x1 + a2 * x2 + s[:, :P2], a1 * x2 + a3 * x1 + s[:, P2:])

    zero = jnp.zeros((bsz, P2), F32)
    lax.fori_loop(0, chunks, step, (zero, zero))
    y_ref[0] = (_dot(ub, intra_ref[0]) + _dot(xin_scr[...].astype(BF16), fromst_ref[0]) + dvec_ref[0] * u)


def _ssm(u, tables):
    intra, to_st, from_st, adv, dvec = tables
    B, L, W = u.shape
    T, G, H, P = SSM_CHUNK, SSM_GROUPS, SSM_GROUP, SSM_STATE
    C = L // T
    ut = u.reshape(B, C, T, G, H).transpose(3, 1, 0, 2, 4).reshape(G, C * B, T * H)
    yt = pl.pallas_call(
        functools.partial(_ssm_body, bsz=B, chunks=C),
        grid=(G,),
        in_specs=[pl.BlockSpec((1, C * B, T * H), lambda g: (g, 0, 0)),
                  pl.BlockSpec((1, T * H, T * H), lambda g: (g, 0, 0)),
                  pl.BlockSpec((1, T * H, 4 * P), lambda g: (g, 0, 0)),
                  pl.BlockSpec((1, 2 * P, T * H), lambda g: (g, 0, 0)),
                  pl.BlockSpec((1, 8, 2 * P), lambda g: (g, 0, 0)),
                  pl.BlockSpec((1, 1, T * H), lambda g: (g, 0, 0))],
        out_specs=pl.BlockSpec((1, C * B, T * H), lambda g: (g, 0, 0)),
        out_shape=jax.ShapeDtypeStruct((G, C * B, T * H), F32),
        scratch_shapes=[pltpu.VMEM((C * B, 4 * P), F32), pltpu.VMEM((C * B, 2 * P), F32)],
        compiler_params=_params("parallel"),
        name="s5_scan",
    )(ut, intra, to_st, from_st, adv, dvec)
    return yt.reshape(G, C, B, T, H).transpose(2, 1, 3, 0, 4).reshape(B, L, W)


def _outproj0_body(x_ref, oa_ref, y_ref, wglu_ref, woa_ref, wob_ref, g_ref, b_ref, o_ref):
    h = _gelu(y_ref[...])
    ob = h * _sigmoid(_dot(h.astype(BF16), wglu_ref[...]))
    m = _dot(oa_ref[...], woa_ref[...]) + _dot(ob.astype(BF16), wob_ref[...])
    o_ref[...] = _layer_norm(ALPHA * x_ref[...] + m, g_ref[...], b_ref[...])


def _outproj0(x, oa, y, wglu, woa, wob, g, b, tm=512):
    N, D = x.shape
    tm = min(tm, N)
    row = lambda i: (i, 0)
    fix = lambda i: (0, 0)
    return pl.pallas_call(
        _outproj0_body,
        grid=(N // tm,),
        in_specs=[pl.BlockSpec((tm, D), row), pl.BlockSpec((tm, SB_WIDTH), row), pl.BlockSpec((tm, SSM_WIDTH), row),
                  pl.BlockSpec((SSM_WIDTH, SSM_WIDTH), fix), pl.BlockSpec((SB_WIDTH, D), fix),
                  pl.BlockSpec((SSM_WIDTH, D), fix), pl.BlockSpec((1, D), fix), pl.BlockSpec((1, D), fix)],
        out_specs=pl.BlockSpec((tm, D), row),
        out_shape=jax.ShapeDtypeStruct((N, D), F32),
        compiler_params=_params("parallel"),
        name="outproj0",
    )(x, oa, y, wglu, woa, wob, g, b)


def _outproj1_body(x_ref, o_ref_in, w_ref, g_ref, b_ref, o_ref):
    m = _dot(o_ref_in[...].astype(BF16), w_ref[...])
    o_ref[...] = _layer_norm(ALPHA * x_ref[...] + m, g_ref[...], b_ref[...])


def _outproj1(x, o, w, g, b, tm=512):
    N, D = x.shape
    tm = min(tm, N)
    row = lambda i: (i, 0)
    fix = lambda i: (0, 0)
    return pl.pallas_call(
        _outproj1_body,
        grid=(N // tm,),
        in_specs=[pl.BlockSpec((tm, D), row), pl.BlockSpec((tm, o.shape[1]), row),
                  pl.BlockSpec(w.shape, fix), pl.BlockSpec((1, D), fix), pl.BlockSpec((1, D), fix)],
        out_specs=pl.BlockSpec((tm, D), row),
        out_shape=jax.ShapeDtypeStruct((N, D), F32),
        compiler_params=_params("parallel"),
        name="outproj1",
    )(x, o, w, g, b)


def _router_body(x_ref, wh_ref, wl_ref, b_ref, g_ref, grp_ref):
    xh, xl = _split_bf16(x_ref[...])
    wh = wh_ref[...]
    logits = _dot_nt(wh, xh) + _dot_nt(wh, xl) + _dot_nt(wl_ref[...], xh)
    aff = _sigmoid(logits)
    sel = aff + b_ref[...]
    E, K = EXPERTS_PER_GROUP, N_GROUPS
    s = [sel[e:e + 1, :] for e in range(N_EXPERTS)]
    a = [aff[e:e + 1, :] for e in range(N_EXPERTS)]
    gscore = []
    for k in range(K):
        v = s[k * E:(k + 1) * E]
        best = None
        for x in range(E):
            for y in range(x + 1, E):
                pair = v[x] + v[y]
                best = pair if best is None else jnp.maximum(best, pair)
        gscore.append(best)
    top = functools.reduce(jnp.maximum, gscore)
    is_g = []
    taken = None
    for k in range(K):
        hit = gscore[k] == top
        if taken is None:
            is_g.append(hit)
            taken = hit
        else:
            is_g.append(jnp.logical_and(hit, jnp.logical_not(taken)))
            taken = jnp.logical_or(taken, hit)

    def pick(rows, j):
        out = rows[(K - 1) * E + j]
        for k in range(K - 2, -1, -1):
            out = jnp.where(is_g[k], rows[k * E + j], out)
        return out

    v = [pick(s, j) for j in range(E)]
    av = [pick(a, j) for j in range(E)]
    chosen = []
    for j in range(E):
        r = jnp.zeros_like(v[j])
        for j2 in range(E):
            if j2 == j:
                continue
            ahead = (v[j2] >= v[j]) if j2 < j else (v[j2] > v[j])
            r = r + jnp.where(ahead, 1.0, 0.0)
        chosen.append(r < 2.0)
    wj = [jnp.where(chosen[j], av[j], 0.0) for j in range(E)]
    tot = wj[0] + wj[1] + wj[2] + wj[3]
    gj = [w / tot for w in wj]
    rows = [jnp.where(is_g[e // E], gj[e % E], 0.0) for e in range(N_EXPERTS)]
    g_ref[...] = jnp.concatenate(rows, axis=0)
    grp_ref[...] = jnp.concatenate([jnp.where(is_g[k], 1.0, 0.0) for k in range(K)], axis=0)


def _router(x, w_router, b_router, tm=512):
    N, D = x.shape
    tm = min(tm, N)
    wt = w_router.astype(F32).T
    wh, wl = _split_bf16(wt)
    fix = lambda i: (0, 0)
    return pl.pallas_call(
        _router_body,
        grid=(N // tm,),
        in_specs=[pl.BlockSpec((tm, D), lambda i: (i, 0)), pl.BlockSpec((N_EXPERTS, D), fix),
                  pl.BlockSpec((N_EXPERTS, D), fix), pl.BlockSpec((N_EXPERTS, 1), fix)],
        out_specs=[pl.BlockSpec((N_EXPERTS, tm), lambda i: (0, i)), pl.BlockSpec((N_GROUPS, tm), lambda i: (0, i))],
        out_shape=[jax.ShapeDtypeStruct((N_EXPERTS, N), F32), jax.ShapeDtypeStruct((N_GROUPS, N), F32)],
        compiler_params=_params("parallel"),
        name="router",
    )(x, wh, wl, b_router.astype(F32).reshape(N_EXPERTS, 1))


MOE_TILE = 1024
MOE_PASS_ROWS = 304
MOE_CHUNK = 256
ROW_ALIGN = 16


def _moe_body(seg_ref, x_ref, gt_ref, grp_ref, w1_ref, w3_ref, w2_ref, g_ref, b_ref, o_ref,
              xs_scr, ys_scr, pt_scr, gs_scr):
    i = pl.program_id(0)
    e = pl.program_id(1)
    T = x_ref.shape[0]
    K = N_GROUPS
    cap = MOE_PASS_ROWS
    ch = min(MOE_CHUNK, T)

    @pl.when(e == 0)
    def _():
        grp = grp_ref[...]
        grp_b = grp.astype(BF16)
        xb = x_ref[...].astype(BF16)
        gates = gt_ref[...]
        gh, gm = _split_bf16(gates)
        gl = (gates - gh.astype(F32) - gm.astype(F32)).astype(BF16)
        cnt = jnp.sum(grp, axis=1, keepdims=True)
        off = [jnp.zeros((1, 1), F32)]
        for k in range(K - 1):
            off.append(off[-1] + cnt[k:k + 1])
        r_i = lax.broadcasted_iota(jnp.int32, (ch, T), 0)
        c_i = lax.broadcasted_iota(jnp.int32, (ch, T), 1)
        pos_row = []
        pos_col = []
        for c in range(T // ch):
            c0 = c * ch
            earlier = jnp.where(c_i < r_i + c0, 1.0, 0.0).astype(BF16)
            same = jnp.where(c_i == r_i + c0, 1.0, 0.0).astype(BF16)
            before_c = _dot_nt(earlier, grp_b)
            member_c = _dot_nt(same, grp_b)
            before_r = _dot_nt(grp_b, earlier)
            pc = jnp.zeros((ch, 1), F32)
            pr = jnp.zeros((1, ch), F32)
            for k in range(K):
                pc = pc + member_c[:, k:k + 1] * (off[k] + before_c[:, k:k + 1])
                pr = pr + grp[k:k + 1, c0:c0 + ch] * (off[k] + before_r[k:k + 1, :])
            pos_col.append(pc)
            pos_row.append(pr)
        pos_row = jnp.concatenate(pos_row, axis=1)
        for c in range(T // ch):
            c0 = c * ch
            perm = jnp.where((r_i + c0).astype(F32) == pos_row, 1.0, 0.0).astype(BF16)
            xs_scr[c0:c0 + ch, :] = _dot(perm, xb).astype(BF16)
            gs_scr[c0:c0 + ch, :] = _dot_nt(perm, gh) + _dot_nt(perm, gm) + _dot_nt(perm, gl)
            pt_scr[c0:c0 + ch, :] = jnp.where(c_i.astype(F32) == pos_col[c], 1.0, 0.0).astype(BF16)
        xs_scr[T:, :] = jnp.zeros((cap, xs_scr.shape[1]), BF16)
        gs_scr[T:, :] = jnp.zeros((cap, gs_scr.shape[1]), F32)
        ys_scr[...] = jnp.zeros_like(ys_scr)

    k = e // EXPERTS_PER_GROUP
    off = seg_ref[i, k]
    cnt = seg_ref[i, K + k]
    start0 = (off // ROW_ALIGN) * ROW_ALIGN
    n_pass = (off + cnt - start0 + cap - 1) // cap

    def one_pass(n, c):
        start = pl.multiple_of(start0 + n * cap, ROW_ALIGN)
        xc = xs_scr[pl.ds(start, cap), :]
        a = _dot(xc, w1_ref[0])
        hm = (a * _sigmoid(a)) * _dot(xc, w3_ref[0])
        gsc = gs_scr[pl.ds(start, cap), :]
        lane = lax.broadcasted_iota(jnp.int32, gsc.shape, 1)
        gcol = jnp.sum(jnp.where(lane == e, gsc, 0.0), axis=1, keepdims=True)
        ys_scr[pl.ds(start, cap), :] += gcol * _dot(hm.astype(BF16), w2_ref[0])
        return c

    lax.fori_loop(0, n_pass, one_pass, 0)

    @pl.when(e == pl.num_programs(1) - 1)
    def _():
        y = _dot(pt_scr[...], ys_scr[0:T, :].astype(BF16))
        o_ref[...] = _layer_norm(ALPHA * x_ref[...] + y, g_ref[...], b_ref[...])


def _moe(x, gates_t, grp_t, w1, w3, w2, g, b):
    N, D = x.shape
    T = min(MOE_TILE, N)
    K = N_GROUPS
    cnt = grp_t.reshape(K, N // T, T).sum(-1).astype(jnp.int32).T
    seg = jnp.concatenate([jnp.cumsum(cnt, axis=1) - cnt, cnt], axis=1)
    fix = lambda i, e, s: (0, 0)
    rows = T + MOE_PASS_ROWS
    grid_spec = pltpu.PrefetchScalarGridSpec(
        num_scalar_prefetch=1,
        grid=(N // T, N_EXPERTS),
        in_specs=[pl.BlockSpec((T, D), lambda i, e, s: (i, 0)),
                  pl.BlockSpec((N_EXPERTS, T), lambda i, e, s: (0, i)),
                  pl.BlockSpec((K, T), lambda i, e, s: (0, i)),
                  pl.BlockSpec((1, D, D_EXPERT), lambda i, e, s: (e, 0, 0)),
                  pl.BlockSpec((1, D, D_EXPERT), lambda i, e, s: (e, 0, 0)),
                  pl.BlockSpec((1, D_EXPERT, D), lambda i, e, s: (e, 0, 0)),
                  pl.BlockSpec((1, D), fix), pl.BlockSpec((1, D), fix)],
        out_specs=pl.BlockSpec((T, D), lambda i, e, s: (i, 0)),
        scratch_shapes=[pltpu.VMEM((rows, D), BF16), pltpu.VMEM((rows, D), F32),
                        pltpu.VMEM((T, T), BF16), pltpu.VMEM((rows, N_EXPERTS), F32)])
    return pl.pallas_call(
        _moe_body,
        grid_spec=grid_spec,
        out_shape=jax.ShapeDtypeStruct((N, D), F32),
        compiler_params=_params("parallel", "arbitrary"),
        name="moe",
    )(seg, x, gates_t, grp_t, w1, w3, w2, g, b)


NSA_KVW = NSA_KV_HEADS * HEAD_DIM
NSA_ROPE_W = NSA_HEADS * HEAD_DIM + 2 * NSA_KVW
NSA_Q_SCALE = ATTN_SCALE * math.log2(math.e)


def _rot_cols(w):
    K, n = w.shape
    w3 = w.reshape(K, n // HEAD_DIM, 2, HEAD_DIM // 2)
    return jnp.stack([-w3[:, :, 1], w3[:, :, 0]], axis=2).reshape(K, n)


def _inproj1_weights(w_in):
    H, G, d = NSA_HEADS, NSA_KV_HEADS, HEAD_DIM
    cuts = [H * d + i * NSA_KVW for i in range(7)]
    q, kc, vc, ks, vs, kw, vw, gate = jnp.split(w_in.astype(F32), cuts, axis=1)
    rope = jnp.concatenate([q, ks, kw], axis=1)
    gate = jnp.pad(gate, ((0, 0), (0, 128 - gate.shape[1])))
    w_all = jnp.concatenate([rope, _rot_cols(rope), kc, vc, gate], axis=1).astype(BF16)
    w_vt = jnp.concatenate([vs, vw], axis=1).T.astype(BF16)
    return w_all, w_vt


def _rope_tables(pos):
    inv = ROPE_THETA ** (-jnp.arange(0, HEAD_DIM, 2, dtype=F32) / HEAD_DIM)
    ang = pos.astype(F32)[:, None] * inv[None, :]
    c, s = jnp.cos(ang), jnp.sin(ang)
    return jnp.concatenate([c, c], axis=-1), jnp.concatenate([s, s], axis=-1)


V_ROWS = HEAD_DIM + 16


def _inproj1_body(x_ref, w_ref, wvt_ref, cos_ref, sin_ref, q_ref, ks_ref, kw_ref, kc_ref, vc_ref, vst_ref, vwt_ref,
                  gate_ref):
    xb = x_ref[0].astype(BF16)
    p = _dot(xb, w_ref[...])
    vt = _dot_nt(wvt_ref[...], xb)
    ones = jnp.ones((V_ROWS - HEAD_DIM, xb.shape[0]), BF16)
    for n, ref in enumerate((vst_ref, vwt_ref)):
        for g in range(NSA_KV_HEADS):
            r0 = n * NSA_KVW + g * HEAD_DIM
            ref[0, g, 0:HEAD_DIM, :] = vt[r0:r0 + HEAD_DIM, :].astype(BF16)
            ref[0, g, HEAD_DIM:V_ROWS, :] = ones
    R = NSA_ROPE_W
    cos = cos_ref[...]
    sin = sin_ref[...]
    roped = []
    for c in range(R // 128):
        roped.append(p[:, c * 128:(c + 1) * 128] * cos + p[:, R + c * 128:R + (c + 1) * 128] * sin)
    d = HEAD_DIM

    def head(chunks, h):
        blk = chunks[h // 2]
        return blk[:, (h % 2) * d:(h % 2 + 1) * d]

    for h in range(NSA_HEADS):
        q_ref[0, h] = (head(roped, h) * NSA_Q_SCALE).astype(BF16)
    nq = NSA_HEADS // 2
    for g in range(NSA_KV_HEADS):
        ks_ref[0, g] = head(roped[nq:], g).astype(BF16)
        kw_ref[0, g] = head(roped[nq + 2:], g).astype(BF16)
    base = 2 * R
    for n, ref in enumerate((kc_ref, vc_ref)):
        for g in range(NSA_KV_HEADS):
            c0 = base + n * NSA_KVW + g * d
            ref[0, g] = p[:, c0:c0 + d].astype(BF16)
    gate_ref[0] = p[:, base + 2 * NSA_KVW:]


def _inproj1(x, w_all, w_vt, cos2, sin2, tl=256):
    B, L, D = x.shape
    tl = min(tl, L)
    nout = w_all.shape[1]
    G = NSA_KV_HEADS
    qh = jax.ShapeDtypeStruct((B, NSA_HEADS, L, HEAD_DIM), BF16)
    kvh = jax.ShapeDtypeStruct((B, G, L, HEAD_DIM), BF16)
    vth = jax.ShapeDtypeStruct((B, G, V_ROWS, L), BF16)
    q_spec = pl.BlockSpec((1, NSA_HEADS, tl, HEAD_DIM), lambda b, i: (b, 0, i, 0))
    kv_spec = pl.BlockSpec((1, G, tl, HEAD_DIM), lambda b, i: (b, 0, i, 0))
    vt_spec = pl.BlockSpec((1, G, V_ROWS, tl), lambda b, i: (b, 0, 0, i))
    return pl.pallas_call(
        _inproj1_body,
        grid=(B, L // tl),
        in_specs=[pl.BlockSpec((1, tl, D), lambda b, i: (b, i, 0)),
                  pl.BlockSpec((D, nout), lambda b, i: (0, 0)),
                  pl.BlockSpec(w_vt.shape, lambda b, i: (0, 0)),
                  pl.BlockSpec((tl, 128), lambda b, i: (i, 0)),
                  pl.BlockSpec((tl, 128), lambda b, i: (i, 0))],
        out_specs=[q_spec] + [kv_spec] * 4 + [vt_spec] * 2 + [pl.BlockSpec((1, tl, 128), lambda b, i: (b, i, 0))],
        out_shape=[qh] + [kvh] * 4 + [vth] * 2 + [jax.ShapeDtypeStruct((B, L, 128), F32)],
        compiler_params=_params("parallel", "parallel"),
        name="inproj1",
    )(x, w_all, w_vt, cos2, sin2)


def _compress_body(kc_ref, vc_ref, posk_ref, posv_ref, w1k_ref, w1v_ref, w2k_ref, w2kr_ref, w2v_ref,
                   cos_ref, sin_ref, kcmp_ref, vcmp_ref):
    def hidden(a_ref, pos_ref, w1_ref):
        a = a_ref[0, 0].astype(F32)
        nrow = a.shape[0]
        lo = _dot((a + pos_ref[0:1, :]).astype(BF16), w1_ref[0])
        hi = _dot((a + pos_ref[1:2, :]).astype(BF16), w1_ref[1])
        hi_next = pltpu.roll(hi, nrow - 1, 0)
        return _gelu(lo + hi_next).astype(BF16)

    hk = hidden(kc_ref, posk_ref, w1k_ref)
    kcmp = _dot(hk, w2k_ref[...]) * cos_ref[...] + _dot(hk, w2kr_ref[...]) * sin_ref[...]
    kcmp_ref[0, 0] = kcmp.astype(BF16)
    hv = hidden(vc_ref, posv_ref, w1v_ref)
    vcmp_ref[0, 0] = _dot_nt(w2v_ref[...], hv).astype(BF16)


def _compress(kc, vc, pos_k, w1_k, w2_k, pos_v, w1_v, w2_v):
    B, G, L, d = kc.shape
    half = CMP_STRIDE * d
    nb = L // CMP_STRIDE
    kc2 = kc.reshape(B, G, nb, half)
    vc2 = vc.reshape(B, G, nb, half)
    posk = pos_k.astype(F32).reshape(2, half)
    posv = pos_v.astype(F32).reshape(2, half)
    w1k = w1_k.astype(BF16).reshape(2, half, CMP_HIDDEN)
    w1v = w1_v.astype(BF16).reshape(2, half, CMP_HIDDEN)
    w2k = w2_k.astype(F32)
    cos, sin = _rope_tables(jnp.arange(nb) * CMP_STRIDE + CMP_LEN - 1)
    blk = pl.BlockSpec((1, 1, nb, half), lambda b, g: (b, g, 0, 0))
    out = pl.BlockSpec((1, 1, nb, d), lambda b, g: (b, g, 0, 0))
    out_t = pl.BlockSpec((1, 1, d, nb), lambda b, g: (b, g, 0, 0))
    fix2 = lambda b, g: (0, 0)
    fix3 = lambda b, g: (0, 0, 0)
    return pl.pallas_call(
        _compress_body,
        grid=(B, G),
        in_specs=[blk, blk, pl.BlockSpec((2, half), fix2), pl.BlockSpec((2, half), fix2),
                  pl.BlockSpec((2, half, CMP_HIDDEN), fix3), pl.BlockSpec((2, half, CMP_HIDDEN), fix3),
                  pl.BlockSpec((CMP_HIDDEN, d), fix2), pl.BlockSpec((CMP_HIDDEN, d), fix2),
                  pl.BlockSpec((d, CMP_HIDDEN), fix2), pl.BlockSpec((nb, d), fix2), pl.BlockSpec((nb, d), fix2)],
        out_specs=[out, out_t],
        out_shape=[jax.ShapeDtypeStruct((B, G, nb, d), BF16), jax.ShapeDtypeStruct((B, G, d, nb), BF16)],
        compiler_params=_params("parallel", "parallel"),
        name="compress",
    )(kc2, vc2, posk, posv, w1k, w1v, w2k.astype(BF16), _rot_cols(w2k).astype(BF16), w2_v.T.astype(BF16), cos, sin)


NSA_KBLOCK = 256
NSA_GROUPS_PER_STEP = 4


def _nsa_gate_layout(gate):
    B, L, _ = gate.shape
    G, R, tq = NSA_KV_HEADS, NSA_REP, min(ATT_BLOCK, L)
    g5 = gate[:, :, :3 * NSA_HEADS].reshape(B, L // tq, tq, G, R, 3)
    return g5.transpose(0, 3, 1, 5, 4, 2).reshape(B, G, L // tq, 3, R * tq)


def _nsa_body(q_ref, kcmp_ref, vcmpt_ref, ks_ref, vst_ref, kw_ref, vwt_ref, gate_ref, o_ref,
              m_scr, acc_scr, *, seq):
    i = pl.program_id(2)
    tq = q_ref.shape[2]
    gb = kcmp_ref.shape[1]
    tk = min(NSA_KBLOCK, seq)
    R, d = NSA_REP, HEAD_DIM
    nq = R * tq
    nb = seq // SEL_LEN
    mc = seq // CMP_STRIDE
    t_row = i * tq + lax.broadcasted_iota(jnp.int32, (1, tq), 1)
    t_all = jnp.concatenate([t_row] * R, axis=1)
    m_col = lax.broadcasted_iota(jnp.int32, (mc, 1), 0)
    m_row = lax.broadcasted_iota(jnp.int32, (1, mc), 1)
    n_col = lax.broadcasted_iota(jnp.int32, (nb, 1), 0)

    def q_rows(g):
        return q_ref[0, g * R:(g + 1) * R].reshape(nq, d)

    valid_c = (m_col * CMP_STRIDE + (CMP_LEN - 1)) <= t_all
    ovl = jnp.logical_and(m_row * CMP_STRIDE < (n_col + 1) * SEL_LEN,
                          m_row * CMP_STRIDE + CMP_LEN > n_col * SEL_LEN)
    ovl = jnp.where(ovl, 1.0, 0.0).astype(BF16)
    cur = t_row // SEL_LEN
    forced = jnp.logical_or(n_col == 0, jnp.logical_or(n_col == cur, n_col == cur - 1))
    bonus = jnp.where(forced, FORCE_BONUS, 0.0)
    valid_s = n_col * SEL_LEN <= t_row
    o_c, sel = [], []
    for g in range(gb):
        s = jnp.where(valid_c, _dot_nt(kcmp_ref[0, g], q_rows(g)), NEG)
        e = jnp.where(valid_c, jnp.exp2(s - jnp.max(s, axis=0, keepdims=True)), 0.0)
        den = jnp.sum(e, axis=0, keepdims=True)
        p = e / jnp.where(den > 0.0, den, 1.0)
        o_c.append(_dot(vcmpt_ref[0, g], p.astype(BF16)))
        psum = p[:, 0:tq]
        for r in range(1, R):
            psum = psum + p[:, r * tq:(r + 1) * tq]
        ph, plo = _split_bf16(psum)
        score = jnp.where(valid_s, _dot(ovl, ph) + _dot(ovl, plo) + bonus, NEG)
        rank = jnp.zeros((nb, tq), F32)
        for n2 in range(nb):
            other = score[n2:n2 + 1, :]
            tie = jnp.where(n_col > n2, 1.0, 0.0)
            rank = rank + jnp.where(other > score, 1.0, jnp.where(other == score, tie, 0.0))
        sel.append(jnp.where(rank < float(SEL_TOP), 1.0, 0.0).astype(BF16))

    k_col = lax.broadcasted_iota(jnp.int32, (tk, 1), 0)
    j_hi = (i * tq + tq - 1) // tk + 1

    def sweep(k_ref, vt_ref, j_lo, bias_fn):
        for g in range(gb):
            m_scr[g] = jnp.full((1, nq), NEG, F32)
            acc_scr[g] = jnp.zeros((V_ROWS, nq), F32)

        def body(j, c):
            k0 = pl.multiple_of(j * tk, tk)
            biases = bias_fn(k0 + k_col)
            for g in range(gb):
                kb = k_ref[0, g, pl.ds(k0, tk), :]
                vtb = vt_ref[0, g, :, pl.ds(k0, tk)]
                sc = _dot_nt(kb, q_rows(g)) + jnp.concatenate([biases[g]] * R, axis=1)
                m_run = m_scr[g]
                m_new = jnp.maximum(m_run, jnp.max(sc, axis=0, keepdims=True))
                pr = jnp.exp2(sc - m_new)
                acc_scr[g] = jnp.exp2(m_run - m_new) * acc_scr[g] + _dot(vtb, pr.astype(BF16))
                m_scr[g] = m_new
            return c

        lax.fori_loop(j_lo, j_hi, body, 0)
        outs = []
        for g in range(gb):
            acc = acc_scr[g]
            outs.append(acc[0:d] / acc[d:d + 1])
        return outs

    def sel_bias(kpos):
        pick = jnp.where(kpos // SEL_LEN == lax.broadcasted_iota(jnp.int32, (1, nb), 1), 1.0, 0.0).astype(BF16)
        causal = kpos <= t_row
        return [jnp.where(jnp.logical_and(_dot(pick, sel[g]) > 0.5, causal), 0.0, NEG) for g in range(gb)]

    o_s = sweep(ks_ref, vst_ref, 0, sel_bias)

    def win_bias(kpos):
        return [jnp.where(jnp.logical_and(kpos <= t_row, kpos > t_row - WINDOW), 0.0, NEG)] * gb

    o_w = sweep(kw_ref, vwt_ref, jnp.maximum(i * tq - (WINDOW - 1), 0) // tk, win_bias)

    outs = []
    for g in range(gb):
        sg = _sigmoid(gate_ref[0, g, 0])
        o_t = sg[0:1] * o_c[g] + sg[1:2] * o_s[g] + sg[2:3] * o_w[g]
        for r in range(R):
            outs.append(o_t[:, r * tq:(r + 1) * tq].T)
    o_ref[0] = jnp.concatenate(outs, axis=-1)


def _nsa_attention(q, kcmp, vcmpt, ks, vst, kw, vwt, gate):
    B, H, L, d = q.shape
    G, R = NSA_KV_HEADS, NSA_REP
    gb = NSA_GROUPS_PER_STEP
    tq = min(ATT_BLOCK, L)
    mc = L // CMP_STRIDE
    k_spec = pl.BlockSpec((1, gb, L, d), lambda b, g, i: (b, g, 0, 0))
    vt_spec = pl.BlockSpec((1, gb, V_ROWS, L), lambda b, g, i: (b, g, 0, 0))
    return pl.pallas_call(
        functools.partial(_nsa_body, seq=L),
        grid=(B, G // gb, L // tq),
        in_specs=[pl.BlockSpec((1, gb * R, tq, d), lambda b, g, i: (b, g, i, 0)),
                  pl.BlockSpec((1, gb, mc, d), lambda b, g, i: (b, g, 0, 0)),
                  pl.BlockSpec((1, gb, d, mc), lambda b, g, i: (b, g, 0, 0)),
                  k_spec, vt_spec, k_spec, vt_spec,
                  pl.BlockSpec((1, gb, 1, 3, R * tq), lambda b, g, i: (b, g, i, 0, 0))],
        out_specs=pl.BlockSpec((1, tq, gb * R * d), lambda b, g, i: (b, i, g)),
        out_shape=jax.ShapeDtypeStruct((B, L, H * d), F32),
        scratch_shapes=[pltpu.VMEM((gb, 1, R * tq), F32), pltpu.VMEM((gb, V_ROWS, R * tq), F32)],
        compiler_params=_params("parallel", "parallel", "arbitrary"),
        name="nsa_attention",
    )(q, kcmp, vcmpt, ks, vst, kw, vwt, gate)


def kernel(x, w_in_0, ssm_lam_re, ssm_lam_im, ssm_log_dt, ssm_b_re, ssm_b_im, ssm_c_re, ssm_c_im, ssm_d, w_glu, w_out_0, ln_mix_g_0, ln_mix_b_0, ln_ffn_g_0, ln_ffn_b_0, w1_0, w3_0, w2_0, w_in_1, cmp_pos_k, cmp_w1_k, cmp_w2_k, cmp_pos_v, cmp_w1_v, cmp_w2_v, w_out_1, ln_mix_g_1, ln_mix_b_1, ln_ffn_g_1, ln_ffn_b_1, w1_1, w3_1, w2_1, w_router, b_router):
    B, L, D = x.shape
    N = B * L
    vec = lambda a: a.astype(F32).reshape(1, D)

    def ffn(h, w1, w3, w2, g, b):
        gates_t, grp_t = _router(h, w_router, b_router)
        return _moe(h, gates_t, grp_t, w1.astype(BF16), w3.astype(BF16), w2.astype(BF16), vec(g), vec(b))

    q, k, v, u = _inproj0(x, w_in_0.astype(BF16))
    o_a = _sb_attention(q, k, v).reshape(N, SB_WIDTH)
    y = _ssm(u, _ssm_tables(ssm_lam_re, ssm_lam_im, ssm_log_dt, ssm_b_re, ssm_b_im, ssm_c_re, ssm_c_im, ssm_d))
    w_out_0b = w_out_0.astype(BF16)
    h = _outproj0(x.reshape(N, D), o_a, y.reshape(N, SSM_WIDTH), w_glu.astype(BF16),
                  w_out_0b[:SB_WIDTH], w_out_0b[SB_WIDTH:], vec(ln_mix_g_0), vec(ln_mix_b_0))
    h = ffn(h, w1_0, w3_0, w2_0, ln_ffn_g_0, ln_ffn_b_0)

    cos, sin = _rope_tables(jnp.arange(L))
    cos2 = jnp.concatenate([cos, cos], axis=-1)
    sin2 = jnp.concatenate([sin, sin], axis=-1)
    q, ks, kw, kc, vc, vst, vwt, gate = _inproj1(h.reshape(B, L, D), *_inproj1_weights(w_in_1), cos2, sin2)
    kcmp, vcmpt = _compress(kc, vc, cmp_pos_k, cmp_w1_k, cmp_w2_k, cmp_pos_v, cmp_w1_v, cmp_w2_v)
    o = _nsa_attention(q, kcmp, vcmpt, ks, vst, kw, vwt, _nsa_gate_layout(gate)).reshape(N, NSA_HEADS * HEAD_DIM)
    h = _outproj1(h, o, w_out_1.astype(BF16), vec(ln_mix_g_1), vec(ln_mix_b_1))
    h = ffn(h, w1_1, w3_1, w2_1, ln_ffn_g_1, ln_ffn_b_1)
    return h.reshape(B, L, D)
```

```python
import functools
import math

import numpy as np
import jax
import jax.numpy as jnp
from jax import lax
from jax.experimental import pallas as pl
from jax.experimental.pallas import tpu as pltpu

F32 = jnp.float32
BF16 = jnp.bfloat16

D_MODEL = 1024
DEPTH = 2
SB_HEADS = 8
HEAD_DIM = 64
SB_WIDTH = SB_HEADS * HEAD_DIM
SSM_WIDTH = D_MODEL - SB_WIDTH
SSM_GROUP = 16
SSM_GROUPS = SSM_WIDTH // SSM_GROUP
SSM_STATE = 64
SSM_CHUNK = 16
NSA_HEADS = 16
NSA_KV_HEADS = 4
NSA_REP = NSA_HEADS // NSA_KV_HEADS
CMP_LEN = 32
CMP_STRIDE = 16
CMP_HIDDEN = 256
SEL_LEN = 64
SEL_TOP = 8
WINDOW = 512
ROPE_THETA = 10000.0
FORCE_BONUS = 1e4
NEG = -1e30
N_EXPERTS = 16
N_GROUPS = 4
EXPERTS_PER_GROUP = N_EXPERTS // N_GROUPS
D_EXPERT = 512
ALPHA = (2 * DEPTH) ** 0.25
LN_EPS = 1e-5
ATTN_SCALE = HEAD_DIM ** -0.5
ATT_BLOCK = 128
GELU_C = math.sqrt(2.0 / math.pi)


def _params(*sem):
    return pltpu.CompilerParams(dimension_semantics=sem, vmem_limit_bytes=56 * 1024 * 1024)


def _sigmoid(x):
    return 1.0 / (1.0 + jnp.exp(-x))


def _gelu(x):
    return 0.5 * x * (1.0 + jnp.tanh(GELU_C * (x + 0.044715 * (x * x * x))))


def _layer_norm(r, g, b):
    mu = jnp.mean(r, axis=-1, keepdims=True)
    d = r - mu
    var = jnp.mean(d * d, axis=-1, keepdims=True)
    return d * lax.rsqrt(var + LN_EPS) * g + b


def _dot(a, b):
    return jnp.dot(a, b, preferred_element_type=F32)


def _dot_nt(a, b):
    return lax.dot_general(a, b, (((1,), (1,)), ((), ())), preferred_element_type=F32)


def _split_bf16(x):
    hi = x.astype(BF16)
    lo = (x - hi.astype(F32)).astype(BF16)
    return hi, lo


def _inproj0_body(x_ref, w_ref, q_ref, k_ref, v_ref, u_ref):
    p = _dot(x_ref[0].astype(BF16), w_ref[...])
    for h in range(SB_HEADS):
        c = h * HEAD_DIM
        q_ref[0, h] = (p[:, c:c + HEAD_DIM] * ATTN_SCALE).astype(BF16)
        k_ref[0, h] = p[:, SB_WIDTH + c:SB_WIDTH + c + HEAD_DIM].astype(BF16)
        v_ref[0, h] = p[:, 2 * SB_WIDTH + c:2 * SB_WIDTH + c + HEAD_DIM].astype(BF16)
    u_ref[0] = p[:, 3 * SB_WIDTH:]


def _inproj0(x, w_bf16, tl=512):
    B, L, D = x.shape
    tl = min(tl, L)
    nout = w_bf16.shape[1]
    head = jax.ShapeDtypeStruct((B, SB_HEADS, L, HEAD_DIM), BF16)
    head_spec = pl.BlockSpec((1, SB_HEADS, tl, HEAD_DIM), lambda b, i: (b, 0, i, 0))
    return pl.pallas_call(
        _inproj0_body,
        grid=(B, L // tl),
        in_specs=[pl.BlockSpec((1, tl, D), lambda b, i: (b, i, 0)),
                  pl.BlockSpec((D, nout), lambda b, i: (0, 0))],
        out_specs=[head_spec, head_spec, head_spec,
                   pl.BlockSpec((1, tl, SSM_WIDTH), lambda b, i: (b, i, 0))],
        out_shape=[head, head, head, jax.ShapeDtypeStruct((B, L, SSM_WIDTH), F32)],
        compiler_params=_params("parallel", "parallel"),
        name="inproj0",
    )(x, w_bf16)


SB_BLOCK = 256
SB_HEADS_PER_STEP = 4
EXP_UNDERFLOW = -104.0


def _sb_attn_body(q_ref, k_ref, v_ref, o_ref, acc_scr, cs_scr):
    i = pl.program_id(2)
    t = q_ref.shape[2]
    nh = q_ref.shape[1]
    row = lax.broadcasted_iota(jnp.int32, (t, t), 0)
    col = lax.broadcasted_iota(jnp.int32, (t, t), 1)
    suffix = jnp.where(row > col, 1.0, 0.0).astype(BF16)
    below = col < row

    def logits(hh, k0):
        return _dot_nt(q_ref[0, hh], k_ref[0, hh, pl.ds(k0, t), :])

    def block(hh, k0, diagonal, z):
        vb = v_ref[0, hh, pl.ds(k0, t), :]
        sp = jnp.maximum(z, 0.0) + jnp.log(1.0 + jnp.exp(-jnp.abs(z)))
        l1 = jnp.where(below, -sp, 0.0) if diagonal else -sp
        hi, lo = _split_bf16(l1)
        within = _dot(hi, suffix) + _dot(lo, suffix)
        if diagonal:
            w = jnp.where(below, jnp.exp(z - sp + within), 0.0)
            acc_scr[hh] = _dot(w.astype(BF16), vb)
            cs = jnp.sum(l1, axis=1, keepdims=True)
        else:
            w = jnp.exp(z - sp + within + cs_scr[hh])
            acc_scr[hh] += _dot(w.astype(BF16), vb)
            cs = cs_scr[hh] + jnp.sum(l1, axis=1, keepdims=True)
        cs_scr[hh] = cs
        return cs

    def alive(css):
        return (jnp.max(functools.reduce(jnp.maximum, css)) > EXP_UNDERFLOW).astype(jnp.int32)

    def all_heads(k0, diagonal):
        css = []
        z_next = logits(0, k0)
        for hh in range(nh):
            z = z_next
            if hh + 1 < nh:
                z_next = logits(hh + 1, k0)
            css.append(block(hh, k0, diagonal, z))
        return alive(css)

    first = all_heads(pl.multiple_of(i * t, t), True)

    def cond(c):
        return jnp.logical_and(c[0] <= i, c[1] > 0)

    def body(c):
        return c[0] + 1, all_heads(pl.multiple_of((i - c[0]) * t, t), False)

    lax.while_loop(cond, body, (jnp.int32(1), first))
    o_ref[0] = jnp.concatenate([acc_scr[hh] for hh in range(nh)], axis=-1).astype(o_ref.dtype)


def _sb_attention(q, k, v):
    B, H, L, d = q.shape
    t = min(SB_BLOCK, L)
    nh = SB_HEADS_PER_STEP
    return pl.pallas_call(
        _sb_attn_body,
        grid=(B, H // nh, L // t),
        in_specs=[pl.BlockSpec((1, nh, t, d), lambda b, h, i: (b, h, i, 0)),
                  pl.BlockSpec((1, nh, L, d), lambda b, h, i: (b, h, 0, 0)),
                  pl.BlockSpec((1, nh, L, d), lambda b, h, i: (b, h, 0, 0))],
        out_specs=pl.BlockSpec((1, t, nh * d), lambda b, h, i: (b, i, h)),
        out_shape=jax.ShapeDtypeStruct((B, L, H * d), BF16),
        scratch_shapes=[pltpu.VMEM((nh, t, d), F32), pltpu.VMEM((nh, t, 1), F32)],
        compiler_params=_params("parallel", "parallel", "arbitrary"),
        name="sb_attention",
    )(q, k, v)


def _ssm_tables(lam_re, lam_im, log_dt, b_re, b_im, c_re, c_im, d_skip):
    T, G, P, H = SSM_CHUNK, SSM_GROUPS, SSM_STATE, SSM_GROUP
    hp = lax.Precision.HIGHEST
    dt = jnp.exp(log_dt.astype(F32))[:, None]
    lr = lam_re.astype(F32)
    li = lam_im.astype(F32)
    mag = jnp.exp(lr * dt)
    a_re = mag * jnp.cos(li * dt)
    a_im = mag * jnp.sin(li * dt)
    den = lr * lr + li * li
    nr = a_re - 1.0
    f_re = (nr * lr + a_im * li) / den
    f_im = (a_im * lr - nr * li) / den
    br = b_re.astype(F32)
    bi = b_im.astype(F32)
    bb_re = f_re[..., None] * br - f_im[..., None] * bi
    bb_im = f_re[..., None] * bi + f_im[..., None] * br
    tau = jnp.arange(T + 1, dtype=F32)[None, :, None]
    pmag = jnp.exp(tau * (lr * dt)[:, None, :])
    pw_re = pmag * jnp.cos(tau * (li * dt)[:, None, :])
    pw_im = pmag * jnp.sin(tau * (li * dt)[:, None, :])
    cr = c_re.astype(F32)[:, None]
    ci = c_im.astype(F32)[:, None]
    cpw_re = cr * pw_re[:, :, None, :] - ci * pw_im[:, :, None, :]
    cpw_im = cr * pw_im[:, :, None, :] + ci * pw_re[:, :, None, :]
    kern = (jnp.einsum('gthp,gpi->gthi', cpw_re, bb_re, precision=hp)
            - jnp.einsum('gthp,gpi->gthi', cpw_im, bb_im, precision=hp))
    s_idx = np.arange(T)[:, None]
    t_idx = np.arange(T)[None, :]
    lag = np.clip(t_idx - s_idx, 0, T)
    toe = kern[:, lag]
    toe = jnp.where(jnp.asarray(t_idx >= s_idx)[None, :, :, None, None], toe, 0.0)
    intra = toe.transpose(0, 1, 4, 2, 3).reshape(G, T * H, T * H)
    rev = np.arange(T - 1, -1, -1)
    qr = pw_re[:, rev][:, :, None, :]
    qi = pw_im[:, rev][:, :, None, :]
    bbr = bb_re.transpose(0, 2, 1)[:, None]
    bbi = bb_im.transpose(0, 2, 1)[:, None]
    st_re = (qr * bbr - qi * bbi).reshape(G, T * H, P)
    st_im = (qr * bbi + qi * bbr).reshape(G, T * H, P)
    to_st = jnp.concatenate([st_re, st_im, st_im, st_re], axis=-1)
    top = cpw_re[:, 1:].transpose(0, 3, 1, 2).reshape(G, P, T * H)
    bot = (-cpw_im[:, 1:]).transpose(0, 3, 1, 2).reshape(G, P, T * H)
    from_st = jnp.concatenate([top, bot], axis=1)
    at_re = pw_re[:, T]
    at_im = pw_im[:, T]
    zeros = jnp.zeros_like(at_re)
    adv = jnp.stack([jnp.concatenate([at_re, at_re], -1),
                     jnp.concatenate([-at_im, at_im], -1),
                     jnp.concatenate([at_im, -at_im], -1)]
                    + [jnp.concatenate([zeros, zeros], -1)] * 5, axis=1)
    dvec = jnp.tile(d_skip.astype(F32).reshape(G, 1, H), (1, 1, T))
    return intra.astype(BF16), to_st.astype(BF16), from_st.astype(BF16), adv, dvec


def _ssm_body(u_ref, intra_ref, tost_ref, fromst_ref, adv_ref, dvec_ref, y_ref, s_scr, xin_scr, *, bsz, chunks):
    P2 = 2 * SSM_STATE
    u = u_ref[0]
    ub = u.astype(BF16)
    s_scr[...] = _dot(ub, tost_ref[0])
    a1 = adv_ref[0, 0:1, :]
    a2 = adv_ref[0, 1:2, :]
    a3 = adv_ref[0, 2:3, :]

    def step(c, carry):
        x1, x2 = carry
        r0 = pl.multiple_of(c * bsz, bsz)
        xin_scr[pl.ds(r0, bsz), :] = x1
        s = s_scr[pl.ds(r0, bsz), :]
        return (a1 * x1 + a2 * x2 + s[:, :P2], a1 * x2 + a3 * x1 + s[:, P2:])

    zero = jnp.zeros((bsz, P2), F32)
    lax.fori_loop(0, chunks, step, (zero, zero))
    y_ref[0] = (_dot(ub, intra_ref[0]) + _dot(xin_scr[...].astype(BF16), fromst_ref[0]) + dvec_ref[0] * u)


def _ssm(u, tables):
    intra, to_st, from_st, adv, dvec = tables
    B, L, W = u.shape
    T, G, H, P = SSM_CHUNK, SSM_GROUPS, SSM_GROUP, SSM_STATE
    C = L // T
    ut = u.reshape(B, C, T, G, H).transpose(3, 1, 0, 2, 4).reshape(G, C * B, T * H)
    yt = pl.pallas_call(
        functools.partial(_ssm_body, bsz=B, chunks=C),
        grid=(G,),
        in_specs=[pl.BlockSpec((1, C * B, T * H), lambda g: (g, 0, 0)),
                  pl.BlockSpec((1, T * H, T * H), lambda g: (g, 0, 0)),
                  pl.BlockSpec((1, T * H, 4 * P), lambda g: (g, 0, 0)),
                  pl.BlockSpec((1, 2 * P, T * H), lambda g: (g, 0, 0)),
                  pl.BlockSpec((1, 8, 2 * P), lambda g: (g, 0, 0)),
                  pl.BlockSpec((1, 1, T * H), lambda g: (g, 0, 0))],
        out_specs=pl.BlockSpec((1, C * B, T * H), lambda g: (g, 0, 0)),
        out_shape=jax.ShapeDtypeStruct((G, C * B, T * H), F32),
        scratch_shapes=[pltpu.VMEM((C * B, 4 * P), F32), pltpu.VMEM((C * B, 2 * P), F32)],
        compiler_params=_params("parallel"),
        name="s5_scan",
    )(ut, intra, to_st, from_st, adv, dvec)
    return yt.reshape(G, C, B, T, H).transpose(2, 1, 3, 0, 4).reshape(B, L, W)


def _outproj0_body(x_ref, oa_ref, y_ref, wglu_ref, woa_ref, wob_ref, g_ref, b_ref, o_ref):
    h = _gelu(y_ref[...])
    ob = h * _sigmoid(_dot(h.astype(BF16), wglu_ref[...]))
    m = _dot(oa_ref[...], woa_ref[...]) + _dot(ob.astype(BF16), wob_ref[...])
    o_ref[...] = _layer_norm(ALPHA * x_ref[...] + m, g_ref[...], b_ref[...])


def _outproj0(x, oa, y, wglu, woa, wob, g, b, tm=512):
    N, D = x.shape
    tm = min(tm, N)
    row = lambda i: (i, 0)
    fix = lambda i: (0, 0)
    return pl.pallas_call(
        _outproj0_body,
        grid=(N // tm,),
        in_specs=[pl.BlockSpec((tm, D), row), pl.BlockSpec((tm, SB_WIDTH), row), pl.BlockSpec((tm, SSM_WIDTH), row),
                  pl.BlockSpec((SSM_WIDTH, SSM_WIDTH), fix), pl.BlockSpec((SB_WIDTH, D), fix),
                  pl.BlockSpec((SSM_WIDTH, D), fix), pl.BlockSpec((1, D), fix), pl.BlockSpec((1, D), fix)],
        out_specs=pl.BlockSpec((tm, D), row),
        out_shape=jax.ShapeDtypeStruct((N, D), F32),
        compiler_params=_params("parallel"),
        name="outproj0",
    )(x, oa, y, wglu, woa, wob, g, b)


def _outproj1_body(x_ref, o_ref_in, w_ref, g_ref, b_ref, o_ref):
    m = _dot(o_ref_in[...].astype(BF16), w_ref[...])
    o_ref[...] = _layer_norm(ALPHA * x_ref[...] + m, g_ref[...], b_ref[...])


def _outproj1(x, o, w, g, b, tm=512):
    N, D = x.shape
    tm = min(tm, N)
    row = lambda i: (i, 0)
    fix = lambda i: (0, 0)
    return pl.pallas_call(
        _outproj1_body,
        grid=(N // tm,),
        in_specs=[pl.BlockSpec((tm, D), row), pl.BlockSpec((tm, o.shape[1]), row),
                  pl.BlockSpec(w.shape, fix), pl.BlockSpec((1, D), fix), pl.BlockSpec((1, D), fix)],
        out_specs=pl.BlockSpec((tm, D), row),
        out_shape=jax.ShapeDtypeStruct((N, D), F32),
        compiler_params=_params("parallel"),
        name="outproj1",
    )(x, o, w, g, b)


def _router_body(x_ref, wh_ref, wl_ref, b_ref, g_ref, grp_ref):
    xh, xl = _split_bf16(x_ref[...])
    wh = wh_ref[...]
    logits = _dot_nt(wh, xh) + _dot_nt(wh, xl) + _dot_nt(wl_ref[...], xh)
    aff = _sigmoid(logits)
    sel = aff + b_ref[...]
    E, K = EXPERTS_PER_GROUP, N_GROUPS
    s = [sel[e:e + 1, :] for e in range(N_EXPERTS)]
    a = [aff[e:e + 1, :] for e in range(N_EXPERTS)]
    gscore = []
    for k in range(K):
        v = s[k * E:(k + 1) * E]
        best = None
        for x in range(E):
            for y in range(x + 1, E):
                pair = v[x] + v[y]
                best = pair if best is None else jnp.maximum(best, pair)
        gscore.append(best)
    top = functools.reduce(jnp.maximum, gscore)
    is_g = []
    taken = None
    for k in range(K):
        hit = gscore[k] == top
        if taken is None:
            is_g.append(hit)
            taken = hit
        else:
            is_g.append(jnp.logical_and(hit, jnp.logical_not(taken)))
            taken = jnp.logical_or(taken, hit)

    def pick(rows, j):
        out = rows[(K - 1) * E + j]
        for k in range(K - 2, -1, -1):
            out = jnp.where(is_g[k], rows[k * E + j], out)
        return out

    v = [pick(s, j) for j in range(E)]
    av = [pick(a, j) for j in range(E)]
    chosen = []
    for j in range(E):
        r = jnp.zeros_like(v[j])
        for j2 in range(E):
            if j2 == j:
                continue
            ahead = (v[j2] >= v[j]) if j2 < j else (v[j2] > v[j])
            r = r + jnp.where(ahead, 1.0, 0.0)
        chosen.append(r < 2.0)
    wj = [jnp.where(chosen[j], av[j], 0.0) for j in range(E)]
    tot = wj[0] + wj[1] + wj[2] + wj[3]
    gj = [w / tot for w in wj]
    rows = [jnp.where(is_g[e // E], gj[e % E], 0.0) for e in range(N_EXPERTS)]
    g_ref[...] = jnp.concatenate(rows, axis=0)
    grp_ref[...] = jnp.concatenate([jnp.where(is_g[k], 1.0, 0.0) for k in range(K)], axis=0)


def _router(x, w_router, b_router, tm=512):
    N, D = x.shape
    tm = min(tm, N)
    wt = w_router.astype(F32).T
    wh, wl = _split_bf16(wt)
    fix = lambda i: (0, 0)
    return pl.pallas_call(
        _router_body,
        grid=(N // tm,),
        in_specs=[pl.BlockSpec((tm, D), lambda i: (i, 0)), pl.BlockSpec((N_EXPERTS, D), fix),
                  pl.BlockSpec((N_EXPERTS, D), fix), pl.BlockSpec((N_EXPERTS, 1), fix)],
        out_specs=[pl.BlockSpec((N_EXPERTS, tm), lambda i: (0, i)), pl.BlockSpec((N_GROUPS, tm), lambda i: (0, i))],
        out_shape=[jax.ShapeDtypeStruct((N_EXPERTS, N), F32), jax.ShapeDtypeStruct((N_GROUPS, N), F32)],
        compiler_params=_params("parallel"),
        name="router",
    )(x, wh, wl, b_router.astype(F32).reshape(N_EXPERTS, 1))


MOE_TILE = 1024
MOE_PASS_ROWS = 304
MOE_CHUNK = 256
ROW_ALIGN = 16


def _moe_body(seg_ref, x_ref, gt_ref, grp_ref, w1_ref, w3_ref, w2_ref, g_ref, b_ref, o_ref,
              xs_scr, ys_scr, pt_scr, gs_scr):
    i = pl.program_id(0)
    e = pl.program_id(1)
    T = x_ref.shape[0]
    K = N_GROUPS
    cap = MOE_PASS_ROWS
    ch = min(MOE_CHUNK, T)

    @pl.when(e == 0)
    def _():
        grp = grp_ref[...]
        grp_b = grp.astype(BF16)
        xb = x_ref[...].astype(BF16)
        gates = gt_ref[...]
        gh, gm = _split_bf16(gates)
        gl = (gates - gh.astype(F32) - gm.astype(F32)).astype(BF16)
        cnt = jnp.sum(grp, axis=1, keepdims=True)
        off = [jnp.zeros((1, 1), F32)]
        for k in range(K - 1):
            off.append(off[-1] + cnt[k:k + 1])
        r_i = lax.broadcasted_iota(jnp.int32, (ch, T), 0)
        c_i = lax.broadcasted_iota(jnp.int32, (ch, T), 1)
        pos_row = []
        pos_col = []
        for c in range(T // ch):
            c0 = c * ch
            earlier = jnp.where(c_i < r_i + c0, 1.0, 0.0).astype(BF16)
            same = jnp.where(c_i == r_i + c0, 1.0, 0.0).astype(BF16)
            before_c = _dot_nt(earlier, grp_b)
            member_c = _dot_nt(same, grp_b)
            before_r = _dot_nt(grp_b, earlier)
            pc = jnp.zeros((ch, 1), F32)
            pr = jnp.zeros((1, ch), F32)
            for k in range(K):
                pc = pc + member_c[:, k:k + 1] * (off[k] + before_c[:, k:k + 1])
                pr = pr + grp[k:k + 1, c0:c0 + ch] * (off[k] + before_r[k:k + 1, :])
            pos_col.append(pc)
            pos_row.append(pr)
        pos_row = jnp.concatenate(pos_row, axis=1)
        for c in range(T // ch):
            c0 = c * ch
            perm = jnp.where((r_i + c0).astype(F32) == pos_row, 1.0, 0.0).astype(BF16)
            xs_scr[c0:c0 + ch, :] = _dot(perm, xb).astype(BF16)
            gs_scr[c0:c0 + ch, :] = _dot_nt(perm, gh) + _dot_nt(perm, gm) + _dot_nt(perm, gl)
            pt_scr[c0:c0 + ch, :] = jnp.where(c_i.astype(F32) == pos_col[c], 1.0, 0.0).astype(BF16)
        xs_scr[T:, :] = jnp.zeros((cap, xs_scr.shape[1]), BF16)
        gs_scr[T:, :] = jnp.zeros((cap, gs_scr.shape[1]), F32)
        ys_scr[...] = jnp.zeros_like(ys_scr)

    k = e // EXPERTS_PER_GROUP
    off = seg_ref[i, k]
    cnt = seg_ref[i, K + k]
    start0 = (off // ROW_ALIGN) * ROW_ALIGN
    n_pass = (off + cnt - start0 + cap - 1) // cap

    def one_pass(n, c):
        start = pl.multiple_of(start0 + n * cap, ROW_ALIGN)
        xc = xs_scr[pl.ds(start, cap), :]
        a = _dot(xc, w1_ref[0])
        hm = (a * _sigmoid(a)) * _dot(xc, w3_ref[0])
        gsc = gs_scr[pl.ds(start, cap), :]
        lane = lax.broadcasted_iota(jnp.int32, gsc.shape, 1)
        gcol = jnp.sum(jnp.where(lane == e, gsc, 0.0), axis=1, keepdims=True)
        ys_scr[pl.ds(start, cap), :] += gcol * _dot(hm.astype(BF16), w2_ref[0])
        return c

    lax.fori_loop(0, n_pass, one_pass, 0)

    @pl.when(e == pl.num_programs(1) - 1)
    def _():
        y = _dot(pt_scr[...], ys_scr[0:T, :].astype(BF16))
        o_ref[...] = _layer_norm(ALPHA * x_ref[...] + y, g_ref[...], b_ref[...])


def _moe(x, gates_t, grp_t, w1, w3, w2, g, b):
    N, D = x.shape
    T = min(MOE_TILE, N)
    K = N_GROUPS
    cnt = grp_t.reshape(K, N // T, T).sum(-1).astype(jnp.int32).T
    seg = jnp.concatenate([jnp.cumsum(cnt, axis=1) - cnt, cnt], axis=1)
    fix = lambda i, e, s: (0, 0)
    rows = T + MOE_PASS_ROWS
    grid_spec = pltpu.PrefetchScalarGridSpec(
        num_scalar_prefetch=1,
        grid=(N // T, N_EXPERTS),
        in_specs=[pl.BlockSpec((T, D), lambda i, e, s: (i, 0)),
                  pl.BlockSpec((N_EXPERTS, T), lambda i, e, s: (0, i)),
                  pl.BlockSpec((K, T), lambda i, e, s: (0, i)),
                  pl.BlockSpec((1, D, D_EXPERT), lambda i, e, s: (e, 0, 0)),
                  pl.BlockSpec((1, D, D_EXPERT), lambda i, e, s: (e, 0, 0)),
                  pl.BlockSpec((1, D_EXPERT, D), lambda i, e, s: (e, 0, 0)),
                  pl.BlockSpec((1, D), fix), pl.BlockSpec((1, D), fix)],
        out_specs=pl.BlockSpec((T, D), lambda i, e, s: (i, 0)),
        scratch_shapes=[pltpu.VMEM((rows, D), BF16), pltpu.VMEM((rows, D), F32),
                        pltpu.VMEM((T, T), BF16), pltpu.VMEM((rows, N_EXPERTS), F32)])
    return pl.pallas_call(
        _moe_body,
        grid_spec=grid_spec,
        out_shape=jax.ShapeDtypeStruct((N, D), F32),
        compiler_params=_params("parallel", "arbitrary"),
        name="moe",
    )(seg, x, gates_t, grp_t, w1, w3, w2, g, b)


NSA_KVW = NSA_KV_HEADS * HEAD_DIM
NSA_ROPE_W = NSA_HEADS * HEAD_DIM + 2 * NSA_KVW
NSA_Q_SCALE = ATTN_SCALE * math.log2(math.e)


def _rot_cols(w):
    K, n = w.shape
    w3 = w.reshape(K, n // HEAD_DIM, 2, HEAD_DIM // 2)
    return jnp.stack([-w3[:, :, 1], w3[:, :, 0]], axis=2).reshape(K, n)


def _inproj1_weights(w_in):
    H, G, d = NSA_HEADS, NSA_KV_HEADS, HEAD_DIM
    cuts = [H * d + i * NSA_KVW for i in range(7)]
    q, kc, vc, ks, vs, kw, vw, gate = jnp.split(w_in.astype(F32), cuts, axis=1)
    rope = jnp.concatenate([q, ks, kw], axis=1)
    gate = jnp.pad(gate, ((0, 0), (0, 128 - gate.shape[1])))
    w_all = jnp.concatenate([rope, _rot_cols(rope), kc, vc, gate], axis=1).astype(BF16)
    w_vt = jnp.concatenate([vs, vw], axis=1).T.astype(BF16)
    return w_all, w_vt


def _rope_tables(pos):
    inv = ROPE_THETA ** (-jnp.arange(0, HEAD_DIM, 2, dtype=F32) / HEAD_DIM)
    ang = pos.astype(F32)[:, None] * inv[None, :]
    c, s = jnp.cos(ang), jnp.sin(ang)
    return jnp.concatenate([c, c], axis=-1), jnp.concatenate([s, s], axis=-1)


V_ROWS = HEAD_DIM + 16


def _inproj1_body(x_ref, w_ref, wvt_ref, cos_ref, sin_ref, q_ref, ks_ref, kw_ref, kc_ref, vc_ref, vst_ref, vwt_ref,
                  gate_ref):
    xb = x_ref[0].astype(BF16)
    p = _dot(xb, w_ref[...])
    vt = _dot_nt(wvt_ref[...], xb)
    ones = jnp.ones((V_ROWS - HEAD_DIM, xb.shape[0]), BF16)
    for n, ref in enumerate((vst_ref, vwt_ref)):
        for g in range(NSA_KV_HEADS):
            r0 = n * NSA_KVW + g * HEAD_DIM
            ref[0, g, 0:HEAD_DIM, :] = vt[r0:r0 + HEAD_DIM, :].astype(BF16)
            ref[0, g, HEAD_DIM:V_ROWS, :] = ones
    R = NSA_ROPE_W
    cos = cos_ref[...]
    sin = sin_ref[...]
    roped = []
    for c in range(R // 128):
        roped.append(p[:, c * 128:(c + 1) * 128] * cos + p[:, R + c * 128:R + (c + 1) * 128] * sin)
    d = HEAD_DIM

    def head(chunks, h):
        blk = chunks[h // 2]
        return blk[:, (h % 2) * d:(h % 2 + 1) * d]

    for h in range(NSA_HEADS):
        q_ref[0, h] = (head(roped, h) * NSA_Q_SCALE).astype(BF16)
    nq = NSA_HEADS // 2
    for g in range(NSA_KV_HEADS):
        ks_ref[0, g] = head(roped[nq:], g).astype(BF16)
        kw_ref[0, g] = head(roped[nq + 2:], g).astype(BF16)
    base = 2 * R
    for n, ref in enumerate((kc_ref, vc_ref)):
        for g in range(NSA_KV_HEADS):
            c0 = base + n * NSA_KVW + g * d
            ref[0, g] = p[:, c0:c0 + d].astype(BF16)
    gate_ref[0] = p[:, base + 2 * NSA_KVW:]


def _inproj1(x, w_all, w_vt, cos2, sin2, tl=256):
    B, L, D = x.shape
    tl = min(tl, L)
    nout = w_all.shape[1]
    G = NSA_KV_HEADS
    qh = jax.ShapeDtypeStruct((B, NSA_HEADS, L, HEAD_DIM), BF16)
    kvh = jax.ShapeDtypeStruct((B, G, L, HEAD_DIM), BF16)
    vth = jax.ShapeDtypeStruct((B, G, V_ROWS, L), BF16)
    q_spec = pl.BlockSpec((1, NSA_HEADS, tl, HEAD_DIM), lambda b, i: (b, 0, i, 0))
    kv_spec = pl.BlockSpec((1, G, tl, HEAD_DIM), lambda b, i: (b, 0, i, 0))
    vt_spec = pl.BlockSpec((1, G, V_ROWS, tl), lambda b, i: (b, 0, 0, i))
    return pl.pallas_call(
        _inproj1_body,
        grid=(B, L // tl),
        in_specs=[pl.BlockSpec((1, tl, D), lambda b, i: (b, i, 0)),
                  pl.BlockSpec((D, nout), lambda b, i: (0, 0)),
                  pl.BlockSpec(w_vt.shape, lambda b, i: (0, 0)),
                  pl.BlockSpec((tl, 128), lambda b, i: (i, 0)),
                  pl.BlockSpec((tl, 128), lambda b, i: (i, 0))],
        out_specs=[q_spec] + [kv_spec] * 4 + [vt_spec] * 2 + [pl.BlockSpec((1, tl, 128), lambda b, i: (b, i, 0))],
        out_shape=[qh] + [kvh] * 4 + [vth] * 2 + [jax.ShapeDtypeStruct((B, L, 128), F32)],
        compiler_params=_params("parallel", "parallel"),
        name="inproj1",
    )(x, w_all, w_vt, cos2, sin2)


def _compress_body(kc_ref, vc_ref, posk_ref, posv_ref, w1k_ref, w1v_ref, w2k_ref, w2kr_ref, w2v_ref,
                   cos_ref, sin_ref, kcmp_ref, vcmp_ref):
    def hidden(a_ref, pos_ref, w1_ref):
        a = a_ref[0, 0].astype(F32)
        nrow = a.shape[0]
        lo = _dot((a + pos_ref[0:1, :]).astype(BF16), w1_ref[0])
        hi = _dot((a + pos_ref[1:2, :]).astype(BF16), w1_ref[1])
        hi_next = pltpu.roll(hi, nrow - 1, 0)
        return _gelu(lo + hi_next).astype(BF16)

    hk = hidden(kc_ref, posk_ref, w1k_ref)
    kcmp = _dot(hk, w2k_ref[...]) * cos_ref[...] + _dot(hk, w2kr_ref[...]) * sin_ref[...]
    kcmp_ref[0, 0] = kcmp.astype(BF16)
    hv = hidden(vc_ref, posv_ref, w1v_ref)
    vcmp_ref[0, 0] = _dot_nt(w2v_ref[...], hv).astype(BF16)


def _compress(kc, vc, pos_k, w1_k, w2_k, pos_v, w1_v, w2_v):
    B, G, L, d = kc.shape
    half = CMP_STRIDE * d
    nb = L // CMP_STRIDE
    kc2 = kc.reshape(B, G, nb, half)
    vc2 = vc.reshape(B, G, nb, half)
    posk = pos_k.astype(F32).reshape(2, half)
    posv = pos_v.astype(F32).reshape(2, half)
    w1k = w1_k.astype(BF16).reshape(2, half, CMP_HIDDEN)
    w1v = w1_v.astype(BF16).reshape(2, half, CMP_HIDDEN)
    w2k = w2_k.astype(F32)
    cos, sin = _rope_tables(jnp.arange(nb) * CMP_STRIDE + CMP_LEN - 1)
    blk = pl.BlockSpec((1, 1, nb, half), lambda b, g: (b, g, 0, 0))
    out = pl.BlockSpec((1, 1, nb, d), lambda b, g: (b, g, 0, 0))
    out_t = pl.BlockSpec((1, 1, d, nb), lambda b, g: (b, g, 0, 0))
    fix2 = lambda b, g: (0, 0)
    fix3 = lambda b, g: (0, 0, 0)
    return pl.pallas_call(
        _compress_body,
        grid=(B, G),
        in_specs=[blk, blk, pl.BlockSpec((2, half), fix2), pl.BlockSpec((2, half), fix2),
                  pl.BlockSpec((2, half, CMP_HIDDEN), fix3), pl.BlockSpec((2, half, CMP_HIDDEN), fix3),
                  pl.BlockSpec((CMP_HIDDEN, d), fix2), pl.BlockSpec((CMP_HIDDEN, d), fix2),
                  pl.BlockSpec((d, CMP_HIDDEN), fix2), pl.BlockSpec((nb, d), fix2), pl.BlockSpec((nb, d), fix2)],
        out_specs=[out, out_t],
        out_shape=[jax.ShapeDtypeStruct((B, G, nb, d), BF16), jax.ShapeDtypeStruct((B, G, d, nb), BF16)],
        compiler_params=_params("parallel", "parallel"),
        name="compress",
    )(kc2, vc2, posk, posv, w1k, w1v, w2k.astype(BF16), _rot_cols(w2k).astype(BF16), w2_v.T.astype(BF16), cos, sin)


NSA_KBLOCK = 256
NSA_GROUPS_PER_STEP = 4
NSA_HEADS_PER_CHAIN = 4


def _nsa_gate_layout(gate):
    B, L, _ = gate.shape
    G, R, tq = NSA_KV_HEADS, NSA_REP, min(ATT_BLOCK, L)
    g5 = gate[:, :, :3 * NSA_HEADS].reshape(B, L // tq, tq, G, R, 3)
    return g5.transpose(0, 3, 1, 5, 4, 2).reshape(B, G, L // tq, 3, R * tq)


def _nsa_body(q_ref, kcmp_ref, vcmpt_ref, ks_ref, vst_ref, kw_ref, vwt_ref, gate_ref, o_ref, *, seq):
    i = pl.program_id(2)
    tq = q_ref.shape[2]
    gb = kcmp_ref.shape[1]
    tk = min(NSA_KBLOCK, seq)
    R, d = NSA_REP, HEAD_DIM
    nq = R * tq
    nb = seq // SEL_LEN
    mc = seq // CMP_STRIDE
    t_row = i * tq + lax.broadcasted_iota(jnp.int32, (1, tq), 1)
    t_all = jnp.concatenate([t_row] * R, axis=1)
    m_col = lax.broadcasted_iota(jnp.int32, (mc, 1), 0)
    m_row = lax.broadcasted_iota(jnp.int32, (1, mc), 1)
    n_col = lax.broadcasted_iota(jnp.int32, (nb, 1), 0)

    def q_rows(g):
        return q_ref[0, g * R:(g + 1) * R].reshape(nq, d)

    valid_c = (m_col * CMP_STRIDE + (CMP_LEN - 1)) <= t_all
    ovl = jnp.logical_and(m_row * CMP_STRIDE < (n_col + 1) * SEL_LEN,
                          m_row * CMP_STRIDE + CMP_LEN > n_col * SEL_LEN)
    ovl = jnp.where(ovl, 1.0, 0.0).astype(BF16)
    cur = t_row // SEL_LEN
    forced = jnp.logical_or(n_col == 0, jnp.logical_or(n_col == cur, n_col == cur - 1))
    bonus = jnp.where(forced, FORCE_BONUS, 0.0)
    valid_s = n_col * SEL_LEN <= t_row
    o_c, sel = [], []
    for g in range(gb):
        s = jnp.where(valid_c, _dot_nt(kcmp_ref[0, g], q_rows(g)), NEG)
        e = jnp.where(valid_c, jnp.exp2(s - jnp.max(s, axis=0, keepdims=True)), 0.0)
        den = jnp.sum(e, axis=0, keepdims=True)
        p = e / jnp.where(den > 0.0, den, 1.0)
        o_c.append(_dot(vcmpt_ref[0, g], p.astype(BF16)))
        psum = p[:, 0:tq]
        for r in range(1, R):
            psum = psum + p[:, r * tq:(r + 1) * tq]
        ph, plo = _split_bf16(psum)
        score = jnp.where(valid_s, _dot(ovl, ph) + _dot(ovl, plo) + bonus, NEG)
        rank = jnp.zeros((nb, tq), F32)
        for n2 in range(nb):
            other = score[n2:n2 + 1, :]
            tie = jnp.where(n_col > n2, 1.0, 0.0)
            rank = rank + jnp.where(other > score, 1.0, jnp.where(other == score, tie, 0.0))
        sel.append(jnp.where(rank < float(SEL_TOP), 1.0, 0.0).astype(BF16))

    k_col = lax.broadcasted_iota(jnp.int32, (tk, 1), 0)
    hp = NSA_HEADS_PER_CHAIN
    j_hi = (i * tq + tq - 1) // tk + 1

    def sweep(k_ref, vt_ref, j_lo, bias_fn):
        chains = [(g, h0) for g in range(gb) for h0 in range(0, R, hp)]

        def body(j, state):
            k0 = pl.multiple_of(j * tk, tk)
            biases = bias_fn(k0 + k_col)
            def scores(c):
                g, h0 = chains[c]
                kb = k_ref[0, g, pl.ds(k0, tk), :]
                qh = q_ref[0, g * R + h0:g * R + h0 + hp].reshape(hp * tq, d)
                return _dot_nt(kb, qh) + jnp.concatenate([biases[g]] * hp, axis=1)

            new_state = []
            sc_next = scores(0)
            for c, (m_run, acc) in enumerate(state):
                sc = sc_next
                if c + 1 < len(chains):
                    sc_next = scores(c + 1)
                vtb = vt_ref[0, chains[c][0], :, pl.ds(k0, tk)]
                m_new = jnp.maximum(m_run, jnp.max(sc, axis=0, keepdims=True))
                pr = jnp.exp2(sc - m_new)
                new_state.append((m_new, jnp.exp2(m_run - m_new) * acc + _dot(vtb, pr.astype(BF16))))
            return tuple(new_state)

        init = tuple((jnp.full((1, hp * tq), NEG, F32), jnp.zeros((V_ROWS, hp * tq), F32)) for _ in chains)
        state = lax.fori_loop(j_lo, j_hi, body, init)
        outs = []
        for g in range(gb):
            acc = jnp.concatenate([a for (cg, _), (_, a) in zip(chains, state) if cg == g], axis=1)
            outs.append(acc[0:d] / acc[d:d + 1])
        return outs

    def sel_bias(kpos):
        pick = jnp.where(kpos // SEL_LEN == lax.broadcasted_iota(jnp.int32, (1, nb), 1), 1.0, 0.0).astype(BF16)
        causal = kpos <= t_row
        return [jnp.where(jnp.logical_and(_dot(pick, sel[g]) > 0.5, causal), 0.0, NEG) for g in range(gb)]

    o_s = sweep(ks_ref, vst_ref, 0, sel_bias)

    def win_bias(kpos):
        return [jnp.where(jnp.logical_and(kpos <= t_row, kpos > t_row - WINDOW), 0.0, NEG)] * gb

    o_w = sweep(kw_ref, vwt_ref, jnp.maximum(i * tq - (WINDOW - 1), 0) // tk, win_bias)

    outs = []
    for g in range(gb):
        sg = _sigmoid(gate_ref[0, g, 0])
        o_t = sg[0:1] * o_c[g] + sg[1:2] * o_s[g] + sg[2:3] * o_w[g]
        for r in range(R):
            outs.append(o_t[:, r * tq:(r + 1) * tq].T)
    o_ref[0] = jnp.concatenate(outs, axis=-1)


def _nsa_attention(q, kcmp, vcmpt, ks, vst, kw, vwt, gate):
    B, H, L, d = q.shape
    G, R = NSA_KV_HEADS, NSA_REP
    gb = NSA_GROUPS_PER_STEP
    tq = min(ATT_BLOCK, L)
    mc = L // CMP_STRIDE
    k_spec = pl.BlockSpec((1, gb, L, d), lambda b, g, i: (b, g, 0, 0))
    vt_spec = pl.BlockSpec((1, gb, V_ROWS, L), lambda b, g, i: (b, g, 0, 0))
    return pl.pallas_call(
        functools.partial(_nsa_body, seq=L),
        grid=(B, G // gb, L // tq),
        in_specs=[pl.BlockSpec((1, gb * R, tq, d), lambda b, g, i: (b, g, i, 0)),
                  pl.BlockSpec((1, gb, mc, d), lambda b, g, i: (b, g, 0, 0)),
                  pl.BlockSpec((1, gb, d, mc), lambda b, g, i: (b, g, 0, 0)),
                  k_spec, vt_spec, k_spec, vt_spec,
                  pl.BlockSpec((1, gb, 1, 3, R * tq), lambda b, g, i: (b, g, i, 0, 0))],
        out_specs=pl.BlockSpec((1, tq, gb * R * d), lambda b, g, i: (b, i, g)),
        out_shape=jax.ShapeDtypeStruct((B, L, H * d), F32),
        compiler_params=_params("parallel", "parallel", "arbitrary"),
        name="nsa_attention",
    )(q, kcmp, vcmpt, ks, vst, kw, vwt, gate)


def kernel(x, w_in_0, ssm_lam_re, ssm_lam_im, ssm_log_dt, ssm_b_re, ssm_b_im, ssm_c_re, ssm_c_im, ssm_d, w_glu, w_out_0, ln_mix_g_0, ln_mix_b_0, ln_ffn_g_0, ln_ffn_b_0, w1_0, w3_0, w2_0, w_in_1, cmp_pos_k, cmp_w1_k, cmp_w2_k, cmp_pos_v, cmp_w1_v, cmp_w2_v, w_out_1, ln_mix_g_1, ln_mix_b_1, ln_ffn_g_1, ln_ffn_b_1, w1_1, w3_1, w2_1, w_router, b_router):
    B, L, D = x.shape
    N = B * L
    vec = lambda a: a.astype(F32).reshape(1, D)

    def ffn(h, w1, w3, w2, g, b):
        gates_t, grp_t = _router(h, w_router, b_router)
        return _moe(h, gates_t, grp_t, w1.astype(BF16), w3.astype(BF16), w2.astype(BF16), vec(g), vec(b))

    q, k, v, u = _inproj0(x, w_in_0.astype(BF16))
    o_a = _sb_attention(q, k, v).reshape(N, SB_WIDTH)
    y = _ssm(u, _ssm_tables(ssm_lam_re, ssm_lam_im, ssm_log_dt, ssm_b_re, ssm_b_im, ssm_c_re, ssm_c_im, ssm_d))
    w_out_0b = w_out_0.astype(BF16)
    h = _outproj0(x.reshape(N, D), o_a, y.reshape(N, SSM_WIDTH), w_glu.astype(BF16),
                  w_out_0b[:SB_WIDTH], w_out_0b[SB_WIDTH:], vec(ln_mix_g_0), vec(ln_mix_b_0))
    h = ffn(h, w1_0, w3_0, w2_0, ln_ffn_g_0, ln_ffn_b_0)

    cos, sin = _rope_tables(jnp.arange(L))
    cos2 = jnp.concatenate([cos, cos], axis=-1)
    sin2 = jnp.concatenate([sin, sin], axis=-1)
    q, ks, kw, kc, vc, vst, vwt, gate = _inproj1(h.reshape(B, L, D), *_inproj1_weights(w_in_1), cos2, sin2)
    kcmp, vcmpt = _compress(kc, vc, cmp_pos_k, cmp_w1_k, cmp_w2_k, cmp_pos_v, cmp_w1_v, cmp_w2_v)
    o = _nsa_attention(q, kcmp, vcmpt, ks, vst, kw, vwt, _nsa_gate_layout(gate)).reshape(N, NSA_HEADS * HEAD_DIM)
    h = _outproj1(h, o, w_out_1.astype(BF16), vec(ln_mix_g_1), vec(ln_mix_b_1))
    h = ffn(h, w1_1, w3_1, w2_1, ln_ffn_g_1, ln_ffn_b_1)
    return h.reshape(B, L, D)
```

```python
import functools
import math

import numpy as np
import jax
import jax.numpy as jnp
from jax import lax
from jax.experimental import pallas as pl
from jax.experimental.pallas import tpu as pltpu

F32 = jnp.float32
BF16 = jnp.bfloat16

D_MODEL = 1024
DEPTH = 2
SB_HEADS = 8
HEAD_DIM = 64
SB_WIDTH = SB_HEADS * HEAD_DIM
SSM_WIDTH = D_MODEL - SB_WIDTH
SSM_GROUP = 16
SSM_GROUPS = SSM_WIDTH // SSM_GROUP
SSM_STATE = 64
SSM_CHUNK = 16
SSM_LANE_TILES = SSM_WIDTH // 128
SSM_TILE_GROUPS = 128 // SSM_GROUP
NSA_HEADS = 16
NSA_KV_HEADS = 4
NSA_REP = NSA_HEADS // NSA_KV_HEADS
CMP_LEN = 32
CMP_STRIDE = 16
CMP_HIDDEN = 256
SEL_LEN = 64
SEL_TOP = 8
WINDOW = 512
ROPE_THETA = 10000.0
FORCE_BONUS = 1e4
NEG = -1e30
N_EXPERTS = 16
N_GROUPS = 4
EXPERTS_PER_GROUP = N_EXPERTS // N_GROUPS
D_EXPERT = 512
ALPHA = (2 * DEPTH) ** 0.25
LN_EPS = 1e-5
ATTN_SCALE = HEAD_DIM ** -0.5
ATT_BLOCK = 128
GELU_C = math.sqrt(2.0 / math.pi)


def _params(*sem):
    return pltpu.CompilerParams(dimension_semantics=sem, vmem_limit_bytes=56 * 1024 * 1024)


def _sigmoid(x):
    return 1.0 / (1.0 + jnp.exp(-x))


def _gelu(x):
    return 0.5 * x * (1.0 + jnp.tanh(GELU_C * (x + 0.044715 * (x * x * x))))


def _layer_norm(r, g, b):
    mu = jnp.mean(r, axis=-1, keepdims=True)
    d = r - mu
    var = jnp.mean(d * d, axis=-1, keepdims=True)
    return d * lax.rsqrt(var + LN_EPS) * g + b


def _dot(a, b):
    return jnp.dot(a, b, preferred_element_type=F32)


def _dot_nt(a, b):
    return lax.dot_general(a, b, (((1,), (1,)), ((), ())), preferred_element_type=F32)


def _split_bf16(x):
    hi = x.astype(BF16)
    lo = (x - hi.astype(F32)).astype(BF16)
    return hi, lo


def _inproj0_body(x_ref, w_ref, q_ref, k_ref, v_ref, u_ref):
    p = _dot(x_ref[0].astype(BF16), w_ref[...])
    for h in range(SB_HEADS):
        c = h * HEAD_DIM
        q_ref[0, h] = (p[:, c:c + HEAD_DIM] * ATTN_SCALE).astype(BF16)
        k_ref[0, h] = p[:, SB_WIDTH + c:SB_WIDTH + c + HEAD_DIM].astype(BF16)
        v_ref[0, h] = p[:, 2 * SB_WIDTH + c:2 * SB_WIDTH + c + HEAD_DIM].astype(BF16)
    tl = p.shape[0]
    for v in range(SSM_LANE_TILES):
        c = 3 * SB_WIDTH + v * 128
        u_ref[v, :, 0] = p[:, c:c + 128].reshape(tl // SSM_CHUNK, SSM_CHUNK, 128)


def _inproj0(x, w_bf16, tl=512):
    B, L, D = x.shape
    tl = min(tl, L)
    nout = w_bf16.shape[1]
    head = jax.ShapeDtypeStruct((B, SB_HEADS, L, HEAD_DIM), BF16)
    head_spec = pl.BlockSpec((1, SB_HEADS, tl, HEAD_DIM), lambda b, i: (b, 0, i, 0))
    return pl.pallas_call(
        _inproj0_body,
        grid=(B, L // tl),
        in_specs=[pl.BlockSpec((1, tl, D), lambda b, i: (b, i, 0)),
                  pl.BlockSpec((D, nout), lambda b, i: (0, 0))],
        out_specs=[head_spec, head_spec, head_spec,
                   pl.BlockSpec((SSM_LANE_TILES, tl // SSM_CHUNK, 1, SSM_CHUNK, 128), lambda b, i: (0, i, b, 0, 0))],
        out_shape=[head, head, head,
                   jax.ShapeDtypeStruct((SSM_LANE_TILES, L // SSM_CHUNK, B, SSM_CHUNK, 128), F32)],
        compiler_params=_params("parallel", "parallel"),
        name="inproj0",
    )(x, w_bf16)


SB_BLOCK = 256
SB_HEADS_PER_STEP = 4
EXP_UNDERFLOW = -104.0


def _sb_attn_body(q_ref, k_ref, v_ref, o_ref, acc_scr, cs_scr):
    i = pl.program_id(2)
    t = q_ref.shape[2]
    nh = q_ref.shape[1]
    row = lax.broadcasted_iota(jnp.int32, (t, t), 0)
    col = lax.broadcasted_iota(jnp.int32, (t, t), 1)
    suffix = jnp.where(row > col, 1.0, 0.0).astype(BF16)
    below = col < row

    def logits(hh, k0):
        return _dot_nt(q_ref[0, hh], k_ref[0, hh, pl.ds(k0, t), :])

    def block(hh, k0, diagonal, z):
        vb = v_ref[0, hh, pl.ds(k0, t), :]
        sp = jnp.maximum(z, 0.0) + jnp.log(1.0 + jnp.exp(-jnp.abs(z)))
        l1 = jnp.where(below, -sp, 0.0) if diagonal else -sp
        hi, lo = _split_bf16(l1)
        within = _dot(hi, suffix) + _dot(lo, suffix)
        if diagonal:
            w = jnp.where(below, jnp.exp(z - sp + within), 0.0)
            acc_scr[hh] = _dot(w.astype(BF16), vb)
            cs = jnp.sum(l1, axis=1, keepdims=True)
        else:
            w = jnp.exp(z - sp + within + cs_scr[hh])
            acc_scr[hh] += _dot(w.astype(BF16), vb)
            cs = cs_scr[hh] + jnp.sum(l1, axis=1, keepdims=True)
        cs_scr[hh] = cs
        return cs

    def alive(css):
        return (jnp.max(functools.reduce(jnp.maximum, css)) > EXP_UNDERFLOW).astype(jnp.int32)

    def all_heads(k0, diagonal):
        css = []
        z_next = logits(0, k0)
        for hh in range(nh):
            z = z_next
            if hh + 1 < nh:
                z_next = logits(hh + 1, k0)
            css.append(block(hh, k0, diagonal, z))
        return alive(css)

    first = all_heads(pl.multiple_of(i * t, t), True)

    def cond(c):
        return jnp.logical_and(c[0] <= i, c[1] > 0)

    def body(c):
        return c[0] + 1, all_heads(pl.multiple_of((i - c[0]) * t, t), False)

    lax.while_loop(cond, body, (jnp.int32(1), first))
    o_ref[0] = jnp.concatenate([acc_scr[hh] for hh in range(nh)], axis=-1).astype(o_ref.dtype)


def _sb_attention(q, k, v):
    B, H, L, d = q.shape
    t = min(SB_BLOCK, L)
    nh = SB_HEADS_PER_STEP
    return pl.pallas_call(
        _sb_attn_body,
        grid=(B, H // nh, L // t),
        in_specs=[pl.BlockSpec((1, nh, t, d), lambda b, h, i: (b, h, i, 0)),
                  pl.BlockSpec((1, nh, L, d), lambda b, h, i: (b, h, 0, 0)),
                  pl.BlockSpec((1, nh, L, d), lambda b, h, i: (b, h, 0, 0))],
        out_specs=pl.BlockSpec((1, t, nh * d), lambda b, h, i: (b, i, h)),
        out_shape=jax.ShapeDtypeStruct((B, L, H * d), BF16),
        scratch_shapes=[pltpu.VMEM((nh, t, d), F32), pltpu.VMEM((nh, t, 1), F32)],
        compiler_params=_params("parallel", "parallel", "arbitrary"),
        name="sb_attention",
    )(q, k, v)


def _ssm_tables(lam_re, lam_im, log_dt, b_re, b_im, c_re, c_im, d_skip):
    T, G, P, H = SSM_CHUNK, SSM_GROUPS, SSM_STATE, SSM_GROUP
    hp = lax.Precision.HIGHEST
    dt = jnp.exp(log_dt.astype(F32))[:, None]
    lr = lam_re.astype(F32)
    li = lam_im.astype(F32)
    mag = jnp.exp(lr * dt)
    a_re = mag * jnp.cos(li * dt)
    a_im = mag * jnp.sin(li * dt)
    den = lr * lr + li * li
    nr = a_re - 1.0
    f_re = (nr * lr + a_im * li) / den
    f_im = (a_im * lr - nr * li) / den
    br = b_re.astype(F32)
    bi = b_im.astype(F32)
    bb_re = f_re[..., None] * br - f_im[..., None] * bi
    bb_im = f_re[..., None] * bi + f_im[..., None] * br
    tau = jnp.arange(T + 1, dtype=F32)[None, :, None]
    pmag = jnp.exp(tau * (lr * dt)[:, None, :])
    pw_re = pmag * jnp.cos(tau * (li * dt)[:, None, :])
    pw_im = pmag * jnp.sin(tau * (li * dt)[:, None, :])
    cr = c_re.astype(F32)[:, None]
    ci = c_im.astype(F32)[:, None]
    cpw_re = cr * pw_re[:, :, None, :] - ci * pw_im[:, :, None, :]
    cpw_im = cr * pw_im[:, :, None, :] + ci * pw_re[:, :, None, :]
    kern = (jnp.einsum('gthp,gpi->gthi', cpw_re, bb_re, precision=hp)
            - jnp.einsum('gthp,gpi->gthi', cpw_im, bb_im, precision=hp))
    s_idx = np.arange(T)[:, None]
    t_idx = np.arange(T)[None, :]
    lag = np.clip(t_idx - s_idx, 0, T)
    toe = kern[:, lag]
    toe = jnp.where(jnp.asarray(t_idx >= s_idx)[None, :, :, None, None], toe, 0.0)
    intra = toe.transpose(0, 1, 4, 2, 3).reshape(G, T * H, T * H)
    rev = np.arange(T - 1, -1, -1)
    qr = pw_re[:, rev][:, :, None, :]
    qi = pw_im[:, rev][:, :, None, :]
    bbr = bb_re.transpose(0, 2, 1)[:, None]
    bbi = bb_im.transpose(0, 2, 1)[:, None]
    st_re = (qr * bbr - qi * bbi).reshape(G, T * H, P)
    st_im = (qr * bbi + qi * bbr).reshape(G, T * H, P)
    to_st = jnp.concatenate([st_re, st_im, st_im, st_re], axis=-1)
    top = cpw_re[:, 1:].transpose(0, 3, 1, 2).reshape(G, P, T * H)
    bot = (-cpw_im[:, 1:]).transpose(0, 3, 1, 2).reshape(G, P, T * H)
    from_st = jnp.concatenate([top, bot], axis=1)
    at_re = pw_re[:, T]
    at_im = pw_im[:, T]
    zeros = jnp.zeros_like(at_re)
    adv = jnp.stack([jnp.concatenate([at_re, at_re], -1),
                     jnp.concatenate([-at_im, at_im], -1),
                     jnp.concatenate([at_im, -at_im], -1)]
                    + [jnp.concatenate([zeros, zeros], -1)] * 5, axis=1)
    dvec = jnp.tile(d_skip.astype(F32).reshape(G, 1, H), (1, 1, T))
    return intra, to_st, from_st, adv, dvec


def _ssm_expand(tables):
    intra, to_st, from_st, adv, dvec = tables
    V, W, T, H, P = SSM_LANE_TILES, SSM_TILE_GROUPS, SSM_CHUNK, SSM_GROUP, SSM_STATE
    same = jnp.eye(W, dtype=bool)
    blk = lambda t, shape: jnp.where(same.reshape(shape), t, 0.0).astype(BF16)
    it = intra.reshape(V, W, T, H, T, H).transpose(0, 2, 1, 3, 4, 5).reshape(V, T, W, H, T, 1, H)
    m8 = blk(it, (1, 1, W, 1, 1, W, 1)).reshape(V, T * W * H, T * W * H)
    ts = to_st.reshape(V, W, T, H, 2, 2 * P).transpose(0, 2, 1, 3, 4, 5).reshape(V, T, W, H, 2, 1, 2 * P)
    ws8 = blk(ts, (1, 1, W, 1, 1, W, 1)).reshape(V, T * W * H, 2 * W * 2 * P)
    fs = from_st.reshape(V, W, 2 * P, T, 1, H)
    wi8 = blk(fs, (1, W, 1, 1, W, 1)).reshape(V, W * 2 * P, T * W * H)
    adv8 = adv.reshape(V, W, 8, 2 * P).transpose(0, 2, 1, 3).reshape(V, 8, W * 2 * P)
    d8 = jnp.broadcast_to(dvec.reshape(V, 1, W, T, H)[:, :, :, 0:1], (V, 1, W, T, H))
    d8 = d8.transpose(0, 1, 3, 2, 4).reshape(V, 1, T * W * H)
    return m8, ws8, wi8, adv8, d8


def _ssm_body(u_ref, m8_ref, ws8_ref, wi8_ref, adv_ref, dvec_ref, y_ref, s_scr, xin_scr, x1_scr, x2_scr, *, bsz):
    half = x1_scr.shape[1]

    @pl.when(pl.program_id(1) == 0)
    def _():
        x1_scr[...] = jnp.zeros_like(x1_scr)
        x2_scr[...] = jnp.zeros_like(x2_scr)

    u = u_ref[0]
    ub = u.astype(BF16)
    s_scr[...] = _dot(ub, ws8_ref[0])
    a1 = adv_ref[0, 0:1, :]
    a2 = adv_ref[0, 1:2, :]
    a3 = adv_ref[0, 2:3, :]

    def step(c, carry):
        x1, x2 = carry
        r0 = pl.multiple_of(c * bsz, bsz)
        xin_scr[pl.ds(r0, bsz), :] = x1
        s = s_scr[pl.ds(r0, bsz), :]
        return (a1 * x1 + a2 * x2 + s[:, :half], a1 * x2 + a3 * x1 + s[:, half:])

    x1, x2 = lax.fori_loop(0, u.shape[0] // bsz, step, (x1_scr[...], x2_scr[...]))
    x1_scr[...] = x1
    x2_scr[...] = x2
    y_ref[0] = _dot(ub, m8_ref[0]) + _dot(xin_scr[...].astype(BF16), wi8_ref[0]) + dvec_ref[0] * u


def _ssm(u5, tables, rows=256):
    m8, ws8, wi8, adv8, d8 = _ssm_expand(tables)
    V, C, B, T, _ = u5.shape
    width = T * 128
    half = SSM_TILE_GROUPS * 2 * SSM_STATE
    rows = min(rows, C * B)
    once = dict(pipeline_mode=pl.Buffered(1))
    y = pl.pallas_call(
        functools.partial(_ssm_body, bsz=B),
        grid=(V, C * B // rows),
        in_specs=[pl.BlockSpec((1, rows, width), lambda v, r: (v, r, 0)),
                  pl.BlockSpec((1, width, width), lambda v, r: (v, 0, 0), **once),
                  pl.BlockSpec((1, width, 2 * half), lambda v, r: (v, 0, 0), **once),
                  pl.BlockSpec((1, half, width), lambda v, r: (v, 0, 0), **once),
                  pl.BlockSpec((1, 8, half), lambda v, r: (v, 0, 0)),
                  pl.BlockSpec((1, 1, width), lambda v, r: (v, 0, 0))],
        out_specs=pl.BlockSpec((1, rows, width), lambda v, r: (v, r, 0)),
        out_shape=jax.ShapeDtypeStruct((V, C * B, width), F32),
        scratch_shapes=[pltpu.VMEM((rows, 2 * half), F32), pltpu.VMEM((rows, half), F32),
                        pltpu.VMEM((B, half), F32), pltpu.VMEM((B, half), F32)],
        compiler_params=_params("parallel", "arbitrary"),
        name="s5_scan",
    )(u5.reshape(V, C * B, width), m8, ws8, wi8, adv8, d8)
    return y.reshape(V, C, B, T, 128)


def _outproj0_body(x_ref, oa_ref, y_ref, wglu_ref, woa_ref, wob_ref, g_ref, b_ref, o_ref):
    tl = x_ref.shape[1]
    y = jnp.concatenate([y_ref[v, :, 0].reshape(tl, 128) for v in range(SSM_LANE_TILES)], axis=-1)
    h = _gelu(y)
    ob = h * _sigmoid(_dot(h.astype(BF16), wglu_ref[...]))
    m = _dot(oa_ref[0], woa_ref[...]) + _dot(ob.astype(BF16), wob_ref[...])
    o_ref[0] = _layer_norm(ALPHA * x_ref[0] + m, g_ref[...], b_ref[...])


def _outproj0(x, oa, y5, wglu, woa, wob, g, b, tl=512):
    B, L, D = x.shape
    tl = min(tl, L)
    row = lambda b, i: (b, i, 0)
    fix = lambda b, i: (0, 0)
    return pl.pallas_call(
        _outproj0_body,
        grid=(B, L // tl),
        in_specs=[pl.BlockSpec((1, tl, D), row), pl.BlockSpec((1, tl, SB_WIDTH), row),
                  pl.BlockSpec((SSM_LANE_TILES, tl // SSM_CHUNK, 1, SSM_CHUNK, 128), lambda b, i: (0, i, b, 0, 0)),
                  pl.BlockSpec((SSM_WIDTH, SSM_WIDTH), fix), pl.BlockSpec((SB_WIDTH, D), fix),
                  pl.BlockSpec((SSM_WIDTH, D), fix), pl.BlockSpec((1, D), fix), pl.BlockSpec((1, D), fix)],
        out_specs=pl.BlockSpec((1, tl, D), row),
        out_shape=jax.ShapeDtypeStruct((B, L, D), F32),
        compiler_params=_params("parallel", "parallel"),
        name="outproj0",
    )(x, oa, y5, wglu, woa, wob, g, b)


def _outproj1_body(x_ref, o_ref_in, w_ref, g_ref, b_ref, o_ref):
    m = _dot(o_ref_in[...].astype(BF16), w_ref[...])
    o_ref[...] = _layer_norm(ALPHA * x_ref[...] + m, g_ref[...], b_ref[...])


def _outproj1(x, o, w, g, b, tm=512):
    N, D = x.shape
    tm = min(tm, N)
    row = lambda i: (i, 0)
    fix = lambda i: (0, 0)
    return pl.pallas_call(
        _outproj1_body,
        grid=(N // tm,),
        in_specs=[pl.BlockSpec((tm, D), row), pl.BlockSpec((tm, o.shape[1]), row),
                  pl.BlockSpec(w.shape, fix), pl.BlockSpec((1, D), fix), pl.BlockSpec((1, D), fix)],
        out_specs=pl.BlockSpec((tm, D), row),
        out_shape=jax.ShapeDtypeStruct((N, D), F32),
        compiler_params=_params("parallel"),
        name="outproj1",
    )(x, o, w, g, b)


def _router_body(x_ref, wh_ref, wl_ref, b_ref, g_ref, grp_ref):
    xh, xl = _split_bf16(x_ref[...])
    wh = wh_ref[...]
    logits = _dot_nt(wh, xh) + _dot_nt(wh, xl) + _dot_nt(wl_ref[...], xh)
    aff = _sigmoid(logits)
    sel = aff + b_ref[...]
    E, K = EXPERTS_PER_GROUP, N_GROUPS
    s = [sel[e:e + 1, :] for e in range(N_EXPERTS)]
    a = [aff[e:e + 1, :] for e in range(N_EXPERTS)]
    gscore = []
    for k in range(K):
        v = s[k * E:(k + 1) * E]
        best = None
        for x in range(E):
            for y in range(x + 1, E):
                pair = v[x] + v[y]
                best = pair if best is None else jnp.maximum(best, pair)
        gscore.append(best)
    top = functools.reduce(jnp.maximum, gscore)
    is_g = []
    taken = None
    for k in range(K):
        hit = gscore[k] == top
        if taken is None:
            is_g.append(hit)
            taken = hit
        else:
            is_g.append(jnp.logical_and(hit, jnp.logical_not(taken)))
            taken = jnp.logical_or(taken, hit)

    def pick(rows, j):
        out = rows[(K - 1) * E + j]
        for k in range(K - 2, -1, -1):
            out = jnp.where(is_g[k], rows[k * E + j], out)
        return out

    v = [pick(s, j) for j in range(E)]
    av = [pick(a, j) for j in range(E)]
    chosen = []
    for j in range(E):
        r = jnp.zeros_like(v[j])
        for j2 in range(E):
            if j2 == j:
                continue
            ahead = (v[j2] >= v[j]) if j2 < j else (v[j2] > v[j])
            r = r + jnp.where(ahead, 1.0, 0.0)
        chosen.append(r < 2.0)
    wj = [jnp.where(chosen[j], av[j], 0.0) for j in range(E)]
    tot = wj[0] + wj[1] + wj[2] + wj[3]
    gj = [w / tot for w in wj]
    rows = [jnp.where(is_g[e // E], gj[e % E], 0.0) for e in range(N_EXPERTS)]
    g_ref[...] = jnp.concatenate(rows, axis=0)
    grp_ref[...] = jnp.concatenate([jnp.where(is_g[k], 1.0, 0.0) for k in range(K)], axis=0)


def _router(x, w_router, b_router, tm=512):
    N, D = x.shape
    tm = min(tm, N)
    wt = w_router.astype(F32).T
    wh, wl = _split_bf16(wt)
    fix = lambda i: (0, 0)
    return pl.pallas_call(
        _router_body,
        grid=(N // tm,),
        in_specs=[pl.BlockSpec((tm, D), lambda i: (i, 0)), pl.BlockSpec((N_EXPERTS, D), fix),
                  pl.BlockSpec((N_EXPERTS, D), fix), pl.BlockSpec((N_EXPERTS, 1), fix)],
        out_specs=[pl.BlockSpec((N_EXPERTS, tm), lambda i: (0, i)), pl.BlockSpec((N_GROUPS, tm), lambda i: (0, i))],
        out_shape=[jax.ShapeDtypeStruct((N_EXPERTS, N), F32), jax.ShapeDtypeStruct((N_GROUPS, N), F32)],
        compiler_params=_params("parallel"),
        name="router",
    )(x, wh, wl, b_router.astype(F32).reshape(N_EXPERTS, 1))


MOE_TILE = 1024
MOE_PASS_ROWS = 304
MOE_CHUNK = 256
MOE_EXPERTS_PER_STEP = 2
ROW_ALIGN = 16


def _moe_body(seg_ref, x_ref, gt_ref, grp_ref, w1_ref, w3_ref, w2_ref, g_ref, b_ref, o_ref,
              xs_scr, ys_scr, pt_scr, gs_scr):
    i = pl.program_id(0)
    e = pl.program_id(1)
    T = x_ref.shape[0]
    K = N_GROUPS
    cap = MOE_PASS_ROWS
    ch = min(MOE_CHUNK, T)

    @pl.when(e == 0)
    def _():
        grp = grp_ref[...]
        grp_b = grp.astype(BF16)
        xb = x_ref[...].astype(BF16)
        gates = gt_ref[...]
        gh, gm = _split_bf16(gates)
        gl = (gates - gh.astype(F32) - gm.astype(F32)).astype(BF16)
        cnt = jnp.sum(grp, axis=1, keepdims=True)
        off = [jnp.zeros((1, 1), F32)]
        for k in range(K - 1):
            off.append(off[-1] + cnt[k:k + 1])
        r_i = lax.broadcasted_iota(jnp.int32, (ch, T), 0)
        c_i = lax.broadcasted_iota(jnp.int32, (ch, T), 1)
        pos_row = []
        pos_col = []
        for c in range(T // ch):
            c0 = c * ch
            earlier = jnp.where(c_i < r_i + c0, 1.0, 0.0).astype(BF16)
            same = jnp.where(c_i == r_i + c0, 1.0, 0.0).astype(BF16)
            before_c = _dot_nt(earlier, grp_b)
            member_c = _dot_nt(same, grp_b)
            before_r = _dot_nt(grp_b, earlier)
            pc = jnp.zeros((ch, 1), F32)
            pr = jnp.zeros((1, ch), F32)
            for k in range(K):
                pc = pc + member_c[:, k:k + 1] * (off[k] + before_c[:, k:k + 1])
                pr = pr + grp[k:k + 1, c0:c0 + ch] * (off[k] + before_r[k:k + 1, :])
            pos_col.append(pc)
            pos_row.append(pr)
        pos_row = jnp.concatenate(pos_row, axis=1)
        for c in range(T // ch):
            c0 = c * ch
            perm = jnp.where((r_i + c0).astype(F32) == pos_row, 1.0, 0.0).astype(BF16)
            xs_scr[c0:c0 + ch, :] = _dot(perm, xb).astype(BF16)
            gs_scr[c0:c0 + ch, :] = _dot_nt(perm, gh) + _dot_nt(perm, gm) + _dot_nt(perm, gl)
            pt_scr[c0:c0 + ch, :] = jnp.where(c_i.astype(F32) == pos_col[c], 1.0, 0.0).astype(BF16)
        xs_scr[T:, :] = jnp.zeros((cap, xs_scr.shape[1]), BF16)
        gs_scr[T:, :] = jnp.zeros((cap, gs_scr.shape[1]), F32)
        ys_scr[...] = jnp.zeros_like(ys_scr)

    ne = w1_ref.shape[0]
    k = (e * ne) // EXPERTS_PER_GROUP
    off = seg_ref[i, k]
    cnt = seg_ref[i, K + k]
    start0 = (off // ROW_ALIGN) * ROW_ALIGN
    n_pass = (off + cnt - start0 + cap - 1) // cap

    def one_pass(n, c):
        start = pl.multiple_of(start0 + n * cap, ROW_ALIGN)
        xc = xs_scr[pl.ds(start, cap), :]
        gsc = gs_scr[pl.ds(start, cap), :]
        lane = lax.broadcasted_iota(jnp.int32, gsc.shape, 1)
        y = None
        for x in range(ne):
            a = _dot(xc, w1_ref[x])
            hm = (a * _sigmoid(a)) * _dot(xc, w3_ref[x])
            gcol = jnp.sum(jnp.where(lane == e * ne + x, gsc, 0.0), axis=1, keepdims=True)
            yx = gcol * _dot(hm.astype(BF16), w2_ref[x])
            y = yx if y is None else y + yx
        ys_scr[pl.ds(start, cap), :] += y
        return c

    lax.fori_loop(0, n_pass, one_pass, 0)

    @pl.when(e == pl.num_programs(1) - 1)
    def _():
        y = _dot(pt_scr[...], ys_scr[0:T, :].astype(BF16))
        o_ref[...] = _layer_norm(ALPHA * x_ref[...] + y, g_ref[...], b_ref[...])


def _moe(x, gates_t, grp_t, w1, w3, w2, g, b):
    N, D = x.shape
    T = min(MOE_TILE, N)
    K = N_GROUPS
    cnt = grp_t.reshape(K, N // T, T).sum(-1).astype(jnp.int32).T
    seg = jnp.concatenate([jnp.cumsum(cnt, axis=1) - cnt, cnt], axis=1)
    fix = lambda i, e, s: (0, 0)
    ne = MOE_EXPERTS_PER_STEP
    rows = T + MOE_PASS_ROWS
    grid_spec = pltpu.PrefetchScalarGridSpec(
        num_scalar_prefetch=1,
        grid=(N // T, N_EXPERTS // ne),
        in_specs=[pl.BlockSpec((T, D), lambda i, e, s: (i, 0)),
                  pl.BlockSpec((N_EXPERTS, T), lambda i, e, s: (0, i)),
                  pl.BlockSpec((K, T), lambda i, e, s: (0, i)),
                  pl.BlockSpec((ne, D, D_EXPERT), lambda i, e, s: (e, 0, 0)),
                  pl.BlockSpec((ne, D, D_EXPERT), lambda i, e, s: (e, 0, 0)),
                  pl.BlockSpec((ne, D_EXPERT, D), lambda i, e, s: (e, 0, 0)),
                  pl.BlockSpec((1, D), fix), pl.BlockSpec((1, D), fix)],
        out_specs=pl.BlockSpec((T, D), lambda i, e, s: (i, 0)),
        scratch_shapes=[pltpu.VMEM((rows, D), BF16), pltpu.VMEM((rows, D), F32),
                        pltpu.VMEM((T, T), BF16), pltpu.VMEM((rows, N_EXPERTS), F32)])
    return pl.pallas_call(
        _moe_body,
        grid_spec=grid_spec,
        out_shape=jax.ShapeDtypeStruct((N, D), F32),
        compiler_params=_params("parallel", "arbitrary"),
        name="moe",
    )(seg, x, gates_t, grp_t, w1, w3, w2, g, b)


NSA_KVW = NSA_KV_HEADS * HEAD_DIM
NSA_ROPE_W = NSA_HEADS * HEAD_DIM + 2 * NSA_KVW
NSA_Q_SCALE = ATTN_SCALE * math.log2(math.e)


def _rot_cols(w):
    K, n = w.shape
    w3 = w.reshape(K, n // HEAD_DIM, 2, HEAD_DIM // 2)
    return jnp.stack([-w3[:, :, 1], w3[:, :, 0]], axis=2).reshape(K, n)


def _inproj1_weights(w_in):
    H, G, d = NSA_HEADS, NSA_KV_HEADS, HEAD_DIM
    cuts = [H * d + i * NSA_KVW for i in range(7)]
    q, kc, vc, ks, vs, kw, vw, gate = jnp.split(w_in.astype(F32), cuts, axis=1)
    rope = jnp.concatenate([q, ks, kw], axis=1)
    gate = jnp.pad(gate, ((0, 0), (0, 128 - gate.shape[1])))
    w_all = jnp.concatenate([rope, _rot_cols(rope), kc, vc, gate], axis=1).astype(BF16)
    w_vt = jnp.concatenate([vs, vw], axis=1).T.astype(BF16)
    return w_all, w_vt


def _rope_tables(pos):
    inv = ROPE_THETA ** (-jnp.arange(0, HEAD_DIM, 2, dtype=F32) / HEAD_DIM)
    ang = pos.astype(F32)[:, None] * inv[None, :]
    c, s = jnp.cos(ang), jnp.sin(ang)
    return jnp.concatenate([c, c], axis=-1), jnp.concatenate([s, s], axis=-1)


V_ROWS = HEAD_DIM + 16


def _inproj1_body(x_ref, w_ref, wvt_ref, cos_ref, sin_ref, q_ref, ks_ref, kw_ref, kc_ref, vc_ref, vst_ref, vwt_ref,
                  gate_ref):
    xb = x_ref[0].astype(BF16)
    p = _dot(xb, w_ref[...])
    vt = _dot_nt(wvt_ref[...], xb)
    ones = jnp.ones((V_ROWS - HEAD_DIM, xb.shape[0]), BF16)
    for n, ref in enumerate((vst_ref, vwt_ref)):
        for g in range(NSA_KV_HEADS):
            r0 = n * NSA_KVW + g * HEAD_DIM
            ref[0, g, 0:HEAD_DIM, :] = vt[r0:r0 + HEAD_DIM, :].astype(BF16)
            ref[0, g, HEAD_DIM:V_ROWS, :] = ones
    R = NSA_ROPE_W
    cos = cos_ref[...]
    sin = sin_ref[...]
    roped = []
    for c in range(R // 128):
        roped.append(p[:, c * 128:(c + 1) * 128] * cos + p[:, R + c * 128:R + (c + 1) * 128] * sin)
    d = HEAD_DIM

    def head(chunks, h):
        blk = chunks[h // 2]
        return blk[:, (h % 2) * d:(h % 2 + 1) * d]

    for h in range(NSA_HEADS):
        q_ref[0, h] = (head(roped, h) * NSA_Q_SCALE).astype(BF16)
    nq = NSA_HEADS // 2
    for g in range(NSA_KV_HEADS):
        ks_ref[0, g] = head(roped[nq:], g).astype(BF16)
        kw_ref[0, g] = head(roped[nq + 2:], g).astype(BF16)
    base = 2 * R
    for n, ref in enumerate((kc_ref, vc_ref)):
        for g in range(NSA_KV_HEADS):
            c0 = base + n * NSA_KVW + g * d
            ref[0, g] = p[:, c0:c0 + d].astype(BF16)
    gate_ref[0] = p[:, base + 2 * NSA_KVW:]


def _inproj1(x, w_all, w_vt, cos2, sin2, tl=256):
    B, L, D = x.shape
    tl = min(tl, L)
    nout = w_all.shape[1]
    G = NSA_KV_HEADS
    qh = jax.ShapeDtypeStruct((B, NSA_HEADS, L, HEAD_DIM), BF16)
    kvh = jax.ShapeDtypeStruct((B, G, L, HEAD_DIM), BF16)
    vth = jax.ShapeDtypeStruct((B, G, V_ROWS, L), BF16)
    q_spec = pl.BlockSpec((1, NSA_HEADS, tl, HEAD_DIM), lambda b, i: (b, 0, i, 0))
    kv_spec = pl.BlockSpec((1, G, tl, HEAD_DIM), lambda b, i: (b, 0, i, 0))
    vt_spec = pl.BlockSpec((1, G, V_ROWS, tl), lambda b, i: (b, 0, 0, i))
    return pl.pallas_call(
        _inproj1_body,
        grid=(B, L // tl),
        in_specs=[pl.BlockSpec((1, tl, D), lambda b, i: (b, i, 0)),
                  pl.BlockSpec((D, nout), lambda b, i: (0, 0)),
                  pl.BlockSpec(w_vt.shape, lambda b, i: (0, 0)),
                  pl.BlockSpec((tl, 128), lambda b, i: (i, 0)),
                  pl.BlockSpec((tl, 128), lambda b, i: (i, 0))],
        out_specs=[q_spec] + [kv_spec] * 4 + [vt_spec] * 2 + [pl.BlockSpec((1, tl, 128), lambda b, i: (b, i, 0))],
        out_shape=[qh] + [kvh] * 4 + [vth] * 2 + [jax.ShapeDtypeStruct((B, L, 128), F32)],
        compiler_params=_params("parallel", "parallel"),
        name="inproj1",
    )(x, w_all, w_vt, cos2, sin2)


def _compress_body(kc_ref, vc_ref, posk_ref, posv_ref, w1k_ref, w1v_ref, w2k_ref, w2kr_ref, w2v_ref,
                   cos_ref, sin_ref, kcmp_ref, vcmp_ref):
    def hidden(a_ref, pos_ref, w1_ref):
        a = a_ref[0, 0].astype(F32)
        nrow = a.shape[0]
        lo = _dot((a + pos_ref[0:1, :]).astype(BF16), w1_ref[0])
        hi = _dot((a + pos_ref[1:2, :]).astype(BF16), w1_ref[1])
        hi_next = pltpu.roll(hi, nrow - 1, 0)
        return _gelu(lo + hi_next).astype(BF16)

    hk = hidden(kc_ref, posk_ref, w1k_ref)
    kcmp = _dot(hk, w2k_ref[...]) * cos_ref[...] + _dot(hk, w2kr_ref[...]) * sin_ref[...]
    kcmp_ref[0, 0] = kcmp.astype(BF16)
    hv = hidden(vc_ref, posv_ref, w1v_ref)
    vcmp_ref[0, 0] = _dot_nt(w2v_ref[...], hv).astype(BF16)


def _compress(kc, vc, pos_k, w1_k, w2_k, pos_v, w1_v, w2_v):
    B, G, L, d = kc.shape
    half = CMP_STRIDE * d
    nb = L // CMP_STRIDE
    kc2 = kc.reshape(B, G, nb, half)
    vc2 = vc.reshape(B, G, nb, half)
    posk = pos_k.astype(F32).reshape(2, half)
    posv = pos_v.astype(F32).reshape(2, half)
    w1k = w1_k.astype(BF16).reshape(2, half, CMP_HIDDEN)
    w1v = w1_v.astype(BF16).reshape(2, half, CMP_HIDDEN)
    w2k = w2_k.astype(F32)
    cos, sin = _rope_tables(jnp.arange(nb) * CMP_STRIDE + CMP_LEN - 1)
    blk = pl.BlockSpec((1, 1, nb, half), lambda b, g: (b, g, 0, 0))
    out = pl.BlockSpec((1, 1, nb, d), lambda b, g: (b, g, 0, 0))
    out_t = pl.BlockSpec((1, 1, d, nb), lambda b, g: (b, g, 0, 0))
    fix2 = lambda b, g: (0, 0)
    fix3 = lambda b, g: (0, 0, 0)
    return pl.pallas_call(
        _compress_body,
        grid=(B, G),
        in_specs=[blk, blk, pl.BlockSpec((2, half), fix2), pl.BlockSpec((2, half), fix2),
                  pl.BlockSpec((2, half, CMP_HIDDEN), fix3), pl.BlockSpec((2, half, CMP_HIDDEN), fix3),
                  pl.BlockSpec((CMP_HIDDEN, d), fix2), pl.BlockSpec((CMP_HIDDEN, d), fix2),
                  pl.BlockSpec((d, CMP_HIDDEN), fix2), pl.BlockSpec((nb, d), fix2), pl.BlockSpec((nb, d), fix2)],
        out_specs=[out, out_t],
        out_shape=[jax.ShapeDtypeStruct((B, G, nb, d), BF16), jax.ShapeDtypeStruct((B, G, d, nb), BF16)],
        compiler_params=_params("parallel", "parallel"),
        name="compress",
    )(kc2, vc2, posk, posv, w1k, w1v, w2k.astype(BF16), _rot_cols(w2k).astype(BF16), w2_v.T.astype(BF16), cos, sin)


NSA_KBLOCK = 256
NSA_GROUPS_PER_STEP = 4
NSA_HEADS_PER_CHAIN = 4


def _nsa_gate_layout(gate):
    B, L, _ = gate.shape
    G, R, tq = NSA_KV_HEADS, NSA_REP, min(ATT_BLOCK, L)
    g5 = gate[:, :, :3 * NSA_HEADS].reshape(B, L // tq, tq, G, R, 3)
    return g5.transpose(0, 3, 1, 5, 4, 2).reshape(B, G, L // tq, 3, R * tq)


def _nsa_body(q_ref, kcmp_ref, vcmpt_ref, ks_ref, vst_ref, kw_ref, vwt_ref, gate_ref, o_ref, *, seq):
    i = pl.program_id(2)
    tq = q_ref.shape[2]
    gb = kcmp_ref.shape[1]
    tk = min(NSA_KBLOCK, seq)
    R, d = NSA_REP, HEAD_DIM
    nq = R * tq
    nb = seq // SEL_LEN
    mc = seq // CMP_STRIDE
    t_row = i * tq + lax.broadcasted_iota(jnp.int32, (1, tq), 1)
    t_all = jnp.concatenate([t_row] * R, axis=1)
    m_col = lax.broadcasted_iota(jnp.int32, (mc, 1), 0)
    m_row = lax.broadcasted_iota(jnp.int32, (1, mc), 1)
    n_col = lax.broadcasted_iota(jnp.int32, (nb, 1), 0)

    def q_rows(g):
        return q_ref[0, g * R:(g + 1) * R].reshape(nq, d)

    valid_c = (m_col * CMP_STRIDE + (CMP_LEN - 1)) <= t_all
    ovl = jnp.logical_and(m_row * CMP_STRIDE < (n_col + 1) * SEL_LEN,
                          m_row * CMP_STRIDE + CMP_LEN > n_col * SEL_LEN)
    ovl = jnp.where(ovl, 1.0, 0.0).astype(BF16)
    cur = t_row // SEL_LEN
    forced = jnp.logical_or(n_col == 0, jnp.logical_or(n_col == cur, n_col == cur - 1))
    bonus = jnp.where(forced, FORCE_BONUS, 0.0)
    valid_s = n_col * SEL_LEN <= t_row
    o_c, sel = [], []
    for g in range(gb):
        s = jnp.where(valid_c, _dot_nt(kcmp_ref[0, g], q_rows(g)), NEG)
        e = jnp.where(valid_c, jnp.exp2(s - jnp.max(s, axis=0, keepdims=True)), 0.0)
        den = jnp.sum(e, axis=0, keepdims=True)
        p = e / jnp.where(den > 0.0, den, 1.0)
        o_c.append(_dot(vcmpt_ref[0, g], p.astype(BF16)))
        psum = p[:, 0:tq]
        for r in range(1, R):
            psum = psum + p[:, r * tq:(r + 1) * tq]
        ph, plo = _split_bf16(psum)
        score = jnp.where(valid_s, _dot(ovl, ph) + _dot(ovl, plo) + bonus, NEG)
        rank = jnp.zeros((nb, tq), F32)
        for n2 in range(nb):
            other = score[n2:n2 + 1, :]
            tie = jnp.where(n_col > n2, 1.0, 0.0)
            rank = rank + jnp.where(other > score, 1.0, jnp.where(other == score, tie, 0.0))
        sel.append(jnp.where(rank < float(SEL_TOP), 1.0, 0.0).astype(BF16))

    k_col = lax.broadcasted_iota(jnp.int32, (tk, 1), 0)
    hp = NSA_HEADS_PER_CHAIN
    j_hi = (i * tq + tq - 1) // tk + 1

    def sweep(k_ref, vt_ref, j_lo, bias_fn):
        chains = [(g, h0) for g in range(gb) for h0 in range(0, R, hp)]

        def body(j, state):
            k0 = pl.multiple_of(j * tk, tk)
            biases = bias_fn(k0 + k_col)
            def scores(c):
                g, h0 = chains[c]
                kb = k_ref[0, g, pl.ds(k0, tk), :]
                qh = q_ref[0, g * R + h0:g * R + h0 + hp].reshape(hp * tq, d)
                return _dot_nt(kb, qh) + jnp.concatenate([biases[g]] * hp, axis=1)

            new_state = []
            sc_next = scores(0)
            for c, (m_run, acc) in enumerate(state):
                sc = sc_next
                if c + 1 < len(chains):
                    sc_next = scores(c + 1)
                vtb = vt_ref[0, chains[c][0], :, pl.ds(k0, tk)]
                m_new = jnp.maximum(m_run, jnp.max(sc, axis=0, keepdims=True))
                pr = jnp.exp2(sc - m_new)
                new_state.append((m_new, jnp.exp2(m_run - m_new) * acc + _dot(vtb, pr.astype(BF16))))
            return tuple(new_state)

        init = tuple((jnp.full((1, hp * tq), NEG, F32), jnp.zeros((V_ROWS, hp * tq), F32)) for _ in chains)
        state = lax.fori_loop(j_lo, j_hi, body, init)
        outs = []
        for g in range(gb):
            acc = jnp.concatenate([a for (cg, _), (_, a) in zip(chains, state) if cg == g], axis=1)
            outs.append(acc[0:d] / acc[d:d + 1])
        return outs

    def sel_bias(kpos):
        pick = jnp.where(kpos // SEL_LEN == lax.broadcasted_iota(jnp.int32, (1, nb), 1), 1.0, 0.0).astype(BF16)
        causal = kpos <= t_row
        return [jnp.where(jnp.logical_and(_dot(pick, sel[g]) > 0.5, causal), 0.0, NEG) for g in range(gb)]

    o_s = sweep(ks_ref, vst_ref, 0, sel_bias)

    def win_bias(kpos):
        return [jnp.where(jnp.logical_and(kpos <= t_row, kpos > t_row - WINDOW), 0.0, NEG)] * gb

    o_w = sweep(kw_ref, vwt_ref, jnp.maximum(i * tq - (WINDOW - 1), 0) // tk, win_bias)

    outs = []
    for g in range(gb):
        sg = _sigmoid(gate_ref[0, g, 0])
        o_t = sg[0:1] * o_c[g] + sg[1:2] * o_s[g] + sg[2:3] * o_w[g]
        for r in range(R):
            outs.append(o_t[:, r * tq:(r + 1) * tq].T)
    o_ref[0] = jnp.concatenate(outs, axis=-1)


def _nsa_attention(q, kcmp, vcmpt, ks, vst, kw, vwt, gate):
    B, H, L, d = q.shape
    G, R = NSA_KV_HEADS, NSA_REP
    gb = NSA_GROUPS_PER_STEP
    tq = min(ATT_BLOCK, L)
    mc = L // CMP_STRIDE
    k_spec = pl.BlockSpec((1, gb, L, d), lambda b, g, i: (b, g, 0, 0))
    vt_spec = pl.BlockSpec((1, gb, V_ROWS, L), lambda b, g, i: (b, g, 0, 0))
    return pl.pallas_call(
        functools.partial(_nsa_body, seq=L),
        grid=(B, G // gb, L // tq),
        in_specs=[pl.BlockSpec((1, gb * R, tq, d), lambda b, g, i: (b, g, i, 0)),
                  pl.BlockSpec((1, gb, mc, d), lambda b, g, i: (b, g, 0, 0)),
                  pl.BlockSpec((1, gb, d, mc), lambda b, g, i: (b, g, 0, 0)),
                  k_spec, vt_spec, k_spec, vt_spec,
                  pl.BlockSpec((1, gb, 1, 3, R * tq), lambda b, g, i: (b, g, i, 0, 0))],
        out_specs=pl.BlockSpec((1, tq, gb * R * d), lambda b, g, i: (b, i, g)),
        out_shape=jax.ShapeDtypeStruct((B, L, H * d), F32),
        compiler_params=_params("parallel", "parallel", "arbitrary"),
        name="nsa_attention",
    )(q, kcmp, vcmpt, ks, vst, kw, vwt, gate)


def kernel(x, w_in_0, ssm_lam_re, ssm_lam_im, ssm_log_dt, ssm_b_re, ssm_b_im, ssm_c_re, ssm_c_im, ssm_d, w_glu, w_out_0, ln_mix_g_0, ln_mix_b_0, ln_ffn_g_0, ln_ffn_b_0, w1_0, w3_0, w2_0, w_in_1, cmp_pos_k, cmp_w1_k, cmp_w2_k, cmp_pos_v, cmp_w1_v, cmp_w2_v, w_out_1, ln_mix_g_1, ln_mix_b_1, ln_ffn_g_1, ln_ffn_b_1, w1_1, w3_1, w2_1, w_router, b_router):
    B, L, D = x.shape
    N = B * L
    vec = lambda a: a.astype(F32).reshape(1, D)

    def ffn(h, w1, w3, w2, g, b):
        gates_t, grp_t = _router(h, w_router, b_router)
        return _moe(h, gates_t, grp_t, w1.astype(BF16), w3.astype(BF16), w2.astype(BF16), vec(g), vec(b))

    q, k, v, u = _inproj0(x, w_in_0.astype(BF16))
    o_a = _sb_attention(q, k, v)
    y = _ssm(u, _ssm_tables(ssm_lam_re, ssm_lam_im, ssm_log_dt, ssm_b_re, ssm_b_im, ssm_c_re, ssm_c_im, ssm_d))
    w_out_0b = w_out_0.astype(BF16)
    h = _outproj0(x, o_a, y, w_glu.astype(BF16), w_out_0b[:SB_WIDTH], w_out_0b[SB_WIDTH:],
                  vec(ln_mix_g_0), vec(ln_mix_b_0)).reshape(N, D)
    h = ffn(h, w1_0, w3_0, w2_0, ln_ffn_g_0, ln_ffn_b_0)

    cos, sin = _rope_tables(jnp.arange(L))
    cos2 = jnp.concatenate([cos, cos], axis=-1)
    sin2 = jnp.concatenate([sin, sin], axis=-1)
    q, ks, kw, kc, vc, vst, vwt, gate = _inproj1(h.reshape(B, L, D), *_inproj1_weights(w_in_1), cos2, sin2)
    kcmp, vcmpt = _compress(kc, vc, cmp_pos_k, cmp_w1_k, cmp_w2_k, cmp_pos_v, cmp_w1_v, cmp_w2_v)
    o = _nsa_attention(q, kcmp, vcmpt, ks, vst, kw, vwt, _nsa_gate_layout(gate)).reshape(N, NSA_HEADS * HEAD_DIM)
    h = _outproj1(h, o, w_out_1.astype(BF16), vec(ln_mix_g_1), vec(ln_mix_b_1))
    h = ffn(h, w1_1, w3_1, w2_1, ln_ffn_g_1, ln_ffn_b_1)
    return h.reshape(B, L, D)
```

```python
import functools
import math

import numpy as np
import jax
import jax.numpy as jnp
from jax import lax
from jax.experimental import pallas as pl
from jax.experimental.pallas import tpu as pltpu

F32 = jnp.float32
BF16 = jnp.bfloat16

D_MODEL = 1024
DEPTH = 2
SB_HEADS = 8
HEAD_DIM = 64
SB_WIDTH = SB_HEADS * HEAD_DIM
SSM_WIDTH = D_MODEL - SB_WIDTH
SSM_GROUP = 16
SSM_GROUPS = SSM_WIDTH // SSM_GROUP
SSM_STATE = 64
SSM_CHUNK = 16
SSM_LANE_TILES = SSM_WIDTH // 128
SSM_TILE_GROUPS = 128 // SSM_GROUP
NSA_HEADS = 16
NSA_KV_HEADS = 4
NSA_REP = NSA_HEADS // NSA_KV_HEADS
CMP_LEN = 32
CMP_STRIDE = 16
CMP_HIDDEN = 256
SEL_LEN = 64
SEL_TOP = 8
WINDOW = 512
ROPE_THETA = 10000.0
FORCE_BONUS = 1e4
NEG = -1e30
N_EXPERTS = 16
N_GROUPS = 4
EXPERTS_PER_GROUP = N_EXPERTS // N_GROUPS
D_EXPERT = 512
ALPHA = (2 * DEPTH) ** 0.25
LN_EPS = 1e-5
ATTN_SCALE = HEAD_DIM ** -0.5
ATT_BLOCK = 128
GELU_C = math.sqrt(2.0 / math.pi)


def _params(*sem):
    return pltpu.CompilerParams(dimension_semantics=sem, vmem_limit_bytes=56 * 1024 * 1024)


def _sigmoid(x):
    return 1.0 / (1.0 + jnp.exp(-x))


def _gelu(x):
    return 0.5 * x * (1.0 + jnp.tanh(GELU_C * (x + 0.044715 * (x * x * x))))


def _layer_norm(r, g, b):
    mu = jnp.mean(r, axis=-1, keepdims=True)
    d = r - mu
    var = jnp.mean(d * d, axis=-1, keepdims=True)
    return d * lax.rsqrt(var + LN_EPS) * g + b


def _dot(a, b):
    return jnp.dot(a, b, preferred_element_type=F32)


def _dot_nt(a, b):
    return lax.dot_general(a, b, (((1,), (1,)), ((), ())), preferred_element_type=F32)


def _split_bf16(x):
    hi = x.astype(BF16)
    lo = (x - hi.astype(F32)).astype(BF16)
    return hi, lo


def _inproj0_body(x_ref, w_ref, q_ref, k_ref, v_ref, u_ref):
    p = _dot(x_ref[0].astype(BF16), w_ref[...])
    for h in range(SB_HEADS):
        c = h * HEAD_DIM
        q_ref[0, h] = (p[:, c:c + HEAD_DIM] * ATTN_SCALE).astype(BF16)
        k_ref[0, h] = p[:, SB_WIDTH + c:SB_WIDTH + c + HEAD_DIM].astype(BF16)
        v_ref[0, h] = p[:, 2 * SB_WIDTH + c:2 * SB_WIDTH + c + HEAD_DIM].astype(BF16)
    tl = p.shape[0]
    for v in range(SSM_LANE_TILES):
        c = 3 * SB_WIDTH + v * 128
        u_ref[v, :, 0] = p[:, c:c + 128].reshape(tl // SSM_CHUNK, SSM_CHUNK, 128)


def _inproj0(x, w_bf16, tl=512):
    B, L, D = x.shape
    tl = min(tl, L)
    nout = w_bf16.shape[1]
    head = jax.ShapeDtypeStruct((B, SB_HEADS, L, HEAD_DIM), BF16)
    head_spec = pl.BlockSpec((1, SB_HEADS, tl, HEAD_DIM), lambda b, i: (b, 0, i, 0))
    return pl.pallas_call(
        _inproj0_body,
        grid=(B, L // tl),
        in_specs=[pl.BlockSpec((1, tl, D), lambda b, i: (b, i, 0)),
                  pl.BlockSpec((D, nout), lambda b, i: (0, 0))],
        out_specs=[head_spec, head_spec, head_spec,
                   pl.BlockSpec((SSM_LANE_TILES, tl // SSM_CHUNK, 1, SSM_CHUNK, 128), lambda b, i: (0, i, b, 0, 0))],
        out_shape=[head, head, head,
                   jax.ShapeDtypeStruct((SSM_LANE_TILES, L // SSM_CHUNK, B, SSM_CHUNK, 128), F32)],
        compiler_params=_params("parallel", "parallel"),
        name="inproj0",
    )(x, w_bf16)


SB_BLOCK = 256
SB_HEADS_PER_STEP = 4
EXP_UNDERFLOW = -104.0


def _sb_attn_body(q_ref, k_ref, v_ref, o_ref, acc_scr, cs_scr):
    i = pl.program_id(2)
    t = q_ref.shape[2]
    nh = q_ref.shape[1]
    row = lax.broadcasted_iota(jnp.int32, (t, t), 0)
    col = lax.broadcasted_iota(jnp.int32, (t, t), 1)
    suffix = jnp.where(row > col, 1.0, 0.0).astype(BF16)
    below = col < row

    def logits(hh, k0):
        return _dot_nt(q_ref[0, hh], k_ref[0, hh, pl.ds(k0, t), :])

    def block(hh, k0, diagonal, z):
        vb = v_ref[0, hh, pl.ds(k0, t), :]
        sp = jnp.maximum(z, 0.0) + jnp.log(1.0 + jnp.exp(-jnp.abs(z)))
        l1 = jnp.where(below, -sp, 0.0) if diagonal else -sp
        within = _dot(l1.astype(BF16), suffix)
        if diagonal:
            w = jnp.where(below, jnp.exp(z - sp + within), 0.0)
            acc_scr[hh] = _dot(w.astype(BF16), vb)
            cs = jnp.sum(l1, axis=1, keepdims=True)
        else:
            w = jnp.exp(z - sp + within + cs_scr[hh])
            acc_scr[hh] += _dot(w.astype(BF16), vb)
            cs = cs_scr[hh] + jnp.sum(l1, axis=1, keepdims=True)
        cs_scr[hh] = cs
        return cs

    def alive(css):
        return (jnp.max(functools.reduce(jnp.maximum, css)) > EXP_UNDERFLOW).astype(jnp.int32)

    def all_heads(k0, diagonal):
        css = []
        z_next = logits(0, k0)
        for hh in range(nh):
            z = z_next
            if hh + 1 < nh:
                z_next = logits(hh + 1, k0)
            css.append(block(hh, k0, diagonal, z))
        return alive(css)

    first = all_heads(pl.multiple_of(i * t, t), True)

    def cond(c):
        return jnp.logical_and(c[0] <= i, c[1] > 0)

    def body(c):
        return c[0] + 1, all_heads(pl.multiple_of((i - c[0]) * t, t), False)

    lax.while_loop(cond, body, (jnp.int32(1), first))
    o_ref[0] = jnp.concatenate([acc_scr[hh] for hh in range(nh)], axis=-1).astype(o_ref.dtype)


def _sb_attention(q, k, v):
    B, H, L, d = q.shape
    t = min(SB_BLOCK, L)
    nh = SB_HEADS_PER_STEP
    return pl.pallas_call(
        _sb_attn_body,
        grid=(B, H // nh, L // t),
        in_specs=[pl.BlockSpec((1, nh, t, d), lambda b, h, i: (b, h, i, 0)),
                  pl.BlockSpec((1, nh, L, d), lambda b, h, i: (b, h, 0, 0)),
                  pl.BlockSpec((1, nh, L, d), lambda b, h, i: (b, h, 0, 0))],
        out_specs=pl.BlockSpec((1, t, nh * d), lambda b, h, i: (b, i, h)),
        out_shape=jax.ShapeDtypeStruct((B, L, H * d), BF16),
        scratch_shapes=[pltpu.VMEM((nh, t, d), F32), pltpu.VMEM((nh, t, 1), F32)],
        compiler_params=_params("parallel", "parallel", "arbitrary"),
        name="sb_attention",
    )(q, k, v)


def _ssm_tables(lam_re, lam_im, log_dt, b_re, b_im, c_re, c_im, d_skip):
    T, G, P, H = SSM_CHUNK, SSM_GROUPS, SSM_STATE, SSM_GROUP
    hp = lax.Precision.HIGHEST
    dt = jnp.exp(log_dt.astype(F32))[:, None]
    lr = lam_re.astype(F32)
    li = lam_im.astype(F32)
    mag = jnp.exp(lr * dt)
    a_re = mag * jnp.cos(li * dt)
    a_im = mag * jnp.sin(li * dt)
    den = lr * lr + li * li
    nr = a_re - 1.0
    f_re = (nr * lr + a_im * li) / den
    f_im = (a_im * lr - nr * li) / den
    br = b_re.astype(F32)
    bi = b_im.astype(F32)
    bb_re = f_re[..., None] * br - f_im[..., None] * bi
    bb_im = f_re[..., None] * bi + f_im[..., None] * br
    tau = jnp.arange(T + 1, dtype=F32)[None, :, None]
    pmag = jnp.exp(tau * (lr * dt)[:, None, :])
    pw_re = pmag * jnp.cos(tau * (li * dt)[:, None, :])
    pw_im = pmag * jnp.sin(tau * (li * dt)[:, None, :])
    cr = c_re.astype(F32)[:, None]
    ci = c_im.astype(F32)[:, None]
    cpw_re = cr * pw_re[:, :, None, :] - ci * pw_im[:, :, None, :]
    cpw_im = cr * pw_im[:, :, None, :] + ci * pw_re[:, :, None, :]
    kern = (jnp.einsum('gthp,gpi->gthi', cpw_re, bb_re, precision=hp)
            - jnp.einsum('gthp,gpi->gthi', cpw_im, bb_im, precision=hp))
    s_idx = np.arange(T)[:, None]
    t_idx = np.arange(T)[None, :]
    lag = np.clip(t_idx - s_idx, 0, T)
    toe = kern[:, lag]
    toe = jnp.where(jnp.asarray(t_idx >= s_idx)[None, :, :, None, None], toe, 0.0)
    intra = toe.transpose(0, 1, 4, 2, 3).reshape(G, T * H, T * H)
    rev = np.arange(T - 1, -1, -1)
    qr = pw_re[:, rev][:, :, None, :]
    qi = pw_im[:, rev][:, :, None, :]
    bbr = bb_re.transpose(0, 2, 1)[:, None]
    bbi = bb_im.transpose(0, 2, 1)[:, None]
    st_re = (qr * bbr - qi * bbi).reshape(G, T * H, P)
    st_im = (qr * bbi + qi * bbr).reshape(G, T * H, P)
    to_st = jnp.concatenate([st_re, st_im, st_im, st_re], axis=-1)
    top = cpw_re[:, 1:].transpose(0, 3, 1, 2).reshape(G, P, T * H)
    bot = (-cpw_im[:, 1:]).transpose(0, 3, 1, 2).reshape(G, P, T * H)
    from_st = jnp.concatenate([top, bot], axis=1)
    at_re = pw_re[:, T]
    at_im = pw_im[:, T]
    zeros = jnp.zeros_like(at_re)
    adv = jnp.stack([jnp.concatenate([at_re, at_re], -1),
                     jnp.concatenate([-at_im, at_im], -1),
                     jnp.concatenate([at_im, -at_im], -1)]
                    + [jnp.concatenate([zeros, zeros], -1)] * 5, axis=1)
    dvec = jnp.tile(d_skip.astype(F32).reshape(G, 1, H), (1, 1, T))
    return intra, to_st, from_st, adv, dvec


def _ssm_expand(tables):
    intra, to_st, from_st, adv, dvec = tables
    V, W, T, H, P = SSM_LANE_TILES, SSM_TILE_GROUPS, SSM_CHUNK, SSM_GROUP, SSM_STATE
    n = T * W * H
    k_in = lax.broadcasted_iota(jnp.int32, (T * H, n), 0)
    col = lax.broadcasted_iota(jnp.int32, (T * H, n), 1)
    spread_th = (k_in == (col // 128) * H + col % H).astype(F32)
    spread_pq = (k_in == (col // (W * 2 * P)) * 2 * P + col % (2 * P)).astype(F32)
    row_g = (lax.broadcasted_iota(jnp.int32, (n, 1), 0) // H) % W
    st_row_g = lax.broadcasted_iota(jnp.int32, (W * 2 * P, 1), 0) // (2 * P)
    col_g_th = (lax.broadcasted_iota(jnp.int32, (1, n), 1) // H) % W
    col_g_pq = (lax.broadcasted_iota(jnp.int32, (1, n), 1) // (2 * P)) % W

    def blockdiag(rows, spread, row_group, col_group):
        full = jnp.einsum('vrk,kc->vrc', rows, spread)
        return jnp.where(row_group == col_group, full, 0.0).astype(BF16)

    by_row = lambda t: t.reshape(V, W, T, H, T * H).transpose(0, 2, 1, 3, 4).reshape(V, n, T * H)
    m8 = blockdiag(by_row(intra), spread_th, row_g, col_g_th)
    ws8 = blockdiag(by_row(to_st), spread_pq, row_g, col_g_pq)
    wi8 = blockdiag(from_st.reshape(V, W * 2 * P, T * H), spread_th, st_row_g, col_g_th)
    adv8 = adv.reshape(V, W, 8, 2 * P).transpose(0, 2, 1, 3).reshape(V, 8, W * 2 * P)
    d8 = jnp.broadcast_to(dvec.reshape(V, 1, W, T, H)[:, :, :, 0:1], (V, 1, W, T, H))
    d8 = d8.transpose(0, 1, 3, 2, 4).reshape(V, 1, T * W * H)
    return m8, ws8, wi8, adv8, d8


def _ssm_body(u_ref, m8_ref, ws8_ref, wi8_ref, adv_ref, dvec_ref, y_ref, s_scr, xin_scr, x1_scr, x2_scr, *, bsz):
    half = x1_scr.shape[1]

    @pl.when(pl.program_id(1) == 0)
    def _():
        x1_scr[...] = jnp.zeros_like(x1_scr)
        x2_scr[...] = jnp.zeros_like(x2_scr)

    u = u_ref[0]
    ub = u.astype(BF16)
    s_scr[...] = _dot(ub, ws8_ref[0])
    a1 = adv_ref[0, 0:1, :]
    a2 = adv_ref[0, 1:2, :]
    a3 = adv_ref[0, 2:3, :]

    def step(c, carry):
        x1, x2 = carry
        r0 = pl.multiple_of(c * bsz, bsz)
        xin_scr[pl.ds(r0, bsz), :] = x1
        s = s_scr[pl.ds(r0, bsz), :]
        return (a1 * x1 + a2 * x2 + s[:, :half], a1 * x2 + a3 * x1 + s[:, half:])

    x1, x2 = lax.fori_loop(0, u.shape[0] // bsz, step, (x1_scr[...], x2_scr[...]))
    x1_scr[...] = x1
    x2_scr[...] = x2
    y_ref[0] = _dot(ub, m8_ref[0]) + _dot(xin_scr[...].astype(BF16), wi8_ref[0]) + dvec_ref[0] * u


def _ssm(u5, tables, rows=256):
    m8, ws8, wi8, adv8, d8 = _ssm_expand(tables)
    V, C, B, T, _ = u5.shape
    width = T * 128
    half = SSM_TILE_GROUPS * 2 * SSM_STATE
    rows = min(rows, C * B)
    once = dict(pipeline_mode=pl.Buffered(1))
    y = pl.pallas_call(
        functools.partial(_ssm_body, bsz=B),
        grid=(V, C * B // rows),
        in_specs=[pl.BlockSpec((1, rows, width), lambda v, r: (v, r, 0)),
                  pl.BlockSpec((1, width, width), lambda v, r: (v, 0, 0), **once),
                  pl.BlockSpec((1, width, 2 * half), lambda v, r: (v, 0, 0), **once),
                  pl.BlockSpec((1, half, width), lambda v, r: (v, 0, 0), **once),
                  pl.BlockSpec((1, 8, half), lambda v, r: (v, 0, 0)),
                  pl.BlockSpec((1, 1, width), lambda v, r: (v, 0, 0))],
        out_specs=pl.BlockSpec((1, rows, width), lambda v, r: (v, r, 0)),
        out_shape=jax.ShapeDtypeStruct((V, C * B, width), F32),
        scratch_shapes=[pltpu.VMEM((rows, 2 * half), F32), pltpu.VMEM((rows, half), F32),
                        pltpu.VMEM((B, half), F32), pltpu.VMEM((B, half), F32)],
        compiler_params=_params("parallel", "arbitrary"),
        name="s5_scan",
    )(u5.reshape(V, C * B, width), m8, ws8, wi8, adv8, d8)
    return y.reshape(V, C, B, T, 128)


def _outproj0_body(x_ref, oa_ref, y_ref, wglu_ref, woa_ref, wob_ref, g_ref, b_ref, o_ref):
    tl = x_ref.shape[1]
    y = jnp.concatenate([y_ref[v, :, 0].reshape(tl, 128) for v in range(SSM_LANE_TILES)], axis=-1)
    h = _gelu(y)
    ob = h * _sigmoid(_dot(h.astype(BF16), wglu_ref[...]))
    m = _dot(oa_ref[0], woa_ref[...]) + _dot(ob.astype(BF16), wob_ref[...])
    o_ref[0] = _layer_norm(ALPHA * x_ref[0] + m, g_ref[...], b_ref[...])


def _outproj0(x, oa, y5, wglu, woa, wob, g, b, tl=512):
    B, L, D = x.shape
    tl = min(tl, L)
    row = lambda b, i: (b, i, 0)
    fix = lambda b, i: (0, 0)
    return pl.pallas_call(
        _outproj0_body,
        grid=(B, L // tl),
        in_specs=[pl.BlockSpec((1, tl, D), row), pl.BlockSpec((1, tl, SB_WIDTH), row),
                  pl.BlockSpec((SSM_LANE_TILES, tl // SSM_CHUNK, 1, SSM_CHUNK, 128), lambda b, i: (0, i, b, 0, 0)),
                  pl.BlockSpec((SSM_WIDTH, SSM_WIDTH), fix), pl.BlockSpec((SB_WIDTH, D), fix),
                  pl.BlockSpec((SSM_WIDTH, D), fix), pl.BlockSpec((1, D), fix), pl.BlockSpec((1, D), fix)],
        out_specs=pl.BlockSpec((1, tl, D), row),
        out_shape=jax.ShapeDtypeStruct((B, L, D), F32),
        compiler_params=_params("parallel", "parallel"),
        name="outproj0",
    )(x, oa, y5, wglu, woa, wob, g, b)


def _outproj1_body(x_ref, o_ref_in, w_ref, g_ref, b_ref, o_ref):
    m = _dot(o_ref_in[...].astype(BF16), w_ref[...])
    o_ref[...] = _layer_norm(ALPHA * x_ref[...] + m, g_ref[...], b_ref[...])


def _outproj1(x, o, w, g, b, tm=512):
    N, D = x.shape
    tm = min(tm, N)
    row = lambda i: (i, 0)
    fix = lambda i: (0, 0)
    return pl.pallas_call(
        _outproj1_body,
        grid=(N // tm,),
        in_specs=[pl.BlockSpec((tm, D), row), pl.BlockSpec((tm, o.shape[1]), row),
                  pl.BlockSpec(w.shape, fix), pl.BlockSpec((1, D), fix), pl.BlockSpec((1, D), fix)],
        out_specs=pl.BlockSpec((tm, D), row),
        out_shape=jax.ShapeDtypeStruct((N, D), F32),
        compiler_params=_params("parallel"),
        name="outproj1",
    )(x, o, w, g, b)


def _router_body(x_ref, wh_ref, wl_ref, b_ref, g_ref, grp_ref):
    xh, xl = _split_bf16(x_ref[...])
    wh = wh_ref[...]
    logits = _dot_nt(wh, xh) + _dot_nt(wh, xl) + _dot_nt(wl_ref[...], xh)
    aff = _sigmoid(logits)
    sel = aff + b_ref[...]
    E, K = EXPERTS_PER_GROUP, N_GROUPS
    s = [sel[e:e + 1, :] for e in range(N_EXPERTS)]
    a = [aff[e:e + 1, :] for e in range(N_EXPERTS)]
    gscore = []
    for k in range(K):
        v = s[k * E:(k + 1) * E]
        best = None
        for x in range(E):
            for y in range(x + 1, E):
                pair = v[x] + v[y]
                best = pair if best is None else jnp.maximum(best, pair)
        gscore.append(best)
    top = functools.reduce(jnp.maximum, gscore)
    is_g = []
    taken = None
    for k in range(K):
        hit = gscore[k] == top
        if taken is None:
            is_g.append(hit)
            taken = hit
        else:
            is_g.append(jnp.logical_and(hit, jnp.logical_not(taken)))
            taken = jnp.logical_or(taken, hit)

    def pick(rows, j):
        out = rows[(K - 1) * E + j]
        for k in range(K - 2, -1, -1):
            out = jnp.where(is_g[k], rows[k * E + j], out)
        return out

    v = [pick(s, j) for j in range(E)]
    av = [pick(a, j) for j in range(E)]
    chosen = []
    for j in range(E):
        r = jnp.zeros_like(v[j])
        for j2 in range(E):
            if j2 == j:
                continue
            ahead = (v[j2] >= v[j]) if j2 < j else (v[j2] > v[j])
            r = r + jnp.where(ahead, 1.0, 0.0)
        chosen.append(r < 2.0)
    wj = [jnp.where(chosen[j], av[j], 0.0) for j in range(E)]
    tot = wj[0] + wj[1] + wj[2] + wj[3]
    gj = [w / tot for w in wj]
    rows = [jnp.where(is_g[e // E], gj[e % E], 0.0) for e in range(N_EXPERTS)]
    g_ref[...] = jnp.concatenate(rows, axis=0)
    grp_ref[...] = jnp.concatenate([jnp.where(is_g[k], 1.0, 0.0) for k in range(K)], axis=0)


def _router(x, w_router, b_router, tm=512):
    N, D = x.shape
    tm = min(tm, N)
    wt = w_router.astype(F32).T
    wh, wl = _split_bf16(wt)
    fix = lambda i: (0, 0)
    return pl.pallas_call(
        _router_body,
        grid=(N // tm,),
        in_specs=[pl.BlockSpec((tm, D), lambda i: (i, 0)), pl.BlockSpec((N_EXPERTS, D), fix),
                  pl.BlockSpec((N_EXPERTS, D), fix), pl.BlockSpec((N_EXPERTS, 1), fix)],
        out_specs=[pl.BlockSpec((N_EXPERTS, tm), lambda i: (0, i)), pl.BlockSpec((N_GROUPS, tm), lambda i: (0, i))],
        out_shape=[jax.ShapeDtypeStruct((N_EXPERTS, N), F32), jax.ShapeDtypeStruct((N_GROUPS, N), F32)],
        compiler_params=_params("parallel"),
        name="router",
    )(x, wh, wl, b_router.astype(F32).reshape(N_EXPERTS, 1))


MOE_TILE = 1024
MOE_PASS_ROWS = 304
MOE_CHUNK = 256
MOE_EXPERTS_PER_STEP = 2
ROW_ALIGN = 16


def _moe_body(seg_ref, x_ref, gt_ref, grp_ref, w1_ref, w3_ref, w2_ref, g_ref, b_ref, o_ref,
              xs_scr, ys_scr, pt_scr, gs_scr):
    i = pl.program_id(0)
    e = pl.program_id(1)
    T = x_ref.shape[0]
    K = N_GROUPS
    cap = MOE_PASS_ROWS
    ch = min(MOE_CHUNK, T)

    @pl.when(e == 0)
    def _():
        grp = grp_ref[...]
        grp_b = grp.astype(BF16)
        xb = x_ref[...].astype(BF16)
        gates = gt_ref[...]
        gh, gm = _split_bf16(gates)
        gl = (gates - gh.astype(F32) - gm.astype(F32)).astype(BF16)
        cnt = jnp.sum(grp, axis=1, keepdims=True)
        off = [jnp.zeros((1, 1), F32)]
        for k in range(K - 1):
            off.append(off[-1] + cnt[k:k + 1])
        r_i = lax.broadcasted_iota(jnp.int32, (ch, T), 0)
        c_i = lax.broadcasted_iota(jnp.int32, (ch, T), 1)
        pos_row = []
        pos_col = []
        for c in range(T // ch):
            c0 = c * ch
            earlier = jnp.where(c_i < r_i + c0, 1.0, 0.0).astype(BF16)
            same = jnp.where(c_i == r_i + c0, 1.0, 0.0).astype(BF16)
            before_c = _dot_nt(earlier, grp_b)
            member_c = _dot_nt(same, grp_b)
            before_r = _dot_nt(grp_b, earlier)
            pc = jnp.zeros((ch, 1), F32)
            pr = jnp.zeros((1, ch), F32)
            for k in range(K):
                pc = pc + member_c[:, k:k + 1] * (off[k] + before_c[:, k:k + 1])
                pr = pr + grp[k:k + 1, c0:c0 + ch] * (off[k] + before_r[k:k + 1, :])
            pos_col.append(pc)
            pos_row.append(pr)
        pos_row = jnp.concatenate(pos_row, axis=1)
        for c in range(T // ch):
            c0 = c * ch
            perm = jnp.where((r_i + c0).astype(F32) == pos_row, 1.0, 0.0).astype(BF16)
            xs_scr[c0:c0 + ch, :] = _dot(perm, xb).astype(BF16)
            gs_scr[c0:c0 + ch, :] = _dot_nt(perm, gh) + _dot_nt(perm, gm) + _dot_nt(perm, gl)
            pt_scr[c0:c0 + ch, :] = jnp.where(c_i.astype(F32) == pos_col[c], 1.0, 0.0).astype(BF16)
        xs_scr[T:, :] = jnp.zeros((cap, xs_scr.shape[1]), BF16)
        gs_scr[T:, :] = jnp.zeros((cap, gs_scr.shape[1]), F32)
        ys_scr[...] = jnp.zeros_like(ys_scr)

    ne = w1_ref.shape[0]
    k = (e * ne) // EXPERTS_PER_GROUP
    off = seg_ref[i, k]
    cnt = seg_ref[i, K + k]
    start0 = (off // ROW_ALIGN) * ROW_ALIGN
    n_pass = (off + cnt - start0 + cap - 1) // cap

    def one_pass(n, c):
        start = pl.multiple_of(start0 + n * cap, ROW_ALIGN)
        xc = xs_scr[pl.ds(start, cap), :]
        gsc = gs_scr[pl.ds(start, cap), :]
        lane = lax.broadcasted_iota(jnp.int32, gsc.shape, 1)
        y = None
        for x in range(ne):
            a = _dot(xc, w1_ref[x])
            hm = (a * _sigmoid(a)) * _dot(xc, w3_ref[x])
            gcol = jnp.sum(jnp.where(lane == e * ne + x, gsc, 0.0), axis=1, keepdims=True)
            yx = gcol * _dot(hm.astype(BF16), w2_ref[x])
            y = yx if y is None else y + yx
        ys_scr[pl.ds(start, cap), :] += y
        return c

    lax.fori_loop(0, n_pass, one_pass, 0)

    @pl.when(e == pl.num_programs(1) - 1)
    def _():
        y = _dot(pt_scr[...], ys_scr[0:T, :].astype(BF16))
        o_ref[...] = _layer_norm(ALPHA * x_ref[...] + y, g_ref[...], b_ref[...])


def _moe(x, gates_t, grp_t, w1, w3, w2, g, b):
    N, D = x.shape
    T = min(MOE_TILE, N)
    K = N_GROUPS
    cnt = grp_t.reshape(K, N // T, T).sum(-1).astype(jnp.int32).T
    seg = jnp.concatenate([jnp.cumsum(cnt, axis=1) - cnt, cnt], axis=1)
    fix = lambda i, e, s: (0, 0)
    ne = MOE_EXPERTS_PER_STEP
    rows = T + MOE_PASS_ROWS
    grid_spec = pltpu.PrefetchScalarGridSpec(
        num_scalar_prefetch=1,
        grid=(N // T, N_EXPERTS // ne),
        in_specs=[pl.BlockSpec((T, D), lambda i, e, s: (i, 0)),
                  pl.BlockSpec((N_EXPERTS, T), lambda i, e, s: (0, i)),
                  pl.BlockSpec((K, T), lambda i, e, s: (0, i)),
                  pl.BlockSpec((ne, D, D_EXPERT), lambda i, e, s: (e, 0, 0)),
                  pl.BlockSpec((ne, D, D_EXPERT), lambda i, e, s: (e, 0, 0)),
                  pl.BlockSpec((ne, D_EXPERT, D), lambda i, e, s: (e, 0, 0)),
                  pl.BlockSpec((1, D), fix), pl.BlockSpec((1, D), fix)],
        out_specs=pl.BlockSpec((T, D), lambda i, e, s: (i, 0)),
        scratch_shapes=[pltpu.VMEM((rows, D), BF16), pltpu.VMEM((rows, D), F32),
                        pltpu.VMEM((T, T), BF16), pltpu.VMEM((rows, N_EXPERTS), F32)])
    return pl.pallas_call(
        _moe_body,
        grid_spec=grid_spec,
        out_shape=jax.ShapeDtypeStruct((N, D), F32),
        compiler_params=_params("parallel", "arbitrary"),
        name="moe",
    )(seg, x, gates_t, grp_t, w1, w3, w2, g, b)


NSA_KVW = NSA_KV_HEADS * HEAD_DIM
NSA_ROPE_W = NSA_HEADS * HEAD_DIM + 2 * NSA_KVW
NSA_Q_SCALE = ATTN_SCALE * math.log2(math.e)


def _rot_cols(w):
    K, n = w.shape
    w3 = w.reshape(K, n // HEAD_DIM, 2, HEAD_DIM // 2)
    return jnp.stack([-w3[:, :, 1], w3[:, :, 0]], axis=2).reshape(K, n)


def _inproj1_weights(w_in):
    H, G, d = NSA_HEADS, NSA_KV_HEADS, HEAD_DIM
    cuts = [H * d + i * NSA_KVW for i in range(7)]
    q, kc, vc, ks, vs, kw, vw, gate = jnp.split(w_in.astype(F32), cuts, axis=1)
    rope = jnp.concatenate([q, ks, kw], axis=1)
    gate = jnp.pad(gate, ((0, 0), (0, 128 - gate.shape[1])))
    w_all = jnp.concatenate([rope, _rot_cols(rope), kc, vc, gate], axis=1).astype(BF16)
    w_vt = jnp.concatenate([vs, vw], axis=1).T.astype(BF16)
    return w_all, w_vt


def _rope_tables(pos):
    inv = ROPE_THETA ** (-jnp.arange(0, HEAD_DIM, 2, dtype=F32) / HEAD_DIM)
    ang = pos.astype(F32)[:, None] * inv[None, :]
    c, s = jnp.cos(ang), jnp.sin(ang)
    return jnp.concatenate([c, c], axis=-1), jnp.concatenate([s, s], axis=-1)


V_ROWS = HEAD_DIM + 16


def _inproj1_body(x_ref, w_ref, wvt_ref, cos_ref, sin_ref, q_ref, ks_ref, kw_ref, kc_ref, vc_ref, vst_ref, vwt_ref,
                  gate_ref):
    xb = x_ref[0].astype(BF16)
    p = _dot(xb, w_ref[...])
    vt = _dot_nt(wvt_ref[...], xb)
    ones = jnp.ones((V_ROWS - HEAD_DIM, xb.shape[0]), BF16)
    for n, ref in enumerate((vst_ref, vwt_ref)):
        for g in range(NSA_KV_HEADS):
            r0 = n * NSA_KVW + g * HEAD_DIM
            ref[0, g, 0:HEAD_DIM, :] = vt[r0:r0 + HEAD_DIM, :].astype(BF16)
            ref[0, g, HEAD_DIM:V_ROWS, :] = ones
    R = NSA_ROPE_W
    cos = cos_ref[...]
    sin = sin_ref[...]
    roped = []
    for c in range(R // 128):
        roped.append(p[:, c * 128:(c + 1) * 128] * cos + p[:, R + c * 128:R + (c + 1) * 128] * sin)
    d = HEAD_DIM

    def head(chunks, h):
        blk = chunks[h // 2]
        return blk[:, (h % 2) * d:(h % 2 + 1) * d]

    for h in range(NSA_HEADS):
        q_ref[0, h] = (head(roped, h) * NSA_Q_SCALE).astype(BF16)
    nq = NSA_HEADS // 2
    for g in range(NSA_KV_HEADS):
        ks_ref[0, g] = head(roped[nq:], g).astype(BF16)
        kw_ref[0, g] = head(roped[nq + 2:], g).astype(BF16)
    base = 2 * R
    for n, ref in enumerate((kc_ref, vc_ref)):
        for g in range(NSA_KV_HEADS):
            c0 = base + n * NSA_KVW + g * d
            ref[0, g] = p[:, c0:c0 + d].astype(BF16)
    gate_ref[0] = p[:, base + 2 * NSA_KVW:]


def _inproj1(x, w_all, w_vt, cos2, sin2, tl=256):
    B, L, D = x.shape
    tl = min(tl, L)
    nout = w_all.shape[1]
    G = NSA_KV_HEADS
    qh = jax.ShapeDtypeStruct((B, NSA_HEADS, L, HEAD_DIM), BF16)
    kvh = jax.ShapeDtypeStruct((B, G, L, HEAD_DIM), BF16)
    vth = jax.ShapeDtypeStruct((B, G, V_ROWS, L), BF16)
    q_spec = pl.BlockSpec((1, NSA_HEADS, tl, HEAD_DIM), lambda b, i: (b, 0, i, 0))
    kv_spec = pl.BlockSpec((1, G, tl, HEAD_DIM), lambda b, i: (b, 0, i, 0))
    vt_spec = pl.BlockSpec((1, G, V_ROWS, tl), lambda b, i: (b, 0, 0, i))
    return pl.pallas_call(
        _inproj1_body,
        grid=(B, L // tl),
        in_specs=[pl.BlockSpec((1, tl, D), lambda b, i: (b, i, 0)),
                  pl.BlockSpec((D, nout), lambda b, i: (0, 0)),
                  pl.BlockSpec(w_vt.shape, lambda b, i: (0, 0)),
                  pl.BlockSpec((tl, 128), lambda b, i: (i, 0)),
                  pl.BlockSpec((tl, 128), lambda b, i: (i, 0))],
        out_specs=[q_spec] + [kv_spec] * 4 + [vt_spec] * 2 + [pl.BlockSpec((1, tl, 128), lambda b, i: (b, i, 0))],
        out_shape=[qh] + [kvh] * 4 + [vth] * 2 + [jax.ShapeDtypeStruct((B, L, 128), F32)],
        compiler_params=_params("parallel", "parallel"),
        name="inproj1",
    )(x, w_all, w_vt, cos2, sin2)


def _compress_body(kc_ref, vc_ref, posk_ref, posv_ref, w1k_ref, w1v_ref, w2k_ref, w2kr_ref, w2v_ref,
                   cos_ref, sin_ref, kcmp_ref, vcmp_ref):
    def hidden(a_ref, pos_ref, w1_ref):
        a = a_ref[0, 0].astype(F32)
        nrow = a.shape[0]
        lo = _dot((a + pos_ref[0:1, :]).astype(BF16), w1_ref[0])
        hi = _dot((a + pos_ref[1:2, :]).astype(BF16), w1_ref[1])
        hi_next = pltpu.roll(hi, nrow - 1, 0)
        return _gelu(lo + hi_next).astype(BF16)

    hk = hidden(kc_ref, posk_ref, w1k_ref)
    kcmp = _dot(hk, w2k_ref[...]) * cos_ref[...] + _dot(hk, w2kr_ref[...]) * sin_ref[...]
    kcmp_ref[0, 0] = kcmp.astype(BF16)
    hv = hidden(vc_ref, posv_ref, w1v_ref)
    vcmp_ref[0, 0] = _dot_nt(w2v_ref[...], hv).astype(BF16)


def _compress(kc, vc, pos_k, w1_k, w2_k, pos_v, w1_v, w2_v):
    B, G, L, d = kc.shape
    half = CMP_STRIDE * d
    nb = L // CMP_STRIDE
    kc2 = kc.reshape(B, G, nb, half)
    vc2 = vc.reshape(B, G, nb, half)
    posk = pos_k.astype(F32).reshape(2, half)
    posv = pos_v.astype(F32).reshape(2, half)
    w1k = w1_k.astype(BF16).reshape(2, half, CMP_HIDDEN)
    w1v = w1_v.astype(BF16).reshape(2, half, CMP_HIDDEN)
    w2k = w2_k.astype(F32)
    cos, sin = _rope_tables(jnp.arange(nb) * CMP_STRIDE + CMP_LEN - 1)
    blk = pl.BlockSpec((1, 1, nb, half), lambda b, g: (b, g, 0, 0))
    out = pl.BlockSpec((1, 1, nb, d), lambda b, g: (b, g, 0, 0))
    out_t = pl.BlockSpec((1, 1, d, nb), lambda b, g: (b, g, 0, 0))
    fix2 = lambda b, g: (0, 0)
    fix3 = lambda b, g: (0, 0, 0)
    return pl.pallas_call(
        _compress_body,
        grid=(B, G),
        in_specs=[blk, blk, pl.BlockSpec((2, half), fix2), pl.BlockSpec((2, half), fix2),
                  pl.BlockSpec((2, half, CMP_HIDDEN), fix3), pl.BlockSpec((2, half, CMP_HIDDEN), fix3),
                  pl.BlockSpec((CMP_HIDDEN, d), fix2), pl.BlockSpec((CMP_HIDDEN, d), fix2),
                  pl.BlockSpec((d, CMP_HIDDEN), fix2), pl.BlockSpec((nb, d), fix2), pl.BlockSpec((nb, d), fix2)],
        out_specs=[out, out_t],
        out_shape=[jax.ShapeDtypeStruct((B, G, nb, d), BF16), jax.ShapeDtypeStruct((B, G, d, nb), BF16)],
        compiler_params=_params("parallel", "parallel"),
        name="compress",
    )(kc2, vc2, posk, posv, w1k, w1v, w2k.astype(BF16), _rot_cols(w2k).astype(BF16), w2_v.T.astype(BF16), cos, sin)


NSA_KBLOCK = 256
NSA_GROUPS_PER_STEP = 4
NSA_HEADS_PER_CHAIN = 4
NSA_SCORE_LOOKAHEAD = 3


def _nsa_gate_layout(gate):
    B, L, _ = gate.shape
    G, R, tq = NSA_KV_HEADS, NSA_REP, min(ATT_BLOCK, L)
    g5 = gate[:, :, :3 * NSA_HEADS].reshape(B, L // tq, tq, G, R, 3)
    return g5.transpose(0, 3, 1, 5, 4, 2).reshape(B, G, L // tq, 3, R * tq)


def _nsa_body(q_ref, kcmp_ref, vcmpt_ref, ks_ref, vst_ref, kw_ref, vwt_ref, gate_ref, o_ref, *, seq):
    i = pl.program_id(2)
    tq = q_ref.shape[2]
    gb = kcmp_ref.shape[1]
    tk = min(NSA_KBLOCK, seq)
    R, d = NSA_REP, HEAD_DIM
    nq = R * tq
    nb = seq // SEL_LEN
    mc = seq // CMP_STRIDE
    t_row = i * tq + lax.broadcasted_iota(jnp.int32, (1, tq), 1)
    t_all = jnp.concatenate([t_row] * R, axis=1)
    m_col = lax.broadcasted_iota(jnp.int32, (mc, 1), 0)
    m_row = lax.broadcasted_iota(jnp.int32, (1, mc), 1)
    n_col = lax.broadcasted_iota(jnp.int32, (nb, 1), 0)

    def q_rows(g):
        return q_ref[0, g * R:(g + 1) * R].reshape(nq, d)

    valid_c = (m_col * CMP_STRIDE + (CMP_LEN - 1)) <= t_all
    ovl = jnp.logical_and(m_row * CMP_STRIDE < (n_col + 1) * SEL_LEN,
                          m_row * CMP_STRIDE + CMP_LEN > n_col * SEL_LEN)
    ovl = jnp.where(ovl, 1.0, 0.0).astype(BF16)
    cur = t_row // SEL_LEN
    forced = jnp.logical_or(n_col == 0, jnp.logical_or(n_col == cur, n_col == cur - 1))
    bonus = jnp.where(forced, FORCE_BONUS, 0.0)
    valid_s = n_col * SEL_LEN <= t_row
    o_c, sel = [], []
    for g in range(gb):
        s = jnp.where(valid_c, _dot_nt(kcmp_ref[0, g], q_rows(g)), NEG)
        e = jnp.where(valid_c, jnp.exp2(s - jnp.max(s, axis=0, keepdims=True)), 0.0)
        den = jnp.sum(e, axis=0, keepdims=True)
        p = e / jnp.where(den > 0.0, den, 1.0)
        o_c.append(_dot(vcmpt_ref[0, g], p.astype(BF16)))
        psum = p[:, 0:tq]
        for r in range(1, R):
            psum = psum + p[:, r * tq:(r + 1) * tq]
        ph, plo = _split_bf16(psum)
        score = jnp.where(valid_s, _dot(ovl, ph) + _dot(ovl, plo) + bonus, NEG)
        rank = jnp.zeros((nb, tq), F32)
        for n2 in range(nb):
            other = score[n2:n2 + 1, :]
            tie = jnp.where(n_col > n2, 1.0, 0.0)
            rank = rank + jnp.where(other > score, 1.0, jnp.where(other == score, tie, 0.0))
        sel.append(jnp.where(rank < float(SEL_TOP), 1.0, 0.0).astype(BF16))

    k_col = lax.broadcasted_iota(jnp.int32, (tk, 1), 0)
    hp = NSA_HEADS_PER_CHAIN
    j_hi = (i * tq + tq - 1) // tk + 1

    def sweep(k_ref, vt_ref, j_lo, bias_fn):
        chains = [(g, h0) for g in range(gb) for h0 in range(0, R, hp)]

        def body(j, state):
            k0 = pl.multiple_of(j * tk, tk)
            biases = bias_fn(k0 + k_col)
            def scores(c):
                g, h0 = chains[c]
                kb = k_ref[0, g, pl.ds(k0, tk), :]
                qh = q_ref[0, g * R + h0:g * R + h0 + hp].reshape(hp * tq, d)
                return _dot_nt(kb, qh) + jnp.concatenate([biases[g]] * hp, axis=1)

            new_state = []
            ahead = NSA_SCORE_LOOKAHEAD
            pending = [scores(c) for c in range(min(ahead, len(chains)))]
            for c, (m_run, acc) in enumerate(state):
                sc = pending.pop(0)
                if c + ahead < len(chains):
                    pending.append(scores(c + ahead))
                vtb = vt_ref[0, chains[c][0], :, pl.ds(k0, tk)]
                m_new = jnp.maximum(m_run, jnp.max(sc, axis=0, keepdims=True))
                pr = jnp.exp2(sc - m_new)
                new_state.append((m_new, jnp.exp2(m_run - m_new) * acc + _dot(vtb, pr.astype(BF16))))
            return tuple(new_state)

        init = tuple((jnp.full((1, hp * tq), NEG, F32), jnp.zeros((V_ROWS, hp * tq), F32)) for _ in chains)
        state = lax.fori_loop(j_lo, j_hi, body, init)
        outs = []
        for g in range(gb):
            acc = jnp.concatenate([a for (cg, _), (_, a) in zip(chains, state) if cg == g], axis=1)
            outs.append(acc[0:d] / acc[d:d + 1])
        return outs

    def sel_bias(kpos):
        pick = jnp.where(kpos // SEL_LEN == lax.broadcasted_iota(jnp.int32, (1, nb), 1), 1.0, 0.0).astype(BF16)
        causal = kpos <= t_row
        return [jnp.where(jnp.logical_and(_dot(pick, sel[g]) > 0.5, causal), 0.0, NEG) for g in range(gb)]

    o_s = sweep(ks_ref, vst_ref, 0, sel_bias)

    def win_bias(kpos):
        return [jnp.where(jnp.logical_and(kpos <= t_row, kpos > t_row - WINDOW), 0.0, NEG)] * gb

    o_w = sweep(kw_ref, vwt_ref, jnp.maximum(i * tq - (WINDOW - 1), 0) // tk, win_bias)

    outs = []
    for g in range(gb):
        sg = _sigmoid(gate_ref[0, g, 0])
        o_t = sg[0:1] * o_c[g] + sg[1:2] * o_s[g] + sg[2:3] * o_w[g]
        for r in range(R):
            outs.append(o_t[:, r * tq:(r + 1) * tq].T)
    o_ref[0] = jnp.concatenate(outs, axis=-1)


def _nsa_attention(q, kcmp, vcmpt, ks, vst, kw, vwt, gate):
    B, H, L, d = q.shape
    G, R = NSA_KV_HEADS, NSA_REP
    gb = NSA_GROUPS_PER_STEP
    tq = min(ATT_BLOCK, L)
    mc = L // CMP_STRIDE
    k_spec = pl.BlockSpec((1, gb, L, d), lambda b, g, i: (b, g, 0, 0))
    vt_spec = pl.BlockSpec((1, gb, V_ROWS, L), lambda b, g, i: (b, g, 0, 0))
    return pl.pallas_call(
        functools.partial(_nsa_body, seq=L),
        grid=(B, G // gb, L // tq),
        in_specs=[pl.BlockSpec((1, gb * R, tq, d), lambda b, g, i: (b, g, i, 0)),
                  pl.BlockSpec((1, gb, mc, d), lambda b, g, i: (b, g, 0, 0)),
                  pl.BlockSpec((1, gb, d, mc), lambda b, g, i: (b, g, 0, 0)),
                  k_spec, vt_spec, k_spec, vt_spec,
                  pl.BlockSpec((1, gb, 1, 3, R * tq), lambda b, g, i: (b, g, i, 0, 0))],
        out_specs=pl.BlockSpec((1, tq, gb * R * d), lambda b, g, i: (b, i, g)),
        out_shape=jax.ShapeDtypeStruct((B, L, H * d), F32),
        compiler_params=_params("parallel", "parallel", "arbitrary"),
        name="nsa_attention",
    )(q, kcmp, vcmpt, ks, vst, kw, vwt, gate)


def kernel(x, w_in_0, ssm_lam_re, ssm_lam_im, ssm_log_dt, ssm_b_re, ssm_b_im, ssm_c_re, ssm_c_im, ssm_d, w_glu, w_out_0, ln_mix_g_0, ln_mix_b_0, ln_ffn_g_0, ln_ffn_b_0, w1_0, w3_0, w2_0, w_in_1, cmp_pos_k, cmp_w1_k, cmp_w2_k, cmp_pos_v, cmp_w1_v, cmp_w2_v, w_out_1, ln_mix_g_1, ln_mix_b_1, ln_ffn_g_1, ln_ffn_b_1, w1_1, w3_1, w2_1, w_router, b_router):
    B, L, D = x.shape
    N = B * L
    vec = lambda a: a.astype(F32).reshape(1, D)

    def ffn(h, w1, w3, w2, g, b):
        gates_t, grp_t = _router(h, w_router, b_router)
        return _moe(h, gates_t, grp_t, w1.astype(BF16), w3.astype(BF16), w2.astype(BF16), vec(g), vec(b))

    q, k, v, u = _inproj0(x, w_in_0.astype(BF16))
    o_a = _sb_attention(q, k, v)
    y = _ssm(u, _ssm_tables(ssm_lam_re, ssm_lam_im, ssm_log_dt, ssm_b_re, ssm_b_im, ssm_c_re, ssm_c_im, ssm_d))
    w_out_0b = w_out_0.astype(BF16)
    h = _outproj0(x, o_a, y, w_glu.astype(BF16), w_out_0b[:SB_WIDTH], w_out_0b[SB_WIDTH:],
                  vec(ln_mix_g_0), vec(ln_mix_b_0)).reshape(N, D)
    h = ffn(h, w1_0, w3_0, w2_0, ln_ffn_g_0, ln_ffn_b_0)

    cos, sin = _rope_tables(jnp.arange(L))
    cos2 = jnp.concatenate([cos, cos], axis=-1)
    sin2 = jnp.concatenate([sin, sin], axis=-1)
    q, ks, kw, kc, vc, vst, vwt, gate = _inproj1(h.reshape(B, L, D), *_inproj1_weights(w_in_1), cos2, sin2)
    kcmp, vcmpt = _compress(kc, vc, cmp_pos_k, cmp_w1_k, cmp_w2_k, cmp_pos_v, cmp_w1_v, cmp_w2_v)
    o = _nsa_attention(q, kcmp, vcmpt, ks, vst, kw, vwt, _nsa_gate_layout(gate)).reshape(N, NSA_HEADS * HEAD_DIM)
    h = _outproj1(h, o, w_out_1.astype(BF16), vec(ln_mix_g_1), vec(ln_mix_b_1))
    h = ffn(h, w1_1, w3_1, w2_1, ln_ffn_g_1, ln_ffn_b_1)
    return h.reshape(B, L, D)
```

```python
import functools
import math

import numpy as np
import jax
import jax.numpy as jnp
from jax import lax
from jax.experimental import pallas as pl
from jax.experimental.pallas import tpu as pltpu

F32 = jnp.float32
BF16 = jnp.bfloat16

D_MODEL = 1024
DEPTH = 2
SB_HEADS = 8
HEAD_DIM = 64
SB_WIDTH = SB_HEADS * HEAD_DIM
SSM_WIDTH = D_MODEL - SB_WIDTH
SSM_GROUP = 16
SSM_GROUPS = SSM_WIDTH // SSM_GROUP
SSM_STATE = 64
SSM_CHUNK = 16
SSM_LANE_TILES = SSM_WIDTH // 128
SSM_TILE_GROUPS = 128 // SSM_GROUP
NSA_HEADS = 16
NSA_KV_HEADS = 4
NSA_REP = NSA_HEADS // NSA_KV_HEADS
CMP_LEN = 32
CMP_STRIDE = 16
CMP_HIDDEN = 256
SEL_LEN = 64
SEL_TOP = 8
WINDOW = 512
ROPE_THETA = 10000.0
FORCE_BONUS = 1e4
NEG = -1e30
N_EXPERTS = 16
N_GROUPS = 4
EXPERTS_PER_GROUP = N_EXPERTS // N_GROUPS
D_EXPERT = 512
ALPHA = (2 * DEPTH) ** 0.25
LN_EPS = 1e-5
ATTN_SCALE = HEAD_DIM ** -0.5
ATT_BLOCK = 128
GELU_C = math.sqrt(2.0 / math.pi)


def _params(*sem):
    return pltpu.CompilerParams(dimension_semantics=sem, vmem_limit_bytes=56 * 1024 * 1024)


def _sigmoid(x):
    return 1.0 / (1.0 + jnp.exp(-x))


def _gelu(x):
    return 0.5 * x * (1.0 + jnp.tanh(GELU_C * (x + 0.044715 * (x * x * x))))


def _layer_norm(r, g, b):
    mu = jnp.mean(r, axis=-1, keepdims=True)
    d = r - mu
    var = jnp.mean(d * d, axis=-1, keepdims=True)
    return d * lax.rsqrt(var + LN_EPS) * g + b


def _dot(a, b):
    return jnp.dot(a, b, preferred_element_type=F32)


def _dot_nt(a, b):
    return lax.dot_general(a, b, (((1,), (1,)), ((), ())), preferred_element_type=F32)


def _split_bf16(x):
    hi = x.astype(BF16)
    lo = (x - hi.astype(F32)).astype(BF16)
    return hi, lo


def _inproj0_body(x_ref, w_ref, q_ref, k_ref, v_ref, u_ref):
    p = _dot(x_ref[0].astype(BF16), w_ref[...])
    for h in range(SB_HEADS):
        c = h * HEAD_DIM
        q_ref[0, h] = (p[:, c:c + HEAD_DIM] * ATTN_SCALE).astype(BF16)
        k_ref[0, h] = p[:, SB_WIDTH + c:SB_WIDTH + c + HEAD_DIM].astype(BF16)
        v_ref[0, h] = p[:, 2 * SB_WIDTH + c:2 * SB_WIDTH + c + HEAD_DIM].astype(BF16)
    tl = p.shape[0]
    for v in range(SSM_LANE_TILES):
        c = 3 * SB_WIDTH + v * 128
        u_ref[v, :, 0] = p[:, c:c + 128].reshape(tl // SSM_CHUNK, SSM_CHUNK, 128)


def _inproj0(x, w_bf16, tl=512):
    B, L, D = x.shape
    tl = min(tl, L)
    nout = w_bf16.shape[1]
    head = jax.ShapeDtypeStruct((B, SB_HEADS, L, HEAD_DIM), BF16)
    head_spec = pl.BlockSpec((1, SB_HEADS, tl, HEAD_DIM), lambda b, i: (b, 0, i, 0))
    return pl.pallas_call(
        _inproj0_body,
        grid=(B, L // tl),
        in_specs=[pl.BlockSpec((1, tl, D), lambda b, i: (b, i, 0)),
                  pl.BlockSpec((D, nout), lambda b, i: (0, 0))],
        out_specs=[head_spec, head_spec, head_spec,
                   pl.BlockSpec((SSM_LANE_TILES, tl // SSM_CHUNK, 1, SSM_CHUNK, 128), lambda b, i: (0, i, b, 0, 0))],
        out_shape=[head, head, head,
                   jax.ShapeDtypeStruct((SSM_LANE_TILES, L // SSM_CHUNK, B, SSM_CHUNK, 128), F32)],
        compiler_params=_params("parallel", "parallel"),
        name="inproj0",
    )(x, w_bf16)


SB_BLOCK = 256
SB_HEADS_PER_STEP = 4
EXP_UNDERFLOW = -104.0


def _sb_attn_body(q_ref, k_ref, v_ref, o_ref, acc_scr, cs_scr):
    i = pl.program_id(2)
    t = q_ref.shape[2]
    nh = q_ref.shape[1]
    row = lax.broadcasted_iota(jnp.int32, (t, t), 0)
    col = lax.broadcasted_iota(jnp.int32, (t, t), 1)
    suffix = jnp.where(row > col, 1.0, 0.0).astype(BF16)
    below = col < row

    def logits(hh, k0):
        return _dot_nt(q_ref[0, hh], k_ref[0, hh, pl.ds(k0, t), :])

    def block(hh, k0, diagonal, z):
        vb = v_ref[0, hh, pl.ds(k0, t), :]
        sp = jnp.maximum(z, 0.0) + jnp.log(1.0 + jnp.exp(-jnp.abs(z)))
        l1 = jnp.where(below, -sp, 0.0) if diagonal else -sp
        within = _dot(l1.astype(BF16), suffix)
        if diagonal:
            w = jnp.where(below, jnp.exp(z - sp + within), 0.0)
            acc_scr[hh] = _dot(w.astype(BF16), vb)
            cs = jnp.sum(l1, axis=1, keepdims=True)
        else:
            w = jnp.exp(z - sp + within + cs_scr[hh])
            acc_scr[hh] += _dot(w.astype(BF16), vb)
            cs = cs_scr[hh] + jnp.sum(l1, axis=1, keepdims=True)
        cs_scr[hh] = cs
        return cs

    def alive(css):
        return (jnp.max(functools.reduce(jnp.maximum, css)) > EXP_UNDERFLOW).astype(jnp.int32)

    def all_heads(k0, diagonal):
        css = []
        z_next = logits(0, k0)
        for hh in range(nh):
            z = z_next
            if hh + 1 < nh:
                z_next = logits(hh + 1, k0)
            css.append(block(hh, k0, diagonal, z))
        return alive(css)

    first = all_heads(pl.multiple_of(i * t, t), True)

    def cond(c):
        return jnp.logical_and(c[0] <= i, c[1] > 0)

    def body(c):
        return c[0] + 1, all_heads(pl.multiple_of((i - c[0]) * t, t), False)

    lax.while_loop(cond, body, (jnp.int32(1), first))
    o_ref[0] = jnp.concatenate([acc_scr[hh] for hh in range(nh)], axis=-1).astype(o_ref.dtype)


def _sb_attention(q, k, v):
    B, H, L, d = q.shape
    t = min(SB_BLOCK, L)
    nh = SB_HEADS_PER_STEP
    return pl.pallas_call(
        _sb_attn_body,
        grid=(B, H // nh, L // t),
        in_specs=[pl.BlockSpec((1, nh, t, d), lambda b, h, i: (b, h, i, 0)),
                  pl.BlockSpec((1, nh, L, d), lambda b, h, i: (b, h, 0, 0)),
                  pl.BlockSpec((1, nh, L, d), lambda b, h, i: (b, h, 0, 0))],
        out_specs=pl.BlockSpec((1, t, nh * d), lambda b, h, i: (b, i, h)),
        out_shape=jax.ShapeDtypeStruct((B, L, H * d), BF16),
        scratch_shapes=[pltpu.VMEM((nh, t, d), F32), pltpu.VMEM((nh, t, 1), F32)],
        compiler_params=_params("parallel", "parallel", "arbitrary"),
        name="sb_attention",
    )(q, k, v)


def _ssm_tables(lam_re, lam_im, log_dt, b_re, b_im, c_re, c_im, d_skip):
    T, G, P, H = SSM_CHUNK, SSM_GROUPS, SSM_STATE, SSM_GROUP
    hp = lax.Precision.HIGHEST
    dt = jnp.exp(log_dt.astype(F32))[:, None]
    lr = lam_re.astype(F32)
    li = lam_im.astype(F32)
    mag = jnp.exp(lr * dt)
    a_re = mag * jnp.cos(li * dt)
    a_im = mag * jnp.sin(li * dt)
    den = lr * lr + li * li
    nr = a_re - 1.0
    f_re = (nr * lr + a_im * li) / den
    f_im = (a_im * lr - nr * li) / den
    br = b_re.astype(F32)
    bi = b_im.astype(F32)
    bb_re = f_re[..., None] * br - f_im[..., None] * bi
    bb_im = f_re[..., None] * bi + f_im[..., None] * br
    tau = jnp.arange(T + 1, dtype=F32)[None, :, None]
    pmag = jnp.exp(tau * (lr * dt)[:, None, :])
    pw_re = pmag * jnp.cos(tau * (li * dt)[:, None, :])
    pw_im = pmag * jnp.sin(tau * (li * dt)[:, None, :])
    cr = c_re.astype(F32)[:, None]
    ci = c_im.astype(F32)[:, None]
    cpw_re = cr * pw_re[:, :, None, :] - ci * pw_im[:, :, None, :]
    cpw_im = cr * pw_im[:, :, None, :] + ci * pw_re[:, :, None, :]
    kern = (jnp.einsum('gthp,gpi->gthi', cpw_re, bb_re, precision=hp)
            - jnp.einsum('gthp,gpi->gthi', cpw_im, bb_im, precision=hp))
    s_idx = np.arange(T)[:, None]
    t_idx = np.arange(T)[None, :]
    lag = np.clip(t_idx - s_idx, 0, T)
    toe = kern[:, lag]
    toe = jnp.where(jnp.asarray(t_idx >= s_idx)[None, :, :, None, None], toe, 0.0)
    intra = toe.transpose(0, 1, 4, 2, 3).reshape(G, T * H, T * H)
    rev = np.arange(T - 1, -1, -1)
    qr = pw_re[:, rev][:, :, None, :]
    qi = pw_im[:, rev][:, :, None, :]
    bbr = bb_re.transpose(0, 2, 1)[:, None]
    bbi = bb_im.transpose(0, 2, 1)[:, None]
    st_re = (qr * bbr - qi * bbi).reshape(G, T * H, P)
    st_im = (qr * bbi + qi * bbr).reshape(G, T * H, P)
    to_st = jnp.concatenate([st_re, st_im, st_im, st_re], axis=-1)
    top = cpw_re[:, 1:].transpose(0, 3, 1, 2).reshape(G, P, T * H)
    bot = (-cpw_im[:, 1:]).transpose(0, 3, 1, 2).reshape(G, P, T * H)
    from_st = jnp.concatenate([top, bot], axis=1)
    at_re = pw_re[:, T]
    at_im = pw_im[:, T]
    zeros = jnp.zeros_like(at_re)
    adv = jnp.stack([jnp.concatenate([at_re, at_re], -1),
                     jnp.concatenate([-at_im, at_im], -1),
                     jnp.concatenate([at_im, -at_im], -1)]
                    + [jnp.concatenate([zeros, zeros], -1)] * 5, axis=1)
    dvec = jnp.tile(d_skip.astype(F32).reshape(G, 1, H), (1, 1, T))
    return intra, to_st, from_st, adv, dvec


def _ssm_expand(tables):
    intra, to_st, from_st, adv, dvec = tables
    V, W, T, H, P = SSM_LANE_TILES, SSM_TILE_GROUPS, SSM_CHUNK, SSM_GROUP, SSM_STATE
    n = T * W * H
    k_in = lax.broadcasted_iota(jnp.int32, (T * H, n), 0)
    col = lax.broadcasted_iota(jnp.int32, (T * H, n), 1)
    spread_th = (k_in == (col // 128) * H + col % H).astype(F32)
    spread_pq = (k_in == (col // (W * 2 * P)) * 2 * P + col % (2 * P)).astype(F32)
    row_g = (lax.broadcasted_iota(jnp.int32, (n, 1), 0) // H) % W
    st_row_g = lax.broadcasted_iota(jnp.int32, (W * 2 * P, 1), 0) // (2 * P)
    col_g_th = (lax.broadcasted_iota(jnp.int32, (1, n), 1) // H) % W
    col_g_pq = (lax.broadcasted_iota(jnp.int32, (1, n), 1) // (2 * P)) % W

    def blockdiag(rows, spread, row_group, col_group):
        full = jnp.einsum('vrk,kc->vrc', rows, spread)
        return jnp.where(row_group == col_group, full, 0.0).astype(BF16)

    by_row = lambda t: t.reshape(V, W, T, H, T * H).transpose(0, 2, 1, 3, 4).reshape(V, n, T * H)
    m8 = blockdiag(by_row(intra), spread_th, row_g, col_g_th)
    ws8 = blockdiag(by_row(to_st), spread_pq, row_g, col_g_pq)
    wi8 = blockdiag(from_st.reshape(V, W * 2 * P, T * H), spread_th, st_row_g, col_g_th)
    adv8 = adv.reshape(V, W, 8, 2 * P).transpose(0, 2, 1, 3).reshape(V, 8, W * 2 * P)
    d8 = jnp.broadcast_to(dvec.reshape(V, 1, W, T, H)[:, :, :, 0:1], (V, 1, W, T, H))
    d8 = d8.transpose(0, 1, 3, 2, 4).reshape(V, 1, T * W * H)
    return m8, ws8, wi8, adv8, d8


def _ssm_body(u_ref, m8_ref, ws8_ref, wi8_ref, adv_ref, dvec_ref, y_ref, s_scr, xin_scr, x1_scr, x2_scr, *, bsz):
    half = x1_scr.shape[1]

    @pl.when(pl.program_id(1) == 0)
    def _():
        x1_scr[...] = jnp.zeros_like(x1_scr)
        x2_scr[...] = jnp.zeros_like(x2_scr)

    u = u_ref[0]
    ub = u.astype(BF16)
    s_scr[...] = _dot(ub, ws8_ref[0])
    a1 = adv_ref[0, 0:1, :]
    a2 = adv_ref[0, 1:2, :]
    a3 = adv_ref[0, 2:3, :]

    def step(c, carry):
        x1, x2 = carry
        r0 = pl.multiple_of(c * bsz, bsz)
        xin_scr[pl.ds(r0, bsz), :] = x1
        s = s_scr[pl.ds(r0, bsz), :]
        return (a1 * x1 + a2 * x2 + s[:, :half], a1 * x2 + a3 * x1 + s[:, half:])

    x1, x2 = lax.fori_loop(0, u.shape[0] // bsz, step, (x1_scr[...], x2_scr[...]))
    x1_scr[...] = x1
    x2_scr[...] = x2
    y_ref[0] = _dot(ub, m8_ref[0]) + _dot(xin_scr[...].astype(BF16), wi8_ref[0]) + dvec_ref[0] * u


def _ssm(u5, tables, rows=256):
    m8, ws8, wi8, adv8, d8 = _ssm_expand(tables)
    V, C, B, T, _ = u5.shape
    width = T * 128
    half = SSM_TILE_GROUPS * 2 * SSM_STATE
    rows = min(rows, C * B)
    once = dict(pipeline_mode=pl.Buffered(1))
    y = pl.pallas_call(
        functools.partial(_ssm_body, bsz=B),
        grid=(V, C * B // rows),
        in_specs=[pl.BlockSpec((1, rows, width), lambda v, r: (v, r, 0)),
                  pl.BlockSpec((1, width, width), lambda v, r: (v, 0, 0), **once),
                  pl.BlockSpec((1, width, 2 * half), lambda v, r: (v, 0, 0), **once),
                  pl.BlockSpec((1, half, width), lambda v, r: (v, 0, 0), **once),
                  pl.BlockSpec((1, 8, half), lambda v, r: (v, 0, 0)),
                  pl.BlockSpec((1, 1, width), lambda v, r: (v, 0, 0))],
        out_specs=pl.BlockSpec((1, rows, width), lambda v, r: (v, r, 0)),
        out_shape=jax.ShapeDtypeStruct((V, C * B, width), F32),
        scratch_shapes=[pltpu.VMEM((rows, 2 * half), F32), pltpu.VMEM((rows, half), F32),
                        pltpu.VMEM((B, half), F32), pltpu.VMEM((B, half), F32)],
        compiler_params=_params("parallel", "arbitrary"),
        name="s5_scan",
    )(u5.reshape(V, C * B, width), m8, ws8, wi8, adv8, d8)
    return y.reshape(V, C, B, T, 128)


def _outproj0_body(x_ref, oa_ref, y_ref, wglu_ref, woa_ref, wob_ref, g_ref, b_ref, o_ref):
    tl = x_ref.shape[1]
    y = jnp.concatenate([y_ref[v, :, 0].reshape(tl, 128) for v in range(SSM_LANE_TILES)], axis=-1)
    h = _gelu(y)
    ob = h * _sigmoid(_dot(h.astype(BF16), wglu_ref[...]))
    m = _dot(oa_ref[0], woa_ref[...]) + _dot(ob.astype(BF16), wob_ref[...])
    o_ref[0] = _layer_norm(ALPHA * x_ref[0] + m, g_ref[...], b_ref[...])


def _outproj0(x, oa, y5, wglu, woa, wob, g, b, tl=512):
    B, L, D = x.shape
    tl = min(tl, L)
    row = lambda b, i: (b, i, 0)
    fix = lambda b, i: (0, 0)
    return pl.pallas_call(
        _outproj0_body,
        grid=(B, L // tl),
        in_specs=[pl.BlockSpec((1, tl, D), row), pl.BlockSpec((1, tl, SB_WIDTH), row),
                  pl.BlockSpec((SSM_LANE_TILES, tl // SSM_CHUNK, 1, SSM_CHUNK, 128), lambda b, i: (0, i, b, 0, 0)),
                  pl.BlockSpec((SSM_WIDTH, SSM_WIDTH), fix), pl.BlockSpec((SB_WIDTH, D), fix),
                  pl.BlockSpec((SSM_WIDTH, D), fix), pl.BlockSpec((1, D), fix), pl.BlockSpec((1, D), fix)],
        out_specs=pl.BlockSpec((1, tl, D), row),
        out_shape=jax.ShapeDtypeStruct((B, L, D), F32),
        compiler_params=_params("parallel", "parallel"),
        name="outproj0",
    )(x, oa, y5, wglu, woa, wob, g, b)


def _outproj1_body(x_ref, o_ref_in, w_ref, g_ref, b_ref, o_ref):
    m = _dot(o_ref_in[...], w_ref[...])
    o_ref[...] = _layer_norm(ALPHA * x_ref[...] + m, g_ref[...], b_ref[...])


def _outproj1(x, o, w, g, b, tm=512):
    N, D = x.shape
    tm = min(tm, N)
    row = lambda i: (i, 0)
    fix = lambda i: (0, 0)
    return pl.pallas_call(
        _outproj1_body,
        grid=(N // tm,),
        in_specs=[pl.BlockSpec((tm, D), row), pl.BlockSpec((tm, o.shape[1]), row),
                  pl.BlockSpec(w.shape, fix), pl.BlockSpec((1, D), fix), pl.BlockSpec((1, D), fix)],
        out_specs=pl.BlockSpec((tm, D), row),
        out_shape=jax.ShapeDtypeStruct((N, D), F32),
        compiler_params=_params("parallel"),
        name="outproj1",
    )(x, o, w, g, b)


def _router_body(x_ref, wh_ref, wl_ref, b_ref, g_ref, grp_ref):
    xh, xl = _split_bf16(x_ref[...])
    wh = wh_ref[...]
    logits = _dot_nt(wh, xh) + _dot_nt(wh, xl) + _dot_nt(wl_ref[...], xh)
    aff = _sigmoid(logits)
    sel = aff + b_ref[...]
    E, K = EXPERTS_PER_GROUP, N_GROUPS
    s = [sel[e:e + 1, :] for e in range(N_EXPERTS)]
    a = [aff[e:e + 1, :] for e in range(N_EXPERTS)]
    gscore = []
    for k in range(K):
        v = s[k * E:(k + 1) * E]
        best = None
        for x in range(E):
            for y in range(x + 1, E):
                pair = v[x] + v[y]
                best = pair if best is None else jnp.maximum(best, pair)
        gscore.append(best)
    top = functools.reduce(jnp.maximum, gscore)
    is_g = []
    taken = None
    for k in range(K):
        hit = gscore[k] == top
        if taken is None:
            is_g.append(hit)
            taken = hit
        else:
            is_g.append(jnp.logical_and(hit, jnp.logical_not(taken)))
            taken = jnp.logical_or(taken, hit)

    def pick(rows, j):
        out = rows[(K - 1) * E + j]
        for k in range(K - 2, -1, -1):
            out = jnp.where(is_g[k], rows[k * E + j], out)
        return out

    v = [pick(s, j) for j in range(E)]
    av = [pick(a, j) for j in range(E)]
    chosen = []
    for j in range(E):
        r = jnp.zeros_like(v[j])
        for j2 in range(E):
            if j2 == j:
                continue
            ahead = (v[j2] >= v[j]) if j2 < j else (v[j2] > v[j])
            r = r + jnp.where(ahead, 1.0, 0.0)
        chosen.append(r < 2.0)
    wj = [jnp.where(chosen[j], av[j], 0.0) for j in range(E)]
    tot = wj[0] + wj[1] + wj[2] + wj[3]
    gj = [w / tot for w in wj]
    rows = [jnp.where(is_g[e // E], gj[e % E], 0.0) for e in range(N_EXPERTS)]
    g_ref[...] = jnp.concatenate(rows, axis=0)
    grp_ref[...] = jnp.concatenate([jnp.where(is_g[k], 1.0, 0.0) for k in range(K)], axis=0)


def _router(x, w_router, b_router, tm=512):
    N, D = x.shape
    tm = min(tm, N)
    wt = w_router.astype(F32).T
    wh, wl = _split_bf16(wt)
    fix = lambda i: (0, 0)
    return pl.pallas_call(
        _router_body,
        grid=(N // tm,),
        in_specs=[pl.BlockSpec((tm, D), lambda i: (i, 0)), pl.BlockSpec((N_EXPERTS, D), fix),
                  pl.BlockSpec((N_EXPERTS, D), fix), pl.BlockSpec((N_EXPERTS, 1), fix)],
        out_specs=[pl.BlockSpec((N_EXPERTS, tm), lambda i: (0, i)), pl.BlockSpec((N_GROUPS, tm), lambda i: (0, i))],
        out_shape=[jax.ShapeDtypeStruct((N_EXPERTS, N), F32), jax.ShapeDtypeStruct((N_GROUPS, N), F32)],
        compiler_params=_params("parallel"),
        name="router",
    )(x, wh, wl, b_router.astype(F32).reshape(N_EXPERTS, 1))


MOE_TILE = 1024
MOE_PASS_ROWS = 304
MOE_CHUNK = 256
MOE_EXPERTS_PER_STEP = 2
ROW_ALIGN = 16


def _moe_body(seg_ref, x_ref, gt_ref, grp_ref, w1_ref, w3_ref, w2_ref, g_ref, b_ref, o_ref,
              xs_scr, ys_scr, pt_scr, gs_scr):
    i = pl.program_id(0)
    e = pl.program_id(1)
    T = x_ref.shape[0]
    K = N_GROUPS
    cap = MOE_PASS_ROWS
    ch = min(MOE_CHUNK, T)

    @pl.when(e == 0)
    def _():
        grp = grp_ref[...]
        xb = x_ref[...].astype(BF16)
        gates = gt_ref[...]
        gh, gm = _split_bf16(gates)
        gl = (gates - gh.astype(F32) - gm.astype(F32)).astype(BF16)
        cnt = jnp.sum(grp, axis=1, keepdims=True)
        off = [jnp.zeros((1, 1), F32)]
        for k in range(K - 1):
            off.append(off[-1] + cnt[k:k + 1])
        r_i = lax.broadcasted_iota(jnp.int32, (ch, T), 0)
        c_i = lax.broadcasted_iota(jnp.int32, (ch, T), 1)
        lane = lax.broadcasted_iota(jnp.int32, (2 * K, T), 1)
        run = jnp.concatenate([grp, jnp.zeros_like(grp)], axis=0)
        sh = 1
        while sh < T:
            run = run + jnp.where(lane >= sh, pltpu.roll(run, sh, 1), 0.0)
            sh *= 2
        pos_row = jnp.zeros((1, T), F32)
        for k in range(K):
            pos_row = pos_row + grp[k:k + 1, :] * (off[k] + run[k:k + 1, :] - 1.0)
        pos_t = jnp.broadcast_to(pos_row, (2 * K, T))
        pos_col = [pos_t[:, c * ch:(c + 1) * ch].T[:, 0:1] for c in range(T // ch)]
        for c in range(T // ch):
            c0 = c * ch
            perm = jnp.where((r_i + c0).astype(F32) == pos_row, 1.0, 0.0).astype(BF16)
            xs_scr[c0:c0 + ch, :] = _dot(perm, xb).astype(BF16)
            gs_scr[c0:c0 + ch, :] = _dot_nt(perm, gh) + _dot_nt(perm, gm) + _dot_nt(perm, gl)
            pt_scr[c0:c0 + ch, :] = jnp.where(c_i.astype(F32) == pos_col[c], 1.0, 0.0).astype(BF16)
        xs_scr[T:, :] = jnp.zeros((cap, xs_scr.shape[1]), BF16)
        gs_scr[T:, :] = jnp.zeros((cap, gs_scr.shape[1]), F32)
        ys_scr[...] = jnp.zeros_like(ys_scr)

    ne = w1_ref.shape[0]
    k = (e * ne) // EXPERTS_PER_GROUP
    off = seg_ref[i, k]
    cnt = seg_ref[i, K + k]
    start0 = (off // ROW_ALIGN) * ROW_ALIGN
    n_pass = (off + cnt - start0 + cap - 1) // cap

    def one_pass(n, c):
        start = pl.multiple_of(start0 + n * cap, ROW_ALIGN)
        xc = xs_scr[pl.ds(start, cap), :]
        gsc = gs_scr[pl.ds(start, cap), :]
        lane = lax.broadcasted_iota(jnp.int32, gsc.shape, 1)
        y = None
        for x in range(ne):
            a = _dot(xc, w1_ref[x])
            hm = (a * _sigmoid(a)) * _dot(xc, w3_ref[x])
            gcol = jnp.sum(jnp.where(lane == e * ne + x, gsc, 0.0), axis=1, keepdims=True)
            yx = gcol * _dot(hm.astype(BF16), w2_ref[x])
            y = yx if y is None else y + yx
        ys_scr[pl.ds(start, cap), :] += y
        return c

    lax.fori_loop(0, n_pass, one_pass, 0)

    @pl.when(e == pl.num_programs(1) - 1)
    def _():
        y = _dot(pt_scr[...], ys_scr[0:T, :].astype(BF16))
        o_ref[...] = _layer_norm(ALPHA * x_ref[...] + y, g_ref[...], b_ref[...])


def _moe(x, gates_t, grp_t, w1, w3, w2, g, b):
    N, D = x.shape
    T = min(MOE_TILE, N)
    K = N_GROUPS
    cnt = grp_t.reshape(K, N // T, T).sum(-1).astype(jnp.int32).T
    seg = jnp.concatenate([jnp.cumsum(cnt, axis=1) - cnt, cnt], axis=1)
    fix = lambda i, e, s: (0, 0)
    ne = MOE_EXPERTS_PER_STEP
    rows = T + MOE_PASS_ROWS
    grid_spec = pltpu.PrefetchScalarGridSpec(
        num_scalar_prefetch=1,
        grid=(N // T, N_EXPERTS // ne),
        in_specs=[pl.BlockSpec((T, D), lambda i, e, s: (i, 0)),
                  pl.BlockSpec((N_EXPERTS, T), lambda i, e, s: (0, i)),
                  pl.BlockSpec((K, T), lambda i, e, s: (0, i)),
                  pl.BlockSpec((ne, D, D_EXPERT), lambda i, e, s: (e, 0, 0)),
                  pl.BlockSpec((ne, D, D_EXPERT), lambda i, e, s: (e, 0, 0)),
                  pl.BlockSpec((ne, D_EXPERT, D), lambda i, e, s: (e, 0, 0)),
                  pl.BlockSpec((1, D), fix), pl.BlockSpec((1, D), fix)],
        out_specs=pl.BlockSpec((T, D), lambda i, e, s: (i, 0)),
        scratch_shapes=[pltpu.VMEM((rows, D), BF16), pltpu.VMEM((rows, D), F32),
                        pltpu.VMEM((T, T), BF16), pltpu.VMEM((rows, N_EXPERTS), F32)])
    return pl.pallas_call(
        _moe_body,
        grid_spec=grid_spec,
        out_shape=jax.ShapeDtypeStruct((N, D), F32),
        compiler_params=_params("parallel", "arbitrary"),
        name="moe",
    )(seg, x, gates_t, grp_t, w1, w3, w2, g, b)


NSA_KVW = NSA_KV_HEADS * HEAD_DIM
NSA_ROPE_W = NSA_HEADS * HEAD_DIM + 2 * NSA_KVW
NSA_Q_SCALE = ATTN_SCALE * math.log2(math.e)


def _rot_cols(w):
    K, n = w.shape
    w3 = w.reshape(K, n // HEAD_DIM, 2, HEAD_DIM // 2)
    return jnp.stack([-w3[:, :, 1], w3[:, :, 0]], axis=2).reshape(K, n)


def _inproj1_weights(w_in):
    H, G, d = NSA_HEADS, NSA_KV_HEADS, HEAD_DIM
    cuts = [H * d + i * NSA_KVW for i in range(7)]
    q, kc, vc, ks, vs, kw, vw, gate = jnp.split(w_in.astype(F32), cuts, axis=1)
    w_all = jnp.concatenate([q, ks, kw, kc, vc], axis=1).astype(BF16)
    w_t = jnp.concatenate([vs, vw, gate], axis=1).T.astype(BF16)
    return w_all, w_t


def _rope_tables(pos):
    inv = ROPE_THETA ** (-jnp.arange(0, HEAD_DIM, 2, dtype=F32) / HEAD_DIM)
    ang = pos.astype(F32)[:, None] * inv[None, :]
    c, s = jnp.cos(ang), jnp.sin(ang)
    return jnp.concatenate([c, c], axis=-1), jnp.concatenate([s, s], axis=-1)


def _rope_roll_tables(pos):
    inv = ROPE_THETA ** (-jnp.arange(0, HEAD_DIM, 2, dtype=F32) / HEAD_DIM)
    ang = pos.astype(F32)[:, None] * inv[None, :]
    c, s, z = jnp.cos(ang), jnp.sin(ang), jnp.zeros_like(ang)
    two = lambda a, b: jnp.concatenate([a, b, a, b], axis=-1)
    return two(c, c), two(-s, z), two(z, s)


V_ROWS = HEAD_DIM + 16


def _inproj1_body(x_ref, w_ref, wt_ref, cos_ref, sinlo_ref, sinhi_ref, q_ref, ks_ref, kw_ref, kc_ref, vc_ref,
                  vst_ref, vwt_ref, gate_ref):
    xb = x_ref[0].astype(BF16)
    tl = xb.shape[0]
    d = HEAD_DIM
    p = _dot(xb, w_ref[...])
    pt = _dot_nt(wt_ref[...], xb)
    ones = jnp.ones((V_ROWS - d, tl), BF16)
    for n, ref in enumerate((vst_ref, vwt_ref)):
        for g in range(NSA_KV_HEADS):
            r0 = n * NSA_KVW + g * d
            ref[0, g, 0:d, :] = pt[r0:r0 + d, :].astype(BF16)
            ref[0, g, d:V_ROWS, :] = ones
    gate_ref[0] = pt[2 * NSA_KVW:, :]
    cos = cos_ref[...]
    sin_lo = sinlo_ref[...]
    sin_hi = sinhi_ref[...]
    roped = []
    for c in range(NSA_ROPE_W // 128):
        xc = p[:, c * 128:(c + 1) * 128]
        roped.append(xc * cos + pltpu.roll(xc, 96, 1) * sin_lo + pltpu.roll(xc, 32, 1) * sin_hi)

    def head(chunks, h):
        blk = chunks[h // 2]
        return blk[:, (h % 2) * d:(h % 2 + 1) * d]

    for h in range(NSA_HEADS):
        q_ref[0, h] = (head(roped, h) * NSA_Q_SCALE).astype(BF16)
    nq = NSA_HEADS // 2
    pos = pl.program_id(1) * tl + lax.broadcasted_iota(jnp.int32, (tl, 128 - d), 0)
    block_onehot = jnp.where(pos // SEL_LEN == lax.broadcasted_iota(jnp.int32, (tl, 128 - d), 1), 1.0, 0.0)
    for g in range(NSA_KV_HEADS):
        ks_ref[0, g] = jnp.concatenate([head(roped[nq:], g), block_onehot], axis=1).astype(BF16)
        kw_ref[0, g] = head(roped[nq + 2:], g).astype(BF16)
    for n, ref in enumerate((kc_ref, vc_ref)):
        for g in range(NSA_KV_HEADS):
            c0 = NSA_ROPE_W + n * NSA_KVW + g * d
            ref[0, g] = p[:, c0:c0 + d].astype(BF16)


def _inproj1(x, w_all, w_t, tl=256):
    B, L, D = x.shape
    tl = min(tl, L)
    G = NSA_KV_HEADS
    n_gate = w_t.shape[0] - 2 * NSA_KVW
    tables = _rope_roll_tables(jnp.arange(L))
    qh = jax.ShapeDtypeStruct((B, NSA_HEADS, L, HEAD_DIM), BF16)
    kvh = jax.ShapeDtypeStruct((B, G, L, HEAD_DIM), BF16)
    vth = jax.ShapeDtypeStruct((B, G, V_ROWS, L), BF16)
    q_spec = pl.BlockSpec((1, NSA_HEADS, tl, HEAD_DIM), lambda b, i: (b, 0, i, 0))
    kv_spec = pl.BlockSpec((1, G, tl, HEAD_DIM), lambda b, i: (b, 0, i, 0))
    ksel_spec = pl.BlockSpec((1, G, tl, 128), lambda b, i: (b, 0, i, 0))
    vt_spec = pl.BlockSpec((1, G, V_ROWS, tl), lambda b, i: (b, 0, 0, i))
    tab_spec = pl.BlockSpec((tl, 128), lambda b, i: (i, 0))
    return pl.pallas_call(
        _inproj1_body,
        grid=(B, L // tl),
        in_specs=[pl.BlockSpec((1, tl, D), lambda b, i: (b, i, 0)),
                  pl.BlockSpec(w_all.shape, lambda b, i: (0, 0)),
                  pl.BlockSpec(w_t.shape, lambda b, i: (0, 0)),
                  tab_spec, tab_spec, tab_spec],
        out_specs=([q_spec, ksel_spec] + [kv_spec] * 3 + [vt_spec] * 2
                   + [pl.BlockSpec((1, n_gate, tl), lambda b, i: (b, 0, i))]),
        out_shape=([qh, jax.ShapeDtypeStruct((B, G, L, 128), BF16)] + [kvh] * 3 + [vth] * 2
                   + [jax.ShapeDtypeStruct((B, n_gate, L), F32)]),
        compiler_params=_params("parallel", "parallel"),
        name="inproj1",
    )(x, w_all, w_t, *tables)


def _compress_body(kc_ref, vc_ref, posk_ref, posv_ref, w1k_ref, w1v_ref, w2k_ref, w2kr_ref, w2v_ref,
                   cos_ref, sin_ref, kcmp_ref, vcmp_ref):
    def hidden(a_ref, pos_ref, w1_ref):
        a = a_ref[0, 0].astype(F32)
        nrow = a.shape[0]
        lo = _dot((a + pos_ref[0:1, :]).astype(BF16), w1_ref[0])
        hi = _dot((a + pos_ref[1:2, :]).astype(BF16), w1_ref[1])
        hi_next = pltpu.roll(hi, nrow - 1, 0)
        return _gelu(lo + hi_next).astype(BF16)

    hk = hidden(kc_ref, posk_ref, w1k_ref)
    kcmp = _dot(hk, w2k_ref[...]) * cos_ref[...] + _dot(hk, w2kr_ref[...]) * sin_ref[...]
    kcmp_ref[0, 0] = kcmp.astype(BF16)
    hv = hidden(vc_ref, posv_ref, w1v_ref)
    vcmp_ref[0, 0] = _dot_nt(w2v_ref[...], hv).astype(BF16)


def _compress(kc, vc, pos_k, w1_k, w2_k, pos_v, w1_v, w2_v):
    B, G, L, d = kc.shape
    half = CMP_STRIDE * d
    nb = L // CMP_STRIDE
    kc2 = kc.reshape(B, G, nb, half)
    vc2 = vc.reshape(B, G, nb, half)
    posk = pos_k.astype(F32).reshape(2, half)
    posv = pos_v.astype(F32).reshape(2, half)
    w1k = w1_k.astype(BF16).reshape(2, half, CMP_HIDDEN)
    w1v = w1_v.astype(BF16).reshape(2, half, CMP_HIDDEN)
    w2k = w2_k.astype(F32)
    cos, sin = _rope_tables(jnp.arange(nb) * CMP_STRIDE + CMP_LEN - 1)
    blk = pl.BlockSpec((1, 1, nb, half), lambda b, g: (b, g, 0, 0))
    out = pl.BlockSpec((1, 1, nb, d), lambda b, g: (b, g, 0, 0))
    out_t = pl.BlockSpec((1, 1, d, nb), lambda b, g: (b, g, 0, 0))
    fix2 = lambda b, g: (0, 0)
    fix3 = lambda b, g: (0, 0, 0)
    return pl.pallas_call(
        _compress_body,
        grid=(B, G),
        in_specs=[blk, blk, pl.BlockSpec((2, half), fix2), pl.BlockSpec((2, half), fix2),
                  pl.BlockSpec((2, half, CMP_HIDDEN), fix3), pl.BlockSpec((2, half, CMP_HIDDEN), fix3),
                  pl.BlockSpec((CMP_HIDDEN, d), fix2), pl.BlockSpec((CMP_HIDDEN, d), fix2),
                  pl.BlockSpec((d, CMP_HIDDEN), fix2), pl.BlockSpec((nb, d), fix2), pl.BlockSpec((nb, d), fix2)],
        out_specs=[out, out_t],
        out_shape=[jax.ShapeDtypeStruct((B, G, nb, d), BF16), jax.ShapeDtypeStruct((B, G, d, nb), BF16)],
        compiler_params=_params("parallel", "parallel"),
        name="compress",
    )(kc2, vc2, posk, posv, w1k, w1v, w2k.astype(BF16), _rot_cols(w2k).astype(BF16), w2_v.T.astype(BF16), cos, sin)


NSA_KBLOCK = 256
NSA_GROUPS_PER_STEP = 4
NSA_HEADS_PER_CHAIN = 4
NSA_SCORE_LOOKAHEAD = 3


def _nsa_body(q_ref, kcmp_ref, vcmpt_ref, ks_ref, vst_ref, kw_ref, vwt_ref, gate_ref, o_ref, qsel_scr, *, seq):
    i = pl.program_id(2)
    tq = q_ref.shape[2]
    gb = kcmp_ref.shape[1]
    tk = min(NSA_KBLOCK, seq)
    R, d = NSA_REP, HEAD_DIM
    nq = R * tq
    nb = seq // SEL_LEN
    mc = seq // CMP_STRIDE
    t_row = i * tq + lax.broadcasted_iota(jnp.int32, (1, tq), 1)
    t_all = jnp.concatenate([t_row] * R, axis=1)
    m_col = lax.broadcasted_iota(jnp.int32, (mc, 1), 0)
    m_row = lax.broadcasted_iota(jnp.int32, (1, mc), 1)
    n_col = lax.broadcasted_iota(jnp.int32, (nb, 1), 0)

    def q_rows(g):
        return q_ref[0, g * R:(g + 1) * R].reshape(nq, d)

    valid_c = (m_col * CMP_STRIDE + (CMP_LEN - 1)) <= t_all
    ovl = jnp.logical_and(m_row * CMP_STRIDE < (n_col + 1) * SEL_LEN,
                          m_row * CMP_STRIDE + CMP_LEN > n_col * SEL_LEN)
    ovl = jnp.where(ovl, 1.0, 0.0).astype(BF16)
    cur = t_row // SEL_LEN
    forced = jnp.logical_or(n_col == 0, jnp.logical_or(n_col == cur, n_col == cur - 1))
    bonus = jnp.where(forced, FORCE_BONUS, 0.0)
    valid_s = n_col * SEL_LEN <= t_row
    o_c = []
    ties = [jnp.where(n_col > n2, 1.0, 0.0) for n2 in range(nb)]
    cmp_scores = [_dot_nt(kcmp_ref[0, g], q_rows(g)) for g in range(gb)]
    for g in range(gb):
        s = jnp.where(valid_c, cmp_scores[g], NEG)
        e = jnp.where(valid_c, jnp.exp2(s - jnp.max(s, axis=0, keepdims=True)), 0.0)
        den = jnp.sum(e, axis=0, keepdims=True)
        p = e / jnp.where(den > 0.0, den, 1.0)
        o_c.append(_dot(vcmpt_ref[0, g], p.astype(BF16)))
        psum = p[:, 0:tq]
        for r in range(1, R):
            psum = psum + p[:, r * tq:(r + 1) * tq]
        ph, plo = _split_bf16(psum)
        score = jnp.where(valid_s, _dot(ovl, ph) + _dot(ovl, plo) + bonus, NEG)
        rank = jnp.zeros((nb, tq), F32)
        for n2 in range(nb):
            other = score[n2:n2 + 1, :]
            rank = rank + jnp.where(other > score, 1.0, jnp.where(other == score, ties[n2], 0.0))
        pen = jnp.where(rank < float(SEL_TOP), 0.0, NEG).T.astype(BF16)
        pen = jnp.concatenate([pen, jnp.zeros((tq, 128 - d - nb), BF16)], axis=1)
        qsel_scr[g] = jnp.concatenate([q_rows(g), jnp.concatenate([pen] * R, axis=0)], axis=1)

    k_col = lax.broadcasted_iota(jnp.int32, (tk, 1), 0)
    hp = NSA_HEADS_PER_CHAIN
    j_hi = (i * tq + tq - 1) // tk + 1

    def sweep(q_fn, k_ref, vt_ref, j_lo, bias_fn, bias_every_block):
        chains = [(g, h0) for g in range(gb) for h0 in range(0, R, hp)]

        def body(j, state, with_bias):
            k0 = pl.multiple_of(j * tk, tk)
            bias = jnp.concatenate([bias_fn(k0 + k_col)] * hp, axis=1) if with_bias else None

            def scores(c):
                g, h0 = chains[c]
                sc = _dot_nt(k_ref[0, g, pl.ds(k0, tk), :], q_fn(g, h0))
                return sc + bias if with_bias else sc

            new_state = []
            ahead = NSA_SCORE_LOOKAHEAD
            pending = [scores(c) for c in range(min(ahead, len(chains)))]
            for c, (m_run, acc) in enumerate(state):
                sc = pending.pop(0)
                if c + ahead < len(chains):
                    pending.append(scores(c + ahead))
                vtb = vt_ref[0, chains[c][0], :, pl.ds(k0, tk)]
                m_new = jnp.maximum(m_run, jnp.max(sc, axis=0, keepdims=True))
                pr = jnp.exp2(sc - m_new)
                new_state.append((m_new, jnp.exp2(m_run - m_new) * acc + _dot(vtb, pr.astype(BF16))))
            return tuple(new_state)

        state = tuple((jnp.full((1, hp * tq), NEG, F32), jnp.zeros((V_ROWS, hp * tq), F32)) for _ in chains)
        state = lax.fori_loop(j_lo, j_hi - 1, functools.partial(body, with_bias=bias_every_block), state)
        state = body(j_hi - 1, state, True)
        outs = []
        for g in range(gb):
            acc = jnp.concatenate([a for (cg, _), (_, a) in zip(chains, state) if cg == g], axis=1)
            outs.append(acc[0:d] / acc[d:d + 1])
        return outs

    def causal_bias(kpos):
        return jnp.where(kpos <= t_row, 0.0, NEG)

    o_s = sweep(lambda g, h0: qsel_scr[g, h0 * tq:(h0 + hp) * tq, :], ks_ref, vst_ref, 0, causal_bias, False)

    def win_bias(kpos):
        return jnp.where(jnp.logical_and(kpos <= t_row, kpos > t_row - WINDOW), 0.0, NEG)

    o_w = sweep(lambda g, h0: q_ref[0, g * R + h0:g * R + h0 + hp].reshape(hp * tq, d), kw_ref, vwt_ref,
                jnp.maximum(i * tq - (WINDOW - 1), 0) // tk, win_bias, True)

    outs = []
    for g in range(gb):
        row0 = (pl.program_id(1) * gb + g) * 3 * R

        def gate(branch):
            rows = [gate_ref[0, pl.ds(row0 + 3 * r + branch, 1), :] for r in range(R)]
            return _sigmoid(jnp.concatenate(rows, axis=1))

        o_t = gate(0) * o_c[g] + gate(1) * o_s[g] + gate(2) * o_w[g]
        for r in range(R):
            outs.append(o_t[:, r * tq:(r + 1) * tq].T)
    o_ref[0] = jnp.concatenate(outs, axis=-1).astype(o_ref.dtype)


def _nsa_attention(q, kcmp, vcmpt, ks, vst, kw, vwt, gate):
    B, H, L, d = q.shape
    G, R = NSA_KV_HEADS, NSA_REP
    gb = NSA_GROUPS_PER_STEP
    tq = min(ATT_BLOCK, L)
    mc = L // CMP_STRIDE
    k_spec = pl.BlockSpec((1, gb, L, d), lambda b, g, i: (b, g, 0, 0))
    ksel_spec = pl.BlockSpec((1, gb, L, 128), lambda b, g, i: (b, g, 0, 0))
    vt_spec = pl.BlockSpec((1, gb, V_ROWS, L), lambda b, g, i: (b, g, 0, 0))
    return pl.pallas_call(
        functools.partial(_nsa_body, seq=L),
        grid=(B, G // gb, L // tq),
        in_specs=[pl.BlockSpec((1, gb * R, tq, d), lambda b, g, i: (b, g, i, 0)),
                  pl.BlockSpec((1, gb, mc, d), lambda b, g, i: (b, g, 0, 0)),
                  pl.BlockSpec((1, gb, d, mc), lambda b, g, i: (b, g, 0, 0)),
                  ksel_spec, vt_spec, k_spec, vt_spec,
                  pl.BlockSpec((1, gate.shape[1], tq), lambda b, g, i: (b, 0, i))],
        out_specs=pl.BlockSpec((1, tq, gb * R * d), lambda b, g, i: (b, i, g)),
        out_shape=jax.ShapeDtypeStruct((B, L, H * d), BF16),
        scratch_shapes=[pltpu.VMEM((gb, R * tq, 128), BF16)],
        compiler_params=_params("parallel", "parallel", "arbitrary"),
        name="nsa_attention",
    )(q, kcmp, vcmpt, ks, vst, kw, vwt, gate)


def kernel(x, w_in_0, ssm_lam_re, ssm_lam_im, ssm_log_dt, ssm_b_re, ssm_b_im, ssm_c_re, ssm_c_im, ssm_d, w_glu, w_out_0, ln_mix_g_0, ln_mix_b_0, ln_ffn_g_0, ln_ffn_b_0, w1_0, w3_0, w2_0, w_in_1, cmp_pos_k, cmp_w1_k, cmp_w2_k, cmp_pos_v, cmp_w1_v, cmp_w2_v, w_out_1, ln_mix_g_1, ln_mix_b_1, ln_ffn_g_1, ln_ffn_b_1, w1_1, w3_1, w2_1, w_router, b_router):
    B, L, D = x.shape
    N = B * L
    vec = lambda a: a.astype(F32).reshape(1, D)

    def ffn(h, w1, w3, w2, g, b):
        gates_t, grp_t = _router(h, w_router, b_router)
        return _moe(h, gates_t, grp_t, w1.astype(BF16), w3.astype(BF16), w2.astype(BF16), vec(g), vec(b))

    q, k, v, u = _inproj0(x, w_in_0.astype(BF16))
    o_a = _sb_attention(q, k, v)
    y = _ssm(u, _ssm_tables(ssm_lam_re, ssm_lam_im, ssm_log_dt, ssm_b_re, ssm_b_im, ssm_c_re, ssm_c_im, ssm_d))
    w_out_0b = w_out_0.astype(BF16)
    h = _outproj0(x, o_a, y, w_glu.astype(BF16), w_out_0b[:SB_WIDTH], w_out_0b[SB_WIDTH:],
                  vec(ln_mix_g_0), vec(ln_mix_b_0)).reshape(N, D)
    h = ffn(h, w1_0, w3_0, w2_0, ln_ffn_g_0, ln_ffn_b_0)

    q, ks, kw, kc, vc, vst, vwt, gate = _inproj1(h.reshape(B, L, D), *_inproj1_weights(w_in_1))
    kcmp, vcmpt = _compress(kc, vc, cmp_pos_k, cmp_w1_k, cmp_w2_k, cmp_pos_v, cmp_w1_v, cmp_w2_v)
    o = _nsa_attention(q, kcmp, vcmpt, ks, vst, kw, vwt, gate).reshape(N, NSA_HEADS * HEAD_DIM)
    h = _outproj1(h, o, w_out_1.astype(BF16), vec(ln_mix_g_1), vec(ln_mix_b_1))
    h = ffn(h, w1_1, w3_1, w2_1, ln_ffn_g_1, ln_ffn_b_1)
    return h.reshape(B, L, D)
```

```python
import functools
import math

import numpy as np
import jax
import jax.numpy as jnp
from jax import lax
from jax.experimental import pallas as pl
from jax.experimental.pallas import tpu as pltpu

F32 = jnp.float32
BF16 = jnp.bfloat16

D_MODEL = 1024
DEPTH = 2
SB_HEADS = 8
HEAD_DIM = 64
SB_WIDTH = SB_HEADS * HEAD_DIM
SSM_WIDTH = D_MODEL - SB_WIDTH
SSM_GROUP = 16
SSM_GROUPS = SSM_WIDTH // SSM_GROUP
SSM_STATE = 64
SSM_CHUNK = 16
SSM_LANE_TILES = SSM_WIDTH // 128
SSM_TILE_GROUPS = 128 // SSM_GROUP
NSA_HEADS = 16
NSA_KV_HEADS = 4
NSA_REP = NSA_HEADS // NSA_KV_HEADS
CMP_LEN = 32
CMP_STRIDE = 16
CMP_HIDDEN = 256
SEL_LEN = 64
SEL_TOP = 8
WINDOW = 512
ROPE_THETA = 10000.0
FORCE_BONUS = 1e4
NEG = -1e30
N_EXPERTS = 16
N_GROUPS = 4
EXPERTS_PER_GROUP = N_EXPERTS // N_GROUPS
D_EXPERT = 512
ALPHA = (2 * DEPTH) ** 0.25
LN_EPS = 1e-5
ATTN_SCALE = HEAD_DIM ** -0.5
ATT_BLOCK = 128
GELU_C = math.sqrt(2.0 / math.pi)


def _params(*sem):
    return pltpu.CompilerParams(dimension_semantics=sem, vmem_limit_bytes=56 * 1024 * 1024)


def _sigmoid(x):
    return 1.0 / (1.0 + jnp.exp(-x))


def _gelu(x):
    return 0.5 * x * (1.0 + jnp.tanh(GELU_C * (x + 0.044715 * (x * x * x))))


def _layer_norm(r, g, b):
    mu = jnp.mean(r, axis=-1, keepdims=True)
    d = r - mu
    var = jnp.mean(d * d, axis=-1, keepdims=True)
    return d * lax.rsqrt(var + LN_EPS) * g + b


def _dot(a, b):
    return jnp.dot(a, b, preferred_element_type=F32)


def _dot_nt(a, b):
    return lax.dot_general(a, b, (((1,), (1,)), ((), ())), preferred_element_type=F32)


def _split_bf16(x):
    hi = x.astype(BF16)
    lo = (x - hi.astype(F32)).astype(BF16)
    return hi, lo


def _inproj0_body(x_ref, w_ref, q_ref, k_ref, v_ref, u_ref, slab_scr):
    nb, tl, D = x_ref.shape
    p = _dot(x_ref[...].reshape(nb * tl, D).astype(BF16), w_ref[...])
    for h in range(SB_HEADS):
        c = h * HEAD_DIM
        q_ref[:, h] = (p[:, c:c + HEAD_DIM] * ATTN_SCALE).astype(BF16).reshape(nb, tl, HEAD_DIM)
        k_ref[:, h] = p[:, SB_WIDTH + c:SB_WIDTH + c + HEAD_DIM].astype(BF16).reshape(nb, tl, HEAD_DIM)
        v_ref[:, h] = p[:, 2 * SB_WIDTH + c:2 * SB_WIDTH + c + HEAD_DIM].astype(BF16).reshape(nb, tl, HEAD_DIM)
    T = SSM_CHUNK
    for v in range(SSM_LANE_TILES):
        c0 = 3 * SB_WIDTH + v * 128
        slab_scr[...] = p[:, c0:c0 + 128]
        for c in range(tl // T):
            for s in range(T):
                u_ref[v, c * nb:(c + 1) * nb, s * 128:(s + 1) * 128] = slab_scr[pl.ds(c * T + s, nb, stride=tl), :]


def _inproj0(x, w_bf16, tl=64):
    B, L, D = x.shape
    tl = min(tl, L)
    nout = w_bf16.shape[1]
    T = SSM_CHUNK
    head = jax.ShapeDtypeStruct((B, SB_HEADS, L, HEAD_DIM), BF16)
    head_spec = pl.BlockSpec((B, SB_HEADS, tl, HEAD_DIM), lambda i: (0, 0, i, 0))
    return pl.pallas_call(
        _inproj0_body,
        grid=(L // tl,),
        in_specs=[pl.BlockSpec((B, tl, D), lambda i: (0, i, 0)),
                  pl.BlockSpec((D, nout), lambda i: (0, 0))],
        out_specs=[head_spec, head_spec, head_spec,
                   pl.BlockSpec((SSM_LANE_TILES, tl // T * B, T * 128), lambda i: (0, i, 0))],
        out_shape=[head, head, head,
                   jax.ShapeDtypeStruct((SSM_LANE_TILES, L // T * B, T * 128), F32)],
        scratch_shapes=[pltpu.VMEM((B * tl, 128), F32)],
        compiler_params=_params("parallel"),
        name="inproj0",
    )(x, w_bf16)


SB_BLOCK = 256
SB_HEADS_PER_STEP = 4
SB_LOGIT_LOOKAHEAD = 3
EXP_UNDERFLOW = -104.0


def _sb_attn_body(q_ref, k_ref, v_ref, o_ref, acc_scr, cs_scr):
    i = pl.program_id(2)
    t = q_ref.shape[2]
    nh = q_ref.shape[1]
    row = lax.broadcasted_iota(jnp.int32, (t, t), 0)
    col = lax.broadcasted_iota(jnp.int32, (t, t), 1)
    suffix = jnp.where(row > col, 1.0, 0.0).astype(BF16)
    below = col < row

    def logits(hh, k0):
        return _dot_nt(q_ref[0, hh], k_ref[0, hh, pl.ds(k0, t), :])

    def block(hh, k0, diagonal, z):
        vb = v_ref[0, hh, pl.ds(k0, t), :]
        sp = jnp.maximum(z, 0.0) + jnp.log(1.0 + jnp.exp(-jnp.abs(z)))
        l1 = jnp.where(below, -sp, 0.0) if diagonal else -sp
        within = _dot(l1.astype(BF16), suffix)
        if diagonal:
            w = jnp.where(below, jnp.exp(z - sp + within), 0.0)
            acc_scr[hh] = _dot(w.astype(BF16), vb)
            cs = jnp.sum(l1, axis=1, keepdims=True)
        else:
            w = jnp.exp(z - sp + within + cs_scr[hh])
            acc_scr[hh] += _dot(w.astype(BF16), vb)
            cs = cs_scr[hh] + jnp.sum(l1, axis=1, keepdims=True)
        cs_scr[hh] = cs
        return cs

    def alive(css):
        return (jnp.max(functools.reduce(jnp.maximum, css)) > EXP_UNDERFLOW).astype(jnp.int32)

    def all_heads(k0, diagonal):
        css = []
        pending = [logits(hh, k0) for hh in range(min(SB_LOGIT_LOOKAHEAD, nh))]
        for hh in range(nh):
            z = pending.pop(0)
            if hh + SB_LOGIT_LOOKAHEAD < nh:
                pending.append(logits(hh + SB_LOGIT_LOOKAHEAD, k0))
            css.append(block(hh, k0, diagonal, z))
        return alive(css)

    first = all_heads(pl.multiple_of(i * t, t), True)

    def cond(c):
        return jnp.logical_and(c[0] <= i, c[1] > 0)

    def body(c):
        return c[0] + 1, all_heads(pl.multiple_of((i - c[0]) * t, t), False)

    lax.while_loop(cond, body, (jnp.int32(1), first))
    o_ref[0] = jnp.concatenate([acc_scr[hh] for hh in range(nh)], axis=-1).astype(o_ref.dtype)


def _sb_attention(q, k, v):
    B, H, L, d = q.shape
    t = min(SB_BLOCK, L)
    nh = SB_HEADS_PER_STEP
    return pl.pallas_call(
        _sb_attn_body,
        grid=(B, H // nh, L // t),
        in_specs=[pl.BlockSpec((1, nh, t, d), lambda b, h, i: (b, h, i, 0)),
                  pl.BlockSpec((1, nh, L, d), lambda b, h, i: (b, h, 0, 0)),
                  pl.BlockSpec((1, nh, L, d), lambda b, h, i: (b, h, 0, 0))],
        out_specs=pl.BlockSpec((1, t, nh * d), lambda b, h, i: (b, i, h)),
        out_shape=jax.ShapeDtypeStruct((B, L, H * d), BF16),
        scratch_shapes=[pltpu.VMEM((nh, t, d), F32), pltpu.VMEM((nh, t, 1), F32)],
        compiler_params=_params("parallel", "parallel", "arbitrary"),
        name="sb_attention",
    )(q, k, v)


def _ssm_tables(lam_re, lam_im, log_dt, b_re, b_im, c_re, c_im, d_skip):
    T, G, P, H = SSM_CHUNK, SSM_GROUPS, SSM_STATE, SSM_GROUP
    hp = lax.Precision.HIGHEST
    dt = jnp.exp(log_dt.astype(F32))[:, None]
    lr = lam_re.astype(F32)
    li = lam_im.astype(F32)
    mag = jnp.exp(lr * dt)
    a_re = mag * jnp.cos(li * dt)
    a_im = mag * jnp.sin(li * dt)
    den = lr * lr + li * li
    nr = a_re - 1.0
    f_re = (nr * lr + a_im * li) / den
    f_im = (a_im * lr - nr * li) / den
    br = b_re.astype(F32)
    bi = b_im.astype(F32)
    bb_re = f_re[..., None] * br - f_im[..., None] * bi
    bb_im = f_re[..., None] * bi + f_im[..., None] * br
    tau = jnp.arange(T + 1, dtype=F32)[None, :, None]
    pmag = jnp.exp(tau * (lr * dt)[:, None, :])
    pw_re = pmag * jnp.cos(tau * (li * dt)[:, None, :])
    pw_im = pmag * jnp.sin(tau * (li * dt)[:, None, :])
    cr = c_re.astype(F32)[:, None]
    ci = c_im.astype(F32)[:, None]
    cpw_re = cr * pw_re[:, :, None, :] - ci * pw_im[:, :, None, :]
    cpw_im = cr * pw_im[:, :, None, :] + ci * pw_re[:, :, None, :]
    kern = (jnp.einsum('gthp,gpi->gthi', cpw_re, bb_re, precision=hp)
            - jnp.einsum('gthp,gpi->gthi', cpw_im, bb_im, precision=hp))
    s_idx = np.arange(T)[:, None]
    t_idx = np.arange(T)[None, :]
    lag = np.clip(t_idx - s_idx, 0, T)
    toe = kern[:, lag]
    toe = jnp.where(jnp.asarray(t_idx >= s_idx)[None, :, :, None, None], toe, 0.0)
    intra = toe.transpose(0, 1, 4, 2, 3).reshape(G, T * H, T * H)
    rev = np.arange(T - 1, -1, -1)
    qr = pw_re[:, rev][:, :, None, :]
    qi = pw_im[:, rev][:, :, None, :]
    bbr = bb_re.transpose(0, 2, 1)[:, None]
    bbi = bb_im.transpose(0, 2, 1)[:, None]
    st_re = (qr * bbr - qi * bbi).reshape(G, T * H, P)
    st_im = (qr * bbi + qi * bbr).reshape(G, T * H, P)
    to_st = jnp.concatenate([st_re, st_im, st_im, st_re], axis=-1)
    top = cpw_re[:, 1:].transpose(0, 3, 1, 2).reshape(G, P, T * H)
    bot = (-cpw_im[:, 1:]).transpose(0, 3, 1, 2).reshape(G, P, T * H)
    from_st = jnp.concatenate([top, bot], axis=1)
    at_re = pw_re[:, T]
    at_im = pw_im[:, T]
    zeros = jnp.zeros_like(at_re)
    adv = jnp.stack([jnp.concatenate([at_re, at_re], -1),
                     jnp.concatenate([-at_im, at_im], -1),
                     jnp.concatenate([at_im, -at_im], -1)]
                    + [jnp.concatenate([zeros, zeros], -1)] * 5, axis=1)
    dvec = jnp.tile(d_skip.astype(F32).reshape(G, 1, H), (1, 1, T))
    return intra, to_st, from_st, adv, dvec


def _ssm_expand(tables):
    intra, to_st, from_st, adv, dvec = tables
    V, W, T, H, P = SSM_LANE_TILES, SSM_TILE_GROUPS, SSM_CHUNK, SSM_GROUP, SSM_STATE
    n = T * W * H
    k_in = lax.broadcasted_iota(jnp.int32, (T * H, n), 0)
    col = lax.broadcasted_iota(jnp.int32, (T * H, n), 1)
    spread_th = (k_in == (col // 128) * H + col % H).astype(F32)
    spread_pq = (k_in == (col // (W * 2 * P)) * 2 * P + col % (2 * P)).astype(F32)
    row_g = (lax.broadcasted_iota(jnp.int32, (n, 1), 0) // H) % W
    st_row_g = lax.broadcasted_iota(jnp.int32, (W * 2 * P, 1), 0) // (2 * P)
    col_g_th = (lax.broadcasted_iota(jnp.int32, (1, n), 1) // H) % W
    col_g_pq = (lax.broadcasted_iota(jnp.int32, (1, n), 1) // (2 * P)) % W

    def blockdiag(rows, spread, row_group, col_group):
        full = jnp.einsum('vrk,kc->vrc', rows, spread)
        return jnp.where(row_group == col_group, full, 0.0).astype(BF16)

    by_row = lambda t: t.reshape(V, W, T, H, T * H).transpose(0, 2, 1, 3, 4).reshape(V, n, T * H)
    m8 = blockdiag(by_row(intra), spread_th, row_g, col_g_th)
    ws8 = blockdiag(by_row(to_st), spread_pq, row_g, col_g_pq)
    wi8 = blockdiag(from_st.reshape(V, W * 2 * P, T * H), spread_th, st_row_g, col_g_th)
    adv8 = adv.reshape(V, W, 8, 2 * P).transpose(0, 2, 1, 3).reshape(V, 8, W * 2 * P)
    d8 = jnp.broadcast_to(dvec.reshape(V, 1, W, T, H)[:, :, :, 0:1], (V, 1, W, T, H))
    d8 = d8.transpose(0, 1, 3, 2, 4).reshape(V, 1, T * W * H)
    return m8, ws8, wi8, adv8, d8


def _ssm_body(u_ref, m8_ref, ws8_ref, wi8_ref, adv_ref, dvec_ref, y_ref, s_scr, xin_scr, x1_scr, x2_scr, *, bsz):
    half = x1_scr.shape[1]

    @pl.when(pl.program_id(1) == 0)
    def _():
        x1_scr[...] = jnp.zeros_like(x1_scr)
        x2_scr[...] = jnp.zeros_like(x2_scr)

    u = u_ref[0]
    ub = u.astype(BF16)
    s_scr[...] = _dot(ub, ws8_ref[0])
    a1 = adv_ref[0, 0:1, :]
    a2 = adv_ref[0, 1:2, :]
    a3 = adv_ref[0, 2:3, :]

    def step(c, carry):
        x1, x2 = carry
        r0 = pl.multiple_of(c * bsz, bsz)
        xin_scr[pl.ds(r0, bsz), :] = x1
        s = s_scr[pl.ds(r0, bsz), :]
        return (a1 * x1 + a2 * x2 + s[:, :half], a1 * x2 + a3 * x1 + s[:, half:])

    x1, x2 = lax.fori_loop(0, u.shape[0] // bsz, step, (x1_scr[...], x2_scr[...]))
    x1_scr[...] = x1
    x2_scr[...] = x2
    y_ref[0] = _dot(ub, m8_ref[0]) + _dot(xin_scr[...].astype(BF16), wi8_ref[0]) + dvec_ref[0] * u


def _ssm(us, tables, bsz, rows=256):
    m8, ws8, wi8, adv8, d8 = _ssm_expand(tables)
    V, CB, width = us.shape
    B = bsz
    half = SSM_TILE_GROUPS * 2 * SSM_STATE
    rows = min(rows, CB)
    once = dict(pipeline_mode=pl.Buffered(1))
    y = pl.pallas_call(
        functools.partial(_ssm_body, bsz=B),
        grid=(V, CB // rows),
        in_specs=[pl.BlockSpec((1, rows, width), lambda v, r: (v, r, 0)),
                  pl.BlockSpec((1, width, width), lambda v, r: (v, 0, 0), **once),
                  pl.BlockSpec((1, width, 2 * half), lambda v, r: (v, 0, 0), **once),
                  pl.BlockSpec((1, half, width), lambda v, r: (v, 0, 0), **once),
                  pl.BlockSpec((1, 8, half), lambda v, r: (v, 0, 0)),
                  pl.BlockSpec((1, 1, width), lambda v, r: (v, 0, 0))],
        out_specs=pl.BlockSpec((1, rows, width), lambda v, r: (v, r, 0)),
        out_shape=jax.ShapeDtypeStruct((V, CB, width), F32),
        scratch_shapes=[pltpu.VMEM((rows, 2 * half), F32), pltpu.VMEM((rows, half), F32),
                        pltpu.VMEM((B, half), F32), pltpu.VMEM((B, half), F32)],
        compiler_params=_params("parallel", "arbitrary"),
        name="s5_scan",
    )(us, m8, ws8, wi8, adv8, d8)
    return y


def _outproj0_body(x_ref, oa_ref, y_ref, wglu_ref, woa_ref, wob_ref, g_ref, b_ref, o_ref, slab_scr):
    nb, tl, D = x_ref.shape
    T = SSM_CHUNK
    ys = []
    for v in range(SSM_LANE_TILES):
        for c in range(tl // T):
            for s in range(T):
                slab_scr[pl.ds(c * T + s, nb, stride=tl), :] = y_ref[v, c * nb:(c + 1) * nb, s * 128:(s + 1) * 128]
        ys.append(slab_scr[...])
    h = _gelu(jnp.concatenate(ys, axis=-1))
    ob = h * _sigmoid(_dot(h.astype(BF16), wglu_ref[...]))
    m = _dot(oa_ref[...].reshape(nb * tl, SB_WIDTH), woa_ref[...]) + _dot(ob.astype(BF16), wob_ref[...])
    r = ALPHA * x_ref[...].reshape(nb * tl, D) + m
    o_ref[...] = _layer_norm(r, g_ref[...], b_ref[...]).reshape(nb, tl, D)


def _outproj0(x, oa, ys, wglu, woa, wob, g, b, tl=64):
    B, L, D = x.shape
    tl = min(tl, L)
    T = SSM_CHUNK
    row = lambda i: (0, i, 0)
    fix = lambda i: (0, 0)
    return pl.pallas_call(
        _outproj0_body,
        grid=(L // tl,),
        in_specs=[pl.BlockSpec((B, tl, D), row), pl.BlockSpec((B, tl, SB_WIDTH), row),
                  pl.BlockSpec((SSM_LANE_TILES, tl // T * B, T * 128), row),
                  pl.BlockSpec((SSM_WIDTH, SSM_WIDTH), fix), pl.BlockSpec((SB_WIDTH, D), fix),
                  pl.BlockSpec((SSM_WIDTH, D), fix), pl.BlockSpec((1, D), fix), pl.BlockSpec((1, D), fix)],
        out_specs=pl.BlockSpec((B, tl, D), row),
        out_shape=jax.ShapeDtypeStruct((B, L, D), F32),
        scratch_shapes=[pltpu.VMEM((B * tl, 128), F32)],
        compiler_params=_params("parallel"),
        name="outproj0",
    )(x, oa, ys, wglu, woa, wob, g, b)


def _outproj1_body(x_ref, o_ref_in, w_ref, g_ref, b_ref, o_ref):
    m = _dot(o_ref_in[...], w_ref[...])
    o_ref[...] = _layer_norm(ALPHA * x_ref[...] + m, g_ref[...], b_ref[...])


def _outproj1(x, o, w, g, b, tm=512):
    N, D = x.shape
    tm = min(tm, N)
    row = lambda i: (i, 0)
    fix = lambda i: (0, 0)
    return pl.pallas_call(
        _outproj1_body,
        grid=(N // tm,),
        in_specs=[pl.BlockSpec((tm, D), row), pl.BlockSpec((tm, o.shape[1]), row),
                  pl.BlockSpec(w.shape, fix), pl.BlockSpec((1, D), fix), pl.BlockSpec((1, D), fix)],
        out_specs=pl.BlockSpec((tm, D), row),
        out_shape=jax.ShapeDtypeStruct((N, D), F32),
        compiler_params=_params("parallel"),
        name="outproj1",
    )(x, o, w, g, b)


def _router_body(x_ref, wh_ref, wl_ref, b_ref, g_ref, grp_ref):
    xh, xl = _split_bf16(x_ref[...])
    wh = wh_ref[...]
    logits = _dot_nt(wh, xh) + _dot_nt(wh, xl) + _dot_nt(wl_ref[...], xh)
    aff = _sigmoid(logits)
    sel = aff + b_ref[...]
    E, K = EXPERTS_PER_GROUP, N_GROUPS
    s = [sel[e:e + 1, :] for e in range(N_EXPERTS)]
    a = [aff[e:e + 1, :] for e in range(N_EXPERTS)]
    gscore = []
    for k in range(K):
        v = s[k * E:(k + 1) * E]
        best = None
        for x in range(E):
            for y in range(x + 1, E):
                pair = v[x] + v[y]
                best = pair if best is None else jnp.maximum(best, pair)
        gscore.append(best)
    top = functools.reduce(jnp.maximum, gscore)
    is_g = []
    taken = None
    for k in range(K):
        hit = gscore[k] == top
        if taken is None:
            is_g.append(hit)
            taken = hit
        else:
            is_g.append(jnp.logical_and(hit, jnp.logical_not(taken)))
            taken = jnp.logical_or(taken, hit)

    def pick(rows, j):
        out = rows[(K - 1) * E + j]
        for k in range(K - 2, -1, -1):
            out = jnp.where(is_g[k], rows[k * E + j], out)
        return out

    v = [pick(s, j) for j in range(E)]
    av = [pick(a, j) for j in range(E)]
    chosen = []
    for j in range(E):
        r = jnp.zeros_like(v[j])
        for j2 in range(E):
            if j2 == j:
                continue
            ahead = (v[j2] >= v[j]) if j2 < j else (v[j2] > v[j])
            r = r + jnp.where(ahead, 1.0, 0.0)
        chosen.append(r < 2.0)
    wj = [jnp.where(chosen[j], av[j], 0.0) for j in range(E)]
    tot = wj[0] + wj[1] + wj[2] + wj[3]
    gj = [w / tot for w in wj]
    rows = [jnp.where(is_g[e // E], gj[e % E], 0.0) for e in range(N_EXPERTS)]
    g_ref[...] = jnp.concatenate(rows, axis=0)
    grp_ref[...] = jnp.concatenate([jnp.where(is_g[k], 1.0, 0.0) for k in range(K)], axis=0)


def _router(x, w_router, b_router, tm=512):
    N, D = x.shape
    tm = min(tm, N)
    wt = w_router.astype(F32).T
    wh, wl = _split_bf16(wt)
    fix = lambda i: (0, 0)
    return pl.pallas_call(
        _router_body,
        grid=(N // tm,),
        in_specs=[pl.BlockSpec((tm, D), lambda i: (i, 0)), pl.BlockSpec((N_EXPERTS, D), fix),
                  pl.BlockSpec((N_EXPERTS, D), fix), pl.BlockSpec((N_EXPERTS, 1), fix)],
        out_specs=[pl.BlockSpec((N_EXPERTS, tm), lambda i: (0, i)), pl.BlockSpec((N_GROUPS, tm), lambda i: (0, i))],
        out_shape=[jax.ShapeDtypeStruct((N_EXPERTS, N), F32), jax.ShapeDtypeStruct((N_GROUPS, N), F32)],
        compiler_params=_params("parallel"),
        name="router",
    )(x, wh, wl, b_router.astype(F32).reshape(N_EXPERTS, 1))


MOE_TILE = 1024
MOE_PASS_ROWS = 304
MOE_CHUNK = 256
MOE_EXPERTS_PER_STEP = 2
ROW_ALIGN = 16


def _moe_body(seg_ref, x_ref, gt_ref, grp_ref, w1_ref, w3_ref, w2_ref, g_ref, b_ref, o_ref,
              xs_scr, ys_scr, pt_scr, gs_scr):
    i = pl.program_id(0)
    e = pl.program_id(1)
    T = x_ref.shape[0]
    K = N_GROUPS
    cap = MOE_PASS_ROWS
    ch = min(MOE_CHUNK, T)

    @pl.when(e == 0)
    def _():
        grp = grp_ref[...]
        xb = x_ref[...].astype(BF16)
        gates = gt_ref[...]
        gh, gm = _split_bf16(gates)
        gl = (gates - gh.astype(F32) - gm.astype(F32)).astype(BF16)
        cnt = jnp.sum(grp, axis=1, keepdims=True)
        off = [jnp.zeros((1, 1), F32)]
        for k in range(K - 1):
            off.append(off[-1] + cnt[k:k + 1])
        r_i = lax.broadcasted_iota(jnp.int32, (ch, T), 0)
        c_i = lax.broadcasted_iota(jnp.int32, (ch, T), 1)
        lane = lax.broadcasted_iota(jnp.int32, (2 * K, T), 1)
        run = jnp.concatenate([grp, jnp.zeros_like(grp)], axis=0)
        sh = 1
        while sh < T:
            run = run + jnp.where(lane >= sh, pltpu.roll(run, sh, 1), 0.0)
            sh *= 2
        pos_row = jnp.zeros((1, T), F32)
        for k in range(K):
            pos_row = pos_row + grp[k:k + 1, :] * (off[k] + run[k:k + 1, :] - 1.0)
        pos_t = jnp.broadcast_to(pos_row, (2 * K, T))
        pos_col = [pos_t[:, c * ch:(c + 1) * ch].T[:, 0:1] for c in range(T // ch)]
        for c in range(T // ch):
            c0 = c * ch
            perm = jnp.where((r_i + c0).astype(F32) == pos_row, 1.0, 0.0).astype(BF16)
            xs_scr[c0:c0 + ch, :] = _dot(perm, xb).astype(BF16)
            gs_scr[c0:c0 + ch, :] = _dot_nt(perm, gh) + _dot_nt(perm, gm) + _dot_nt(perm, gl)
            pt_scr[c0:c0 + ch, :] = jnp.where(c_i.astype(F32) == pos_col[c], 1.0, 0.0).astype(BF16)
        xs_scr[T:, :] = jnp.zeros((cap, xs_scr.shape[1]), BF16)
        gs_scr[T:, :] = jnp.zeros((cap, gs_scr.shape[1]), F32)
        ys_scr[...] = jnp.zeros_like(ys_scr)

    ne = w1_ref.shape[0]
    k = (e * ne) // EXPERTS_PER_GROUP
    off = seg_ref[i, k]
    cnt = seg_ref[i, K + k]
    start0 = (off // ROW_ALIGN) * ROW_ALIGN
    n_pass = (off + cnt - start0 + cap - 1) // cap

    def one_pass(n, c):
        start = pl.multiple_of(start0 + n * cap, ROW_ALIGN)
        xc = xs_scr[pl.ds(start, cap), :]
        gsc = gs_scr[pl.ds(start, cap), :]
        lane = lax.broadcasted_iota(jnp.int32, gsc.shape, 1)
        y = None
        for x in range(ne):
            a = _dot(xc, w1_ref[x])
            hm = (a * _sigmoid(a)) * _dot(xc, w3_ref[x])
            gcol = jnp.sum(jnp.where(lane == e * ne + x, gsc, 0.0), axis=1, keepdims=True)
            yx = gcol * _dot(hm.astype(BF16), w2_ref[x])
            y = yx if y is None else y + yx
        ys_scr[pl.ds(start, cap), :] += y
        return c

    lax.fori_loop(0, n_pass, one_pass, 0)

    @pl.when(e == pl.num_programs(1) - 1)
    def _():
        y = _dot(pt_scr[...], ys_scr[0:T, :].astype(BF16))
        o_ref[...] = _layer_norm(ALPHA * x_ref[...] + y, g_ref[...], b_ref[...])


def _moe(x, gates_t, grp_t, w1, w3, w2, g, b):
    N, D = x.shape
    T = min(MOE_TILE, N)
    K = N_GROUPS
    cnt = grp_t.reshape(K, N // T, T).sum(-1).astype(jnp.int32).T
    seg = jnp.concatenate([jnp.cumsum(cnt, axis=1) - cnt, cnt], axis=1)
    fix = lambda i, e, s: (0, 0)
    ne = MOE_EXPERTS_PER_STEP
    rows = T + MOE_PASS_ROWS
    grid_spec = pltpu.PrefetchScalarGridSpec(
        num_scalar_prefetch=1,
        grid=(N // T, N_EXPERTS // ne),
        in_specs=[pl.BlockSpec((T, D), lambda i, e, s: (i, 0)),
                  pl.BlockSpec((N_EXPERTS, T), lambda i, e, s: (0, i)),
                  pl.BlockSpec((K, T), lambda i, e, s: (0, i)),
                  pl.BlockSpec((ne, D, D_EXPERT), lambda i, e, s: (e, 0, 0)),
                  pl.BlockSpec((ne, D, D_EXPERT), lambda i, e, s: (e, 0, 0)),
                  pl.BlockSpec((ne, D_EXPERT, D), lambda i, e, s: (e, 0, 0)),
                  pl.BlockSpec((1, D), fix), pl.BlockSpec((1, D), fix)],
        out_specs=pl.BlockSpec((T, D), lambda i, e, s: (i, 0)),
        scratch_shapes=[pltpu.VMEM((rows, D), BF16), pltpu.VMEM((rows, D), F32),
                        pltpu.VMEM((T, T), BF16), pltpu.VMEM((rows, N_EXPERTS), F32)])
    return pl.pallas_call(
        _moe_body,
        grid_spec=grid_spec,
        out_shape=jax.ShapeDtypeStruct((N, D), F32),
        compiler_params=_params("parallel", "arbitrary"),
        name="moe",
    )(seg, x, gates_t, grp_t, w1, w3, w2, g, b)


NSA_KVW = NSA_KV_HEADS * HEAD_DIM
NSA_ROPE_W = NSA_HEADS * HEAD_DIM + 2 * NSA_KVW
NSA_Q_SCALE = ATTN_SCALE * math.log2(math.e)


def _rot_cols(w):
    K, n = w.shape
    w3 = w.reshape(K, n // HEAD_DIM, 2, HEAD_DIM // 2)
    return jnp.stack([-w3[:, :, 1], w3[:, :, 0]], axis=2).reshape(K, n)


def _inproj1_weights(w_in):
    H, G, d = NSA_HEADS, NSA_KV_HEADS, HEAD_DIM
    cuts = [H * d + i * NSA_KVW for i in range(7)]
    q, kc, vc, ks, vs, kw, vw, gate = jnp.split(w_in.astype(F32), cuts, axis=1)
    w_all = jnp.concatenate([q, ks, kw, kc, vc], axis=1).astype(BF16)
    w_t = jnp.concatenate([vs, vw, gate], axis=1).T.astype(BF16)
    return w_all, w_t


def _rope_tables(pos):
    inv = ROPE_THETA ** (-jnp.arange(0, HEAD_DIM, 2, dtype=F32) / HEAD_DIM)
    ang = pos.astype(F32)[:, None] * inv[None, :]
    c, s = jnp.cos(ang), jnp.sin(ang)
    return jnp.concatenate([c, c], axis=-1), jnp.concatenate([s, s], axis=-1)


def _rope_roll_tables(pos):
    inv = ROPE_THETA ** (-jnp.arange(0, HEAD_DIM, 2, dtype=F32) / HEAD_DIM)
    ang = pos.astype(F32)[:, None] * inv[None, :]
    c, s, z = jnp.cos(ang), jnp.sin(ang), jnp.zeros_like(ang)
    two = lambda a, b: jnp.concatenate([a, b, a, b], axis=-1)
    return two(c, c), two(-s, z), two(z, s)


V_ROWS = HEAD_DIM + 16


def _inproj1_body(x_ref, w_ref, wt_ref, cos_ref, sinlo_ref, sinhi_ref, q_ref, ks_ref, kw_ref, kc_ref, vc_ref,
                  vst_ref, vwt_ref, gate_ref):
    xb = x_ref[0].astype(BF16)
    tl = xb.shape[0]
    d = HEAD_DIM
    p = _dot(xb, w_ref[...])
    pt = _dot_nt(wt_ref[...], xb)
    ones = jnp.ones((V_ROWS - d, tl), BF16)
    for n, ref in enumerate((vst_ref, vwt_ref)):
        for g in range(NSA_KV_HEADS):
            r0 = n * NSA_KVW + g * d
            ref[0, g, 0:d, :] = pt[r0:r0 + d, :].astype(BF16)
            ref[0, g, d:V_ROWS, :] = ones
    gate_ref[0] = pt[2 * NSA_KVW:, :]
    cos = cos_ref[...]
    sin_lo = sinlo_ref[...]
    sin_hi = sinhi_ref[...]
    roped = []
    for c in range(NSA_ROPE_W // 128):
        xc = p[:, c * 128:(c + 1) * 128]
        roped.append(xc * cos + pltpu.roll(xc, 96, 1) * sin_lo + pltpu.roll(xc, 32, 1) * sin_hi)

    def head(chunks, h):
        blk = chunks[h // 2]
        return blk[:, (h % 2) * d:(h % 2 + 1) * d]

    for h in range(NSA_HEADS):
        q_ref[0, h] = (head(roped, h) * NSA_Q_SCALE).astype(BF16)
    nq = NSA_HEADS // 2
    pos = pl.program_id(1) * tl + lax.broadcasted_iota(jnp.int32, (tl, 128 - d), 0)
    block_onehot = jnp.where(pos // SEL_LEN == lax.broadcasted_iota(jnp.int32, (tl, 128 - d), 1), 1.0, 0.0)
    for g in range(NSA_KV_HEADS):
        ks_ref[0, g] = jnp.concatenate([head(roped[nq:], g), block_onehot], axis=1).astype(BF16)
        kw_ref[0, g] = head(roped[nq + 2:], g).astype(BF16)
    for n, ref in enumerate((kc_ref, vc_ref)):
        for g in range(NSA_KV_HEADS):
            c0 = NSA_ROPE_W + n * NSA_KVW + g * d
            ref[0, g] = p[:, c0:c0 + d].astype(BF16)


def _inproj1(x, w_all, w_t, tl=256):
    B, L, D = x.shape
    tl = min(tl, L)
    G = NSA_KV_HEADS
    n_gate = w_t.shape[0] - 2 * NSA_KVW
    tables = _rope_roll_tables(jnp.arange(L))
    qh = jax.ShapeDtypeStruct((B, NSA_HEADS, L, HEAD_DIM), BF16)
    kvh = jax.ShapeDtypeStruct((B, G, L, HEAD_DIM), BF16)
    vth = jax.ShapeDtypeStruct((B, G, V_ROWS, L), BF16)
    q_spec = pl.BlockSpec((1, NSA_HEADS, tl, HEAD_DIM), lambda b, i: (b, 0, i, 0))
    kv_spec = pl.BlockSpec((1, G, tl, HEAD_DIM), lambda b, i: (b, 0, i, 0))
    ksel_spec = pl.BlockSpec((1, G, tl, 128), lambda b, i: (b, 0, i, 0))
    vt_spec = pl.BlockSpec((1, G, V_ROWS, tl), lambda b, i: (b, 0, 0, i))
    tab_spec = pl.BlockSpec((tl, 128), lambda b, i: (i, 0))
    return pl.pallas_call(
        _inproj1_body,
        grid=(B, L // tl),
        in_specs=[pl.BlockSpec((1, tl, D), lambda b, i: (b, i, 0)),
                  pl.BlockSpec(w_all.shape, lambda b, i: (0, 0)),
                  pl.BlockSpec(w_t.shape, lambda b, i: (0, 0)),
                  tab_spec, tab_spec, tab_spec],
        out_specs=([q_spec, ksel_spec] + [kv_spec] * 3 + [vt_spec] * 2
                   + [pl.BlockSpec((1, n_gate, tl), lambda b, i: (b, 0, i))]),
        out_shape=([qh, jax.ShapeDtypeStruct((B, G, L, 128), BF16)] + [kvh] * 3 + [vth] * 2
                   + [jax.ShapeDtypeStruct((B, n_gate, L), F32)]),
        compiler_params=_params("parallel", "parallel"),
        name="inproj1",
    )(x, w_all, w_t, *tables)


def _compress_body(kc_ref, vc_ref, posk_ref, posv_ref, w1k_ref, w1v_ref, w2k_ref, w2kr_ref, w2v_ref,
                   cos_ref, sin_ref, kcmp_ref, vcmp_ref):
    def hidden(a_ref, pos_ref, w1_ref):
        a = a_ref[0, 0].astype(F32)
        nrow = a.shape[0]
        lo = _dot((a + pos_ref[0:1, :]).astype(BF16), w1_ref[0])
        hi = _dot((a + pos_ref[1:2, :]).astype(BF16), w1_ref[1])
        hi_next = pltpu.roll(hi, nrow - 1, 0)
        return _gelu(lo + hi_next).astype(BF16)

    hk = hidden(kc_ref, posk_ref, w1k_ref)
    kcmp = _dot(hk, w2k_ref[...]) * cos_ref[...] + _dot(hk, w2kr_ref[...]) * sin_ref[...]
    kcmp_ref[0, 0] = kcmp.astype(BF16)
    hv = hidden(vc_ref, posv_ref, w1v_ref)
    vcmp_ref[0, 0] = _dot_nt(w2v_ref[...], hv).astype(BF16)


def _compress(kc, vc, pos_k, w1_k, w2_k, pos_v, w1_v, w2_v):
    B, G, L, d = kc.shape
    half = CMP_STRIDE * d
    nb = L // CMP_STRIDE
    kc2 = kc.reshape(B, G, nb, half)
    vc2 = vc.reshape(B, G, nb, half)
    posk = pos_k.astype(F32).reshape(2, half)
    posv = pos_v.astype(F32).reshape(2, half)
    w1k = w1_k.astype(BF16).reshape(2, half, CMP_HIDDEN)
    w1v = w1_v.astype(BF16).reshape(2, half, CMP_HIDDEN)
    w2k = w2_k.astype(F32)
    cos, sin = _rope_tables(jnp.arange(nb) * CMP_STRIDE + CMP_LEN - 1)
    blk = pl.BlockSpec((1, 1, nb, half), lambda b, g: (b, g, 0, 0))
    out = pl.BlockSpec((1, 1, nb, d), lambda b, g: (b, g, 0, 0))
    out_t = pl.BlockSpec((1, 1, d, nb), lambda b, g: (b, g, 0, 0))
    fix2 = lambda b, g: (0, 0)
    fix3 = lambda b, g: (0, 0, 0)
    return pl.pallas_call(
        _compress_body,
        grid=(B, G),
        in_specs=[blk, blk, pl.BlockSpec((2, half), fix2), pl.BlockSpec((2, half), fix2),
                  pl.BlockSpec((2, half, CMP_HIDDEN), fix3), pl.BlockSpec((2, half, CMP_HIDDEN), fix3),
                  pl.BlockSpec((CMP_HIDDEN, d), fix2), pl.BlockSpec((CMP_HIDDEN, d), fix2),
                  pl.BlockSpec((d, CMP_HIDDEN), fix2), pl.BlockSpec((nb, d), fix2), pl.BlockSpec((nb, d), fix2)],
        out_specs=[out, out_t],
        out_shape=[jax.ShapeDtypeStruct((B, G, nb, d), BF16), jax.ShapeDtypeStruct((B, G, d, nb), BF16)],
        compiler_params=_params("parallel", "parallel"),
        name="compress",
    )(kc2, vc2, posk, posv, w1k, w1v, w2k.astype(BF16), _rot_cols(w2k).astype(BF16), w2_v.T.astype(BF16), cos, sin)


NSA_KBLOCK = 256
NSA_GROUPS_PER_STEP = 4
NSA_HEADS_PER_CHAIN = 4
NSA_SCORE_LOOKAHEAD = 3


def _nsa_body(q_ref, kcmp_ref, vcmpt_ref, ks_ref, vst_ref, kw_ref, vwt_ref, gate_ref, o_ref, qsel_scr, *, seq):
    i = pl.program_id(2)
    tq = q_ref.shape[2]
    gb = kcmp_ref.shape[1]
    tk = min(NSA_KBLOCK, seq)
    R, d = NSA_REP, HEAD_DIM
    nq = R * tq
    nb = seq // SEL_LEN
    mc = seq // CMP_STRIDE
    t_row = i * tq + lax.broadcasted_iota(jnp.int32, (1, tq), 1)
    t_all = jnp.concatenate([t_row] * R, axis=1)
    m_col = lax.broadcasted_iota(jnp.int32, (mc, 1), 0)
    m_row = lax.broadcasted_iota(jnp.int32, (1, mc), 1)
    n_col = lax.broadcasted_iota(jnp.int32, (nb, 1), 0)

    def q_rows(g):
        return q_ref[0, g * R:(g + 1) * R].reshape(nq, d)

    valid_c = (m_col * CMP_STRIDE + (CMP_LEN - 1)) <= t_all
    ovl = jnp.logical_and(m_row * CMP_STRIDE < (n_col + 1) * SEL_LEN,
                          m_row * CMP_STRIDE + CMP_LEN > n_col * SEL_LEN)
    ovl = jnp.where(ovl, 1.0, 0.0).astype(BF16)
    cur = t_row // SEL_LEN
    forced = jnp.logical_or(n_col == 0, jnp.logical_or(n_col == cur, n_col == cur - 1))
    bonus = jnp.where(forced, FORCE_BONUS, 0.0)
    valid_s = n_col * SEL_LEN <= t_row
    o_c = []
    ties = [jnp.where(n_col > n2, 1.0, 0.0) for n2 in range(nb)]
    cmp_scores = [_dot_nt(kcmp_ref[0, g], q_rows(g)) for g in range(gb)]
    for g in range(gb):
        s = jnp.where(valid_c, cmp_scores[g], NEG)
        e = jnp.where(valid_c, jnp.exp2(s - jnp.max(s, axis=0, keepdims=True)), 0.0)
        den = jnp.sum(e, axis=0, keepdims=True)
        p = e / jnp.where(den > 0.0, den, 1.0)
        o_c.append(_dot(vcmpt_ref[0, g], p.astype(BF16)))
        psum = p[:, 0:tq]
        for r in range(1, R):
            psum = psum + p[:, r * tq:(r + 1) * tq]
        ph, plo = _split_bf16(psum)
        score = jnp.where(valid_s, _dot(ovl, ph) + _dot(ovl, plo) + bonus, NEG)
        rank = jnp.zeros((nb, tq), F32)
        for n2 in range(nb):
            other = score[n2:n2 + 1, :]
            rank = rank + jnp.where(other > score, 1.0, jnp.where(other == score, ties[n2], 0.0))
        pen = jnp.where(rank < float(SEL_TOP), 0.0, NEG).T.astype(BF16)
        pen = jnp.concatenate([pen, jnp.zeros((tq, 128 - d - nb), BF16)], axis=1)
        qsel_scr[g] = jnp.concatenate([q_rows(g), jnp.concatenate([pen] * R, axis=0)], axis=1)

    k_col = lax.broadcasted_iota(jnp.int32, (tk, 1), 0)
    hp = NSA_HEADS_PER_CHAIN
    j_hi = (i * tq + tq - 1) // tk + 1

    def sweep(q_fn, k_ref, vt_ref, j_lo, bias_fn, bias_every_block):
        chains = [(g, h0) for g in range(gb) for h0 in range(0, R, hp)]

        def body(j, state, with_bias):
            k0 = pl.multiple_of(j * tk, tk)
            bias = jnp.concatenate([bias_fn(k0 + k_col)] * hp, axis=1) if with_bias else None

            def scores(c):
                g, h0 = chains[c]
                sc = _dot_nt(k_ref[0, g, pl.ds(k0, tk), :], q_fn(g, h0))
                return sc + bias if with_bias else sc

            new_state = []
            ahead = NSA_SCORE_LOOKAHEAD
            pending = [scores(c) for c in range(min(ahead, len(chains)))]
            for c, (m_run, acc) in enumerate(state):
                sc = pending.pop(0)
                if c + ahead < len(chains):
                    pending.append(scores(c + ahead))
                vtb = vt_ref[0, chains[c][0], :, pl.ds(k0, tk)]
                m_new = jnp.maximum(m_run, jnp.max(sc, axis=0, keepdims=True))
                pr = jnp.exp2(sc - m_new)
                new_state.append((m_new, jnp.exp2(m_run - m_new) * acc + _dot(vtb, pr.astype(BF16))))
            return tuple(new_state)

        state = tuple((jnp.full((1, hp * tq), NEG, F32), jnp.zeros((V_ROWS, hp * tq), F32)) for _ in chains)
        state = lax.fori_loop(j_lo, j_hi - 1, functools.partial(body, with_bias=bias_every_block), state)
        state = body(j_hi - 1, state, True)
        outs = []
        for g in range(gb):
            acc = jnp.concatenate([a for (cg, _), (_, a) in zip(chains, state) if cg == g], axis=1)
            outs.append(acc[0:d] / acc[d:d + 1])
        return outs

    def causal_bias(kpos):
        return jnp.where(kpos <= t_row, 0.0, NEG)

    o_s = sweep(lambda g, h0: qsel_scr[g, h0 * tq:(h0 + hp) * tq, :], ks_ref, vst_ref, 0, causal_bias, False)

    def win_bias(kpos):
        return jnp.where(jnp.logical_and(kpos <= t_row, kpos > t_row - WINDOW), 0.0, NEG)

    o_w = sweep(lambda g, h0: q_ref[0, g * R + h0:g * R + h0 + hp].reshape(hp * tq, d), kw_ref, vwt_ref,
                jnp.maximum(i * tq - (WINDOW - 1), 0) // tk, win_bias, True)

    outs = []
    for g in range(gb):
        row0 = (pl.program_id(1) * gb + g) * 3 * R

        def gate(branch):
            rows = [gate_ref[0, pl.ds(row0 + 3 * r + branch, 1), :] for r in range(R)]
            return _sigmoid(jnp.concatenate(rows, axis=1))

        o_t = gate(0) * o_c[g] + gate(1) * o_s[g] + gate(2) * o_w[g]
        for r in range(R):
            outs.append(o_t[:, r * tq:(r + 1) * tq].T)
    o_ref[0] = jnp.concatenate(outs, axis=-1).astype(o_ref.dtype)


def _nsa_attention(q, kcmp, vcmpt, ks, vst, kw, vwt, gate):
    B, H, L, d = q.shape
    G, R = NSA_KV_HEADS, NSA_REP
    gb = NSA_GROUPS_PER_STEP
    tq = min(ATT_BLOCK, L)
    mc = L // CMP_STRIDE
    k_spec = pl.BlockSpec((1, gb, L, d), lambda b, g, i: (b, g, 0, 0))
    ksel_spec = pl.BlockSpec((1, gb, L, 128), lambda b, g, i: (b, g, 0, 0))
    vt_spec = pl.BlockSpec((1, gb, V_ROWS, L), lambda b, g, i: (b, g, 0, 0))
    return pl.pallas_call(
        functools.partial(_nsa_body, seq=L),
        grid=(B, G // gb, L // tq),
        in_specs=[pl.BlockSpec((1, gb * R, tq, d), lambda b, g, i: (b, g, i, 0)),
                  pl.BlockSpec((1, gb, mc, d), lambda b, g, i: (b, g, 0, 0)),
                  pl.BlockSpec((1, gb, d, mc), lambda b, g, i: (b, g, 0, 0)),
                  ksel_spec, vt_spec, k_spec, vt_spec,
                  pl.BlockSpec((1, gate.shape[1], tq), lambda b, g, i: (b, 0, i))],
        out_specs=pl.BlockSpec((1, tq, gb * R * d), lambda b, g, i: (b, i, g)),
        out_shape=jax.ShapeDtypeStruct((B, L, H * d), BF16),
        scratch_shapes=[pltpu.VMEM((gb, R * tq, 128), BF16)],
        compiler_params=_params("parallel", "parallel", "arbitrary"),
        name="nsa_attention",
    )(q, kcmp, vcmpt, ks, vst, kw, vwt, gate)


def kernel(x, w_in_0, ssm_lam_re, ssm_lam_im, ssm_log_dt, ssm_b_re, ssm_b_im, ssm_c_re, ssm_c_im, ssm_d, w_glu, w_out_0, ln_mix_g_0, ln_mix_b_0, ln_ffn_g_0, ln_ffn_b_0, w1_0, w3_0, w2_0, w_in_1, cmp_pos_k, cmp_w1_k, cmp_w2_k, cmp_pos_v, cmp_w1_v, cmp_w2_v, w_out_1, ln_mix_g_1, ln_mix_b_1, ln_ffn_g_1, ln_ffn_b_1, w1_1, w3_1, w2_1, w_router, b_router):
    B, L, D = x.shape
    N = B * L
    vec = lambda a: a.astype(F32).reshape(1, D)

    def ffn(h, w1, w3, w2, g, b):
        gates_t, grp_t = _router(h, w_router, b_router)
        return _moe(h, gates_t, grp_t, w1.astype(BF16), w3.astype(BF16), w2.astype(BF16), vec(g), vec(b))

    q, k, v, u = _inproj0(x, w_in_0.astype(BF16))
    o_a = _sb_attention(q, k, v)
    y = _ssm(u, _ssm_tables(ssm_lam_re, ssm_lam_im, ssm_log_dt, ssm_b_re, ssm_b_im, ssm_c_re, ssm_c_im, ssm_d), B)
    w_out_0b = w_out_0.astype(BF16)
    h = _outproj0(x, o_a, y, w_glu.astype(BF16), w_out_0b[:SB_WIDTH], w_out_0b[SB_WIDTH:],
                  vec(ln_mix_g_0), vec(ln_mix_b_0)).reshape(N, D)
    h = ffn(h, w1_0, w3_0, w2_0, ln_ffn_g_0, ln_ffn_b_0)

    q, ks, kw, kc, vc, vst, vwt, gate = _inproj1(h.reshape(B, L, D), *_inproj1_weights(w_in_1))
    kcmp, vcmpt = _compress(kc, vc, cmp_pos_k, cmp_w1_k, cmp_w2_k, cmp_pos_v, cmp_w1_v, cmp_w2_v)
    o = _nsa_attention(q, kcmp, vcmpt, ks, vst, kw, vwt, gate).reshape(N, NSA_HEADS * HEAD_DIM)
    h = _outproj1(h, o, w_out_1.astype(BF16), vec(ln_mix_g_1), vec(ln_mix_b_1))
    h = ffn(h, w1_1, w3_1, w2_1, ln_ffn_g_1, ln_ffn_b_1)
    return h.reshape(B, L, D)
```

```python
import functools
import math

import numpy as np
import jax
import jax.numpy as jnp
from jax import lax
from jax.experimental import pallas as pl
from jax.experimental.pallas import tpu as pltpu

F32 = jnp.float32
BF16 = jnp.bfloat16

D_MODEL = 1024
DEPTH = 2
SB_HEADS = 8
HEAD_DIM = 64
SB_WIDTH = SB_HEADS * HEAD_DIM
SSM_WIDTH = D_MODEL - SB_WIDTH
SSM_GROUP = 16
SSM_GROUPS = SSM_WIDTH // SSM_GROUP
SSM_STATE = 64
SSM_CHUNK = 16
SSM_LANE_TILES = SSM_WIDTH // 128
SSM_TILE_GROUPS = 128 // SSM_GROUP
NSA_HEADS = 16
NSA_KV_HEADS = 4
NSA_REP = NSA_HEADS // NSA_KV_HEADS
CMP_LEN = 32
CMP_STRIDE = 16
CMP_HIDDEN = 256
SEL_LEN = 64
SEL_TOP = 8
WINDOW = 512
ROPE_THETA = 10000.0
FORCE_BONUS = 1e4
NEG = -1e30
N_EXPERTS = 16
N_GROUPS = 4
EXPERTS_PER_GROUP = N_EXPERTS // N_GROUPS
D_EXPERT = 512
ALPHA = (2 * DEPTH) ** 0.25
LN_EPS = 1e-5
ATTN_SCALE = HEAD_DIM ** -0.5
ATT_BLOCK = 128
GELU_C = math.sqrt(2.0 / math.pi)


def _params(*sem):
    return pltpu.CompilerParams(dimension_semantics=sem, vmem_limit_bytes=56 * 1024 * 1024)


def _sigmoid(x):
    return 1.0 / (1.0 + jnp.exp(-x))


def _gelu(x):
    return 0.5 * x * (1.0 + jnp.tanh(GELU_C * (x + 0.044715 * (x * x * x))))


def _layer_norm(r, g, b):
    mu = jnp.mean(r, axis=-1, keepdims=True)
    d = r - mu
    var = jnp.mean(d * d, axis=-1, keepdims=True)
    return d * lax.rsqrt(var + LN_EPS) * g + b


def _dot(a, b):
    return jnp.dot(a, b, preferred_element_type=F32)


def _dot_nt(a, b):
    return lax.dot_general(a, b, (((1,), (1,)), ((), ())), preferred_element_type=F32)


def _split_bf16(x):
    hi = x.astype(BF16)
    lo = (x - hi.astype(F32)).astype(BF16)
    return hi, lo


def _inproj0_body(x_ref, w_ref, q_ref, k_ref, v_ref, u_ref, slab_scr):
    nb, tl, D = x_ref.shape
    p = _dot(x_ref[...].reshape(nb * tl, D).astype(BF16), w_ref[...])
    for h in range(SB_HEADS):
        c = h * HEAD_DIM
        q_ref[:, h] = (p[:, c:c + HEAD_DIM] * ATTN_SCALE).astype(BF16).reshape(nb, tl, HEAD_DIM)
        k_ref[:, h] = p[:, SB_WIDTH + c:SB_WIDTH + c + HEAD_DIM].astype(BF16).reshape(nb, tl, HEAD_DIM)
        v_ref[:, h] = p[:, 2 * SB_WIDTH + c:2 * SB_WIDTH + c + HEAD_DIM].astype(BF16).reshape(nb, tl, HEAD_DIM)
    T = SSM_CHUNK
    for v in range(SSM_LANE_TILES):
        c0 = 3 * SB_WIDTH + v * 128
        slab_scr[...] = p[:, c0:c0 + 128]
        for c in range(tl // T):
            for s in range(T):
                u_ref[v, c * nb:(c + 1) * nb, s * 128:(s + 1) * 128] = slab_scr[pl.ds(c * T + s, nb, stride=tl), :]


def _inproj0(x, w_bf16, tl=64):
    B, L, D = x.shape
    tl = min(tl, L)
    nout = w_bf16.shape[1]
    T = SSM_CHUNK
    head = jax.ShapeDtypeStruct((B, SB_HEADS, L, HEAD_DIM), BF16)
    head_spec = pl.BlockSpec((B, SB_HEADS, tl, HEAD_DIM), lambda i: (0, 0, i, 0))
    return pl.pallas_call(
        _inproj0_body,
        grid=(L // tl,),
        in_specs=[pl.BlockSpec((B, tl, D), lambda i: (0, i, 0)),
                  pl.BlockSpec((D, nout), lambda i: (0, 0))],
        out_specs=[head_spec, head_spec, head_spec,
                   pl.BlockSpec((SSM_LANE_TILES, tl // T * B, T * 128), lambda i: (0, i, 0))],
        out_shape=[head, head, head,
                   jax.ShapeDtypeStruct((SSM_LANE_TILES, L // T * B, T * 128), F32)],
        scratch_shapes=[pltpu.VMEM((B * tl, 128), F32)],
        compiler_params=_params("parallel"),
        name="inproj0",
    )(x, w_bf16)


SB_BLOCK = 256
SB_HEADS_PER_STEP = 4
SB_LOGIT_LOOKAHEAD = 3
EXP_UNDERFLOW = -104.0


def _sb_attn_body(q_ref, k_ref, v_ref, o_ref, acc_scr, cs_scr):
    i = pl.program_id(2)
    t = q_ref.shape[2]
    nh = q_ref.shape[1]
    row = lax.broadcasted_iota(jnp.int32, (t, t), 0)
    col = lax.broadcasted_iota(jnp.int32, (t, t), 1)
    suffix = jnp.where(row > col, 1.0, 0.0).astype(BF16)
    below = col < row

    def logits(hh, k0):
        return _dot_nt(q_ref[0, hh], k_ref[0, hh, pl.ds(k0, t), :])

    def block(hh, k0, diagonal, z):
        vb = v_ref[0, hh, pl.ds(k0, t), :]
        sp = jnp.maximum(z, 0.0) + jnp.log(1.0 + jnp.exp(-jnp.abs(z)))
        l1 = jnp.where(below, -sp, 0.0) if diagonal else -sp
        within = _dot(l1.astype(BF16), suffix)
        if diagonal:
            w = jnp.where(below, jnp.exp(z - sp + within), 0.0)
            acc_scr[hh] = _dot(w.astype(BF16), vb)
            cs = jnp.sum(l1, axis=1, keepdims=True)
        else:
            w = jnp.exp(z - sp + within + cs_scr[hh])
            acc_scr[hh] += _dot(w.astype(BF16), vb)
            cs = cs_scr[hh] + jnp.sum(l1, axis=1, keepdims=True)
        cs_scr[hh] = cs
        return cs

    def alive(css):
        return (jnp.max(functools.reduce(jnp.maximum, css)) > EXP_UNDERFLOW).astype(jnp.int32)

    def all_heads(k0, diagonal):
        css = []
        pending = [logits(hh, k0) for hh in range(min(SB_LOGIT_LOOKAHEAD, nh))]
        for hh in range(nh):
            z = pending.pop(0)
            if hh + SB_LOGIT_LOOKAHEAD < nh:
                pending.append(logits(hh + SB_LOGIT_LOOKAHEAD, k0))
            css.append(block(hh, k0, diagonal, z))
        return alive(css)

    first = all_heads(pl.multiple_of(i * t, t), True)

    def cond(c):
        return jnp.logical_and(c[0] <= i, c[1] > 0)

    def body(c):
        return c[0] + 1, all_heads(pl.multiple_of((i - c[0]) * t, t), False)

    lax.while_loop(cond, body, (jnp.int32(1), first))
    o_ref[0] = jnp.concatenate([acc_scr[hh] for hh in range(nh)], axis=-1).astype(o_ref.dtype)


def _sb_attention(q, k, v):
    B, H, L, d = q.shape
    t = min(SB_BLOCK, L)
    nh = SB_HEADS_PER_STEP
    return pl.pallas_call(
        _sb_attn_body,
        grid=(B, H // nh, L // t),
        in_specs=[pl.BlockSpec((1, nh, t, d), lambda b, h, i: (b, h, i, 0)),
                  pl.BlockSpec((1, nh, L, d), lambda b, h, i: (b, h, 0, 0)),
                  pl.BlockSpec((1, nh, L, d), lambda b, h, i: (b, h, 0, 0))],
        out_specs=pl.BlockSpec((1, t, nh * d), lambda b, h, i: (b, i, h)),
        out_shape=jax.ShapeDtypeStruct((B, L, H * d), BF16),
        scratch_shapes=[pltpu.VMEM((nh, t, d), F32), pltpu.VMEM((nh, t, 1), F32)],
        compiler_params=_params("parallel", "parallel", "arbitrary"),
        name="sb_attention",
    )(q, k, v)


def _ssm_tables(lam_re, lam_im, log_dt, b_re, b_im, c_re, c_im, d_skip):
    T, G, P, H = SSM_CHUNK, SSM_GROUPS, SSM_STATE, SSM_GROUP
    hp = lax.Precision.HIGHEST
    dt = jnp.exp(log_dt.astype(F32))[:, None]
    lr = lam_re.astype(F32)
    li = lam_im.astype(F32)
    mag = jnp.exp(lr * dt)
    a_re = mag * jnp.cos(li * dt)
    a_im = mag * jnp.sin(li * dt)
    den = lr * lr + li * li
    nr = a_re - 1.0
    f_re = (nr * lr + a_im * li) / den
    f_im = (a_im * lr - nr * li) / den
    br = b_re.astype(F32)
    bi = b_im.astype(F32)
    bbt_re = (f_re[..., None] * br - f_im[..., None] * bi).transpose(0, 2, 1)[:, None]
    bbt_im = (f_re[..., None] * bi + f_im[..., None] * br).transpose(0, 2, 1)[:, None]
    cr = c_re.astype(F32)[:, None]
    ci = c_im.astype(F32)[:, None]

    def powers(k):
        k = jnp.asarray(k, F32)[None, :, None]
        pmag = jnp.exp(k * (lr * dt)[:, None, :])
        ang = k * (li * dt)[:, None, :]
        return (pmag * jnp.cos(ang))[:, :, None, :], (pmag * jnp.sin(ang))[:, :, None, :]

    def cmul(xr, xi, yr, yi, rows):
        return ((xr * yr - xi * yi).reshape(G, rows, P), (xr * yi + xi * yr).reshape(G, rows, P))

    steps = np.arange(T)
    bs_re, bs_im = cmul(*powers(-steps), bbt_re, bbt_im, T * H)
    ct_re, ct_im = cmul(*powers(steps), cr, ci, T * H)
    full = (jnp.einsum('gap,gbp->gab', bs_re, ct_re, precision=hp)
            - jnp.einsum('gap,gbp->gab', bs_im, ct_im, precision=hp))
    row_s = lax.broadcasted_iota(jnp.int32, (T * H, T * H), 0) // H
    col_t = lax.broadcasted_iota(jnp.int32, (T * H, T * H), 1) // H
    intra = jnp.where(row_s <= col_t, full, 0.0)
    st_re, st_im = cmul(*powers(T - 1 - steps), bbt_re, bbt_im, T * H)
    to_st = jnp.concatenate([st_re, st_im, st_im, st_re], axis=-1)
    c1_re, c1_im = cmul(*powers(steps + 1), cr, ci, T * H)
    from_st_t = jnp.concatenate([c1_re, -c1_im], axis=-1)
    at_re, at_im = powers([T])
    at_re, at_im = at_re[:, 0, 0], at_im[:, 0, 0]
    zeros = jnp.zeros_like(at_re)
    adv = jnp.stack([jnp.concatenate([at_re, at_re], -1),
                     jnp.concatenate([-at_im, at_im], -1),
                     jnp.concatenate([at_im, -at_im], -1)]
                    + [jnp.concatenate([zeros, zeros], -1)] * 5, axis=1)
    dvec = jnp.tile(d_skip.astype(F32).reshape(G, 1, H), (1, 1, T))
    return intra, to_st, from_st_t, adv, dvec


def _ssm_expand(tables):
    intra, to_st, from_st_t, adv, dvec = tables
    V, W, T, H, P = SSM_LANE_TILES, SSM_TILE_GROUPS, SSM_CHUNK, SSM_GROUP, SSM_STATE
    n = T * W * H
    k_in = lax.broadcasted_iota(jnp.int32, (T * H, n), 0)
    col = lax.broadcasted_iota(jnp.int32, (T * H, n), 1)
    spread_th = (k_in == (col // 128) * H + col % H).astype(F32)
    spread_pq = (k_in == (col // (W * 2 * P)) * 2 * P + col % (2 * P)).astype(F32)
    row_g = (lax.broadcasted_iota(jnp.int32, (n, 1), 0) // H) % W
    st_row_g = lax.broadcasted_iota(jnp.int32, (W * 2 * P, 1), 0) // (2 * P)
    col_g_th = (lax.broadcasted_iota(jnp.int32, (1, n), 1) // H) % W
    col_g_pq = (lax.broadcasted_iota(jnp.int32, (1, n), 1) // (2 * P)) % W

    def blockdiag(rows, spread, row_group, col_group):
        full = jnp.einsum('vrk,kc->vrc', rows, spread)
        return jnp.where(row_group == col_group, full, 0.0).astype(BF16)

    by_row = lambda t: t.reshape(V, W, T, H, T * H).transpose(0, 2, 1, 3, 4).reshape(V, n, T * H)
    m8 = blockdiag(by_row(intra), spread_th, row_g, col_g_th)
    ws8 = blockdiag(by_row(to_st), spread_pq, row_g, col_g_pq)
    wi_full = jnp.einsum('vwkq,kc->vwqc', from_st_t.reshape(V, W, T * H, 2 * P), spread_th).reshape(V, W * 2 * P, n)
    wi8 = jnp.where(st_row_g == col_g_th, wi_full, 0.0).astype(BF16)
    adv8 = adv.reshape(V, W, 8, 2 * P).transpose(0, 2, 1, 3).reshape(V, 8, W * 2 * P)
    d8 = jnp.broadcast_to(dvec.reshape(V, 1, W, T, H)[:, :, :, 0:1], (V, 1, W, T, H))
    d8 = d8.transpose(0, 1, 3, 2, 4).reshape(V, 1, T * W * H)
    return m8, ws8, wi8, adv8, d8


def _ssm_body(u_ref, m8_ref, ws8_ref, wi8_ref, adv_ref, dvec_ref, y_ref, s_scr, xin_scr, x1_scr, x2_scr, *, bsz):
    half = x1_scr.shape[1]

    @pl.when(pl.program_id(1) == 0)
    def _():
        x1_scr[...] = jnp.zeros_like(x1_scr)
        x2_scr[...] = jnp.zeros_like(x2_scr)

    u = u_ref[0]
    ub = u.astype(BF16)
    s_scr[...] = _dot(ub, ws8_ref[0])
    a1 = adv_ref[0, 0:1, :]
    a2 = adv_ref[0, 1:2, :]
    a3 = adv_ref[0, 2:3, :]

    def step(c, carry):
        x1, x2 = carry
        r0 = pl.multiple_of(c * bsz, bsz)
        xin_scr[pl.ds(r0, bsz), :] = x1
        s = s_scr[pl.ds(r0, bsz), :]
        return (a1 * x1 + a2 * x2 + s[:, :half], a1 * x2 + a3 * x1 + s[:, half:])

    x1, x2 = lax.fori_loop(0, u.shape[0] // bsz, step, (x1_scr[...], x2_scr[...]))
    x1_scr[...] = x1
    x2_scr[...] = x2
    y_ref[0] = _dot(ub, m8_ref[0]) + _dot(xin_scr[...].astype(BF16), wi8_ref[0]) + dvec_ref[0] * u


def _ssm(us, tables, bsz, rows=256):
    m8, ws8, wi8, adv8, d8 = _ssm_expand(tables)
    V, CB, width = us.shape
    B = bsz
    half = SSM_TILE_GROUPS * 2 * SSM_STATE
    rows = min(rows, CB)
    once = dict(pipeline_mode=pl.Buffered(1))
    y = pl.pallas_call(
        functools.partial(_ssm_body, bsz=B),
        grid=(V, CB // rows),
        in_specs=[pl.BlockSpec((1, rows, width), lambda v, r: (v, r, 0)),
                  pl.BlockSpec((1, width, width), lambda v, r: (v, 0, 0), **once),
                  pl.BlockSpec((1, width, 2 * half), lambda v, r: (v, 0, 0), **once),
                  pl.BlockSpec((1, half, width), lambda v, r: (v, 0, 0), **once),
                  pl.BlockSpec((1, 8, half), lambda v, r: (v, 0, 0)),
                  pl.BlockSpec((1, 1, width), lambda v, r: (v, 0, 0))],
        out_specs=pl.BlockSpec((1, rows, width), lambda v, r: (v, r, 0)),
        out_shape=jax.ShapeDtypeStruct((V, CB, width), F32),
        scratch_shapes=[pltpu.VMEM((rows, 2 * half), F32), pltpu.VMEM((rows, half), F32),
                        pltpu.VMEM((B, half), F32), pltpu.VMEM((B, half), F32)],
        compiler_params=_params("parallel", "arbitrary"),
        name="s5_scan",
    )(us, m8, ws8, wi8, adv8, d8)
    return y


def _outproj0_body(x_ref, oa_ref, y_ref, wglu_ref, woa_ref, wob_ref, g_ref, b_ref, o_ref, slab_scr):
    nb, tl, D = x_ref.shape
    T = SSM_CHUNK
    ys = []
    for v in range(SSM_LANE_TILES):
        for c in range(tl // T):
            for s in range(T):
                slab_scr[pl.ds(c * T + s, nb, stride=tl), :] = y_ref[v, c * nb:(c + 1) * nb, s * 128:(s + 1) * 128]
        ys.append(slab_scr[...])
    h = _gelu(jnp.concatenate(ys, axis=-1))
    ob = h * _sigmoid(_dot(h.astype(BF16), wglu_ref[...]))
    m = _dot(oa_ref[...].reshape(nb * tl, SB_WIDTH), woa_ref[...]) + _dot(ob.astype(BF16), wob_ref[...])
    r = ALPHA * x_ref[...].reshape(nb * tl, D) + m
    o_ref[...] = _layer_norm(r, g_ref[...], b_ref[...]).reshape(nb, tl, D)


def _outproj0(x, oa, ys, wglu, woa, wob, g, b, tl=64):
    B, L, D = x.shape
    tl = min(tl, L)
    T = SSM_CHUNK
    row = lambda i: (0, i, 0)
    fix = lambda i: (0, 0)
    return pl.pallas_call(
        _outproj0_body,
        grid=(L // tl,),
        in_specs=[pl.BlockSpec((B, tl, D), row), pl.BlockSpec((B, tl, SB_WIDTH), row),
                  pl.BlockSpec((SSM_LANE_TILES, tl // T * B, T * 128), row),
                  pl.BlockSpec((SSM_WIDTH, SSM_WIDTH), fix), pl.BlockSpec((SB_WIDTH, D), fix),
                  pl.BlockSpec((SSM_WIDTH, D), fix), pl.BlockSpec((1, D), fix), pl.BlockSpec((1, D), fix)],
        out_specs=pl.BlockSpec((B, tl, D), row),
        out_shape=jax.ShapeDtypeStruct((B, L, D), F32),
        scratch_shapes=[pltpu.VMEM((B * tl, 128), F32)],
        compiler_params=_params("parallel"),
        name="outproj0",
    )(x, oa, ys, wglu, woa, wob, g, b)


def _outproj1_body(x_ref, o_ref_in, w_ref, g_ref, b_ref, o_ref):
    m = _dot(o_ref_in[...], w_ref[...])
    o_ref[...] = _layer_norm(ALPHA * x_ref[...] + m, g_ref[...], b_ref[...])


def _outproj1(x, o, w, g, b, tm=512):
    N, D = x.shape
    tm = min(tm, N)
    row = lambda i: (i, 0)
    fix = lambda i: (0, 0)
    return pl.pallas_call(
        _outproj1_body,
        grid=(N // tm,),
        in_specs=[pl.BlockSpec((tm, D), row), pl.BlockSpec((tm, o.shape[1]), row),
                  pl.BlockSpec(w.shape, fix), pl.BlockSpec((1, D), fix), pl.BlockSpec((1, D), fix)],
        out_specs=pl.BlockSpec((tm, D), row),
        out_shape=jax.ShapeDtypeStruct((N, D), F32),
        compiler_params=_params("parallel"),
        name="outproj1",
    )(x, o, w, g, b)


def _router_body(x_ref, wh_ref, wl_ref, b_ref, g_ref, grp_ref):
    xh, xl = _split_bf16(x_ref[...])
    wh = wh_ref[...]
    logits = _dot_nt(wh, xh) + _dot_nt(wh, xl) + _dot_nt(wl_ref[...], xh)
    aff = _sigmoid(logits)
    sel = aff + b_ref[...]
    E, K = EXPERTS_PER_GROUP, N_GROUPS
    s = [sel[e:e + 1, :] for e in range(N_EXPERTS)]
    a = [aff[e:e + 1, :] for e in range(N_EXPERTS)]
    gscore = []
    for k in range(K):
        v = s[k * E:(k + 1) * E]
        best = None
        for x in range(E):
            for y in range(x + 1, E):
                pair = v[x] + v[y]
                best = pair if best is None else jnp.maximum(best, pair)
        gscore.append(best)
    top = functools.reduce(jnp.maximum, gscore)
    is_g = []
    taken = None
    for k in range(K):
        hit = gscore[k] == top
        if taken is None:
            is_g.append(hit)
            taken = hit
        else:
            is_g.append(jnp.logical_and(hit, jnp.logical_not(taken)))
            taken = jnp.logical_or(taken, hit)

    def pick(rows, j):
        out = rows[(K - 1) * E + j]
        for k in range(K - 2, -1, -1):
            out = jnp.where(is_g[k], rows[k * E + j], out)
        return out

    v = [pick(s, j) for j in range(E)]
    av = [pick(a, j) for j in range(E)]
    chosen = []
    for j in range(E):
        r = jnp.zeros_like(v[j])
        for j2 in range(E):
            if j2 == j:
                continue
            ahead = (v[j2] >= v[j]) if j2 < j else (v[j2] > v[j])
            r = r + jnp.where(ahead, 1.0, 0.0)
        chosen.append(r < 2.0)
    wj = [jnp.where(chosen[j], av[j], 0.0) for j in range(E)]
    tot = wj[0] + wj[1] + wj[2] + wj[3]
    gj = [w / tot for w in wj]
    rows = [jnp.where(is_g[e // E], gj[e % E], 0.0) for e in range(N_EXPERTS)]
    g_ref[...] = jnp.concatenate(rows, axis=0)
    grp_ref[...] = jnp.concatenate([jnp.where(is_g[k], 1.0, 0.0) for k in range(K)], axis=0)


def _router(x, w_router, b_router, tm=512):
    N, D = x.shape
    tm = min(tm, N)
    wt = w_router.astype(F32).T
    wh, wl = _split_bf16(wt)
    fix = lambda i: (0, 0)
    return pl.pallas_call(
        _router_body,
        grid=(N // tm,),
        in_specs=[pl.BlockSpec((tm, D), lambda i: (i, 0)), pl.BlockSpec((N_EXPERTS, D), fix),
                  pl.BlockSpec((N_EXPERTS, D), fix), pl.BlockSpec((N_EXPERTS, 1), fix)],
        out_specs=[pl.BlockSpec((N_EXPERTS, tm), lambda i: (0, i)), pl.BlockSpec((N_GROUPS, tm), lambda i: (0, i))],
        out_shape=[jax.ShapeDtypeStruct((N_EXPERTS, N), F32), jax.ShapeDtypeStruct((N_GROUPS, N), F32)],
        compiler_params=_params("parallel"),
        name="router",
    )(x, wh, wl, b_router.astype(F32).reshape(N_EXPERTS, 1))


MOE_TILE = 1024
MOE_PASS_ROWS = 304
MOE_CHUNK = 256
MOE_EXPERTS_PER_STEP = 2
ROW_ALIGN = 16


def _moe_body(seg_ref, x_ref, gt_ref, grp_ref, w1_ref, w3_ref, w2_ref, g_ref, b_ref, o_ref,
              xs_scr, ys_scr, pt_scr, gs_scr):
    i = pl.program_id(0)
    e = pl.program_id(1)
    T = x_ref.shape[0]
    K = N_GROUPS
    cap = MOE_PASS_ROWS
    ch = min(MOE_CHUNK, T)

    @pl.when(e == 0)
    def _():
        grp = grp_ref[...]
        xb = x_ref[...].astype(BF16)
        gates = gt_ref[...]
        gh, gm = _split_bf16(gates)
        gl = (gates - gh.astype(F32) - gm.astype(F32)).astype(BF16)
        cnt = jnp.sum(grp, axis=1, keepdims=True)
        off = [jnp.zeros((1, 1), F32)]
        for k in range(K - 1):
            off.append(off[-1] + cnt[k:k + 1])
        r_i = lax.broadcasted_iota(jnp.int32, (ch, T), 0)
        c_i = lax.broadcasted_iota(jnp.int32, (ch, T), 1)
        lane = lax.broadcasted_iota(jnp.int32, (2 * K, T), 1)
        run = jnp.concatenate([grp, jnp.zeros_like(grp)], axis=0)
        sh = 1
        while sh < T:
            run = run + jnp.where(lane >= sh, pltpu.roll(run, sh, 1), 0.0)
            sh *= 2
        pos_row = jnp.zeros((1, T), F32)
        for k in range(K):
            pos_row = pos_row + grp[k:k + 1, :] * (off[k] + run[k:k + 1, :] - 1.0)
        pos_t = jnp.broadcast_to(pos_row, (2 * K, T))
        pos_col = [pos_t[:, c * ch:(c + 1) * ch].T[:, 0:1] for c in range(T // ch)]
        for c in range(T // ch):
            c0 = c * ch
            perm = jnp.where((r_i + c0).astype(F32) == pos_row, 1.0, 0.0).astype(BF16)
            xs_scr[c0:c0 + ch, :] = _dot(perm, xb).astype(BF16)
            gs_scr[c0:c0 + ch, :] = _dot_nt(perm, gh) + _dot_nt(perm, gm) + _dot_nt(perm, gl)
            pt_scr[c0:c0 + ch, :] = jnp.where(c_i.astype(F32) == pos_col[c], 1.0, 0.0).astype(BF16)
        xs_scr[T:, :] = jnp.zeros((cap, xs_scr.shape[1]), BF16)
        gs_scr[T:, :] = jnp.zeros((cap, gs_scr.shape[1]), F32)
        ys_scr[...] = jnp.zeros_like(ys_scr)

    ne = w1_ref.shape[0]
    k = (e * ne) // EXPERTS_PER_GROUP
    off = seg_ref[i, k]
    cnt = seg_ref[i, K + k]
    start0 = (off // ROW_ALIGN) * ROW_ALIGN
    n_pass = (off + cnt - start0 + cap - 1) // cap

    def one_pass(n, c):
        start = pl.multiple_of(start0 + n * cap, ROW_ALIGN)
        xc = xs_scr[pl.ds(start, cap), :]
        gsc = gs_scr[pl.ds(start, cap), :]
        lane = lax.broadcasted_iota(jnp.int32, gsc.shape, 1)
        y = None
        for x in range(ne):
            a = _dot(xc, w1_ref[x])
            hm = (a * _sigmoid(a)) * _dot(xc, w3_ref[x])
            gcol = jnp.sum(jnp.where(lane == e * ne + x, gsc, 0.0), axis=1, keepdims=True)
            yx = gcol * _dot(hm.astype(BF16), w2_ref[x])
            y = yx if y is None else y + yx
        ys_scr[pl.ds(start, cap), :] += y
        return c

    lax.fori_loop(0, n_pass, one_pass, 0)

    @pl.when(e == pl.num_programs(1) - 1)
    def _():
        y = _dot(pt_scr[...], ys_scr[0:T, :].astype(BF16))
        o_ref[...] = _layer_norm(ALPHA * x_ref[...] + y, g_ref[...], b_ref[...])


def _moe(x, gates_t, grp_t, w1, w3, w2, g, b):
    N, D = x.shape
    T = min(MOE_TILE, N)
    K = N_GROUPS
    cnt = grp_t.reshape(K, N // T, T).sum(-1).astype(jnp.int32).T
    seg = jnp.concatenate([jnp.cumsum(cnt, axis=1) - cnt, cnt], axis=1)
    fix = lambda i, e, s: (0, 0)
    ne = MOE_EXPERTS_PER_STEP
    rows = T + MOE_PASS_ROWS
    grid_spec = pltpu.PrefetchScalarGridSpec(
        num_scalar_prefetch=1,
        grid=(N // T, N_EXPERTS // ne),
        in_specs=[pl.BlockSpec((T, D), lambda i, e, s: (i, 0)),
                  pl.BlockSpec((N_EXPERTS, T), lambda i, e, s: (0, i)),
                  pl.BlockSpec((K, T), lambda i, e, s: (0, i)),
                  pl.BlockSpec((ne, D, D_EXPERT), lambda i, e, s: (e, 0, 0)),
                  pl.BlockSpec((ne, D, D_EXPERT), lambda i, e, s: (e, 0, 0)),
                  pl.BlockSpec((ne, D_EXPERT, D), lambda i, e, s: (e, 0, 0)),
                  pl.BlockSpec((1, D), fix), pl.BlockSpec((1, D), fix)],
        out_specs=pl.BlockSpec((T, D), lambda i, e, s: (i, 0)),
        scratch_shapes=[pltpu.VMEM((rows, D), BF16), pltpu.VMEM((rows, D), F32),
                        pltpu.VMEM((T, T), BF16), pltpu.VMEM((rows, N_EXPERTS), F32)])
    return pl.pallas_call(
        _moe_body,
        grid_spec=grid_spec,
        out_shape=jax.ShapeDtypeStruct((N, D), F32),
        compiler_params=_params("parallel", "arbitrary"),
        name="moe",
    )(seg, x, gates_t, grp_t, w1, w3, w2, g, b)


NSA_KVW = NSA_KV_HEADS * HEAD_DIM
NSA_ROPE_W = NSA_HEADS * HEAD_DIM + 2 * NSA_KVW
NSA_Q_SCALE = ATTN_SCALE * math.log2(math.e)


def _rot_cols(w):
    K, n = w.shape
    w3 = w.reshape(K, n // HEAD_DIM, 2, HEAD_DIM // 2)
    return jnp.stack([-w3[:, :, 1], w3[:, :, 0]], axis=2).reshape(K, n)


def _inproj1_weights(w_in):
    H, G, d = NSA_HEADS, NSA_KV_HEADS, HEAD_DIM
    cuts = [H * d + i * NSA_KVW for i in range(7)]
    q, kc, vc, ks, vs, kw, vw, gate = jnp.split(w_in.astype(F32), cuts, axis=1)
    w_all = jnp.concatenate([q, ks, kw, kc, vc], axis=1).astype(BF16)
    w_t = jnp.concatenate([vs, vw, gate], axis=1).T.astype(BF16)
    return w_all, w_t


def _rope_tables(pos):
    inv = ROPE_THETA ** (-jnp.arange(0, HEAD_DIM, 2, dtype=F32) / HEAD_DIM)
    ang = pos.astype(F32)[:, None] * inv[None, :]
    c, s = jnp.cos(ang), jnp.sin(ang)
    return jnp.concatenate([c, c], axis=-1), jnp.concatenate([s, s], axis=-1)


def _rope_roll_tables(pos):
    inv = ROPE_THETA ** (-jnp.arange(0, HEAD_DIM, 2, dtype=F32) / HEAD_DIM)
    ang = pos.astype(F32)[:, None] * inv[None, :]
    c, s, z = jnp.cos(ang), jnp.sin(ang), jnp.zeros_like(ang)
    two = lambda a, b: jnp.concatenate([a, b, a, b], axis=-1)
    return two(c, c), two(-s, z), two(z, s)


V_ROWS = HEAD_DIM + 16


def _inproj1_body(x_ref, w_ref, wt_ref, cos_ref, sinlo_ref, sinhi_ref, q_ref, ks_ref, kw_ref, kc_ref, vc_ref,
                  vst_ref, vwt_ref, gate_ref):
    xb = x_ref[0].astype(BF16)
    tl = xb.shape[0]
    d = HEAD_DIM
    p = _dot(xb, w_ref[...])
    pt = _dot_nt(wt_ref[...], xb)
    ones = jnp.ones((V_ROWS - d, tl), BF16)
    for n, ref in enumerate((vst_ref, vwt_ref)):
        for g in range(NSA_KV_HEADS):
            r0 = n * NSA_KVW + g * d
            ref[0, g, 0:d, :] = pt[r0:r0 + d, :].astype(BF16)
            ref[0, g, d:V_ROWS, :] = ones
    gate_ref[0] = pt[2 * NSA_KVW:, :]
    cos = cos_ref[...]
    sin_lo = sinlo_ref[...]
    sin_hi = sinhi_ref[...]
    roped = []
    for c in range(NSA_ROPE_W // 128):
        xc = p[:, c * 128:(c + 1) * 128]
        roped.append(xc * cos + pltpu.roll(xc, 96, 1) * sin_lo + pltpu.roll(xc, 32, 1) * sin_hi)

    def head(chunks, h):
        blk = chunks[h // 2]
        return blk[:, (h % 2) * d:(h % 2 + 1) * d]

    for h in range(NSA_HEADS):
        q_ref[0, h] = (head(roped, h) * NSA_Q_SCALE).astype(BF16)
    nq = NSA_HEADS // 2
    pos = pl.program_id(1) * tl + lax.broadcasted_iota(jnp.int32, (tl, 128 - d), 0)
    block_onehot = jnp.where(pos // SEL_LEN == lax.broadcasted_iota(jnp.int32, (tl, 128 - d), 1), 1.0, 0.0)
    for g in range(NSA_KV_HEADS):
        ks_ref[0, g] = jnp.concatenate([head(roped[nq:], g), block_onehot], axis=1).astype(BF16)
        kw_ref[0, g] = head(roped[nq + 2:], g).astype(BF16)
    for n, ref in enumerate((kc_ref, vc_ref)):
        for g in range(NSA_KV_HEADS):
            c0 = NSA_ROPE_W + n * NSA_KVW + g * d
            ref[0, g] = p[:, c0:c0 + d].astype(BF16)


def _inproj1(x, w_all, w_t, tl=256):
    B, L, D = x.shape
    tl = min(tl, L)
    G = NSA_KV_HEADS
    n_gate = w_t.shape[0] - 2 * NSA_KVW
    tables = _rope_roll_tables(jnp.arange(L))
    qh = jax.ShapeDtypeStruct((B, NSA_HEADS, L, HEAD_DIM), BF16)
    kvh = jax.ShapeDtypeStruct((B, G, L, HEAD_DIM), BF16)
    vth = jax.ShapeDtypeStruct((B, G, V_ROWS, L), BF16)
    q_spec = pl.BlockSpec((1, NSA_HEADS, tl, HEAD_DIM), lambda b, i: (b, 0, i, 0))
    kv_spec = pl.BlockSpec((1, G, tl, HEAD_DIM), lambda b, i: (b, 0, i, 0))
    ksel_spec = pl.BlockSpec((1, G, tl, 128), lambda b, i: (b, 0, i, 0))
    vt_spec = pl.BlockSpec((1, G, V_ROWS, tl), lambda b, i: (b, 0, 0, i))
    tab_spec = pl.BlockSpec((tl, 128), lambda b, i: (i, 0))
    return pl.pallas_call(
        _inproj1_body,
        grid=(B, L // tl),
        in_specs=[pl.BlockSpec((1, tl, D), lambda b, i: (b, i, 0)),
                  pl.BlockSpec(w_all.shape, lambda b, i: (0, 0)),
                  pl.BlockSpec(w_t.shape, lambda b, i: (0, 0)),
                  tab_spec, tab_spec, tab_spec],
        out_specs=([q_spec, ksel_spec] + [kv_spec] * 3 + [vt_spec] * 2
                   + [pl.BlockSpec((1, n_gate, tl), lambda b, i: (b, 0, i))]),
        out_shape=([qh, jax.ShapeDtypeStruct((B, G, L, 128), BF16)] + [kvh] * 3 + [vth] * 2
                   + [jax.ShapeDtypeStruct((B, n_gate, L), F32)]),
        compiler_params=_params("parallel", "parallel"),
        name="inproj1",
    )(x, w_all, w_t, *tables)


def _compress_body(kc_ref, vc_ref, posk_ref, posv_ref, w1k_ref, w1v_ref, w2k_ref, w2kr_ref, w2v_ref,
                   cos_ref, sin_ref, kcmp_ref, vcmp_ref):
    def hidden(a_ref, pos_ref, w1_ref):
        a = a_ref[0, 0].astype(F32)
        nrow = a.shape[0]
        lo = _dot((a + pos_ref[0:1, :]).astype(BF16), w1_ref[0])
        hi = _dot((a + pos_ref[1:2, :]).astype(BF16), w1_ref[1])
        hi_next = pltpu.roll(hi, nrow - 1, 0)
        return _gelu(lo + hi_next).astype(BF16)

    hk = hidden(kc_ref, posk_ref, w1k_ref)
    kcmp = _dot(hk, w2k_ref[...]) * cos_ref[...] + _dot(hk, w2kr_ref[...]) * sin_ref[...]
    kcmp_ref[0, 0] = kcmp.astype(BF16)
    hv = hidden(vc_ref, posv_ref, w1v_ref)
    vcmp_ref[0, 0] = _dot_nt(w2v_ref[...], hv).astype(BF16)


def _compress(kc, vc, pos_k, w1_k, w2_k, pos_v, w1_v, w2_v):
    B, G, L, d = kc.shape
    half = CMP_STRIDE * d
    nb = L // CMP_STRIDE
    kc2 = kc.reshape(B, G, nb, half)
    vc2 = vc.reshape(B, G, nb, half)
    posk = pos_k.astype(F32).reshape(2, half)
    posv = pos_v.astype(F32).reshape(2, half)
    w1k = w1_k.astype(BF16).reshape(2, half, CMP_HIDDEN)
    w1v = w1_v.astype(BF16).reshape(2, half, CMP_HIDDEN)
    w2k = w2_k.astype(F32)
    cos, sin = _rope_tables(jnp.arange(nb) * CMP_STRIDE + CMP_LEN - 1)
    blk = pl.BlockSpec((1, 1, nb, half), lambda b, g: (b, g, 0, 0))
    out = pl.BlockSpec((1, 1, nb, d), lambda b, g: (b, g, 0, 0))
    out_t = pl.BlockSpec((1, 1, d, nb), lambda b, g: (b, g, 0, 0))
    fix2 = lambda b, g: (0, 0)
    fix3 = lambda b, g: (0, 0, 0)
    return pl.pallas_call(
        _compress_body,
        grid=(B, G),
        in_specs=[blk, blk, pl.BlockSpec((2, half), fix2), pl.BlockSpec((2, half), fix2),
                  pl.BlockSpec((2, half, CMP_HIDDEN), fix3), pl.BlockSpec((2, half, CMP_HIDDEN), fix3),
                  pl.BlockSpec((CMP_HIDDEN, d), fix2), pl.BlockSpec((CMP_HIDDEN, d), fix2),
                  pl.BlockSpec((d, CMP_HIDDEN), fix2), pl.BlockSpec((nb, d), fix2), pl.BlockSpec((nb, d), fix2)],
        out_specs=[out, out_t],
        out_shape=[jax.ShapeDtypeStruct((B, G, nb, d), BF16), jax.ShapeDtypeStruct((B, G, d, nb), BF16)],
        compiler_params=_params("parallel", "parallel"),
        name="compress",
    )(kc2, vc2, posk, posv, w1k, w1v, w2k.astype(BF16), _rot_cols(w2k).astype(BF16), w2_v.T.astype(BF16), cos, sin)


NSA_KBLOCK = 256
NSA_GROUPS_PER_STEP = 4
NSA_HEADS_PER_CHAIN = 4
NSA_SCORE_LOOKAHEAD = 3


def _nsa_body(q_ref, kcmp_ref, vcmpt_ref, ks_ref, vst_ref, kw_ref, vwt_ref, gate_ref, o_ref, qsel_scr, *, seq):
    i = pl.program_id(2)
    tq = q_ref.shape[2]
    gb = kcmp_ref.shape[1]
    tk = min(NSA_KBLOCK, seq)
    R, d = NSA_REP, HEAD_DIM
    nq = R * tq
    nb = seq // SEL_LEN
    mc = seq // CMP_STRIDE
    t_row = i * tq + lax.broadcasted_iota(jnp.int32, (1, tq), 1)
    t_all = jnp.concatenate([t_row] * R, axis=1)
    m_col = lax.broadcasted_iota(jnp.int32, (mc, 1), 0)
    m_row = lax.broadcasted_iota(jnp.int32, (1, mc), 1)
    n_col = lax.broadcasted_iota(jnp.int32, (nb, 1), 0)

    def q_rows(g):
        return q_ref[0, g * R:(g + 1) * R].reshape(nq, d)

    valid_c = (m_col * CMP_STRIDE + (CMP_LEN - 1)) <= t_all
    ovl = jnp.logical_and(m_row * CMP_STRIDE < (n_col + 1) * SEL_LEN,
                          m_row * CMP_STRIDE + CMP_LEN > n_col * SEL_LEN)
    ovl = jnp.where(ovl, 1.0, 0.0).astype(BF16)
    cur = t_row // SEL_LEN
    forced = jnp.logical_or(n_col == 0, jnp.logical_or(n_col == cur, n_col == cur - 1))
    bonus = jnp.where(forced, FORCE_BONUS, 0.0)
    valid_s = n_col * SEL_LEN <= t_row
    o_c = []
    ties = [jnp.where(n_col > n2, 1.0, 0.0) for n2 in range(nb)]
    cmp_scores = [_dot_nt(kcmp_ref[0, g], q_rows(g)) for g in range(gb)]
    for g in range(gb):
        s = jnp.where(valid_c, cmp_scores[g], NEG)
        e = jnp.where(valid_c, jnp.exp2(s - jnp.max(s, axis=0, keepdims=True)), 0.0)
        den = jnp.sum(e, axis=0, keepdims=True)
        p = e / jnp.where(den > 0.0, den, 1.0)
        o_c.append(_dot(vcmpt_ref[0, g], p.astype(BF16)))
        psum = p[:, 0:tq]
        for r in range(1, R):
            psum = psum + p[:, r * tq:(r + 1) * tq]
        ph, plo = _split_bf16(psum)
        score = jnp.where(valid_s, _dot(ovl, ph) + _dot(ovl, plo) + bonus, NEG)
        rank = jnp.zeros((nb, tq), F32)
        for n2 in range(nb):
            other = score[n2:n2 + 1, :]
            rank = rank + jnp.where(other > score, 1.0, jnp.where(other == score, ties[n2], 0.0))
        pen = jnp.where(rank < float(SEL_TOP), 0.0, NEG).T.astype(BF16)
        pen = jnp.concatenate([pen, jnp.zeros((tq, 128 - d - nb), BF16)], axis=1)
        qsel_scr[g] = jnp.concatenate([q_rows(g), jnp.concatenate([pen] * R, axis=0)], axis=1)

    k_col = lax.broadcasted_iota(jnp.int32, (tk, 1), 0)
    hp = NSA_HEADS_PER_CHAIN
    j_hi = (i * tq + tq - 1) // tk + 1

    def sweep(q_fn, k_ref, vt_ref, j_lo, bias_fn, bias_every_block):
        chains = [(g, h0) for g in range(gb) for h0 in range(0, R, hp)]

        def body(j, state, with_bias):
            k0 = pl.multiple_of(j * tk, tk)
            bias = jnp.concatenate([bias_fn(k0 + k_col)] * hp, axis=1) if with_bias else None

            def scores(c):
                g, h0 = chains[c]
                sc = _dot_nt(k_ref[0, g, pl.ds(k0, tk), :], q_fn(g, h0))
                return sc + bias if with_bias else sc

            new_state = []
            ahead = NSA_SCORE_LOOKAHEAD
            pending = [scores(c) for c in range(min(ahead, len(chains)))]
            for c, (m_run, acc) in enumerate(state):
                sc = pending.pop(0)
                if c + ahead < len(chains):
                    pending.append(scores(c + ahead))
                vtb = vt_ref[0, chains[c][0], :, pl.ds(k0, tk)]
                m_new = jnp.maximum(m_run, jnp.max(sc, axis=0, keepdims=True))
                pr = jnp.exp2(sc - m_new)
                new_state.append((m_new, jnp.exp2(m_run - m_new) * acc + _dot(vtb, pr.astype(BF16))))
            return tuple(new_state)

        state = tuple((jnp.full((1, hp * tq), NEG, F32), jnp.zeros((V_ROWS, hp * tq), F32)) for _ in chains)
        state = lax.fori_loop(j_lo, j_hi - 1, functools.partial(body, with_bias=bias_every_block), state)
        state = body(j_hi - 1, state, True)
        outs = []
        for g in range(gb):
            acc = jnp.concatenate([a for (cg, _), (_, a) in zip(chains, state) if cg == g], axis=1)
            outs.append(acc[0:d] / acc[d:d + 1])
        return outs

    def causal_bias(kpos):
        return jnp.where(kpos <= t_row, 0.0, NEG)

    o_s = sweep(lambda g, h0: qsel_scr[g, h0 * tq:(h0 + hp) * tq, :], ks_ref, vst_ref, 0, causal_bias, False)

    def win_bias(kpos):
        return jnp.where(jnp.logical_and(kpos <= t_row, kpos > t_row - WINDOW), 0.0, NEG)

    o_w = sweep(lambda g, h0: q_ref[0, g * R + h0:g * R + h0 + hp].reshape(hp * tq, d), kw_ref, vwt_ref,
                jnp.maximum(i * tq - (WINDOW - 1), 0) // tk, win_bias, True)

    outs = []
    for g in range(gb):
        row0 = (pl.program_id(1) * gb + g) * 3 * R

        def gate(branch):
            rows = [gate_ref[0, pl.ds(row0 + 3 * r + branch, 1), :] for r in range(R)]
            return _sigmoid(jnp.concatenate(rows, axis=1))

        o_t = gate(0) * o_c[g] + gate(1) * o_s[g] + gate(2) * o_w[g]
        for r in range(R):
            outs.append(o_t[:, r * tq:(r + 1) * tq].T)
    o_ref[0] = jnp.concatenate(outs, axis=-1).astype(o_ref.dtype)


def _nsa_attention(q, kcmp, vcmpt, ks, vst, kw, vwt, gate):
    B, H, L, d = q.shape
    G, R = NSA_KV_HEADS, NSA_REP
    gb = NSA_GROUPS_PER_STEP
    tq = min(ATT_BLOCK, L)
    mc = L // CMP_STRIDE
    k_spec = pl.BlockSpec((1, gb, L, d), lambda b, g, i: (b, g, 0, 0))
    ksel_spec = pl.BlockSpec((1, gb, L, 128), lambda b, g, i: (b, g, 0, 0))
    vt_spec = pl.BlockSpec((1, gb, V_ROWS, L), lambda b, g, i: (b, g, 0, 0))
    return pl.pallas_call(
        functools.partial(_nsa_body, seq=L),
        grid=(B, G // gb, L // tq),
        in_specs=[pl.BlockSpec((1, gb * R, tq, d), lambda b, g, i: (b, g, i, 0)),
                  pl.BlockSpec((1, gb, mc, d), lambda b, g, i: (b, g, 0, 0)),
                  pl.BlockSpec((1, gb, d, mc), lambda b, g, i: (b, g, 0, 0)),
                  ksel_spec, vt_spec, k_spec, vt_spec,
                  pl.BlockSpec((1, gate.shape[1], tq), lambda b, g, i: (b, 0, i))],
        out_specs=pl.BlockSpec((1, tq, gb * R * d), lambda b, g, i: (b, i, g)),
        out_shape=jax.ShapeDtypeStruct((B, L, H * d), BF16),
        scratch_shapes=[pltpu.VMEM((gb, R * tq, 128), BF16)],
        compiler_params=_params("parallel", "parallel", "arbitrary"),
        name="nsa_attention",
    )(q, kcmp, vcmpt, ks, vst, kw, vwt, gate)


def kernel(x, w_in_0, ssm_lam_re, ssm_lam_im, ssm_log_dt, ssm_b_re, ssm_b_im, ssm_c_re, ssm_c_im, ssm_d, w_glu, w_out_0, ln_mix_g_0, ln_mix_b_0, ln_ffn_g_0, ln_ffn_b_0, w1_0, w3_0, w2_0, w_in_1, cmp_pos_k, cmp_w1_k, cmp_w2_k, cmp_pos_v, cmp_w1_v, cmp_w2_v, w_out_1, ln_mix_g_1, ln_mix_b_1, ln_ffn_g_1, ln_ffn_b_1, w1_1, w3_1, w2_1, w_router, b_router):
    B, L, D = x.shape
    N = B * L
    vec = lambda a: a.astype(F32).reshape(1, D)

    def ffn(h, w1, w3, w2, g, b):
        gates_t, grp_t = _router(h, w_router, b_router)
        return _moe(h, gates_t, grp_t, w1.astype(BF16), w3.astype(BF16), w2.astype(BF16), vec(g), vec(b))

    q, k, v, u = _inproj0(x, w_in_0.astype(BF16))
    o_a = _sb_attention(q, k, v)
    y = _ssm(u, _ssm_tables(ssm_lam_re, ssm_lam_im, ssm_log_dt, ssm_b_re, ssm_b_im, ssm_c_re, ssm_c_im, ssm_d), B)
    w_out_0b = w_out_0.astype(BF16)
    h = _outproj0(x, o_a, y, w_glu.astype(BF16), w_out_0b[:SB_WIDTH], w_out_0b[SB_WIDTH:],
                  vec(ln_mix_g_0), vec(ln_mix_b_0)).reshape(N, D)
    h = ffn(h, w1_0, w3_0, w2_0, ln_ffn_g_0, ln_ffn_b_0)

    q, ks, kw, kc, vc, vst, vwt, gate = _inproj1(h.reshape(B, L, D), *_inproj1_weights(w_in_1))
    kcmp, vcmpt = _compress(kc, vc, cmp_pos_k, cmp_w1_k, cmp_w2_k, cmp_pos_v, cmp_w1_v, cmp_w2_v)
    o = _nsa_attention(q, kcmp, vcmpt, ks, vst, kw, vwt, gate).reshape(N, NSA_HEADS * HEAD_DIM)
    h = _outproj1(h, o, w_out_1.astype(BF16), vec(ln_mix_g_1), vec(ln_mix_b_1))
    h = ffn(h, w1_1, w3_1, w2_1, ln_ffn_g_1, ln_ffn_b_1)
    return h.reshape(B, L, D)
```

```python
import functools
import math

import numpy as np
import jax
import jax.numpy as jnp
from jax import lax
from jax.experimental import pallas as pl
from jax.experimental.pallas import tpu as pltpu

F32 = jnp.float32
BF16 = jnp.bfloat16

D_MODEL = 1024
DEPTH = 2
SB_HEADS = 8
HEAD_DIM = 64
SB_WIDTH = SB_HEADS * HEAD_DIM
SSM_WIDTH = D_MODEL - SB_WIDTH
SSM_GROUP = 16
SSM_GROUPS = SSM_WIDTH // SSM_GROUP
SSM_STATE = 64
SSM_CHUNK = 16
SSM_LANE_TILES = SSM_WIDTH // 128
SSM_TILE_GROUPS = 128 // SSM_GROUP
NSA_HEADS = 16
NSA_KV_HEADS = 4
NSA_REP = NSA_HEADS // NSA_KV_HEADS
CMP_LEN = 32
CMP_STRIDE = 16
CMP_HIDDEN = 256
SEL_LEN = 64
SEL_TOP = 8
WINDOW = 512
ROPE_THETA = 10000.0
FORCE_BONUS = 1e4
NEG = -1e30
N_EXPERTS = 16
N_GROUPS = 4
EXPERTS_PER_GROUP = N_EXPERTS // N_GROUPS
D_EXPERT = 512
ALPHA = (2 * DEPTH) ** 0.25
LN_EPS = 1e-5
ATTN_SCALE = HEAD_DIM ** -0.5
LOG2_Q_SCALE = ATTN_SCALE * math.log2(math.e)
ATT_BLOCK = 128
GELU_C = math.sqrt(2.0 / math.pi)


def _params(*sem):
    return pltpu.CompilerParams(dimension_semantics=sem, vmem_limit_bytes=56 * 1024 * 1024)


def _sigmoid(x):
    return 1.0 / (1.0 + jnp.exp(-x))


def _gelu(x):
    return 0.5 * x * (1.0 + jnp.tanh(GELU_C * (x + 0.044715 * (x * x * x))))


def _layer_norm(r, g, b):
    mu = jnp.mean(r, axis=-1, keepdims=True)
    d = r - mu
    var = jnp.mean(d * d, axis=-1, keepdims=True)
    return d * lax.rsqrt(var + LN_EPS) * g + b


def _dot(a, b):
    return jnp.dot(a, b, preferred_element_type=F32)


def _dot_nt(a, b):
    return lax.dot_general(a, b, (((1,), (1,)), ((), ())), preferred_element_type=F32)


def _split_bf16(x):
    hi = x.astype(BF16)
    lo = (x - hi.astype(F32)).astype(BF16)
    return hi, lo


def _inproj0_body(x_ref, w_ref, q_ref, k_ref, v_ref, u_ref, slab_scr):
    nb, tl, D = x_ref.shape
    p = _dot(x_ref[...].reshape(nb * tl, D).astype(BF16), w_ref[...])
    for h in range(SB_HEADS):
        c = h * HEAD_DIM
        q_ref[:, h] = (p[:, c:c + HEAD_DIM] * LOG2_Q_SCALE).astype(BF16).reshape(nb, tl, HEAD_DIM)
        k_ref[:, h] = p[:, SB_WIDTH + c:SB_WIDTH + c + HEAD_DIM].astype(BF16).reshape(nb, tl, HEAD_DIM)
        v_ref[:, h] = p[:, 2 * SB_WIDTH + c:2 * SB_WIDTH + c + HEAD_DIM].astype(BF16).reshape(nb, tl, HEAD_DIM)
    T = SSM_CHUNK
    for v in range(SSM_LANE_TILES):
        c0 = 3 * SB_WIDTH + v * 128
        slab_scr[...] = p[:, c0:c0 + 128]
        for c in range(tl // T):
            for s in range(T):
                u_ref[v, c * nb:(c + 1) * nb, s * 128:(s + 1) * 128] = slab_scr[pl.ds(c * T + s, nb, stride=tl), :]


def _inproj0(x, w_bf16, tl=64):
    B, L, D = x.shape
    tl = min(tl, L)
    nout = w_bf16.shape[1]
    T = SSM_CHUNK
    head = jax.ShapeDtypeStruct((B, SB_HEADS, L, HEAD_DIM), BF16)
    head_spec = pl.BlockSpec((B, SB_HEADS, tl, HEAD_DIM), lambda i: (0, 0, i, 0))
    return pl.pallas_call(
        _inproj0_body,
        grid=(L // tl,),
        in_specs=[pl.BlockSpec((B, tl, D), lambda i: (0, i, 0)),
                  pl.BlockSpec((D, nout), lambda i: (0, 0))],
        out_specs=[head_spec, head_spec, head_spec,
                   pl.BlockSpec((SSM_LANE_TILES, tl // T * B, T * 128), lambda i: (0, i, 0))],
        out_shape=[head, head, head,
                   jax.ShapeDtypeStruct((SSM_LANE_TILES, L // T * B, T * 128), F32)],
        scratch_shapes=[pltpu.VMEM((B * tl, 128), F32)],
        compiler_params=_params("parallel"),
        name="inproj0",
    )(x, w_bf16)


SB_BLOCK = 256
SB_HEADS_PER_STEP = 4
SB_LOGIT_LOOKAHEAD = 3
EXP2_UNDERFLOW = -151.0


def _sb_attn_body(q_ref, k_ref, v_ref, o_ref, acc_scr, cs_scr):
    i = pl.program_id(2)
    t = q_ref.shape[2]
    nh = q_ref.shape[1]
    row = lax.broadcasted_iota(jnp.int32, (t, t), 0)
    col = lax.broadcasted_iota(jnp.int32, (t, t), 1)
    suffix = jnp.where(row > col, 1.0, 0.0).astype(BF16)
    below = col < row

    def logits(hh, k0):
        return _dot_nt(q_ref[0, hh], k_ref[0, hh, pl.ds(k0, t), :])

    def block(hh, k0, diagonal, z):
        vb = v_ref[0, hh, pl.ds(k0, t), :]
        sp = jnp.maximum(z, 0.0) + jnp.log2(1.0 + jnp.exp2(-jnp.abs(z)))
        l1 = jnp.where(below, -sp, 0.0) if diagonal else -sp
        within = _dot(l1.astype(BF16), suffix)
        if diagonal:
            w = jnp.where(below, jnp.exp2(z - sp + within), 0.0)
            acc_scr[hh] = _dot(w.astype(BF16), vb)
            cs = jnp.sum(l1, axis=1, keepdims=True)
        else:
            w = jnp.exp2(z - sp + within + cs_scr[hh])
            acc_scr[hh] += _dot(w.astype(BF16), vb)
            cs = cs_scr[hh] + jnp.sum(l1, axis=1, keepdims=True)
        cs_scr[hh] = cs
        return cs

    def alive(css):
        return (jnp.max(functools.reduce(jnp.maximum, css)) > EXP2_UNDERFLOW).astype(jnp.int32)

    def all_heads(k0, diagonal):
        css = []
        pending = [logits(hh, k0) for hh in range(min(SB_LOGIT_LOOKAHEAD, nh))]
        for hh in range(nh):
            z = pending.pop(0)
            if hh + SB_LOGIT_LOOKAHEAD < nh:
                pending.append(logits(hh + SB_LOGIT_LOOKAHEAD, k0))
            css.append(block(hh, k0, diagonal, z))
        return alive(css)

    first = all_heads(pl.multiple_of(i * t, t), True)

    def cond(c):
        return jnp.logical_and(c[0] <= i, c[1] > 0)

    def body(c):
        return c[0] + 1, all_heads(pl.multiple_of((i - c[0]) * t, t), False)

    lax.while_loop(cond, body, (jnp.int32(1), first))
    o_ref[0] = jnp.concatenate([acc_scr[hh] for hh in range(nh)], axis=-1).astype(o_ref.dtype)


def _sb_attention(q, k, v):
    B, H, L, d = q.shape
    t = min(SB_BLOCK, L)
    nh = SB_HEADS_PER_STEP
    return pl.pallas_call(
        _sb_attn_body,
        grid=(B, H // nh, L // t),
        in_specs=[pl.BlockSpec((1, nh, t, d), lambda b, h, i: (b, h, i, 0)),
                  pl.BlockSpec((1, nh, L, d), lambda b, h, i: (b, h, 0, 0)),
                  pl.BlockSpec((1, nh, L, d), lambda b, h, i: (b, h, 0, 0))],
        out_specs=pl.BlockSpec((1, t, nh * d), lambda b, h, i: (b, i, h)),
        out_shape=jax.ShapeDtypeStruct((B, L, H * d), BF16),
        scratch_shapes=[pltpu.VMEM((nh, t, d), F32), pltpu.VMEM((nh, t, 1), F32)],
        compiler_params=_params("parallel", "parallel", "arbitrary"),
        name="sb_attention",
    )(q, k, v)


def _ssm_tables(lam_re, lam_im, log_dt, b_re, b_im, c_re, c_im, d_skip):
    T, G, P, H = SSM_CHUNK, SSM_GROUPS, SSM_STATE, SSM_GROUP
    hp = lax.Precision.HIGHEST
    dt = jnp.exp(log_dt.astype(F32))[:, None]
    lr = lam_re.astype(F32)
    li = lam_im.astype(F32)
    mag = jnp.exp(lr * dt)
    a_re = mag * jnp.cos(li * dt)
    a_im = mag * jnp.sin(li * dt)
    den = lr * lr + li * li
    nr = a_re - 1.0
    f_re = (nr * lr + a_im * li) / den
    f_im = (a_im * lr - nr * li) / den
    br = b_re.astype(F32)
    bi = b_im.astype(F32)
    bbt_re = (f_re[..., None] * br - f_im[..., None] * bi).transpose(0, 2, 1)[:, None]
    bbt_im = (f_re[..., None] * bi + f_im[..., None] * br).transpose(0, 2, 1)[:, None]
    cr = c_re.astype(F32)[:, None]
    ci = c_im.astype(F32)[:, None]

    def powers(k):
        k = jnp.asarray(k, F32)[None, :, None]
        pmag = jnp.exp(k * (lr * dt)[:, None, :])
        ang = k * (li * dt)[:, None, :]
        return (pmag * jnp.cos(ang))[:, :, None, :], (pmag * jnp.sin(ang))[:, :, None, :]

    def cmul(xr, xi, yr, yi, rows):
        return ((xr * yr - xi * yi).reshape(G, rows, P), (xr * yi + xi * yr).reshape(G, rows, P))

    steps = np.arange(T)
    bs_re, bs_im = cmul(*powers(-steps), bbt_re, bbt_im, T * H)
    ct_re, ct_im = cmul(*powers(steps), cr, ci, T * H)
    full = (jnp.einsum('gap,gbp->gab', bs_re, ct_re, precision=hp)
            - jnp.einsum('gap,gbp->gab', bs_im, ct_im, precision=hp))
    row_s = lax.broadcasted_iota(jnp.int32, (T * H, T * H), 0) // H
    col_t = lax.broadcasted_iota(jnp.int32, (T * H, T * H), 1) // H
    intra = jnp.where(row_s <= col_t, full, 0.0)
    st_re, st_im = cmul(*powers(T - 1 - steps), bbt_re, bbt_im, T * H)
    to_st = jnp.concatenate([st_re, st_im, st_im, st_re], axis=-1)
    c1_re, c1_im = cmul(*powers(steps + 1), cr, ci, T * H)
    from_st_t = jnp.concatenate([c1_re, -c1_im], axis=-1)
    at_re, at_im = powers([T])
    at_re, at_im = at_re[:, 0, 0], at_im[:, 0, 0]
    zeros = jnp.zeros_like(at_re)
    adv = jnp.stack([jnp.concatenate([at_re, at_re], -1),
                     jnp.concatenate([-at_im, at_im], -1),
                     jnp.concatenate([at_im, -at_im], -1)]
                    + [jnp.concatenate([zeros, zeros], -1)] * 5, axis=1)
    dvec = jnp.tile(d_skip.astype(F32).reshape(G, 1, H), (1, 1, T))
    return intra, to_st, from_st_t, adv, dvec


def _ssm_expand(tables):
    intra, to_st, from_st_t, adv, dvec = tables
    V, W, T, H, P = SSM_LANE_TILES, SSM_TILE_GROUPS, SSM_CHUNK, SSM_GROUP, SSM_STATE
    n = T * W * H
    k_in = lax.broadcasted_iota(jnp.int32, (T * H, n), 0)
    col = lax.broadcasted_iota(jnp.int32, (T * H, n), 1)
    spread_th = (k_in == (col // 128) * H + col % H).astype(F32)
    spread_q = (k_in[:2 * P, :W * 2 * P] == col[:2 * P, :W * 2 * P] % (2 * P)).astype(F32)
    row_g = (lax.broadcasted_iota(jnp.int32, (n, 1), 0) // H) % W
    st_row_g = lax.broadcasted_iota(jnp.int32, (W * 2 * P, 1), 0) // (2 * P)
    col_g_th = (lax.broadcasted_iota(jnp.int32, (1, n), 1) // H) % W
    col_g_q = lax.broadcasted_iota(jnp.int32, (1, W * 2 * P), 1) // (2 * P)

    def blockdiag(rows, spread, row_group, col_group):
        full = jnp.einsum('vrk,kc->vrc', rows, spread)
        return jnp.where(row_group == col_group, full, 0.0).astype(BF16)

    by_row = lambda t: t.reshape(V, W, T, H, T * H).transpose(0, 2, 1, 3, 4).reshape(V, n, T * H)
    m8 = blockdiag(by_row(intra), spread_th, row_g, col_g_th)
    ws8 = blockdiag(by_row(to_st)[:, :, :2 * P], spread_q, row_g, col_g_q)
    wi_full = jnp.einsum('vwkq,kc->vwqc', from_st_t.reshape(V, W, T * H, 2 * P), spread_th).reshape(V, W * 2 * P, n)
    wi8 = jnp.where(st_row_g == col_g_th, wi_full, 0.0).astype(BF16)
    adv8 = adv.reshape(V, W, 8, 2 * P).transpose(0, 2, 1, 3).reshape(V, 8, W * 2 * P)
    d8 = jnp.broadcast_to(dvec.reshape(V, 1, W, T, H)[:, :, :, 0:1], (V, 1, W, T, H))
    d8 = d8.transpose(0, 1, 3, 2, 4).reshape(V, 1, T * W * H)
    return m8, ws8, wi8, adv8, d8


def _ssm_body(u_ref, m8_ref, ws8_ref, wi8_ref, adv_ref, dvec_ref, y_ref, s_scr, xin_scr, x1_scr, x2_scr, *, bsz):
    half = x1_scr.shape[1]

    @pl.when(pl.program_id(1) == 0)
    def _():
        x1_scr[...] = jnp.zeros_like(x1_scr)
        x2_scr[...] = jnp.zeros_like(x2_scr)

    u = u_ref[0]
    ub = u.astype(BF16)
    s1 = _dot(ub, ws8_ref[0])
    lane = lax.broadcasted_iota(jnp.int32, s1.shape, 1)
    s_scr[:, :half] = s1
    s_scr[:, half:] = jnp.where(lane % (2 * SSM_STATE) < SSM_STATE,
                                pltpu.roll(s1, half - SSM_STATE, 1), pltpu.roll(s1, SSM_STATE, 1))
    a1 = adv_ref[0, 0:1, :]
    a2 = adv_ref[0, 1:2, :]
    a3 = adv_ref[0, 2:3, :]

    def step(c, carry):
        x1, x2 = carry
        r0 = pl.multiple_of(c * bsz, bsz)
        xin_scr[pl.ds(r0, bsz), :] = x1
        s = s_scr[pl.ds(r0, bsz), :]
        return (a1 * x1 + a2 * x2 + s[:, :half], a1 * x2 + a3 * x1 + s[:, half:])

    x1, x2 = lax.fori_loop(0, u.shape[0] // bsz, step, (x1_scr[...], x2_scr[...]))
    x1_scr[...] = x1
    x2_scr[...] = x2
    y_ref[0] = _dot(ub, m8_ref[0]) + _dot(xin_scr[...].astype(BF16), wi8_ref[0]) + dvec_ref[0] * u


def _ssm(us, tables, bsz, rows=256):
    m8, ws8, wi8, adv8, d8 = _ssm_expand(tables)
    V, CB, width = us.shape
    B = bsz
    half = SSM_TILE_GROUPS * 2 * SSM_STATE
    rows = min(rows, CB)
    once = dict(pipeline_mode=pl.Buffered(1))
    y = pl.pallas_call(
        functools.partial(_ssm_body, bsz=B),
        grid=(V, CB // rows),
        in_specs=[pl.BlockSpec((1, rows, width), lambda v, r: (v, r, 0)),
                  pl.BlockSpec((1, width, width), lambda v, r: (v, 0, 0), **once),
                  pl.BlockSpec((1, width, half), lambda v, r: (v, 0, 0), **once),
                  pl.BlockSpec((1, half, width), lambda v, r: (v, 0, 0), **once),
                  pl.BlockSpec((1, 8, half), lambda v, r: (v, 0, 0)),
                  pl.BlockSpec((1, 1, width), lambda v, r: (v, 0, 0))],
        out_specs=pl.BlockSpec((1, rows, width), lambda v, r: (v, r, 0)),
        out_shape=jax.ShapeDtypeStruct((V, CB, width), F32),
        scratch_shapes=[pltpu.VMEM((rows, 2 * half), F32), pltpu.VMEM((rows, half), F32),
                        pltpu.VMEM((B, half), F32), pltpu.VMEM((B, half), F32)],
        compiler_params=_params("parallel", "arbitrary"),
        name="s5_scan",
    )(us, m8, ws8, wi8, adv8, d8)
    return y


def _outproj0_body(x_ref, oa_ref, y_ref, wglu_ref, woa_ref, wob_ref, g_ref, b_ref, o_ref, slab_scr):
    nb, tl, D = x_ref.shape
    T = SSM_CHUNK
    ys = []
    for v in range(SSM_LANE_TILES):
        for c in range(tl // T):
            for s in range(T):
                slab_scr[pl.ds(c * T + s, nb, stride=tl), :] = y_ref[v, c * nb:(c + 1) * nb, s * 128:(s + 1) * 128]
        ys.append(slab_scr[...])
    h = _gelu(jnp.concatenate(ys, axis=-1))
    ob = h * _sigmoid(_dot(h.astype(BF16), wglu_ref[...]))
    m = _dot(oa_ref[...].reshape(nb * tl, SB_WIDTH), woa_ref[...]) + _dot(ob.astype(BF16), wob_ref[...])
    r = ALPHA * x_ref[...].reshape(nb * tl, D) + m
    o_ref[...] = _layer_norm(r, g_ref[...], b_ref[...]).reshape(nb, tl, D)


def _outproj0(x, oa, ys, wglu, woa, wob, g, b, tl=64):
    B, L, D = x.shape
    tl = min(tl, L)
    T = SSM_CHUNK
    row = lambda i: (0, i, 0)
    fix = lambda i: (0, 0)
    return pl.pallas_call(
        _outproj0_body,
        grid=(L // tl,),
        in_specs=[pl.BlockSpec((B, tl, D), row), pl.BlockSpec((B, tl, SB_WIDTH), row),
                  pl.BlockSpec((SSM_LANE_TILES, tl // T * B, T * 128), row),
                  pl.BlockSpec((SSM_WIDTH, SSM_WIDTH), fix), pl.BlockSpec((SB_WIDTH, D), fix),
                  pl.BlockSpec((SSM_WIDTH, D), fix), pl.BlockSpec((1, D), fix), pl.BlockSpec((1, D), fix)],
        out_specs=pl.BlockSpec((B, tl, D), row),
        out_shape=jax.ShapeDtypeStruct((B, L, D), F32),
        scratch_shapes=[pltpu.VMEM((B * tl, 128), F32)],
        compiler_params=_params("parallel"),
        name="outproj0",
    )(x, oa, ys, wglu, woa, wob, g, b)


def _outproj1_body(x_ref, o_ref_in, w_ref, g_ref, b_ref, o_ref):
    m = _dot(o_ref_in[...], w_ref[...])
    o_ref[...] = _layer_norm(ALPHA * x_ref[...] + m, g_ref[...], b_ref[...])


def _outproj1(x, o, w, g, b, tm=512):
    N, D = x.shape
    tm = min(tm, N)
    row = lambda i: (i, 0)
    fix = lambda i: (0, 0)
    return pl.pallas_call(
        _outproj1_body,
        grid=(N // tm,),
        in_specs=[pl.BlockSpec((tm, D), row), pl.BlockSpec((tm, o.shape[1]), row),
                  pl.BlockSpec(w.shape, fix), pl.BlockSpec((1, D), fix), pl.BlockSpec((1, D), fix)],
        out_specs=pl.BlockSpec((tm, D), row),
        out_shape=jax.ShapeDtypeStruct((N, D), F32),
        compiler_params=_params("parallel"),
        name="outproj1",
    )(x, o, w, g, b)


def _router_body(x_ref, wh_ref, wl_ref, b_ref, g_ref, grp_ref):
    xh, xl = _split_bf16(x_ref[...])
    wh = wh_ref[...]
    logits = _dot_nt(wh, xh) + _dot_nt(wh, xl) + _dot_nt(wl_ref[...], xh)
    aff = _sigmoid(logits)
    sel = aff + b_ref[...]
    E, K = EXPERTS_PER_GROUP, N_GROUPS
    s = [sel[e:e + 1, :] for e in range(N_EXPERTS)]
    a = [aff[e:e + 1, :] for e in range(N_EXPERTS)]
    gscore = []
    for k in range(K):
        v = s[k * E:(k + 1) * E]
        best = None
        for x in range(E):
            for y in range(x + 1, E):
                pair = v[x] + v[y]
                best = pair if best is None else jnp.maximum(best, pair)
        gscore.append(best)
    top = functools.reduce(jnp.maximum, gscore)
    is_g = []
    taken = None
    for k in range(K):
        hit = gscore[k] == top
        if taken is None:
            is_g.append(hit)
            taken = hit
        else:
            is_g.append(jnp.logical_and(hit, jnp.logical_not(taken)))
            taken = jnp.logical_or(taken, hit)

    def pick(rows, j):
        out = rows[(K - 1) * E + j]
        for k in range(K - 2, -1, -1):
            out = jnp.where(is_g[k], rows[k * E + j], out)
        return out

    v = [pick(s, j) for j in range(E)]
    av = [pick(a, j) for j in range(E)]
    chosen = []
    for j in range(E):
        r = jnp.zeros_like(v[j])
        for j2 in range(E):
            if j2 == j:
                continue
            ahead = (v[j2] >= v[j]) if j2 < j else (v[j2] > v[j])
            r = r + jnp.where(ahead, 1.0, 0.0)
        chosen.append(r < 2.0)
    wj = [jnp.where(chosen[j], av[j], 0.0) for j in range(E)]
    tot = wj[0] + wj[1] + wj[2] + wj[3]
    gj = [w / tot for w in wj]
    rows = [jnp.where(is_g[e // E], gj[e % E], 0.0) for e in range(N_EXPERTS)]
    g_ref[...] = jnp.concatenate(rows, axis=0)
    grp_ref[...] = jnp.concatenate([jnp.where(is_g[k], 1.0, 0.0) for k in range(K)], axis=0)


def _router(x, w_router, b_router, tm=512):
    N, D = x.shape
    tm = min(tm, N)
    wt = w_router.astype(F32).T
    wh, wl = _split_bf16(wt)
    fix = lambda i: (0, 0)
    return pl.pallas_call(
        _router_body,
        grid=(N // tm,),
        in_specs=[pl.BlockSpec((tm, D), lambda i: (i, 0)), pl.BlockSpec((N_EXPERTS, D), fix),
                  pl.BlockSpec((N_EXPERTS, D), fix), pl.BlockSpec((N_EXPERTS, 1), fix)],
        out_specs=[pl.BlockSpec((N_EXPERTS, tm), lambda i: (0, i)), pl.BlockSpec((N_GROUPS, tm), lambda i: (0, i))],
        out_shape=[jax.ShapeDtypeStruct((N_EXPERTS, N), F32), jax.ShapeDtypeStruct((N_GROUPS, N), F32)],
        compiler_params=_params("parallel"),
        name="router",
    )(x, wh, wl, b_router.astype(F32).reshape(N_EXPERTS, 1))


MOE_TILE = 1024
MOE_PASS_ROWS = 304
MOE_CHUNK = 256
MOE_EXPERTS_PER_STEP = 2
ROW_ALIGN = 16


def _moe_body(seg_ref, x_ref, gt_ref, grp_ref, w1_ref, w3_ref, w2_ref, g_ref, b_ref, o_ref,
              xs_scr, ys_scr, pt_scr, gs_scr):
    i = pl.program_id(0)
    e = pl.program_id(1)
    T = x_ref.shape[0]
    K = N_GROUPS
    cap = MOE_PASS_ROWS
    ch = min(MOE_CHUNK, T)

    @pl.when(e == 0)
    def _():
        grp = grp_ref[...]
        xb = x_ref[...].astype(BF16)
        gates = gt_ref[...]
        gh, gm = _split_bf16(gates)
        gl = (gates - gh.astype(F32) - gm.astype(F32)).astype(BF16)
        cnt = jnp.sum(grp, axis=1, keepdims=True)
        off = [jnp.zeros((1, 1), F32)]
        for k in range(K - 1):
            off.append(off[-1] + cnt[k:k + 1])
        r_i = lax.broadcasted_iota(jnp.int32, (ch, T), 0)
        c_i = lax.broadcasted_iota(jnp.int32, (ch, T), 1)
        lane = lax.broadcasted_iota(jnp.int32, (2 * K, T), 1)
        run = jnp.concatenate([grp, jnp.zeros_like(grp)], axis=0)
        sh = 1
        while sh < T:
            run = run + jnp.where(lane >= sh, pltpu.roll(run, sh, 1), 0.0)
            sh *= 2
        pos_row = jnp.zeros((1, T), F32)
        for k in range(K):
            pos_row = pos_row + grp[k:k + 1, :] * (off[k] + run[k:k + 1, :] - 1.0)
        pos_t = jnp.broadcast_to(pos_row, (2 * K, T))
        pos_col = [pos_t[:, c * ch:(c + 1) * ch].T[:, 0:1] for c in range(T // ch)]
        for c in range(T // ch):
            c0 = c * ch
            perm = jnp.where((r_i + c0).astype(F32) == pos_row, 1.0, 0.0).astype(BF16)
            xs_scr[c0:c0 + ch, :] = _dot(perm, xb).astype(BF16)
            gs_scr[c0:c0 + ch, :] = _dot_nt(perm, gh) + _dot_nt(perm, gm) + _dot_nt(perm, gl)
            pt_scr[c0:c0 + ch, :] = jnp.where(c_i.astype(F32) == pos_col[c], 1.0, 0.0).astype(BF16)
        xs_scr[T:, :] = jnp.zeros((cap, xs_scr.shape[1]), BF16)
        gs_scr[T:, :] = jnp.zeros((cap, gs_scr.shape[1]), F32)
        ys_scr[...] = jnp.zeros_like(ys_scr)

    ne = w1_ref.shape[0]
    k = (e * ne) // EXPERTS_PER_GROUP
    off = seg_ref[i, k]
    cnt = seg_ref[i, K + k]
    start0 = (off // ROW_ALIGN) * ROW_ALIGN
    n_pass = (off + cnt - start0 + cap - 1) // cap

    def one_pass(n, c):
        start = pl.multiple_of(start0 + n * cap, ROW_ALIGN)
        xc = xs_scr[pl.ds(start, cap), :]
        gsc = gs_scr[pl.ds(start, cap), :]
        lane = lax.broadcasted_iota(jnp.int32, gsc.shape, 1)
        y = None
        for x in range(ne):
            a = _dot(xc, w1_ref[x])
            hm = (a * _sigmoid(a)) * _dot(xc, w3_ref[x])
            gcol = jnp.sum(jnp.where(lane == e * ne + x, gsc, 0.0), axis=1, keepdims=True)
            yx = gcol * _dot(hm.astype(BF16), w2_ref[x])
            y = yx if y is None else y + yx
        ys_scr[pl.ds(start, cap), :] += y
        return c

    lax.fori_loop(0, n_pass, one_pass, 0)

    @pl.when(e == pl.num_programs(1) - 1)
    def _():
        y = _dot(pt_scr[...], ys_scr[0:T, :].astype(BF16))
        o_ref[...] = _layer_norm(ALPHA * x_ref[...] + y, g_ref[...], b_ref[...])


def _moe(x, gates_t, grp_t, w1, w3, w2, g, b):
    N, D = x.shape
    T = min(MOE_TILE, N)
    K = N_GROUPS
    cnt = grp_t.reshape(K, N // T, T).sum(-1).astype(jnp.int32).T
    seg = jnp.concatenate([jnp.cumsum(cnt, axis=1) - cnt, cnt], axis=1)
    fix = lambda i, e, s: (0, 0)
    ne = MOE_EXPERTS_PER_STEP
    rows = T + MOE_PASS_ROWS
    grid_spec = pltpu.PrefetchScalarGridSpec(
        num_scalar_prefetch=1,
        grid=(N // T, N_EXPERTS // ne),
        in_specs=[pl.BlockSpec((T, D), lambda i, e, s: (i, 0)),
                  pl.BlockSpec((N_EXPERTS, T), lambda i, e, s: (0, i)),
                  pl.BlockSpec((K, T), lambda i, e, s: (0, i)),
                  pl.BlockSpec((ne, D, D_EXPERT), lambda i, e, s: (e, 0, 0)),
                  pl.BlockSpec((ne, D, D_EXPERT), lambda i, e, s: (e, 0, 0)),
                  pl.BlockSpec((ne, D_EXPERT, D), lambda i, e, s: (e, 0, 0)),
                  pl.BlockSpec((1, D), fix), pl.BlockSpec((1, D), fix)],
        out_specs=pl.BlockSpec((T, D), lambda i, e, s: (i, 0)),
        scratch_shapes=[pltpu.VMEM((rows, D), BF16), pltpu.VMEM((rows, D), F32),
                        pltpu.VMEM((T, T), BF16), pltpu.VMEM((rows, N_EXPERTS), F32)])
    return pl.pallas_call(
        _moe_body,
        grid_spec=grid_spec,
        out_shape=jax.ShapeDtypeStruct((N, D), F32),
        compiler_params=_params("parallel", "arbitrary"),
        name="moe",
    )(seg, x, gates_t, grp_t, w1, w3, w2, g, b)


NSA_KVW = NSA_KV_HEADS * HEAD_DIM
NSA_ROPE_W = NSA_HEADS * HEAD_DIM + 2 * NSA_KVW


def _rot_cols(w):
    K, n = w.shape
    w3 = w.reshape(K, n // HEAD_DIM, 2, HEAD_DIM // 2)
    return jnp.stack([-w3[:, :, 1], w3[:, :, 0]], axis=2).reshape(K, n)


def _inproj1_weights(w_in):
    H, G, d = NSA_HEADS, NSA_KV_HEADS, HEAD_DIM
    cuts = [H * d + i * NSA_KVW for i in range(7)]
    q, kc, vc, ks, vs, kw, vw, gate = jnp.split(w_in.astype(F32), cuts, axis=1)
    w_all = jnp.concatenate([q, ks, kw, kc, vc], axis=1).astype(BF16)
    w_t = jnp.concatenate([vs, vw, gate], axis=1).T.astype(BF16)
    return w_all, w_t


def _rope_tables(pos):
    inv = ROPE_THETA ** (-jnp.arange(0, HEAD_DIM, 2, dtype=F32) / HEAD_DIM)
    ang = pos.astype(F32)[:, None] * inv[None, :]
    c, s = jnp.cos(ang), jnp.sin(ang)
    return jnp.concatenate([c, c], axis=-1), jnp.concatenate([s, s], axis=-1)


def _rope_roll_tables(pos):
    inv = ROPE_THETA ** (-jnp.arange(0, HEAD_DIM, 2, dtype=F32) / HEAD_DIM)
    ang = pos.astype(F32)[:, None] * inv[None, :]
    c, s, z = jnp.cos(ang), jnp.sin(ang), jnp.zeros_like(ang)
    two = lambda a, b: jnp.concatenate([a, b, a, b], axis=-1)
    return two(c, c), two(-s, z), two(z, s)


V_ROWS = HEAD_DIM + 16


def _inproj1_body(x_ref, w_ref, wt_ref, cos_ref, sinlo_ref, sinhi_ref, q_ref, ks_ref, kw_ref, kc_ref, vc_ref,
                  vst_ref, vwt_ref, gate_ref):
    xb = x_ref[0].astype(BF16)
    tl = xb.shape[0]
    d = HEAD_DIM
    p = _dot(xb, w_ref[...])
    pt = _dot_nt(wt_ref[...], xb)
    ones = jnp.ones((V_ROWS - d, tl), BF16)
    for n, ref in enumerate((vst_ref, vwt_ref)):
        for g in range(NSA_KV_HEADS):
            r0 = n * NSA_KVW + g * d
            ref[0, g, 0:d, :] = pt[r0:r0 + d, :].astype(BF16)
            ref[0, g, d:V_ROWS, :] = ones
    gate_ref[0] = pt[2 * NSA_KVW:, :]
    cos = cos_ref[...]
    sin_lo = sinlo_ref[...]
    sin_hi = sinhi_ref[...]
    roped = []
    for c in range(NSA_ROPE_W // 128):
        xc = p[:, c * 128:(c + 1) * 128]
        roped.append(xc * cos + pltpu.roll(xc, 96, 1) * sin_lo + pltpu.roll(xc, 32, 1) * sin_hi)

    def head(chunks, h):
        blk = chunks[h // 2]
        return blk[:, (h % 2) * d:(h % 2 + 1) * d]

    for h in range(NSA_HEADS):
        q_ref[0, h] = (head(roped, h) * LOG2_Q_SCALE).astype(BF16)
    nq = NSA_HEADS // 2
    pos = pl.program_id(1) * tl + lax.broadcasted_iota(jnp.int32, (tl, 128 - d), 0)
    block_onehot = jnp.where(pos // SEL_LEN == lax.broadcasted_iota(jnp.int32, (tl, 128 - d), 1), 1.0, 0.0)
    for g in range(NSA_KV_HEADS):
        ks_ref[0, g] = jnp.concatenate([head(roped[nq:], g), block_onehot], axis=1).astype(BF16)
        kw_ref[0, g] = head(roped[nq + 2:], g).astype(BF16)
    for n, ref in enumerate((kc_ref, vc_ref)):
        for g in range(NSA_KV_HEADS):
            c0 = NSA_ROPE_W + n * NSA_KVW + g * d
            ref[0, g] = p[:, c0:c0 + d]


def _inproj1(x, w_all, w_t, tl=256):
    B, L, D = x.shape
    tl = min(tl, L)
    G = NSA_KV_HEADS
    n_gate = w_t.shape[0] - 2 * NSA_KVW
    tables = _rope_roll_tables(jnp.arange(L))
    qh = jax.ShapeDtypeStruct((B, NSA_HEADS, L, HEAD_DIM), BF16)
    kvh = jax.ShapeDtypeStruct((B, G, L, HEAD_DIM), BF16)
    vth = jax.ShapeDtypeStruct((B, G, V_ROWS, L), BF16)
    q_spec = pl.BlockSpec((1, NSA_HEADS, tl, HEAD_DIM), lambda b, i: (b, 0, i, 0))
    kv_spec = pl.BlockSpec((1, G, tl, HEAD_DIM), lambda b, i: (b, 0, i, 0))
    ksel_spec = pl.BlockSpec((1, G, tl, 128), lambda b, i: (b, 0, i, 0))
    vt_spec = pl.BlockSpec((1, G, V_ROWS, tl), lambda b, i: (b, 0, 0, i))
    tab_spec = pl.BlockSpec((tl, 128), lambda b, i: (i, 0))
    return pl.pallas_call(
        _inproj1_body,
        grid=(B, L // tl),
        in_specs=[pl.BlockSpec((1, tl, D), lambda b, i: (b, i, 0)),
                  pl.BlockSpec(w_all.shape, lambda b, i: (0, 0)),
                  pl.BlockSpec(w_t.shape, lambda b, i: (0, 0)),
                  tab_spec, tab_spec, tab_spec],
        out_specs=([q_spec, ksel_spec] + [kv_spec] * 3 + [vt_spec] * 2
                   + [pl.BlockSpec((1, n_gate, tl), lambda b, i: (b, 0, i))]),
        out_shape=([qh, jax.ShapeDtypeStruct((B, G, L, 128), BF16), kvh]
                   + [jax.ShapeDtypeStruct((B, G, L, HEAD_DIM), F32)] * 2 + [vth] * 2
                   + [jax.ShapeDtypeStruct((B, n_gate, L), F32)]),
        compiler_params=_params("parallel", "parallel"),
        name="inproj1",
    )(x, w_all, w_t, *tables)


def _compress_body(kc_ref, vc_ref, posk_ref, posv_ref, w1k_ref, w1v_ref, w2k_ref, w2kr_ref, w2v_ref,
                   cos_ref, sin_ref, kcmp_ref, vcmp_ref):
    def hidden(a_ref, pos_ref, w1_ref):
        nrow = a_ref.shape[2] // CMP_STRIDE
        a = jnp.concatenate([a_ref[0, 0, pl.ds(l, nrow, stride=CMP_STRIDE), :] for l in range(CMP_STRIDE)],
                            axis=1).astype(F32)
        lo = _dot((a + pos_ref[0:1, :]).astype(BF16), w1_ref[0])
        hi = _dot((a + pos_ref[1:2, :]).astype(BF16), w1_ref[1])
        hi_next = pltpu.roll(hi, nrow - 1, 0)
        return _gelu(lo + hi_next).astype(BF16)

    hk = hidden(kc_ref, posk_ref, w1k_ref)
    kcmp = _dot(hk, w2k_ref[...]) * cos_ref[...] + _dot(hk, w2kr_ref[...]) * sin_ref[...]
    kcmp_ref[0, 0] = kcmp.astype(BF16)
    hv = hidden(vc_ref, posv_ref, w1v_ref)
    vcmp_ref[0, 0] = _dot_nt(w2v_ref[...], hv).astype(BF16)


def _compress(kc, vc, pos_k, w1_k, w2_k, pos_v, w1_v, w2_v):
    B, G, L, d = kc.shape
    half = CMP_STRIDE * d
    nb = L // CMP_STRIDE
    posk = pos_k.astype(F32).reshape(2, half)
    posv = pos_v.astype(F32).reshape(2, half)
    w1k = w1_k.astype(BF16).reshape(2, half, CMP_HIDDEN)
    w1v = w1_v.astype(BF16).reshape(2, half, CMP_HIDDEN)
    w2k = w2_k.astype(F32)
    cos, sin = _rope_tables(jnp.arange(nb) * CMP_STRIDE + CMP_LEN - 1)
    blk = pl.BlockSpec((1, 1, L, d), lambda b, g: (b, g, 0, 0))
    out = pl.BlockSpec((1, 1, nb, d), lambda b, g: (b, g, 0, 0))
    out_t = pl.BlockSpec((1, 1, d, nb), lambda b, g: (b, g, 0, 0))
    fix2 = lambda b, g: (0, 0)
    fix3 = lambda b, g: (0, 0, 0)
    return pl.pallas_call(
        _compress_body,
        grid=(B, G),
        in_specs=[blk, blk, pl.BlockSpec((2, half), fix2), pl.BlockSpec((2, half), fix2),
                  pl.BlockSpec((2, half, CMP_HIDDEN), fix3), pl.BlockSpec((2, half, CMP_HIDDEN), fix3),
                  pl.BlockSpec((CMP_HIDDEN, d), fix2), pl.BlockSpec((CMP_HIDDEN, d), fix2),
                  pl.BlockSpec((d, CMP_HIDDEN), fix2), pl.BlockSpec((nb, d), fix2), pl.BlockSpec((nb, d), fix2)],
        out_specs=[out, out_t],
        out_shape=[jax.ShapeDtypeStruct((B, G, nb, d), BF16), jax.ShapeDtypeStruct((B, G, d, nb), BF16)],
        compiler_params=_params("parallel", "parallel"),
        name="compress",
    )(kc, vc, posk, posv, w1k, w1v, w2k.astype(BF16), _rot_cols(w2k).astype(BF16), w2_v.T.astype(BF16), cos, sin)


NSA_KBLOCK = 256
NSA_GROUPS_PER_STEP = 4
NSA_HEADS_PER_CHAIN = 4
NSA_SCORE_LOOKAHEAD = 3


def _nsa_body(q_ref, kcmp_ref, vcmpt_ref, ks_ref, vst_ref, kw_ref, vwt_ref, gate_ref, o_ref, qsel_scr, *, seq):
    i = pl.program_id(2)
    tq = q_ref.shape[2]
    gb = kcmp_ref.shape[1]
    tk = min(NSA_KBLOCK, seq)
    R, d = NSA_REP, HEAD_DIM
    nq = R * tq
    nb = seq // SEL_LEN
    mc = seq // CMP_STRIDE
    t_row = i * tq + lax.broadcasted_iota(jnp.int32, (1, tq), 1)
    t_all = jnp.concatenate([t_row] * R, axis=1)
    m_col = lax.broadcasted_iota(jnp.int32, (mc, 1), 0)
    m_row = lax.broadcasted_iota(jnp.int32, (1, mc), 1)
    n_col = lax.broadcasted_iota(jnp.int32, (nb, 1), 0)

    def q_rows(g):
        return q_ref[0, g * R:(g + 1) * R].reshape(nq, d)

    valid_c = (m_col * CMP_STRIDE + (CMP_LEN - 1)) <= t_all
    ovl = jnp.logical_and(m_row * CMP_STRIDE < (n_col + 1) * SEL_LEN,
                          m_row * CMP_STRIDE + CMP_LEN > n_col * SEL_LEN)
    ovl = jnp.where(ovl, 1.0, 0.0).astype(BF16)
    cur = t_row // SEL_LEN
    forced = jnp.logical_or(n_col == 0, jnp.logical_or(n_col == cur, n_col == cur - 1))
    bonus = jnp.where(forced, FORCE_BONUS, 0.0)
    valid_s = n_col * SEL_LEN <= t_row
    o_c = []
    ties = [jnp.where(n_col > n2, 1.0, 0.0) for n2 in range(nb)]
    cmp_scores = [_dot_nt(kcmp_ref[0, g], q_rows(g)) for g in range(gb)]
    for g in range(gb):
        s = jnp.where(valid_c, cmp_scores[g], NEG)
        e = jnp.exp2(s - jnp.maximum(jnp.max(s, axis=0, keepdims=True), 0.5 * NEG))
        den = jnp.sum(e, axis=0, keepdims=True)
        p = e / jnp.where(den > 0.0, den, 1.0)
        o_c.append(_dot(vcmpt_ref[0, g], p.astype(BF16)))
        psum = p[:, 0:tq]
        for r in range(1, R):
            psum = psum + p[:, r * tq:(r + 1) * tq]
        ph, plo = _split_bf16(psum)
        score = jnp.where(valid_s, _dot(ovl, ph) + _dot(ovl, plo) + bonus, NEG)
        rank = jnp.zeros((nb, tq), F32)
        for n2 in range(nb):
            other = score[n2:n2 + 1, :]
            rank = rank + jnp.where(other > score, 1.0, jnp.where(other == score, ties[n2], 0.0))
        pen = jnp.where(rank < float(SEL_TOP), 0.0, NEG).T.astype(BF16)
        pen = jnp.concatenate([pen, jnp.zeros((tq, 128 - d - nb), BF16)], axis=1)
        qsel_scr[g] = jnp.concatenate([q_rows(g), jnp.concatenate([pen] * R, axis=0)], axis=1)

    k_col = lax.broadcasted_iota(jnp.int32, (tk, 1), 0)
    hp = NSA_HEADS_PER_CHAIN
    j_hi = (i * tq + tq - 1) // tk + 1

    def sweep(q_fn, k_ref, vt_ref, j_lo, bias_fn, bias_every_block):
        chains = [(g, h0) for g in range(gb) for h0 in range(0, R, hp)]

        def body(j, state, with_bias):
            k0 = pl.multiple_of(j * tk, tk)
            bias = jnp.concatenate([bias_fn(k0 + k_col)] * hp, axis=1) if with_bias else None

            def scores(c):
                g, h0 = chains[c]
                sc = _dot_nt(k_ref[0, g, pl.ds(k0, tk), :], q_fn(g, h0))
                return sc + bias if with_bias else sc

            new_state = []
            ahead = NSA_SCORE_LOOKAHEAD
            pending = [scores(c) for c in range(min(ahead, len(chains)))]
            for c, (m_run, acc) in enumerate(state):
                sc = pending.pop(0)
                if c + ahead < len(chains):
                    pending.append(scores(c + ahead))
                vtb = vt_ref[0, chains[c][0], :, pl.ds(k0, tk)]
                m_new = jnp.maximum(m_run, jnp.max(sc, axis=0, keepdims=True))
                pr = jnp.exp2(sc - m_new)
                new_state.append((m_new, jnp.exp2(m_run - m_new) * acc + _dot(vtb, pr.astype(BF16))))
            return tuple(new_state)

        state = tuple((jnp.full((1, hp * tq), NEG, F32), jnp.zeros((V_ROWS, hp * tq), F32)) for _ in chains)
        state = lax.fori_loop(j_lo, j_hi - 1, functools.partial(body, with_bias=bias_every_block), state)
        state = body(j_hi - 1, state, True)
        outs = []
        for g in range(gb):
            acc = jnp.concatenate([a for (cg, _), (_, a) in zip(chains, state) if cg == g], axis=1)
            outs.append(acc[0:d] / acc[d:d + 1])
        return outs

    def causal_bias(kpos):
        return jnp.where(kpos <= t_row, 0.0, NEG)

    o_s = sweep(lambda g, h0: qsel_scr[g, h0 * tq:(h0 + hp) * tq, :], ks_ref, vst_ref, 0, causal_bias, False)

    def win_bias(kpos):
        return jnp.where(jnp.logical_and(kpos <= t_row, kpos > t_row - WINDOW), 0.0, NEG)

    o_w = sweep(lambda g, h0: q_ref[0, g * R + h0:g * R + h0 + hp].reshape(hp * tq, d), kw_ref, vwt_ref,
                jnp.maximum(i * tq - (WINDOW - 1), 0) // tk, win_bias, True)

    outs = []
    for g in range(gb):
        row0 = (pl.program_id(1) * gb + g) * 3 * R

        def gate(branch):
            rows = [gate_ref[0, pl.ds(row0 + 3 * r + branch, 1), :] for r in range(R)]
            return _sigmoid(jnp.concatenate(rows, axis=1))

        o_t = gate(0) * o_c[g] + gate(1) * o_s[g] + gate(2) * o_w[g]
        for r in range(R):
            outs.append(o_t[:, r * tq:(r + 1) * tq].T)
    o_ref[0] = jnp.concatenate(outs, axis=-1).astype(o_ref.dtype)


def _nsa_attention(q, kcmp, vcmpt, ks, vst, kw, vwt, gate):
    B, H, L, d = q.shape
    G, R = NSA_KV_HEADS, NSA_REP
    gb = NSA_GROUPS_PER_STEP
    tq = min(ATT_BLOCK, L)
    mc = L // CMP_STRIDE
    k_spec = pl.BlockSpec((1, gb, L, d), lambda b, g, i: (b, g, 0, 0))
    ksel_spec = pl.BlockSpec((1, gb, L, 128), lambda b, g, i: (b, g, 0, 0))
    vt_spec = pl.BlockSpec((1, gb, V_ROWS, L), lambda b, g, i: (b, g, 0, 0))
    return pl.pallas_call(
        functools.partial(_nsa_body, seq=L),
        grid=(B, G // gb, L // tq),
        in_specs=[pl.BlockSpec((1, gb * R, tq, d), lambda b, g, i: (b, g, i, 0)),
                  pl.BlockSpec((1, gb, mc, d), lambda b, g, i: (b, g, 0, 0)),
                  pl.BlockSpec((1, gb, d, mc), lambda b, g, i: (b, g, 0, 0)),
                  ksel_spec, vt_spec, k_spec, vt_spec,
                  pl.BlockSpec((1, gate.shape[1], tq), lambda b, g, i: (b, 0, i))],
        out_specs=pl.BlockSpec((1, tq, gb * R * d), lambda b, g, i: (b, i, g)),
        out_shape=jax.ShapeDtypeStruct((B, L, H * d), BF16),
        scratch_shapes=[pltpu.VMEM((gb, R * tq, 128), BF16)],
        compiler_params=_params("parallel", "parallel", "arbitrary"),
        name="nsa_attention",
    )(q, kcmp, vcmpt, ks, vst, kw, vwt, gate)


def kernel(x, w_in_0, ssm_lam_re, ssm_lam_im, ssm_log_dt, ssm_b_re, ssm_b_im, ssm_c_re, ssm_c_im, ssm_d, w_glu, w_out_0, ln_mix_g_0, ln_mix_b_0, ln_ffn_g_0, ln_ffn_b_0, w1_0, w3_0, w2_0, w_in_1, cmp_pos_k, cmp_w1_k, cmp_w2_k, cmp_pos_v, cmp_w1_v, cmp_w2_v, w_out_1, ln_mix_g_1, ln_mix_b_1, ln_ffn_g_1, ln_ffn_b_1, w1_1, w3_1, w2_1, w_router, b_router):
    B, L, D = x.shape
    N = B * L
    vec = lambda a: a.astype(F32).reshape(1, D)

    def ffn(h, w1, w3, w2, g, b):
        gates_t, grp_t = _router(h, w_router, b_router)
        return _moe(h, gates_t, grp_t, w1.astype(BF16), w3.astype(BF16), w2.astype(BF16), vec(g), vec(b))

    q, k, v, u = _inproj0(x, w_in_0.astype(BF16))
    o_a = _sb_attention(q, k, v)
    y = _ssm(u, _ssm_tables(ssm_lam_re, ssm_lam_im, ssm_log_dt, ssm_b_re, ssm_b_im, ssm_c_re, ssm_c_im, ssm_d), B)
    w_out_0b = w_out_0.astype(BF16)
    h = _outproj0(x, o_a, y, w_glu.astype(BF16), w_out_0b[:SB_WIDTH], w_out_0b[SB_WIDTH:],
                  vec(ln_mix_g_0), vec(ln_mix_b_0)).reshape(N, D)
    h = ffn(h, w1_0, w3_0, w2_0, ln_ffn_g_0, ln_ffn_b_0)

    q, ks, kw, kc, vc, vst, vwt, gate = _inproj1(h.reshape(B, L, D), *_inproj1_weights(w_in_1))
    kcmp, vcmpt = _compress(kc, vc, cmp_pos_k, cmp_w1_k, cmp_w2_k, cmp_pos_v, cmp_w1_v, cmp_w2_v)
    o = _nsa_attention(q, kcmp, vcmpt, ks, vst, kw, vwt, gate).reshape(N, NSA_HEADS * HEAD_DIM)
    h = _outproj1(h, o, w_out_1.astype(BF16), vec(ln_mix_g_1), vec(ln_mix_b_1))
    h = ffn(h, w1_1, w3_1, w2_1, ln_ffn_g_1, ln_ffn_b_1)
    return h.reshape(B, L, D)
```

```python
import functools
import math

import numpy as np
import jax
import jax.numpy as jnp
from jax import lax
from jax.experimental import pallas as pl
from jax.experimental.pallas import tpu as pltpu

F32 = jnp.float32
BF16 = jnp.bfloat16

D_MODEL = 1024
DEPTH = 2
SB_HEADS = 8
HEAD_DIM = 64
SB_WIDTH = SB_HEADS * HEAD_DIM
SSM_WIDTH = D_MODEL - SB_WIDTH
SSM_GROUP = 16
SSM_GROUPS = SSM_WIDTH // SSM_GROUP
SSM_STATE = 64
SSM_CHUNK = 16
SSM_LANE_TILES = SSM_WIDTH // 128
SSM_TILE_GROUPS = 128 // SSM_GROUP
NSA_HEADS = 16
NSA_KV_HEADS = 4
NSA_REP = NSA_HEADS // NSA_KV_HEADS
CMP_LEN = 32
CMP_STRIDE = 16
CMP_HIDDEN = 256
SEL_LEN = 64
SEL_TOP = 8
WINDOW = 512
ROPE_THETA = 10000.0
FORCE_BONUS = 1e4
NEG = -1e30
N_EXPERTS = 16
N_GROUPS = 4
EXPERTS_PER_GROUP = N_EXPERTS // N_GROUPS
D_EXPERT = 512
ALPHA = (2 * DEPTH) ** 0.25
LN_EPS = 1e-5
ATTN_SCALE = HEAD_DIM ** -0.5
LOG2_Q_SCALE = ATTN_SCALE * math.log2(math.e)
ATT_BLOCK = 128
GELU_C = math.sqrt(2.0 / math.pi)


def _params(*sem):
    return pltpu.CompilerParams(dimension_semantics=sem, vmem_limit_bytes=56 * 1024 * 1024)


def _sigmoid(x):
    return 1.0 / (1.0 + jnp.exp(-x))


def _gelu(x):
    return 0.5 * x * (1.0 + jnp.tanh(GELU_C * (x + 0.044715 * (x * x * x))))


def _layer_norm(r, g, b):
    mu = jnp.mean(r, axis=-1, keepdims=True)
    d = r - mu
    var = jnp.mean(d * d, axis=-1, keepdims=True)
    return d * lax.rsqrt(var + LN_EPS) * g + b


def _dot(a, b):
    return jnp.dot(a, b, preferred_element_type=F32)


def _dot_nt(a, b):
    return lax.dot_general(a, b, (((1,), (1,)), ((), ())), preferred_element_type=F32)


def _split_bf16(x):
    hi = x.astype(BF16)
    lo = (x - hi.astype(F32)).astype(BF16)
    return hi, lo


def _inproj0_body(x_ref, w_ref, q_ref, k_ref, v_ref, u_ref, slab_scr):
    nb, tl, D = x_ref.shape
    p = _dot(x_ref[...].reshape(nb * tl, D).astype(BF16), w_ref[...])
    for h in range(SB_HEADS):
        c = h * HEAD_DIM
        q_ref[:, h] = (p[:, c:c + HEAD_DIM] * LOG2_Q_SCALE).astype(BF16).reshape(nb, tl, HEAD_DIM)
        k_ref[:, h] = p[:, SB_WIDTH + c:SB_WIDTH + c + HEAD_DIM].astype(BF16).reshape(nb, tl, HEAD_DIM)
        v_ref[:, h] = p[:, 2 * SB_WIDTH + c:2 * SB_WIDTH + c + HEAD_DIM].astype(BF16).reshape(nb, tl, HEAD_DIM)
    T = SSM_CHUNK
    for v in range(SSM_LANE_TILES):
        c0 = 3 * SB_WIDTH + v * 128
        slab_scr[...] = p[:, c0:c0 + 128]
        for c in range(tl // T):
            for s in range(T):
                u_ref[v, c * nb:(c + 1) * nb, s * 128:(s + 1) * 128] = slab_scr[pl.ds(c * T + s, nb, stride=tl), :]


def _inproj0(x, w_bf16, tl=64):
    B, L, D = x.shape
    tl = min(tl, L)
    nout = w_bf16.shape[1]
    T = SSM_CHUNK
    head = jax.ShapeDtypeStruct((B, SB_HEADS, L, HEAD_DIM), BF16)
    head_spec = pl.BlockSpec((B, SB_HEADS, tl, HEAD_DIM), lambda i: (0, 0, i, 0))
    return pl.pallas_call(
        _inproj0_body,
        grid=(L // tl,),
        in_specs=[pl.BlockSpec((B, tl, D), lambda i: (0, i, 0)),
                  pl.BlockSpec((D, nout), lambda i: (0, 0))],
        out_specs=[head_spec, head_spec, head_spec,
                   pl.BlockSpec((SSM_LANE_TILES, tl // T * B, T * 128), lambda i: (0, i, 0))],
        out_shape=[head, head, head,
                   jax.ShapeDtypeStruct((SSM_LANE_TILES, L // T * B, T * 128), F32)],
        scratch_shapes=[pltpu.VMEM((B * tl, 128), F32)],
        compiler_params=_params("parallel"),
        name="inproj0",
    )(x, w_bf16)


SB_BLOCK = 256
SB_HEADS_PER_STEP = 4
SB_LOGIT_LOOKAHEAD = 3
EXP2_UNDERFLOW = -151.0


def _sb_attn_body(q_ref, k_ref, v_ref, o_ref, acc_scr, cs_scr):
    i = pl.program_id(2)
    t = q_ref.shape[2]
    nh = q_ref.shape[1]
    row = lax.broadcasted_iota(jnp.int32, (t, t), 0)
    col = lax.broadcasted_iota(jnp.int32, (t, t), 1)
    suffix = jnp.where(row > col, 1.0, 0.0).astype(BF16)
    below = col < row

    def logits(hh, k0):
        return _dot_nt(q_ref[0, hh], k_ref[0, hh, pl.ds(k0, t), :])

    def alive(css):
        return (jnp.max(functools.reduce(jnp.maximum, css)) > EXP2_UNDERFLOW).astype(jnp.int32)

    def all_heads(k0, diagonal):
        zs = [logits(hh, k0) for hh in range(nh)]
        sps, l1s = [], []
        for z in zs:
            sp = jnp.maximum(z, 0.0) + jnp.log2(1.0 + jnp.exp2(-jnp.abs(z)))
            sps.append(sp)
            l1s.append(jnp.where(below, -sp, 0.0) if diagonal else -sp)
        within_all = _dot(jnp.concatenate([l1.astype(BF16) for l1 in l1s], axis=0), suffix)
        css = []
        for hh in range(nh):
            vb = v_ref[0, hh, pl.ds(k0, t), :]
            base = zs[hh] - sps[hh] + within_all[hh * t:(hh + 1) * t]
            if diagonal:
                w = jnp.where(below, jnp.exp2(base), 0.0)
                acc_scr[hh] = _dot(w.astype(BF16), vb)
                cs = jnp.sum(l1s[hh], axis=1, keepdims=True)
            else:
                w = jnp.exp2(base + cs_scr[hh])
                acc_scr[hh] += _dot(w.astype(BF16), vb)
                cs = cs_scr[hh] + jnp.sum(l1s[hh], axis=1, keepdims=True)
            cs_scr[hh] = cs
            css.append(cs)
        return alive(css)

    first = all_heads(pl.multiple_of(i * t, t), True)

    def cond(c):
        return jnp.logical_and(c[0] <= i, c[1] > 0)

    def body(c):
        return c[0] + 1, all_heads(pl.multiple_of((i - c[0]) * t, t), False)

    lax.while_loop(cond, body, (jnp.int32(1), first))
    o_ref[0] = jnp.concatenate([acc_scr[hh] for hh in range(nh)], axis=-1).astype(o_ref.dtype)


def _sb_attention(q, k, v):
    B, H, L, d = q.shape
    t = min(SB_BLOCK, L)
    nh = SB_HEADS_PER_STEP
    return pl.pallas_call(
        _sb_attn_body,
        grid=(B, H // nh, L // t),
        in_specs=[pl.BlockSpec((1, nh, t, d), lambda b, h, i: (b, h, i, 0)),
                  pl.BlockSpec((1, nh, L, d), lambda b, h, i: (b, h, 0, 0)),
                  pl.BlockSpec((1, nh, L, d), lambda b, h, i: (b, h, 0, 0))],
        out_specs=pl.BlockSpec((1, t, nh * d), lambda b, h, i: (b, i, h)),
        out_shape=jax.ShapeDtypeStruct((B, L, H * d), BF16),
        scratch_shapes=[pltpu.VMEM((nh, t, d), F32), pltpu.VMEM((nh, t, 1), F32)],
        compiler_params=_params("parallel", "parallel", "arbitrary"),
        name="sb_attention",
    )(q, k, v)


def _ssm_tables(lam_re, lam_im, log_dt, b_re, b_im, c_re, c_im, d_skip):
    T, G, P, H = SSM_CHUNK, SSM_GROUPS, SSM_STATE, SSM_GROUP
    hp = lax.Precision.HIGHEST
    dt = jnp.exp(log_dt.astype(F32))[:, None]
    lr = lam_re.astype(F32)
    li = lam_im.astype(F32)
    mag = jnp.exp(lr * dt)
    a_re = mag * jnp.cos(li * dt)
    a_im = mag * jnp.sin(li * dt)
    den = lr * lr + li * li
    nr = a_re - 1.0
    f_re = (nr * lr + a_im * li) / den
    f_im = (a_im * lr - nr * li) / den
    br = b_re.astype(F32)
    bi = b_im.astype(F32)
    bbt_re = (f_re[..., None] * br - f_im[..., None] * bi).transpose(0, 2, 1)[:, None]
    bbt_im = (f_re[..., None] * bi + f_im[..., None] * br).transpose(0, 2, 1)[:, None]
    cr = c_re.astype(F32)[:, None]
    ci = c_im.astype(F32)[:, None]

    def powers(k):
        k = jnp.asarray(k, F32)[None, :, None]
        pmag = jnp.exp(k * (lr * dt)[:, None, :])
        ang = k * (li * dt)[:, None, :]
        return (pmag * jnp.cos(ang))[:, :, None, :], (pmag * jnp.sin(ang))[:, :, None, :]

    def cmul(xr, xi, yr, yi, rows):
        return ((xr * yr - xi * yi).reshape(G, rows, P), (xr * yi + xi * yr).reshape(G, rows, P))

    steps = np.arange(T)
    bs_re, bs_im = cmul(*powers(-steps), bbt_re, bbt_im, T * H)
    ct_re, ct_im = cmul(*powers(steps), cr, ci, T * H)
    full = (jnp.einsum('gap,gbp->gab', bs_re, ct_re, precision=hp)
            - jnp.einsum('gap,gbp->gab', bs_im, ct_im, precision=hp))
    row_s = lax.broadcasted_iota(jnp.int32, (T * H, T * H), 0) // H
    col_t = lax.broadcasted_iota(jnp.int32, (T * H, T * H), 1) // H
    intra = jnp.where(row_s <= col_t, full, 0.0)
    st_re, st_im = cmul(*powers(T - 1 - steps), bbt_re, bbt_im, T * H)
    to_st = jnp.concatenate([st_re, st_im, st_im, st_re], axis=-1)
    c1_re, c1_im = cmul(*powers(steps + 1), cr, ci, T * H)
    from_st_t = jnp.concatenate([c1_re, -c1_im], axis=-1)
    at_re, at_im = powers([T])
    at_re, at_im = at_re[:, 0, 0], at_im[:, 0, 0]
    zeros = jnp.zeros_like(at_re)
    adv = jnp.stack([jnp.concatenate([at_re, at_re], -1),
                     jnp.concatenate([-at_im, at_im], -1),
                     jnp.concatenate([at_im, -at_im], -1)]
                    + [jnp.concatenate([zeros, zeros], -1)] * 5, axis=1)
    dvec = jnp.tile(d_skip.astype(F32).reshape(G, 1, H), (1, 1, T))
    return intra, to_st, from_st_t, adv, dvec


def _ssm_expand(tables):
    intra, to_st, from_st_t, adv, dvec = tables
    V, W, T, H, P = SSM_LANE_TILES, SSM_TILE_GROUPS, SSM_CHUNK, SSM_GROUP, SSM_STATE
    n = T * W * H
    k_in = lax.broadcasted_iota(jnp.int32, (T * H, n), 0)
    col = lax.broadcasted_iota(jnp.int32, (T * H, n), 1)
    spread_th = (k_in == (col // 128) * H + col % H).astype(F32)
    spread_q = (k_in[:2 * P, :W * 2 * P] == col[:2 * P, :W * 2 * P] % (2 * P)).astype(F32)
    row_g = (lax.broadcasted_iota(jnp.int32, (n, 1), 0) // H) % W
    st_row_g = lax.broadcasted_iota(jnp.int32, (W * 2 * P, 1), 0) // (2 * P)
    col_g_th = (lax.broadcasted_iota(jnp.int32, (1, n), 1) // H) % W
    col_g_q = lax.broadcasted_iota(jnp.int32, (1, W * 2 * P), 1) // (2 * P)

    def blockdiag(rows, spread, row_group, col_group):
        full = jnp.einsum('vrk,kc->vrc', rows, spread)
        return jnp.where(row_group == col_group, full, 0.0).astype(BF16)

    by_row = lambda t: t.reshape(V, W, T, H, T * H).transpose(0, 2, 1, 3, 4).reshape(V, n, T * H)
    m8 = blockdiag(by_row(intra), spread_th, row_g, col_g_th)
    ws8 = blockdiag(by_row(to_st)[:, :, :2 * P], spread_q, row_g, col_g_q)
    wi_full = jnp.einsum('vwkq,kc->vwqc', from_st_t.reshape(V, W, T * H, 2 * P), spread_th).reshape(V, W * 2 * P, n)
    wi8 = jnp.where(st_row_g == col_g_th, wi_full, 0.0).astype(BF16)
    adv8 = adv.reshape(V, W, 8, 2 * P).transpose(0, 2, 1, 3).reshape(V, 8, W * 2 * P)
    d8 = jnp.broadcast_to(dvec.reshape(V, 1, W, T, H)[:, :, :, 0:1], (V, 1, W, T, H))
    d8 = d8.transpose(0, 1, 3, 2, 4).reshape(V, 1, T * W * H)
    return m8, ws8, wi8, adv8, d8


def _ssm_body(u_ref, m8_ref, ws8_ref, wi8_ref, adv_ref, dvec_ref, y_ref, s_scr, xin_scr, x1_scr, x2_scr, *, bsz):
    half = x1_scr.shape[1]

    @pl.when(pl.program_id(1) == 0)
    def _():
        x1_scr[...] = jnp.zeros_like(x1_scr)
        x2_scr[...] = jnp.zeros_like(x2_scr)

    u = u_ref[0]
    ub = u.astype(BF16)
    s1 = _dot(ub, ws8_ref[0])
    lane = lax.broadcasted_iota(jnp.int32, s1.shape, 1)
    s_scr[:, :half] = s1
    s_scr[:, half:] = jnp.where(lane % (2 * SSM_STATE) < SSM_STATE,
                                pltpu.roll(s1, half - SSM_STATE, 1), pltpu.roll(s1, SSM_STATE, 1))
    a1 = adv_ref[0, 0:1, :]
    a2 = adv_ref[0, 1:2, :]
    a3 = adv_ref[0, 2:3, :]

    def step(c, carry):
        x1, x2 = carry
        r0 = pl.multiple_of(c * bsz, bsz)
        xin_scr[pl.ds(r0, bsz), :] = x1
        s = s_scr[pl.ds(r0, bsz), :]
        return (a1 * x1 + a2 * x2 + s[:, :half], a1 * x2 + a3 * x1 + s[:, half:])

    x1, x2 = lax.fori_loop(0, u.shape[0] // bsz, step, (x1_scr[...], x2_scr[...]))
    x1_scr[...] = x1
    x2_scr[...] = x2
    y_ref[0] = _dot(ub, m8_ref[0]) + _dot(xin_scr[...].astype(BF16), wi8_ref[0]) + dvec_ref[0] * u


def _ssm(us, tables, bsz, rows=256):
    m8, ws8, wi8, adv8, d8 = _ssm_expand(tables)
    V, CB, width = us.shape
    B = bsz
    half = SSM_TILE_GROUPS * 2 * SSM_STATE
    rows = min(rows, CB)
    once = dict(pipeline_mode=pl.Buffered(1))
    y = pl.pallas_call(
        functools.partial(_ssm_body, bsz=B),
        grid=(V, CB // rows),
        in_specs=[pl.BlockSpec((1, rows, width), lambda v, r: (v, r, 0)),
                  pl.BlockSpec((1, width, width), lambda v, r: (v, 0, 0), **once),
                  pl.BlockSpec((1, width, half), lambda v, r: (v, 0, 0), **once),
                  pl.BlockSpec((1, half, width), lambda v, r: (v, 0, 0), **once),
                  pl.BlockSpec((1, 8, half), lambda v, r: (v, 0, 0)),
                  pl.BlockSpec((1, 1, width), lambda v, r: (v, 0, 0))],
        out_specs=pl.BlockSpec((1, rows, width), lambda v, r: (v, r, 0)),
        out_shape=jax.ShapeDtypeStruct((V, CB, width), F32),
        scratch_shapes=[pltpu.VMEM((rows, 2 * half), F32), pltpu.VMEM((rows, half), F32),
                        pltpu.VMEM((B, half), F32), pltpu.VMEM((B, half), F32)],
        compiler_params=_params("parallel", "arbitrary"),
        name="s5_scan",
    )(us, m8, ws8, wi8, adv8, d8)
    return y


def _outproj0_body(x_ref, oa_ref, y_ref, wglu_ref, woa_ref, wob_ref, g_ref, b_ref, o_ref, slab_scr):
    nb, tl, D = x_ref.shape
    T = SSM_CHUNK
    ys = []
    for v in range(SSM_LANE_TILES):
        for c in range(tl // T):
            for s in range(T):
                slab_scr[pl.ds(c * T + s, nb, stride=tl), :] = y_ref[v, c * nb:(c + 1) * nb, s * 128:(s + 1) * 128]
        ys.append(slab_scr[...])
    h = _gelu(jnp.concatenate(ys, axis=-1))
    ob = h * _sigmoid(_dot(h.astype(BF16), wglu_ref[...]))
    m = _dot(oa_ref[...].reshape(nb * tl, SB_WIDTH), woa_ref[...]) + _dot(ob.astype(BF16), wob_ref[...])
    r = ALPHA * x_ref[...].reshape(nb * tl, D) + m
    o_ref[...] = _layer_norm(r, g_ref[...], b_ref[...]).reshape(nb, tl, D)


def _outproj0(x, oa, ys, wglu, woa, wob, g, b, tl=64):
    B, L, D = x.shape
    tl = min(tl, L)
    T = SSM_CHUNK
    row = lambda i: (0, i, 0)
    fix = lambda i: (0, 0)
    return pl.pallas_call(
        _outproj0_body,
        grid=(L // tl,),
        in_specs=[pl.BlockSpec((B, tl, D), row), pl.BlockSpec((B, tl, SB_WIDTH), row),
                  pl.BlockSpec((SSM_LANE_TILES, tl // T * B, T * 128), row),
                  pl.BlockSpec((SSM_WIDTH, SSM_WIDTH), fix), pl.BlockSpec((SB_WIDTH, D), fix),
                  pl.BlockSpec((SSM_WIDTH, D), fix), pl.BlockSpec((1, D), fix), pl.BlockSpec((1, D), fix)],
        out_specs=pl.BlockSpec((B, tl, D), row),
        out_shape=jax.ShapeDtypeStruct((B, L, D), F32),
        scratch_shapes=[pltpu.VMEM((B * tl, 128), F32)],
        compiler_params=_params("parallel"),
        name="outproj0",
    )(x, oa, ys, wglu, woa, wob, g, b)


def _outproj1_body(x_ref, o_ref_in, w_ref, g_ref, b_ref, o_ref):
    m = _dot(o_ref_in[...], w_ref[...])
    o_ref[...] = _layer_norm(ALPHA * x_ref[...] + m, g_ref[...], b_ref[...])


def _outproj1(x, o, w, g, b, tm=512):
    N, D = x.shape
    tm = min(tm, N)
    row = lambda i: (i, 0)
    fix = lambda i: (0, 0)
    return pl.pallas_call(
        _outproj1_body,
        grid=(N // tm,),
        in_specs=[pl.BlockSpec((tm, D), row), pl.BlockSpec((tm, o.shape[1]), row),
                  pl.BlockSpec(w.shape, fix), pl.BlockSpec((1, D), fix), pl.BlockSpec((1, D), fix)],
        out_specs=pl.BlockSpec((tm, D), row),
        out_shape=jax.ShapeDtypeStruct((N, D), F32),
        compiler_params=_params("parallel"),
        name="outproj1",
    )(x, o, w, g, b)


def _router_body(x_ref, wh_ref, wl_ref, b_ref, g_ref, grp_ref):
    xh, xl = _split_bf16(x_ref[...])
    wh = wh_ref[...]
    logits = _dot_nt(wh, xh) + _dot_nt(wh, xl) + _dot_nt(wl_ref[...], xh)
    aff = _sigmoid(logits)
    sel = aff + b_ref[...]
    E, K = EXPERTS_PER_GROUP, N_GROUPS
    s = [sel[e:e + 1, :] for e in range(N_EXPERTS)]
    a = [aff[e:e + 1, :] for e in range(N_EXPERTS)]
    gscore = []
    for k in range(K):
        v = s[k * E:(k + 1) * E]
        best = None
        for x in range(E):
            for y in range(x + 1, E):
                pair = v[x] + v[y]
                best = pair if best is None else jnp.maximum(best, pair)
        gscore.append(best)
    top = functools.reduce(jnp.maximum, gscore)
    is_g = []
    taken = None
    for k in range(K):
        hit = gscore[k] == top
        if taken is None:
            is_g.append(hit)
            taken = hit
        else:
            is_g.append(jnp.logical_and(hit, jnp.logical_not(taken)))
            taken = jnp.logical_or(taken, hit)

    def pick(rows, j):
        out = rows[(K - 1) * E + j]
        for k in range(K - 2, -1, -1):
            out = jnp.where(is_g[k], rows[k * E + j], out)
        return out

    v = [pick(s, j) for j in range(E)]
    av = [pick(a, j) for j in range(E)]
    chosen = []
    for j in range(E):
        r = jnp.zeros_like(v[j])
        for j2 in range(E):
            if j2 == j:
                continue
            ahead = (v[j2] >= v[j]) if j2 < j else (v[j2] > v[j])
            r = r + jnp.where(ahead, 1.0, 0.0)
        chosen.append(r < 2.0)
    wj = [jnp.where(chosen[j], av[j], 0.0) for j in range(E)]
    tot = wj[0] + wj[1] + wj[2] + wj[3]
    gj = [w / tot for w in wj]
    rows = [jnp.where(is_g[e // E], gj[e % E], 0.0) for e in range(N_EXPERTS)]
    g_ref[...] = jnp.concatenate(rows, axis=0)
    grp_ref[...] = jnp.concatenate([jnp.where(is_g[k], 1.0, 0.0) for k in range(K)], axis=0)


def _router(x, w_router, b_router, tm=512):
    N, D = x.shape
    tm = min(tm, N)
    wt = w_router.astype(F32).T
    wh, wl = _split_bf16(wt)
    fix = lambda i: (0, 0)
    return pl.pallas_call(
        _router_body,
        grid=(N // tm,),
        in_specs=[pl.BlockSpec((tm, D), lambda i: (i, 0)), pl.BlockSpec((N_EXPERTS, D), fix),
                  pl.BlockSpec((N_EXPERTS, D), fix), pl.BlockSpec((N_EXPERTS, 1), fix)],
        out_specs=[pl.BlockSpec((N_EXPERTS, tm), lambda i: (0, i)), pl.BlockSpec((N_GROUPS, tm), lambda i: (0, i))],
        out_shape=[jax.ShapeDtypeStruct((N_EXPERTS, N), F32), jax.ShapeDtypeStruct((N_GROUPS, N), F32)],
        compiler_params=_params("parallel"),
        name="router",
    )(x, wh, wl, b_router.astype(F32).reshape(N_EXPERTS, 1))


MOE_TILE = 1024
MOE_PASS_ROWS = 304
MOE_CHUNK = 256
MOE_EXPERTS_PER_STEP = 2
ROW_ALIGN = 16


def _moe_body(seg_ref, x_ref, gt_ref, grp_ref, w1_ref, w3_ref, w2_ref, g_ref, b_ref, o_ref,
              xs_scr, ys_scr, pt_scr, gs_scr):
    i = pl.program_id(0)
    e = pl.program_id(1)
    T = x_ref.shape[0]
    K, NE = N_GROUPS, N_EXPERTS
    cap = MOE_PASS_ROWS
    ch = min(MOE_CHUNK, T)

    @pl.when(e == 0)
    def _():
        grp = grp_ref[...]
        xb = x_ref[...].astype(BF16)
        gates = gt_ref[...]
        gh, gm = _split_bf16(gates)
        gl = (gates - gh.astype(F32) - gm.astype(F32)).astype(BF16)
        g3 = jnp.concatenate([gh, gm, gl], axis=0)
        cnt = jnp.sum(grp, axis=1, keepdims=True)
        off = [jnp.zeros((1, 1), F32)]
        for k in range(K - 1):
            off.append(off[-1] + cnt[k:k + 1])
        r_i = lax.broadcasted_iota(jnp.int32, (ch, T), 0)
        c_i = lax.broadcasted_iota(jnp.int32, (ch, T), 1)
        lane = lax.broadcasted_iota(jnp.int32, (2 * K, T), 1)
        run = jnp.concatenate([grp, jnp.zeros_like(grp)], axis=0)
        sh = 1
        while sh < T:
            run = run + jnp.where(lane >= sh, pltpu.roll(run, sh, 1), 0.0)
            sh *= 2
        pos_row = jnp.zeros((1, T), F32)
        for k in range(K):
            pos_row = pos_row + grp[k:k + 1, :] * (off[k] + run[k:k + 1, :] - 1.0)
        pos_t = jnp.broadcast_to(pos_row, (2 * K, T))
        pos_col = [pos_t[:, c * ch:(c + 1) * ch].T[:, 0:1] for c in range(T // ch)]
        for c in range(T // ch):
            c0 = c * ch
            perm = jnp.where((r_i + c0).astype(F32) == pos_row, 1.0, 0.0).astype(BF16)
            xs_scr[c0:c0 + ch, :] = _dot(perm, xb).astype(BF16)
            g3s = _dot_nt(perm, g3)
            gs_scr[c0:c0 + ch, :] = g3s[:, :NE] + g3s[:, NE:2 * NE] + g3s[:, 2 * NE:]
            pt_scr[c0:c0 + ch, :] = jnp.where(c_i.astype(F32) == pos_col[c], 1.0, 0.0).astype(BF16)
        xs_scr[T:, :] = jnp.zeros((cap, xs_scr.shape[1]), BF16)
        gs_scr[T:, :] = jnp.zeros((cap, gs_scr.shape[1]), F32)
        ys_scr[...] = jnp.zeros_like(ys_scr)

    ne = w1_ref.shape[0]
    k = (e * ne) // EXPERTS_PER_GROUP
    off = seg_ref[i, k]
    cnt = seg_ref[i, K + k]
    start0 = (off // ROW_ALIGN) * ROW_ALIGN
    n_pass = (off + cnt - start0 + cap - 1) // cap

    def one_pass(n, c):
        start = pl.multiple_of(start0 + n * cap, ROW_ALIGN)
        xc = xs_scr[pl.ds(start, cap), :]
        gsc = gs_scr[pl.ds(start, cap), :]
        lane = lax.broadcasted_iota(jnp.int32, gsc.shape, 1)
        y = None
        for x in range(ne):
            a = _dot(xc, w1_ref[x])
            hm = (a * _sigmoid(a)) * _dot(xc, w3_ref[x])
            gcol = jnp.sum(jnp.where(lane == e * ne + x, gsc, 0.0), axis=1, keepdims=True)
            yx = gcol * _dot(hm.astype(BF16), w2_ref[x])
            y = yx if y is None else y + yx
        ys_scr[pl.ds(start, cap), :] += y
        return c

    lax.fori_loop(0, n_pass, one_pass, 0)

    @pl.when(e == pl.num_programs(1) - 1)
    def _():
        y = _dot(pt_scr[...], ys_scr[0:T, :].astype(BF16))
        o_ref[...] = _layer_norm(ALPHA * x_ref[...] + y, g_ref[...], b_ref[...])


def _moe(x, gates_t, grp_t, w1, w3, w2, g, b):
    N, D = x.shape
    T = min(MOE_TILE, N)
    K = N_GROUPS
    cnt = grp_t.reshape(K, N // T, T).sum(-1).astype(jnp.int32).T
    seg = jnp.concatenate([jnp.cumsum(cnt, axis=1) - cnt, cnt], axis=1)
    fix = lambda i, e, s: (0, 0)
    ne = MOE_EXPERTS_PER_STEP
    rows = T + MOE_PASS_ROWS
    grid_spec = pltpu.PrefetchScalarGridSpec(
        num_scalar_prefetch=1,
        grid=(N // T, N_EXPERTS // ne),
        in_specs=[pl.BlockSpec((T, D), lambda i, e, s: (i, 0)),
                  pl.BlockSpec((N_EXPERTS, T), lambda i, e, s: (0, i)),
                  pl.BlockSpec((K, T), lambda i, e, s: (0, i)),
                  pl.BlockSpec((ne, D, D_EXPERT), lambda i, e, s: (e, 0, 0)),
                  pl.BlockSpec((ne, D, D_EXPERT), lambda i, e, s: (e, 0, 0)),
                  pl.BlockSpec((ne, D_EXPERT, D), lambda i, e, s: (e, 0, 0)),
                  pl.BlockSpec((1, D), fix), pl.BlockSpec((1, D), fix)],
        out_specs=pl.BlockSpec((T, D), lambda i, e, s: (i, 0)),
        scratch_shapes=[pltpu.VMEM((rows, D), BF16), pltpu.VMEM((rows, D), F32),
                        pltpu.VMEM((T, T), BF16), pltpu.VMEM((rows, N_EXPERTS), F32)])
    return pl.pallas_call(
        _moe_body,
        grid_spec=grid_spec,
        out_shape=jax.ShapeDtypeStruct((N, D), F32),
        compiler_params=_params("parallel", "arbitrary"),
        name="moe",
    )(seg, x, gates_t, grp_t, w1, w3, w2, g, b)


NSA_KVW = NSA_KV_HEADS * HEAD_DIM
NSA_ROPE_W = NSA_HEADS * HEAD_DIM + 2 * NSA_KVW


def _rot_cols(w):
    K, n = w.shape
    w3 = w.reshape(K, n // HEAD_DIM, 2, HEAD_DIM // 2)
    return jnp.stack([-w3[:, :, 1], w3[:, :, 0]], axis=2).reshape(K, n)


def _inproj1_weights(w_in):
    H, G, d = NSA_HEADS, NSA_KV_HEADS, HEAD_DIM
    cuts = [H * d + i * NSA_KVW for i in range(7)]
    q, kc, vc, ks, vs, kw, vw, gate = jnp.split(w_in.astype(F32), cuts, axis=1)
    w_all = jnp.concatenate([q, ks, kw, kc, vc], axis=1).astype(BF16)
    w_t = jnp.concatenate([vs, vw, gate], axis=1).T.astype(BF16)
    return w_all, w_t


def _rope_tables(pos):
    inv = ROPE_THETA ** (-jnp.arange(0, HEAD_DIM, 2, dtype=F32) / HEAD_DIM)
    ang = pos.astype(F32)[:, None] * inv[None, :]
    c, s = jnp.cos(ang), jnp.sin(ang)
    return jnp.concatenate([c, c], axis=-1), jnp.concatenate([s, s], axis=-1)


def _rope_roll_tables(pos):
    inv = ROPE_THETA ** (-jnp.arange(0, HEAD_DIM, 2, dtype=F32) / HEAD_DIM)
    ang = pos.astype(F32)[:, None] * inv[None, :]
    c, s, z = jnp.cos(ang), jnp.sin(ang), jnp.zeros_like(ang)
    two = lambda a, b: jnp.concatenate([a, b, a, b], axis=-1)
    return two(c, c), two(-s, z), two(z, s)


V_ROWS = HEAD_DIM + 16


def _inproj1_body(x_ref, w_ref, wt_ref, cos_ref, sinlo_ref, sinhi_ref, q_ref, ks_ref, kw_ref, kc_ref, vc_ref,
                  vst_ref, vwt_ref, gate_ref):
    xb = x_ref[0].astype(BF16)
    tl = xb.shape[0]
    d = HEAD_DIM
    p = _dot(xb, w_ref[...])
    pt = _dot_nt(wt_ref[...], xb)
    ones = jnp.ones((V_ROWS - d, tl), BF16)
    for n, ref in enumerate((vst_ref, vwt_ref)):
        for g in range(NSA_KV_HEADS):
            r0 = n * NSA_KVW + g * d
            ref[0, g, 0:d, :] = pt[r0:r0 + d, :].astype(BF16)
            ref[0, g, d:V_ROWS, :] = ones
    gate_ref[0] = pt[2 * NSA_KVW:, :]
    cos = cos_ref[...]
    sin_lo = sinlo_ref[...]
    sin_hi = sinhi_ref[...]
    roped = []
    for c in range(NSA_ROPE_W // 128):
        xc = p[:, c * 128:(c + 1) * 128]
        roped.append(xc * cos + pltpu.roll(xc, 96, 1) * sin_lo + pltpu.roll(xc, 32, 1) * sin_hi)

    def head(chunks, h):
        blk = chunks[h // 2]
        return blk[:, (h % 2) * d:(h % 2 + 1) * d]

    for h in range(NSA_HEADS):
        q_ref[0, h] = (head(roped, h) * LOG2_Q_SCALE).astype(BF16)
    nq = NSA_HEADS // 2
    pos = pl.program_id(1) * tl + lax.broadcasted_iota(jnp.int32, (tl, 128 - d), 0)
    block_onehot = jnp.where(pos // SEL_LEN == lax.broadcasted_iota(jnp.int32, (tl, 128 - d), 1), 1.0, 0.0)
    for g in range(NSA_KV_HEADS):
        ks_ref[0, g] = jnp.concatenate([head(roped[nq:], g), block_onehot], axis=1).astype(BF16)
        kw_ref[0, g] = head(roped[nq + 2:], g).astype(BF16)
    for n, ref in enumerate((kc_ref, vc_ref)):
        for g in range(NSA_KV_HEADS):
            c0 = NSA_ROPE_W + n * NSA_KVW + g * d
            ref[0, g] = p[:, c0:c0 + d]


def _inproj1(x, w_all, w_t, tl=512):
    B, L, D = x.shape
    tl = min(tl, L)
    G = NSA_KV_HEADS
    n_gate = w_t.shape[0] - 2 * NSA_KVW
    tables = _rope_roll_tables(jnp.arange(L))
    qh = jax.ShapeDtypeStruct((B, NSA_HEADS, L, HEAD_DIM), BF16)
    kvh = jax.ShapeDtypeStruct((B, G, L, HEAD_DIM), BF16)
    vth = jax.ShapeDtypeStruct((B, G, V_ROWS, L), BF16)
    q_spec = pl.BlockSpec((1, NSA_HEADS, tl, HEAD_DIM), lambda b, i: (b, 0, i, 0))
    kv_spec = pl.BlockSpec((1, G, tl, HEAD_DIM), lambda b, i: (b, 0, i, 0))
    ksel_spec = pl.BlockSpec((1, G, tl, 128), lambda b, i: (b, 0, i, 0))
    vt_spec = pl.BlockSpec((1, G, V_ROWS, tl), lambda b, i: (b, 0, 0, i))
    tab_spec = pl.BlockSpec((tl, 128), lambda b, i: (i, 0))
    return pl.pallas_call(
        _inproj1_body,
        grid=(B, L // tl),
        in_specs=[pl.BlockSpec((1, tl, D), lambda b, i: (b, i, 0)),
                  pl.BlockSpec(w_all.shape, lambda b, i: (0, 0)),
                  pl.BlockSpec(w_t.shape, lambda b, i: (0, 0)),
                  tab_spec, tab_spec, tab_spec],
        out_specs=([q_spec, ksel_spec] + [kv_spec] * 3 + [vt_spec] * 2
                   + [pl.BlockSpec((1, n_gate, tl), lambda b, i: (b, 0, i))]),
        out_shape=([qh, jax.ShapeDtypeStruct((B, G, L, 128), BF16), kvh]
                   + [jax.ShapeDtypeStruct((B, G, L, HEAD_DIM), F32)] * 2 + [vth] * 2
                   + [jax.ShapeDtypeStruct((B, n_gate, L), F32)]),
        compiler_params=_params("parallel", "parallel"),
        name="inproj1",
    )(x, w_all, w_t, *tables)


def _compress_body(kc_ref, vc_ref, posk_ref, posv_ref, w1k_ref, w1v_ref, w2k_ref, w2kr_ref, w2v_ref,
                   cos_ref, sin_ref, kcmp_ref, vcmp_ref):
    def hidden(a_ref, pos_ref, w1_ref):
        nrow = a_ref.shape[2] // CMP_STRIDE
        a = jnp.concatenate([a_ref[0, 0, pl.ds(l, nrow, stride=CMP_STRIDE), :] for l in range(CMP_STRIDE)],
                            axis=1).astype(F32)
        lo = _dot((a + pos_ref[0:1, :]).astype(BF16), w1_ref[0])
        hi = _dot((a + pos_ref[1:2, :]).astype(BF16), w1_ref[1])
        hi_next = pltpu.roll(hi, nrow - 1, 0)
        return _gelu(lo + hi_next).astype(BF16)

    hk = hidden(kc_ref, posk_ref, w1k_ref)
    kcmp = _dot(hk, w2k_ref[...]) * cos_ref[...] + _dot(hk, w2kr_ref[...]) * sin_ref[...]
    kcmp_ref[0, 0] = kcmp.astype(BF16)
    hv = hidden(vc_ref, posv_ref, w1v_ref)
    vcmp_ref[0, 0] = _dot_nt(w2v_ref[...], hv).astype(BF16)


def _compress(kc, vc, pos_k, w1_k, w2_k, pos_v, w1_v, w2_v):
    B, G, L, d = kc.shape
    half = CMP_STRIDE * d
    nb = L // CMP_STRIDE
    posk = pos_k.astype(F32).reshape(2, half)
    posv = pos_v.astype(F32).reshape(2, half)
    w1k = w1_k.astype(BF16).reshape(2, half, CMP_HIDDEN)
    w1v = w1_v.astype(BF16).reshape(2, half, CMP_HIDDEN)
    w2k = w2_k.astype(F32)
    cos, sin = _rope_tables(jnp.arange(nb) * CMP_STRIDE + CMP_LEN - 1)
    blk = pl.BlockSpec((1, 1, L, d), lambda b, g: (b, g, 0, 0))
    out = pl.BlockSpec((1, 1, nb, d), lambda b, g: (b, g, 0, 0))
    out_t = pl.BlockSpec((1, 1, d, nb), lambda b, g: (b, g, 0, 0))
    fix2 = lambda b, g: (0, 0)
    fix3 = lambda b, g: (0, 0, 0)
    return pl.pallas_call(
        _compress_body,
        grid=(B, G),
        in_specs=[blk, blk, pl.BlockSpec((2, half), fix2), pl.BlockSpec((2, half), fix2),
                  pl.BlockSpec((2, half, CMP_HIDDEN), fix3), pl.BlockSpec((2, half, CMP_HIDDEN), fix3),
                  pl.BlockSpec((CMP_HIDDEN, d), fix2), pl.BlockSpec((CMP_HIDDEN, d), fix2),
                  pl.BlockSpec((d, CMP_HIDDEN), fix2), pl.BlockSpec((nb, d), fix2), pl.BlockSpec((nb, d), fix2)],
        out_specs=[out, out_t],
        out_shape=[jax.ShapeDtypeStruct((B, G, nb, d), BF16), jax.ShapeDtypeStruct((B, G, d, nb), BF16)],
        compiler_params=_params("parallel", "parallel"),
        name="compress",
    )(kc, vc, posk, posv, w1k, w1v, w2k.astype(BF16), _rot_cols(w2k).astype(BF16), w2_v.T.astype(BF16), cos, sin)


NSA_KBLOCK = 256
NSA_GROUPS_PER_STEP = 4
NSA_HEADS_PER_CHAIN = 4
NSA_SCORE_LOOKAHEAD = 3


def _nsa_body(q_ref, kcmp_ref, vcmpt_ref, ks_ref, vst_ref, kw_ref, vwt_ref, gate_ref, o_ref, qsel_scr, *, seq):
    i = pl.program_id(2)
    tq = q_ref.shape[2]
    gb = kcmp_ref.shape[1]
    tk = min(NSA_KBLOCK, seq)
    R, d = NSA_REP, HEAD_DIM
    nq = R * tq
    nb = seq // SEL_LEN
    mc = seq // CMP_STRIDE
    t_row = i * tq + lax.broadcasted_iota(jnp.int32, (1, tq), 1)
    t_all = jnp.concatenate([t_row] * R, axis=1)
    m_col = lax.broadcasted_iota(jnp.int32, (mc, 1), 0)
    m_row = lax.broadcasted_iota(jnp.int32, (1, mc), 1)
    n_col = lax.broadcasted_iota(jnp.int32, (nb, 1), 0)

    def q_rows(g):
        return q_ref[0, g * R:(g + 1) * R].reshape(nq, d)

    valid_c = (m_col * CMP_STRIDE + (CMP_LEN - 1)) <= t_all
    ovl = jnp.logical_and(m_row * CMP_STRIDE < (n_col + 1) * SEL_LEN,
                          m_row * CMP_STRIDE + CMP_LEN > n_col * SEL_LEN)
    ovl = jnp.where(ovl, 1.0, 0.0).astype(BF16)
    cur = t_row // SEL_LEN
    forced = jnp.logical_or(n_col == 0, jnp.logical_or(n_col == cur, n_col == cur - 1))
    bonus = jnp.where(forced, FORCE_BONUS, 0.0)
    valid_s = n_col * SEL_LEN <= t_row
    o_c = []
    ties = [jnp.where(n_col > n2, 1.0, 0.0) for n2 in range(nb)]
    cmp_scores = [_dot_nt(kcmp_ref[0, g], q_rows(g)) for g in range(gb)]
    for g in range(gb):
        s = jnp.where(valid_c, cmp_scores[g], NEG)
        e = jnp.exp2(s - jnp.maximum(jnp.max(s, axis=0, keepdims=True), 0.5 * NEG))
        den = jnp.sum(e, axis=0, keepdims=True)
        p = e / jnp.where(den > 0.0, den, 1.0)
        o_c.append(_dot(vcmpt_ref[0, g], p.astype(BF16)))
        psum = p[:, 0:tq]
        for r in range(1, R):
            psum = psum + p[:, r * tq:(r + 1) * tq]
        ph, plo = _split_bf16(psum)
        score = jnp.where(valid_s, _dot(ovl, ph) + _dot(ovl, plo) + bonus, NEG)
        rank = jnp.zeros((nb, tq), F32)
        for n2 in range(nb):
            other = score[n2:n2 + 1, :]
            rank = rank + jnp.where(other > score, 1.0, jnp.where(other == score, ties[n2], 0.0))
        pen = jnp.where(rank < float(SEL_TOP), 0.0, NEG).T.astype(BF16)
        pen = jnp.concatenate([pen, jnp.zeros((tq, 128 - d - nb), BF16)], axis=1)
        qsel_scr[g] = jnp.concatenate([q_rows(g), jnp.concatenate([pen] * R, axis=0)], axis=1)

    k_col = lax.broadcasted_iota(jnp.int32, (tk, 1), 0)
    hp = NSA_HEADS_PER_CHAIN
    j_hi = (i * tq + tq - 1) // tk + 1

    def sweep(q_fn, k_ref, vt_ref, j_lo, bias_fn, bias_every_block):
        chains = [(g, h0) for g in range(gb) for h0 in range(0, R, hp)]

        def body(j, state, with_bias):
            k0 = pl.multiple_of(j * tk, tk)
            bias = jnp.concatenate([bias_fn(k0 + k_col)] * hp, axis=1) if with_bias else None

            def scores(c):
                g, h0 = chains[c]
                sc = _dot_nt(k_ref[0, g, pl.ds(k0, tk), :], q_fn(g, h0))
                return sc + bias if with_bias else sc

            new_state = []
            ahead = NSA_SCORE_LOOKAHEAD
            pending = [scores(c) for c in range(min(ahead, len(chains)))]
            for c, (m_run, acc) in enumerate(state):
                sc = pending.pop(0)
                if c + ahead < len(chains):
                    pending.append(scores(c + ahead))
                vtb = vt_ref[0, chains[c][0], :, pl.ds(k0, tk)]
                m_new = jnp.maximum(m_run, jnp.max(sc, axis=0, keepdims=True))
                pr = jnp.exp2(sc - m_new)
                new_state.append((m_new, jnp.exp2(m_run - m_new) * acc + _dot(vtb, pr.astype(BF16))))
            return tuple(new_state)

        state = tuple((jnp.full((1, hp * tq), NEG, F32), jnp.zeros((V_ROWS, hp * tq), F32)) for _ in chains)
        state = lax.fori_loop(j_lo, j_hi - 1, functools.partial(body, with_bias=bias_every_block), state)
        state = body(j_hi - 1, state, True)
        outs = []
        for g in range(gb):
            acc = jnp.concatenate([a for (cg, _), (_, a) in zip(chains, state) if cg == g], axis=1)
            outs.append(acc[0:d] / acc[d:d + 1])
        return outs

    def causal_bias(kpos):
        return jnp.where(kpos <= t_row, 0.0, NEG)

    o_s = sweep(lambda g, h0: qsel_scr[g, h0 * tq:(h0 + hp) * tq, :], ks_ref, vst_ref, 0, causal_bias, False)

    def win_bias(kpos):
        return jnp.where(jnp.logical_and(kpos <= t_row, kpos > t_row - WINDOW), 0.0, NEG)

    o_w = sweep(lambda g, h0: q_ref[0, g * R + h0:g * R + h0 + hp].reshape(hp * tq, d), kw_ref, vwt_ref,
                jnp.maximum(i * tq - (WINDOW - 1), 0) // tk, win_bias, True)

    outs = []
    for g in range(gb):
        row0 = (pl.program_id(1) * gb + g) * 3 * R

        def gate(branch):
            rows = [gate_ref[0, pl.ds(row0 + 3 * r + branch, 1), :] for r in range(R)]
            return _sigmoid(jnp.concatenate(rows, axis=1))

        o_t = gate(0) * o_c[g] + gate(1) * o_s[g] + gate(2) * o_w[g]
        for r in range(R):
            outs.append(o_t[:, r * tq:(r + 1) * tq].T)
    o_ref[0] = jnp.concatenate(outs, axis=-1).astype(o_ref.dtype)


def _nsa_attention(q, kcmp, vcmpt, ks, vst, kw, vwt, gate):
    B, H, L, d = q.shape
    G, R = NSA_KV_HEADS, NSA_REP
    gb = NSA_GROUPS_PER_STEP
    tq = min(ATT_BLOCK, L)
    mc = L // CMP_STRIDE
    k_spec = pl.BlockSpec((1, gb, L, d), lambda b, g, i: (b, g, 0, 0))
    ksel_spec = pl.BlockSpec((1, gb, L, 128), lambda b, g, i: (b, g, 0, 0))
    vt_spec = pl.BlockSpec((1, gb, V_ROWS, L), lambda b, g, i: (b, g, 0, 0))
    return pl.pallas_call(
        functools.partial(_nsa_body, seq=L),
        grid=(B, G // gb, L // tq),
        in_specs=[pl.BlockSpec((1, gb * R, tq, d), lambda b, g, i: (b, g, i, 0)),
                  pl.BlockSpec((1, gb, mc, d), lambda b, g, i: (b, g, 0, 0)),
                  pl.BlockSpec((1, gb, d, mc), lambda b, g, i: (b, g, 0, 0)),
                  ksel_spec, vt_spec, k_spec, vt_spec,
                  pl.BlockSpec((1, gate.shape[1], tq), lambda b, g, i: (b, 0, i))],
        out_specs=pl.BlockSpec((1, tq, gb * R * d), lambda b, g, i: (b, i, g)),
        out_shape=jax.ShapeDtypeStruct((B, L, H * d), BF16),
        scratch_shapes=[pltpu.VMEM((gb, R * tq, 128), BF16)],
        compiler_params=_params("parallel", "parallel", "arbitrary"),
        name="nsa_attention",
    )(q, kcmp, vcmpt, ks, vst, kw, vwt, gate)


def kernel(x, w_in_0, ssm_lam_re, ssm_lam_im, ssm_log_dt, ssm_b_re, ssm_b_im, ssm_c_re, ssm_c_im, ssm_d, w_glu, w_out_0, ln_mix_g_0, ln_mix_b_0, ln_ffn_g_0, ln_ffn_b_0, w1_0, w3_0, w2_0, w_in_1, cmp_pos_k, cmp_w1_k, cmp_w2_k, cmp_pos_v, cmp_w1_v, cmp_w2_v, w_out_1, ln_mix_g_1, ln_mix_b_1, ln_ffn_g_1, ln_ffn_b_1, w1_1, w3_1, w2_1, w_router, b_router):
    B, L, D = x.shape
    N = B * L
    vec = lambda a: a.astype(F32).reshape(1, D)

    def ffn(h, w1, w3, w2, g, b):
        gates_t, grp_t = _router(h, w_router, b_router)
        return _moe(h, gates_t, grp_t, w1.astype(BF16), w3.astype(BF16), w2.astype(BF16), vec(g), vec(b))

    q, k, v, u = _inproj0(x, w_in_0.astype(BF16))
    o_a = _sb_attention(q, k, v)
    y = _ssm(u, _ssm_tables(ssm_lam_re, ssm_lam_im, ssm_log_dt, ssm_b_re, ssm_b_im, ssm_c_re, ssm_c_im, ssm_d), B)
    w_out_0b = w_out_0.astype(BF16)
    h = _outproj0(x, o_a, y, w_glu.astype(BF16), w_out_0b[:SB_WIDTH], w_out_0b[SB_WIDTH:],
                  vec(ln_mix_g_0), vec(ln_mix_b_0)).reshape(N, D)
    h = ffn(h, w1_0, w3_0, w2_0, ln_ffn_g_0, ln_ffn_b_0)

    q, ks, kw, kc, vc, vst, vwt, gate = _inproj1(h.reshape(B, L, D), *_inproj1_weights(w_in_1))
    kcmp, vcmpt = _compress(kc, vc, cmp_pos_k, cmp_w1_k, cmp_w2_k, cmp_pos_v, cmp_w1_v, cmp_w2_v)
    o = _nsa_attention(q, kcmp, vcmpt, ks, vst, kw, vwt, gate).reshape(N, NSA_HEADS * HEAD_DIM)
    h = _outproj1(h, o, w_out_1.astype(BF16), vec(ln_mix_g_1), vec(ln_mix_b_1))
    h = ffn(h, w1_1, w3_1, w2_1, ln_ffn_g_1, ln_ffn_b_1)
    return h.reshape(B, L, D)
```

```python
import functools
import math

import numpy as np
import jax
import jax.numpy as jnp
from jax import lax
from jax.experimental import pallas as pl
from jax.experimental.pallas import tpu as pltpu

F32 = jnp.float32
BF16 = jnp.bfloat16

D_MODEL = 1024
DEPTH = 2
SB_HEADS = 8
HEAD_DIM = 64
SB_WIDTH = SB_HEADS * HEAD_DIM
SSM_WIDTH = D_MODEL - SB_WIDTH
SSM_GROUP = 16
SSM_GROUPS = SSM_WIDTH // SSM_GROUP
SSM_STATE = 64
SSM_CHUNK = 16
SSM_LANE_TILES = SSM_WIDTH // 128
SSM_TILE_GROUPS = 128 // SSM_GROUP
NSA_HEADS = 16
NSA_KV_HEADS = 4
NSA_REP = NSA_HEADS // NSA_KV_HEADS
CMP_LEN = 32
CMP_STRIDE = 16
CMP_HIDDEN = 256
SEL_LEN = 64
SEL_TOP = 8
WINDOW = 512
ROPE_THETA = 10000.0
FORCE_BONUS = 1e4
NEG = -1e30
N_EXPERTS = 16
N_GROUPS = 4
EXPERTS_PER_GROUP = N_EXPERTS // N_GROUPS
D_EXPERT = 512
ALPHA = (2 * DEPTH) ** 0.25
LN_EPS = 1e-5
ATTN_SCALE = HEAD_DIM ** -0.5
LOG2_Q_SCALE = ATTN_SCALE * math.log2(math.e)
ATT_BLOCK = 256
GELU_C = math.sqrt(2.0 / math.pi)


def _params(*sem):
    return pltpu.CompilerParams(dimension_semantics=sem, vmem_limit_bytes=56 * 1024 * 1024)


def _sigmoid(x):
    return 1.0 / (1.0 + jnp.exp(-x))


def _gelu(x):
    return 0.5 * x * (1.0 + jnp.tanh(GELU_C * (x + 0.044715 * (x * x * x))))


def _layer_norm(r, g, b):
    mu = jnp.mean(r, axis=-1, keepdims=True)
    d = r - mu
    var = jnp.mean(d * d, axis=-1, keepdims=True)
    return d * lax.rsqrt(var + LN_EPS) * g + b


def _dot(a, b):
    return jnp.dot(a, b, preferred_element_type=F32)


def _dot_nt(a, b):
    return lax.dot_general(a, b, (((1,), (1,)), ((), ())), preferred_element_type=F32)


def _split_bf16(x):
    hi = x.astype(BF16)
    lo = (x - hi.astype(F32)).astype(BF16)
    return hi, lo


def _inproj0_body(x_ref, w_ref, q_ref, k_ref, v_ref, u_ref, slab_scr):
    nb, tl, D = x_ref.shape
    p = _dot(x_ref[...].reshape(nb * tl, D).astype(BF16), w_ref[...])
    for h in range(SB_HEADS):
        c = h * HEAD_DIM
        q_ref[:, h] = (p[:, c:c + HEAD_DIM] * LOG2_Q_SCALE).astype(BF16).reshape(nb, tl, HEAD_DIM)
        k_ref[:, h] = p[:, SB_WIDTH + c:SB_WIDTH + c + HEAD_DIM].astype(BF16).reshape(nb, tl, HEAD_DIM)
        v_ref[:, h] = p[:, 2 * SB_WIDTH + c:2 * SB_WIDTH + c + HEAD_DIM].astype(BF16).reshape(nb, tl, HEAD_DIM)
    T = SSM_CHUNK
    for v in range(SSM_LANE_TILES):
        c0 = 3 * SB_WIDTH + v * 128
        slab_scr[...] = p[:, c0:c0 + 128]
        for c in range(tl // T):
            for s in range(T):
                u_ref[v, c * nb:(c + 1) * nb, s * 128:(s + 1) * 128] = slab_scr[pl.ds(c * T + s, nb, stride=tl), :]


def _inproj0(x, w_bf16, tl=64):
    B, L, D = x.shape
    tl = min(tl, L)
    nout = w_bf16.shape[1]
    T = SSM_CHUNK
    head = jax.ShapeDtypeStruct((B, SB_HEADS, L, HEAD_DIM), BF16)
    head_spec = pl.BlockSpec((B, SB_HEADS, tl, HEAD_DIM), lambda i: (0, 0, i, 0))
    return pl.pallas_call(
        _inproj0_body,
        grid=(L // tl,),
        in_specs=[pl.BlockSpec((B, tl, D), lambda i: (0, i, 0)),
                  pl.BlockSpec((D, nout), lambda i: (0, 0))],
        out_specs=[head_spec, head_spec, head_spec,
                   pl.BlockSpec((SSM_LANE_TILES, tl // T * B, T * 128), lambda i: (0, i, 0))],
        out_shape=[head, head, head,
                   jax.ShapeDtypeStruct((SSM_LANE_TILES, L // T * B, T * 128), F32)],
        scratch_shapes=[pltpu.VMEM((B * tl, 128), F32)],
        compiler_params=_params("parallel"),
        name="inproj0",
    )(x, w_bf16)


SB_BLOCK = 256
SB_HEADS_PER_STEP = 4
SB_LOGIT_LOOKAHEAD = 3
EXP2_UNDERFLOW = -151.0


def _sb_attn_body(q_ref, k_ref, v_ref, o_ref, acc_scr, cs_scr):
    i = pl.program_id(2)
    t = q_ref.shape[2]
    nh = q_ref.shape[1]
    row = lax.broadcasted_iota(jnp.int32, (t, t), 0)
    col = lax.broadcasted_iota(jnp.int32, (t, t), 1)
    suffix = jnp.where(row > col, 1.0, 0.0).astype(BF16)
    below = col < row

    def logits(hh, k0):
        return _dot_nt(q_ref[0, hh], k_ref[0, hh, pl.ds(k0, t), :])

    def alive(css):
        return (jnp.max(functools.reduce(jnp.maximum, css)) > EXP2_UNDERFLOW).astype(jnp.int32)

    def all_heads(k0, diagonal):
        zs = [logits(hh, k0) for hh in range(nh)]
        sps, l1s = [], []
        for z in zs:
            sp = jnp.maximum(z, 0.0) + jnp.log2(1.0 + jnp.exp2(-jnp.abs(z)))
            sps.append(sp)
            l1s.append(jnp.where(below, -sp, 0.0) if diagonal else -sp)
        within_all = _dot(jnp.concatenate([l1.astype(BF16) for l1 in l1s], axis=0), suffix)
        css = []
        for hh in range(nh):
            vb = v_ref[0, hh, pl.ds(k0, t), :]
            base = zs[hh] - sps[hh] + within_all[hh * t:(hh + 1) * t]
            if diagonal:
                w = jnp.where(below, jnp.exp2(base), 0.0)
                acc_scr[hh] = _dot(w.astype(BF16), vb)
                cs = jnp.sum(l1s[hh], axis=1, keepdims=True)
            else:
                w = jnp.exp2(base + cs_scr[hh])
                acc_scr[hh] += _dot(w.astype(BF16), vb)
                cs = cs_scr[hh] + jnp.sum(l1s[hh], axis=1, keepdims=True)
            cs_scr[hh] = cs
            css.append(cs)
        return alive(css)

    first = all_heads(pl.multiple_of(i * t, t), True)

    def cond(c):
        return jnp.logical_and(c[0] <= i, c[1] > 0)

    def body(c):
        return c[0] + 1, all_heads(pl.multiple_of((i - c[0]) * t, t), False)

    lax.while_loop(cond, body, (jnp.int32(1), first))
    o_ref[0] = jnp.concatenate([acc_scr[hh] for hh in range(nh)], axis=-1).astype(o_ref.dtype)


def _sb_attention(q, k, v):
    B, H, L, d = q.shape
    t = min(SB_BLOCK, L)
    nh = SB_HEADS_PER_STEP
    return pl.pallas_call(
        _sb_attn_body,
        grid=(B, H // nh, L // t),
        in_specs=[pl.BlockSpec((1, nh, t, d), lambda b, h, i: (b, h, i, 0)),
                  pl.BlockSpec((1, nh, L, d), lambda b, h, i: (b, h, 0, 0)),
                  pl.BlockSpec((1, nh, L, d), lambda b, h, i: (b, h, 0, 0))],
        out_specs=pl.BlockSpec((1, t, nh * d), lambda b, h, i: (b, i, h)),
        out_shape=jax.ShapeDtypeStruct((B, L, H * d), BF16),
        scratch_shapes=[pltpu.VMEM((nh, t, d), F32), pltpu.VMEM((nh, t, 1), F32)],
        compiler_params=_params("parallel", "parallel", "arbitrary"),
        name="sb_attention",
    )(q, k, v)


def _ssm_tables(lam_re, lam_im, log_dt, b_re, b_im, c_re, c_im, d_skip):
    T, G, P, H = SSM_CHUNK, SSM_GROUPS, SSM_STATE, SSM_GROUP
    hp = lax.Precision.HIGHEST
    dt = jnp.exp(log_dt.astype(F32))[:, None]
    lr = lam_re.astype(F32)
    li = lam_im.astype(F32)
    mag = jnp.exp(lr * dt)
    a_re = mag * jnp.cos(li * dt)
    a_im = mag * jnp.sin(li * dt)
    den = lr * lr + li * li
    nr = a_re - 1.0
    f_re = (nr * lr + a_im * li) / den
    f_im = (a_im * lr - nr * li) / den
    br = b_re.astype(F32)
    bi = b_im.astype(F32)
    bbt_re = (f_re[..., None] * br - f_im[..., None] * bi).transpose(0, 2, 1)[:, None]
    bbt_im = (f_re[..., None] * bi + f_im[..., None] * br).transpose(0, 2, 1)[:, None]
    cr = c_re.astype(F32)[:, None]
    ci = c_im.astype(F32)[:, None]

    def powers(k):
        k = jnp.asarray(k, F32)[None, :, None]
        pmag = jnp.exp(k * (lr * dt)[:, None, :])
        ang = k * (li * dt)[:, None, :]
        return (pmag * jnp.cos(ang))[:, :, None, :], (pmag * jnp.sin(ang))[:, :, None, :]

    def cmul(xr, xi, yr, yi, rows):
        return ((xr * yr - xi * yi).reshape(G, rows, P), (xr * yi + xi * yr).reshape(G, rows, P))

    steps = np.arange(T)
    bs_re, bs_im = cmul(*powers(-steps), bbt_re, bbt_im, T * H)
    ct_re, ct_im = cmul(*powers(steps), cr, ci, T * H)
    full = (jnp.einsum('gap,gbp->gab', bs_re, ct_re, precision=hp)
            - jnp.einsum('gap,gbp->gab', bs_im, ct_im, precision=hp))
    row_s = lax.broadcasted_iota(jnp.int32, (T * H, T * H), 0) // H
    col_t = lax.broadcasted_iota(jnp.int32, (T * H, T * H), 1) // H
    intra = jnp.where(row_s <= col_t, full, 0.0)
    st_re, st_im = cmul(*powers(T - 1 - steps), bbt_re, bbt_im, T * H)
    to_st = jnp.concatenate([st_re, st_im, st_im, st_re], axis=-1)
    c1_re, c1_im = cmul(*powers(steps + 1), cr, ci, T * H)
    from_st_t = jnp.concatenate([c1_re, -c1_im], axis=-1)
    at_re, at_im = powers([T])
    at_re, at_im = at_re[:, 0, 0], at_im[:, 0, 0]
    zeros = jnp.zeros_like(at_re)
    adv = jnp.stack([jnp.concatenate([at_re, at_re], -1),
                     jnp.concatenate([-at_im, at_im], -1),
                     jnp.concatenate([at_im, -at_im], -1)]
                    + [jnp.concatenate([zeros, zeros], -1)] * 5, axis=1)
    dvec = jnp.tile(d_skip.astype(F32).reshape(G, 1, H), (1, 1, T))
    return intra, to_st, from_st_t, adv, dvec


def _ssm_expand(tables):
    intra, to_st, from_st_t, adv, dvec = tables
    V, W, T, H, P = SSM_LANE_TILES, SSM_TILE_GROUPS, SSM_CHUNK, SSM_GROUP, SSM_STATE
    n = T * W * H
    k_in = lax.broadcasted_iota(jnp.int32, (T * H, n), 0)
    col = lax.broadcasted_iota(jnp.int32, (T * H, n), 1)
    spread_th = (k_in == (col // 128) * H + col % H).astype(F32)
    spread_q = (k_in[:2 * P, :W * 2 * P] == col[:2 * P, :W * 2 * P] % (2 * P)).astype(F32)
    row_g = (lax.broadcasted_iota(jnp.int32, (n, 1), 0) // H) % W
    st_row_g = lax.broadcasted_iota(jnp.int32, (W * 2 * P, 1), 0) // (2 * P)
    col_g_th = (lax.broadcasted_iota(jnp.int32, (1, n), 1) // H) % W
    col_g_q = lax.broadcasted_iota(jnp.int32, (1, W * 2 * P), 1) // (2 * P)

    def blockdiag(rows, spread, row_group, col_group):
        full = jnp.einsum('vrk,kc->vrc', rows, spread)
        return jnp.where(row_group == col_group, full, 0.0).astype(BF16)

    by_row = lambda t: t.reshape(V, W, T, H, T * H).transpose(0, 2, 1, 3, 4).reshape(V, n, T * H)
    m8 = blockdiag(by_row(intra), spread_th, row_g, col_g_th)
    ws8 = blockdiag(by_row(to_st)[:, :, :2 * P], spread_q, row_g, col_g_q)
    wi_full = jnp.einsum('vwkq,kc->vwqc', from_st_t.reshape(V, W, T * H, 2 * P), spread_th).reshape(V, W * 2 * P, n)
    wi8 = jnp.where(st_row_g == col_g_th, wi_full, 0.0).astype(BF16)
    adv8 = adv.reshape(V, W, 8, 2 * P).transpose(0, 2, 1, 3).reshape(V, 8, W * 2 * P)
    d8 = jnp.broadcast_to(dvec.reshape(V, 1, W, T, H)[:, :, :, 0:1], (V, 1, W, T, H))
    d8 = d8.transpose(0, 1, 3, 2, 4).reshape(V, 1, T * W * H)
    return m8, ws8, wi8, adv8, d8


def _ssm_body(u_ref, m8_ref, ws8_ref, wi8_ref, adv_ref, dvec_ref, y_ref, s_scr, xin_scr, x1_scr, x2_scr, *, bsz):
    half = x1_scr.shape[1]

    @pl.when(pl.program_id(1) == 0)
    def _():
        x1_scr[...] = jnp.zeros_like(x1_scr)
        x2_scr[...] = jnp.zeros_like(x2_scr)

    u = u_ref[0]
    ub = u.astype(BF16)
    s1 = _dot(ub, ws8_ref[0])
    lane = lax.broadcasted_iota(jnp.int32, s1.shape, 1)
    s_scr[:, :half] = s1
    s_scr[:, half:] = jnp.where(lane % (2 * SSM_STATE) < SSM_STATE,
                                pltpu.roll(s1, half - SSM_STATE, 1), pltpu.roll(s1, SSM_STATE, 1))
    a1 = adv_ref[0, 0:1, :]
    a2 = adv_ref[0, 1:2, :]
    a3 = adv_ref[0, 2:3, :]

    def step(c, carry):
        x1, x2 = carry
        r0 = pl.multiple_of(c * bsz, bsz)
        xin_scr[pl.ds(r0, bsz), :] = x1
        s = s_scr[pl.ds(r0, bsz), :]
        return (a1 * x1 + a2 * x2 + s[:, :half], a1 * x2 + a3 * x1 + s[:, half:])

    x1, x2 = lax.fori_loop(0, u.shape[0] // bsz, step, (x1_scr[...], x2_scr[...]))
    x1_scr[...] = x1
    x2_scr[...] = x2
    y_ref[0] = _dot(ub, m8_ref[0]) + _dot(xin_scr[...].astype(BF16), wi8_ref[0]) + dvec_ref[0] * u


def _ssm(us, tables, bsz, rows=256):
    m8, ws8, wi8, adv8, d8 = _ssm_expand(tables)
    V, CB, width = us.shape
    B = bsz
    half = SSM_TILE_GROUPS * 2 * SSM_STATE
    rows = min(rows, CB)
    once = dict(pipeline_mode=pl.Buffered(1))
    y = pl.pallas_call(
        functools.partial(_ssm_body, bsz=B),
        grid=(V, CB // rows),
        in_specs=[pl.BlockSpec((1, rows, width), lambda v, r: (v, r, 0)),
                  pl.BlockSpec((1, width, width), lambda v, r: (v, 0, 0), **once),
                  pl.BlockSpec((1, width, half), lambda v, r: (v, 0, 0), **once),
                  pl.BlockSpec((1, half, width), lambda v, r: (v, 0, 0), **once),
                  pl.BlockSpec((1, 8, half), lambda v, r: (v, 0, 0)),
                  pl.BlockSpec((1, 1, width), lambda v, r: (v, 0, 0))],
        out_specs=pl.BlockSpec((1, rows, width), lambda v, r: (v, r, 0)),
        out_shape=jax.ShapeDtypeStruct((V, CB, width), F32),
        scratch_shapes=[pltpu.VMEM((rows, 2 * half), F32), pltpu.VMEM((rows, half), F32),
                        pltpu.VMEM((B, half), F32), pltpu.VMEM((B, half), F32)],
        compiler_params=_params("parallel", "arbitrary"),
        name="s5_scan",
    )(us, m8, ws8, wi8, adv8, d8)
    return y


def _outproj0_body(x_ref, oa_ref, y_ref, wglu_ref, woa_ref, wob_ref, g_ref, b_ref, o_ref, slab_scr):
    nb, tl, D = x_ref.shape
    T = SSM_CHUNK
    ys = []
    for v in range(SSM_LANE_TILES):
        for c in range(tl // T):
            for s in range(T):
                slab_scr[pl.ds(c * T + s, nb, stride=tl), :] = y_ref[v, c * nb:(c + 1) * nb, s * 128:(s + 1) * 128]
        ys.append(slab_scr[...])
    h = _gelu(jnp.concatenate(ys, axis=-1))
    ob = h * _sigmoid(_dot(h.astype(BF16), wglu_ref[...]))
    m = _dot(oa_ref[...].reshape(nb * tl, SB_WIDTH), woa_ref[...]) + _dot(ob.astype(BF16), wob_ref[...])
    r = ALPHA * x_ref[...].reshape(nb * tl, D) + m
    o_ref[...] = _layer_norm(r, g_ref[...], b_ref[...]).reshape(nb, tl, D)


def _outproj0(x, oa, ys, wglu, woa, wob, g, b, tl=64):
    B, L, D = x.shape
    tl = min(tl, L)
    T = SSM_CHUNK
    row = lambda i: (0, i, 0)
    fix = lambda i: (0, 0)
    return pl.pallas_call(
        _outproj0_body,
        grid=(L // tl,),
        in_specs=[pl.BlockSpec((B, tl, D), row), pl.BlockSpec((B, tl, SB_WIDTH), row),
                  pl.BlockSpec((SSM_LANE_TILES, tl // T * B, T * 128), row),
                  pl.BlockSpec((SSM_WIDTH, SSM_WIDTH), fix), pl.BlockSpec((SB_WIDTH, D), fix),
                  pl.BlockSpec((SSM_WIDTH, D), fix), pl.BlockSpec((1, D), fix), pl.BlockSpec((1, D), fix)],
        out_specs=pl.BlockSpec((B, tl, D), row),
        out_shape=jax.ShapeDtypeStruct((B, L, D), F32),
        scratch_shapes=[pltpu.VMEM((B * tl, 128), F32)],
        compiler_params=_params("parallel"),
        name="outproj0",
    )(x, oa, ys, wglu, woa, wob, g, b)


def _outproj1_body(x_ref, o_ref_in, w_ref, g_ref, b_ref, o_ref):
    m = _dot(o_ref_in[...], w_ref[...])
    o_ref[...] = _layer_norm(ALPHA * x_ref[...] + m, g_ref[...], b_ref[...])


def _outproj1(x, o, w, g, b, tm=512):
    N, D = x.shape
    tm = min(tm, N)
    row = lambda i: (i, 0)
    fix = lambda i: (0, 0)
    return pl.pallas_call(
        _outproj1_body,
        grid=(N // tm,),
        in_specs=[pl.BlockSpec((tm, D), row), pl.BlockSpec((tm, o.shape[1]), row),
                  pl.BlockSpec(w.shape, fix), pl.BlockSpec((1, D), fix), pl.BlockSpec((1, D), fix)],
        out_specs=pl.BlockSpec((tm, D), row),
        out_shape=jax.ShapeDtypeStruct((N, D), F32),
        compiler_params=_params("parallel"),
        name="outproj1",
    )(x, o, w, g, b)


def _router_body(x_ref, whl_ref, b_ref, g_ref, grp_ref):
    NE = N_EXPERTS
    xh, xl = _split_bf16(x_ref[...])
    whl = whl_ref[...]
    part = _dot(xh, whl)
    logits = (part[:, :NE] + part[:, NE:] + _dot(xl, whl[:, :NE])).T
    aff = _sigmoid(logits)
    sel = aff + b_ref[...]
    E, K = EXPERTS_PER_GROUP, N_GROUPS
    s = [sel[e:e + 1, :] for e in range(N_EXPERTS)]
    a = [aff[e:e + 1, :] for e in range(N_EXPERTS)]
    gscore = []
    for k in range(K):
        v = s[k * E:(k + 1) * E]
        best = None
        for x in range(E):
            for y in range(x + 1, E):
                pair = v[x] + v[y]
                best = pair if best is None else jnp.maximum(best, pair)
        gscore.append(best)
    top = functools.reduce(jnp.maximum, gscore)
    is_g = []
    taken = None
    for k in range(K):
        hit = gscore[k] == top
        if taken is None:
            is_g.append(hit)
            taken = hit
        else:
            is_g.append(jnp.logical_and(hit, jnp.logical_not(taken)))
            taken = jnp.logical_or(taken, hit)

    def pick(rows, j):
        out = rows[(K - 1) * E + j]
        for k in range(K - 2, -1, -1):
            out = jnp.where(is_g[k], rows[k * E + j], out)
        return out

    v = [pick(s, j) for j in range(E)]
    av = [pick(a, j) for j in range(E)]
    chosen = []
    for j in range(E):
        r = jnp.zeros_like(v[j])
        for j2 in range(E):
            if j2 == j:
                continue
            ahead = (v[j2] >= v[j]) if j2 < j else (v[j2] > v[j])
            r = r + jnp.where(ahead, 1.0, 0.0)
        chosen.append(r < 2.0)
    wj = [jnp.where(chosen[j], av[j], 0.0) for j in range(E)]
    tot = wj[0] + wj[1] + wj[2] + wj[3]
    gj = [w / tot for w in wj]
    rows = [jnp.where(is_g[e // E], gj[e % E], 0.0) for e in range(N_EXPERTS)]
    g_ref[...] = jnp.concatenate(rows, axis=0)
    grp_ref[...] = jnp.concatenate([jnp.where(is_g[k], 1.0, 0.0) for k in range(K)], axis=0)


def _router(x, w_router, b_router, tm=512):
    N, D = x.shape
    tm = min(tm, N)
    whl = jnp.concatenate(_split_bf16(w_router.astype(F32)), axis=1)
    fix = lambda i: (0, 0)
    return pl.pallas_call(
        _router_body,
        grid=(N // tm,),
        in_specs=[pl.BlockSpec((tm, D), lambda i: (i, 0)), pl.BlockSpec((D, 2 * N_EXPERTS), fix),
                  pl.BlockSpec((N_EXPERTS, 1), fix)],
        out_specs=[pl.BlockSpec((N_EXPERTS, tm), lambda i: (0, i)), pl.BlockSpec((N_GROUPS, tm), lambda i: (0, i))],
        out_shape=[jax.ShapeDtypeStruct((N_EXPERTS, N), F32), jax.ShapeDtypeStruct((N_GROUPS, N), F32)],
        compiler_params=_params("parallel"),
        name="router",
    )(x, whl, b_router.astype(F32).reshape(N_EXPERTS, 1))


MOE_TILE = 1024
MOE_PASS_ROWS = 304
MOE_CHUNK = 256
MOE_EXPERTS_PER_STEP = 2
ROW_ALIGN = 16


def _moe_body(seg_ref, x_ref, gt_ref, grp_ref, w1_ref, w3_ref, w2_ref, g_ref, b_ref, o_ref,
              xs_scr, ys_scr, pt_scr, gs_scr):
    i = pl.program_id(0)
    e = pl.program_id(1)
    T = x_ref.shape[0]
    K, NE = N_GROUPS, N_EXPERTS
    cap = MOE_PASS_ROWS
    ch = min(MOE_CHUNK, T)

    @pl.when(e == 0)
    def _():
        grp = grp_ref[...]
        xb = x_ref[...].astype(BF16)
        gates = gt_ref[...]
        gh, gm = _split_bf16(gates)
        gl = (gates - gh.astype(F32) - gm.astype(F32)).astype(BF16)
        g3 = jnp.concatenate([gh, gm, gl], axis=0)
        cnt = jnp.sum(grp, axis=1, keepdims=True)
        off = [jnp.zeros((1, 1), F32)]
        for k in range(K - 1):
            off.append(off[-1] + cnt[k:k + 1])
        r_i = lax.broadcasted_iota(jnp.int32, (ch, T), 0)
        c_i = lax.broadcasted_iota(jnp.int32, (ch, T), 1)
        lane = lax.broadcasted_iota(jnp.int32, (2 * K, T), 1)
        run = jnp.concatenate([grp, jnp.zeros_like(grp)], axis=0)
        sh = 1
        while sh < T:
            run = run + jnp.where(lane >= sh, pltpu.roll(run, sh, 1), 0.0)
            sh *= 2
        pos_row = jnp.zeros((1, T), F32)
        for k in range(K):
            pos_row = pos_row + grp[k:k + 1, :] * (off[k] + run[k:k + 1, :] - 1.0)
        pos_t = jnp.broadcast_to(pos_row, (2 * K, T))
        pos_col = [pos_t[:, c * ch:(c + 1) * ch].T[:, 0:1] for c in range(T // ch)]
        for c in range(T // ch):
            c0 = c * ch
            perm = jnp.where((r_i + c0).astype(F32) == pos_row, 1.0, 0.0).astype(BF16)
            xs_scr[c0:c0 + ch, :] = _dot(perm, xb).astype(BF16)
            g3s = _dot_nt(perm, g3)
            gs_scr[c0:c0 + ch, :] = g3s[:, :NE] + g3s[:, NE:2 * NE] + g3s[:, 2 * NE:]
            pt_scr[c0:c0 + ch, :] = jnp.where(c_i.astype(F32) == pos_col[c], 1.0, 0.0).astype(BF16)
        xs_scr[T:, :] = jnp.zeros((cap, xs_scr.shape[1]), BF16)
        gs_scr[T:, :] = jnp.zeros((cap, gs_scr.shape[1]), F32)
        ys_scr[...] = jnp.zeros_like(ys_scr)

    ne = w1_ref.shape[0]
    k = (e * ne) // EXPERTS_PER_GROUP
    off = seg_ref[i, k]
    cnt = seg_ref[i, K + k]
    start0 = (off // ROW_ALIGN) * ROW_ALIGN
    n_pass = (off + cnt - start0 + cap - 1) // cap

    def one_pass(n, c):
        start = pl.multiple_of(start0 + n * cap, ROW_ALIGN)
        xc = xs_scr[pl.ds(start, cap), :]
        gsc = gs_scr[pl.ds(start, cap), :]
        lane = lax.broadcasted_iota(jnp.int32, gsc.shape, 1)
        y = None
        for x in range(ne):
            a = _dot(xc, w1_ref[x])
            hm = (a * _sigmoid(a)) * _dot(xc, w3_ref[x])
            gcol = jnp.sum(jnp.where(lane == e * ne + x, gsc, 0.0), axis=1, keepdims=True)
            yx = gcol * _dot(hm.astype(BF16), w2_ref[x])
            y = yx if y is None else y + yx
        ys_scr[pl.ds(start, cap), :] += y
        return c

    lax.fori_loop(0, n_pass, one_pass, 0)

    @pl.when(e == pl.num_programs(1) - 1)
    def _():
        y = _dot(pt_scr[...], ys_scr[0:T, :].astype(BF16))
        o_ref[...] = _layer_norm(ALPHA * x_ref[...] + y, g_ref[...], b_ref[...])


def _moe(x, gates_t, grp_t, w1, w3, w2, g, b):
    N, D = x.shape
    T = min(MOE_TILE, N)
    K = N_GROUPS
    cnt = grp_t.reshape(K, N // T, T).sum(-1).astype(jnp.int32).T
    seg = jnp.concatenate([jnp.cumsum(cnt, axis=1) - cnt, cnt], axis=1)
    fix = lambda i, e, s: (0, 0)
    ne = MOE_EXPERTS_PER_STEP
    rows = T + MOE_PASS_ROWS
    grid_spec = pltpu.PrefetchScalarGridSpec(
        num_scalar_prefetch=1,
        grid=(N // T, N_EXPERTS // ne),
        in_specs=[pl.BlockSpec((T, D), lambda i, e, s: (i, 0)),
                  pl.BlockSpec((N_EXPERTS, T), lambda i, e, s: (0, i)),
                  pl.BlockSpec((K, T), lambda i, e, s: (0, i)),
                  pl.BlockSpec((ne, D, D_EXPERT), lambda i, e, s: (e, 0, 0)),
                  pl.BlockSpec((ne, D, D_EXPERT), lambda i, e, s: (e, 0, 0)),
                  pl.BlockSpec((ne, D_EXPERT, D), lambda i, e, s: (e, 0, 0)),
                  pl.BlockSpec((1, D), fix), pl.BlockSpec((1, D), fix)],
        out_specs=pl.BlockSpec((T, D), lambda i, e, s: (i, 0)),
        scratch_shapes=[pltpu.VMEM((rows, D), BF16), pltpu.VMEM((rows, D), F32),
                        pltpu.VMEM((T, T), BF16), pltpu.VMEM((rows, N_EXPERTS), F32)])
    return pl.pallas_call(
        _moe_body,
        grid_spec=grid_spec,
        out_shape=jax.ShapeDtypeStruct((N, D), F32),
        compiler_params=_params("parallel", "arbitrary"),
        name="moe",
    )(seg, x, gates_t, grp_t, w1, w3, w2, g, b)


NSA_KVW = NSA_KV_HEADS * HEAD_DIM
NSA_ROPE_W = NSA_HEADS * HEAD_DIM + 2 * NSA_KVW


def _rot_cols(w):
    K, n = w.shape
    w3 = w.reshape(K, n // HEAD_DIM, 2, HEAD_DIM // 2)
    return jnp.stack([-w3[:, :, 1], w3[:, :, 0]], axis=2).reshape(K, n)


def _inproj1_weights(w_in):
    H, G, d = NSA_HEADS, NSA_KV_HEADS, HEAD_DIM
    cuts = [H * d + i * NSA_KVW for i in range(7)]
    q, kc, vc, ks, vs, kw, vw, gate = jnp.split(w_in.astype(F32), cuts, axis=1)
    w_all = jnp.concatenate([q, ks, kw, kc, vc], axis=1).astype(BF16)
    w_t = jnp.concatenate([vs, vw, gate], axis=1).T.astype(BF16)
    return w_all, w_t


def _rope_tables(pos):
    inv = ROPE_THETA ** (-jnp.arange(0, HEAD_DIM, 2, dtype=F32) / HEAD_DIM)
    ang = pos.astype(F32)[:, None] * inv[None, :]
    c, s = jnp.cos(ang), jnp.sin(ang)
    return jnp.concatenate([c, c], axis=-1), jnp.concatenate([s, s], axis=-1)


def _rope_roll_tables(pos):
    inv = ROPE_THETA ** (-jnp.arange(0, HEAD_DIM, 2, dtype=F32) / HEAD_DIM)
    ang = pos.astype(F32)[:, None] * inv[None, :]
    c, s, z = jnp.cos(ang), jnp.sin(ang), jnp.zeros_like(ang)
    two = lambda a, b: jnp.concatenate([a, b, a, b], axis=-1)
    return two(c, c), two(-s, z), two(z, s)


V_ROWS = HEAD_DIM + 16


def _inproj1_body(x_ref, w_ref, wt_ref, cos_ref, sinlo_ref, sinhi_ref, q_ref, ks_ref, kw_ref, kc_ref, vc_ref,
                  vst_ref, vwt_ref, gate_ref):
    xb = x_ref[0].astype(BF16)
    tl = xb.shape[0]
    d = HEAD_DIM
    p = _dot(xb, w_ref[...])
    pt = _dot_nt(wt_ref[...], xb)
    ones = jnp.ones((V_ROWS - d, tl), BF16)
    for n, ref in enumerate((vst_ref, vwt_ref)):
        for g in range(NSA_KV_HEADS):
            r0 = n * NSA_KVW + g * d
            ref[0, g, 0:d, :] = pt[r0:r0 + d, :].astype(BF16)
            ref[0, g, d:V_ROWS, :] = ones
    gate_ref[0] = pt[2 * NSA_KVW:, :]
    cos = cos_ref[...]
    sin_lo = sinlo_ref[...]
    sin_hi = sinhi_ref[...]
    roped = []
    for c in range(NSA_ROPE_W // 128):
        xc = p[:, c * 128:(c + 1) * 128]
        roped.append(xc * cos + pltpu.roll(xc, 96, 1) * sin_lo + pltpu.roll(xc, 32, 1) * sin_hi)

    def head(chunks, h):
        blk = chunks[h // 2]
        return blk[:, (h % 2) * d:(h % 2 + 1) * d]

    for h in range(NSA_HEADS):
        q_ref[0, h] = (head(roped, h) * LOG2_Q_SCALE).astype(BF16)
    nq = NSA_HEADS // 2
    pos = pl.program_id(1) * tl + lax.broadcasted_iota(jnp.int32, (tl, 128 - d), 0)
    block_onehot = jnp.where(pos // SEL_LEN == lax.broadcasted_iota(jnp.int32, (tl, 128 - d), 1), 1.0, 0.0)
    for g in range(NSA_KV_HEADS):
        ks_ref[0, g] = jnp.concatenate([head(roped[nq:], g), block_onehot], axis=1).astype(BF16)
        kw_ref[0, g] = head(roped[nq + 2:], g).astype(BF16)
    for n, ref in enumerate((kc_ref, vc_ref)):
        for g in range(NSA_KV_HEADS):
            c0 = NSA_ROPE_W + n * NSA_KVW + g * d
            ref[0, g] = p[:, c0:c0 + d]


def _inproj1(x, w_all, w_t, tl=512):
    B, L, D = x.shape
    tl = min(tl, L)
    G = NSA_KV_HEADS
    n_gate = w_t.shape[0] - 2 * NSA_KVW
    tables = _rope_roll_tables(jnp.arange(L))
    qh = jax.ShapeDtypeStruct((B, NSA_HEADS, L, HEAD_DIM), BF16)
    kvh = jax.ShapeDtypeStruct((B, G, L, HEAD_DIM), BF16)
    vth = jax.ShapeDtypeStruct((B, G, V_ROWS, L), BF16)
    q_spec = pl.BlockSpec((1, NSA_HEADS, tl, HEAD_DIM), lambda b, i: (b, 0, i, 0))
    kv_spec = pl.BlockSpec((1, G, tl, HEAD_DIM), lambda b, i: (b, 0, i, 0))
    ksel_spec = pl.BlockSpec((1, G, tl, 128), lambda b, i: (b, 0, i, 0))
    vt_spec = pl.BlockSpec((1, G, V_ROWS, tl), lambda b, i: (b, 0, 0, i))
    tab_spec = pl.BlockSpec((tl, 128), lambda b, i: (i, 0))
    return pl.pallas_call(
        _inproj1_body,
        grid=(B, L // tl),
        in_specs=[pl.BlockSpec((1, tl, D), lambda b, i: (b, i, 0)),
                  pl.BlockSpec(w_all.shape, lambda b, i: (0, 0)),
                  pl.BlockSpec(w_t.shape, lambda b, i: (0, 0)),
                  tab_spec, tab_spec, tab_spec],
        out_specs=([q_spec, ksel_spec] + [kv_spec] * 3 + [vt_spec] * 2
                   + [pl.BlockSpec((1, n_gate, tl), lambda b, i: (b, 0, i))]),
        out_shape=([qh, jax.ShapeDtypeStruct((B, G, L, 128), BF16), kvh]
                   + [jax.ShapeDtypeStruct((B, G, L, HEAD_DIM), F32)] * 2 + [vth] * 2
                   + [jax.ShapeDtypeStruct((B, n_gate, L), F32)]),
        compiler_params=_params("parallel", "parallel"),
        name="inproj1",
    )(x, w_all, w_t, *tables)


def _compress_body(kc_ref, vc_ref, posk_ref, posv_ref, w1k_ref, w1v_ref, w2k_ref, w2kr_ref, w2v_ref,
                   cos_ref, sin_ref, kcmp_ref, vcmp_ref):
    def hidden(a_ref, pos_ref, w1_ref):
        nrow = a_ref.shape[2] // CMP_STRIDE
        a = jnp.concatenate([a_ref[0, 0, pl.ds(l, nrow, stride=CMP_STRIDE), :] for l in range(CMP_STRIDE)],
                            axis=1).astype(F32)
        lo = _dot((a + pos_ref[0:1, :]).astype(BF16), w1_ref[0])
        hi = _dot((a + pos_ref[1:2, :]).astype(BF16), w1_ref[1])
        hi_next = pltpu.roll(hi, nrow - 1, 0)
        return _gelu(lo + hi_next).astype(BF16)

    hk = hidden(kc_ref, posk_ref, w1k_ref)
    kcmp = _dot(hk, w2k_ref[...]) * cos_ref[...] + _dot(hk, w2kr_ref[...]) * sin_ref[...]
    kcmp_ref[0, 0] = kcmp.astype(BF16)
    hv = hidden(vc_ref, posv_ref, w1v_ref)
    vcmp_ref[0, 0] = _dot_nt(w2v_ref[...], hv).astype(BF16)


def _compress(kc, vc, pos_k, w1_k, w2_k, pos_v, w1_v, w2_v):
    B, G, L, d = kc.shape
    half = CMP_STRIDE * d
    nb = L // CMP_STRIDE
    posk = pos_k.astype(F32).reshape(2, half)
    posv = pos_v.astype(F32).reshape(2, half)
    w1k = w1_k.astype(BF16).reshape(2, half, CMP_HIDDEN)
    w1v = w1_v.astype(BF16).reshape(2, half, CMP_HIDDEN)
    w2k = w2_k.astype(F32)
    cos, sin = _rope_tables(jnp.arange(nb) * CMP_STRIDE + CMP_LEN - 1)
    blk = pl.BlockSpec((1, 1, L, d), lambda b, g: (b, g, 0, 0))
    out = pl.BlockSpec((1, 1, nb, d), lambda b, g: (b, g, 0, 0))
    out_t = pl.BlockSpec((1, 1, d, nb), lambda b, g: (b, g, 0, 0))
    fix2 = lambda b, g: (0, 0)
    fix3 = lambda b, g: (0, 0, 0)
    return pl.pallas_call(
        _compress_body,
        grid=(B, G),
        in_specs=[blk, blk, pl.BlockSpec((2, half), fix2), pl.BlockSpec((2, half), fix2),
                  pl.BlockSpec((2, half, CMP_HIDDEN), fix3), pl.BlockSpec((2, half, CMP_HIDDEN), fix3),
                  pl.BlockSpec((CMP_HIDDEN, d), fix2), pl.BlockSpec((CMP_HIDDEN, d), fix2),
                  pl.BlockSpec((d, CMP_HIDDEN), fix2), pl.BlockSpec((nb, d), fix2), pl.BlockSpec((nb, d), fix2)],
        out_specs=[out, out_t],
        out_shape=[jax.ShapeDtypeStruct((B, G, nb, d), BF16), jax.ShapeDtypeStruct((B, G, d, nb), BF16)],
        compiler_params=_params("parallel", "parallel"),
        name="compress",
    )(kc, vc, posk, posv, w1k, w1v, w2k.astype(BF16), _rot_cols(w2k).astype(BF16), w2_v.T.astype(BF16), cos, sin)


NSA_KBLOCK = 256
NSA_GROUPS_PER_STEP = 4
NSA_HEADS_PER_CHAIN = 4
NSA_SCORE_LOOKAHEAD = 3


def _nsa_body(q_ref, kcmp_ref, vcmpt_ref, ks_ref, vst_ref, kw_ref, vwt_ref, gate_ref, o_ref, qsel_scr, *, seq):
    i = pl.program_id(2)
    tq = q_ref.shape[2]
    gb = kcmp_ref.shape[1]
    tk = min(NSA_KBLOCK, seq)
    R, d = NSA_REP, HEAD_DIM
    nq = R * tq
    nb = seq // SEL_LEN
    mc = seq // CMP_STRIDE
    t_row = i * tq + lax.broadcasted_iota(jnp.int32, (1, tq), 1)
    t_all = jnp.concatenate([t_row] * R, axis=1)
    m_col = lax.broadcasted_iota(jnp.int32, (mc, 1), 0)
    m_row = lax.broadcasted_iota(jnp.int32, (1, mc), 1)
    n_col = lax.broadcasted_iota(jnp.int32, (nb, 1), 0)

    def q_rows(g):
        return q_ref[0, g * R:(g + 1) * R].reshape(nq, d)

    valid_c = (m_col * CMP_STRIDE + (CMP_LEN - 1)) <= t_all
    ovl = jnp.logical_and(m_row * CMP_STRIDE < (n_col + 1) * SEL_LEN,
                          m_row * CMP_STRIDE + CMP_LEN > n_col * SEL_LEN)
    ovl = jnp.where(ovl, 1.0, 0.0).astype(BF16)
    cur = t_row // SEL_LEN
    forced = jnp.logical_or(n_col == 0, jnp.logical_or(n_col == cur, n_col == cur - 1))
    bonus = jnp.where(forced, FORCE_BONUS, 0.0)
    valid_s = n_col * SEL_LEN <= t_row
    o_c = []
    ties = [jnp.where(n_col > n2, 1.0, 0.0) for n2 in range(nb)]
    cmp_scores = [_dot_nt(kcmp_ref[0, g], q_rows(g)) for g in range(gb)]
    for g in range(gb):
        s = jnp.where(valid_c, cmp_scores[g], NEG)
        e = jnp.exp2(s - jnp.maximum(jnp.max(s, axis=0, keepdims=True), 0.5 * NEG))
        den = jnp.sum(e, axis=0, keepdims=True)
        p = e / jnp.where(den > 0.0, den, 1.0)
        o_c.append(_dot(vcmpt_ref[0, g], p.astype(BF16)))
        psum = p[:, 0:tq]
        for r in range(1, R):
            psum = psum + p[:, r * tq:(r + 1) * tq]
        ph, plo = _split_bf16(psum)
        score = jnp.where(valid_s, _dot(ovl, ph) + _dot(ovl, plo) + bonus, NEG)
        rank = jnp.zeros((nb, tq), F32)
        for n2 in range(nb):
            other = score[n2:n2 + 1, :]
            rank = rank + jnp.where(other > score, 1.0, jnp.where(other == score, ties[n2], 0.0))
        pen = jnp.where(rank < float(SEL_TOP), 0.0, NEG).T.astype(BF16)
        pen = jnp.concatenate([pen, jnp.zeros((tq, 128 - d - nb), BF16)], axis=1)
        qsel_scr[g] = jnp.concatenate([q_rows(g), jnp.concatenate([pen] * R, axis=0)], axis=1)

    k_col = lax.broadcasted_iota(jnp.int32, (tk, 1), 0)
    hp = NSA_HEADS_PER_CHAIN
    j_hi = (i * tq + tq - 1) // tk + 1

    def sweep(q_fn, k_ref, vt_ref, j_lo, bias_fn, bias_every_block):
        chains = [(g, h0) for g in range(gb) for h0 in range(0, R, hp)]

        def body(j, state, with_bias):
            k0 = pl.multiple_of(j * tk, tk)
            bias = jnp.concatenate([bias_fn(k0 + k_col)] * hp, axis=1) if with_bias else None

            def scores(c):
                g, h0 = chains[c]
                sc = _dot_nt(k_ref[0, g, pl.ds(k0, tk), :], q_fn(g, h0))
                return sc + bias if with_bias else sc

            new_state = []
            ahead = NSA_SCORE_LOOKAHEAD
            pending = [scores(c) for c in range(min(ahead, len(chains)))]
            for c, (m_run, acc) in enumerate(state):
                sc = pending.pop(0)
                if c + ahead < len(chains):
                    pending.append(scores(c + ahead))
                vtb = vt_ref[0, chains[c][0], :, pl.ds(k0, tk)]
                m_new = jnp.maximum(m_run, jnp.max(sc, axis=0, keepdims=True))
                pr = jnp.exp2(sc - m_new)
                new_state.append((m_new, jnp.exp2(m_run - m_new) * acc + _dot(vtb, pr.astype(BF16))))
            return tuple(new_state)

        state = tuple((jnp.full((1, hp * tq), NEG, F32), jnp.zeros((V_ROWS, hp * tq), F32)) for _ in chains)
        state = lax.fori_loop(j_lo, j_hi - 1, functools.partial(body, with_bias=bias_every_block), state)
        state = body(j_hi - 1, state, True)
        outs = []
        for g in range(gb):
            acc = jnp.concatenate([a for (cg, _), (_, a) in zip(chains, state) if cg == g], axis=1)
            outs.append(acc[0:d] / acc[d:d + 1])
        return outs

    def causal_bias(kpos):
        return jnp.where(kpos <= t_row, 0.0, NEG)

    o_s = sweep(lambda g, h0: qsel_scr[g, h0 * tq:(h0 + hp) * tq, :], ks_ref, vst_ref, 0, causal_bias, False)

    def win_bias(kpos):
        return jnp.where(jnp.logical_and(kpos <= t_row, kpos > t_row - WINDOW), 0.0, NEG)

    o_w = sweep(lambda g, h0: q_ref[0, g * R + h0:g * R + h0 + hp].reshape(hp * tq, d), kw_ref, vwt_ref,
                jnp.maximum(i * tq - (WINDOW - 1), 0) // tk, win_bias, True)

    outs = []
    for g in range(gb):
        row0 = (pl.program_id(1) * gb + g) * 3 * R

        def gate(branch):
            rows = [gate_ref[0, pl.ds(row0 + 3 * r + branch, 1), :] for r in range(R)]
            return _sigmoid(jnp.concatenate(rows, axis=1))

        o_t = gate(0) * o_c[g] + gate(1) * o_s[g] + gate(2) * o_w[g]
        for r in range(R):
            outs.append(o_t[:, r * tq:(r + 1) * tq].T)
    o_ref[0] = jnp.concatenate(outs, axis=-1).astype(o_ref.dtype)


def _nsa_attention(q, kcmp, vcmpt, ks, vst, kw, vwt, gate):
    B, H, L, d = q.shape
    G, R = NSA_KV_HEADS, NSA_REP
    gb = NSA_GROUPS_PER_STEP
    tq = min(ATT_BLOCK, L)
    mc = L // CMP_STRIDE
    k_spec = pl.BlockSpec((1, gb, L, d), lambda b, g, i: (b, g, 0, 0))
    ksel_spec = pl.BlockSpec((1, gb, L, 128), lambda b, g, i: (b, g, 0, 0))
    vt_spec = pl.BlockSpec((1, gb, V_ROWS, L), lambda b, g, i: (b, g, 0, 0))
    return pl.pallas_call(
        functools.partial(_nsa_body, seq=L),
        grid=(B, G // gb, L // tq),
        in_specs=[pl.BlockSpec((1, gb * R, tq, d), lambda b, g, i: (b, g, i, 0)),
                  pl.BlockSpec((1, gb, mc, d), lambda b, g, i: (b, g, 0, 0)),
                  pl.BlockSpec((1, gb, d, mc), lambda b, g, i: (b, g, 0, 0)),
                  ksel_spec, vt_spec, k_spec, vt_spec,
                  pl.BlockSpec((1, gate.shape[1], tq), lambda b, g, i: (b, 0, i))],
        out_specs=pl.BlockSpec((1, tq, gb * R * d), lambda b, g, i: (b, i, g)),
        out_shape=jax.ShapeDtypeStruct((B, L, H * d), BF16),
        scratch_shapes=[pltpu.VMEM((gb, R * tq, 128), BF16)],
        compiler_params=_params("parallel", "parallel", "arbitrary"),
        name="nsa_attention",
    )(q, kcmp, vcmpt, ks, vst, kw, vwt, gate)


def kernel(x, w_in_0, ssm_lam_re, ssm_lam_im, ssm_log_dt, ssm_b_re, ssm_b_im, ssm_c_re, ssm_c_im, ssm_d, w_glu, w_out_0, ln_mix_g_0, ln_mix_b_0, ln_ffn_g_0, ln_ffn_b_0, w1_0, w3_0, w2_0, w_in_1, cmp_pos_k, cmp_w1_k, cmp_w2_k, cmp_pos_v, cmp_w1_v, cmp_w2_v, w_out_1, ln_mix_g_1, ln_mix_b_1, ln_ffn_g_1, ln_ffn_b_1, w1_1, w3_1, w2_1, w_router, b_router):
    B, L, D = x.shape
    N = B * L
    vec = lambda a: a.astype(F32).reshape(1, D)

    def ffn(h, w1, w3, w2, g, b):
        gates_t, grp_t = _router(h, w_router, b_router)
        return _moe(h, gates_t, grp_t, w1.astype(BF16), w3.astype(BF16), w2.astype(BF16), vec(g), vec(b))

    q, k, v, u = _inproj0(x, w_in_0.astype(BF16))
    o_a = _sb_attention(q, k, v)
    y = _ssm(u, _ssm_tables(ssm_lam_re, ssm_lam_im, ssm_log_dt, ssm_b_re, ssm_b_im, ssm_c_re, ssm_c_im, ssm_d), B)
    w_out_0b = w_out_0.astype(BF16)
    h = _outproj0(x, o_a, y, w_glu.astype(BF16), w_out_0b[:SB_WIDTH], w_out_0b[SB_WIDTH:],
                  vec(ln_mix_g_0), vec(ln_mix_b_0)).reshape(N, D)
    h = ffn(h, w1_0, w3_0, w2_0, ln_ffn_g_0, ln_ffn_b_0)

    q, ks, kw, kc, vc, vst, vwt, gate = _inproj1(h.reshape(B, L, D), *_inproj1_weights(w_in_1))
    kcmp, vcmpt = _compress(kc, vc, cmp_pos_k, cmp_w1_k, cmp_w2_k, cmp_pos_v, cmp_w1_v, cmp_w2_v)
    o = _nsa_attention(q, kcmp, vcmpt, ks, vst, kw, vwt, gate).reshape(N, NSA_HEADS * HEAD_DIM)
    h = _outproj1(h, o, w_out_1.astype(BF16), vec(ln_mix_g_1), vec(ln_mix_b_1))
    h = ffn(h, w1_1, w3_1, w2_1, ln_ffn_g_1, ln_ffn_b_1)
    return h.reshape(B, L, D)
```

```python
import functools
import math

import numpy as np
import jax
import jax.numpy as jnp
from jax import lax
from jax.experimental import pallas as pl
from jax.experimental.pallas import tpu as pltpu

F32 = jnp.float32
BF16 = jnp.bfloat16

D_MODEL = 1024
DEPTH = 2
SB_HEADS = 8
HEAD_DIM = 64
SB_WIDTH = SB_HEADS * HEAD_DIM
SSM_WIDTH = D_MODEL - SB_WIDTH
SSM_GROUP = 16
SSM_GROUPS = SSM_WIDTH // SSM_GROUP
SSM_STATE = 64
SSM_CHUNK = 16
SSM_LANE_TILES = SSM_WIDTH // 128
SSM_TILE_GROUPS = 128 // SSM_GROUP
NSA_HEADS = 16
NSA_KV_HEADS = 4
NSA_REP = NSA_HEADS // NSA_KV_HEADS
CMP_LEN = 32
CMP_STRIDE = 16
CMP_HIDDEN = 256
SEL_LEN = 64
SEL_TOP = 8
WINDOW = 512
ROPE_THETA = 10000.0
FORCE_BONUS = 1e4
NEG = -1e30
N_EXPERTS = 16
N_GROUPS = 4
EXPERTS_PER_GROUP = N_EXPERTS // N_GROUPS
D_EXPERT = 512
ALPHA = (2 * DEPTH) ** 0.25
LN_EPS = 1e-5
ATTN_SCALE = HEAD_DIM ** -0.5
LOG2_Q_SCALE = ATTN_SCALE * math.log2(math.e)
ATT_BLOCK = 256
GELU_C = math.sqrt(2.0 / math.pi)


def _params(*sem):
    return pltpu.CompilerParams(dimension_semantics=sem, vmem_limit_bytes=56 * 1024 * 1024)


def _sigmoid(x):
    return 1.0 / (1.0 + jnp.exp(-x))


def _gelu(x):
    return 0.5 * x * (1.0 + jnp.tanh(GELU_C * (x + 0.044715 * (x * x * x))))


def _layer_norm(r, g, b):
    mu = jnp.mean(r, axis=-1, keepdims=True)
    d = r - mu
    var = jnp.mean(d * d, axis=-1, keepdims=True)
    return d * lax.rsqrt(var + LN_EPS) * g + b


def _dot(a, b):
    return jnp.dot(a, b, preferred_element_type=F32)


def _dot_nt(a, b):
    return lax.dot_general(a, b, (((1,), (1,)), ((), ())), preferred_element_type=F32)


def _split_bf16(x):
    hi = x.astype(BF16)
    lo = (x - hi.astype(F32)).astype(BF16)
    return hi, lo


def _inproj0_body(x_ref, w_ref, q_ref, k_ref, v_ref, u_ref, slab_scr):
    nb, tl, D = x_ref.shape
    p = _dot(x_ref[...].reshape(nb * tl, D).astype(BF16), w_ref[...])
    for h in range(SB_HEADS):
        c = h * HEAD_DIM
        q_ref[:, h] = (p[:, c:c + HEAD_DIM] * LOG2_Q_SCALE).astype(BF16).reshape(nb, tl, HEAD_DIM)
        k_ref[:, h] = p[:, SB_WIDTH + c:SB_WIDTH + c + HEAD_DIM].astype(BF16).reshape(nb, tl, HEAD_DIM)
        v_ref[:, h] = p[:, 2 * SB_WIDTH + c:2 * SB_WIDTH + c + HEAD_DIM].astype(BF16).reshape(nb, tl, HEAD_DIM)
    T = SSM_CHUNK
    for v in range(SSM_LANE_TILES):
        c0 = 3 * SB_WIDTH + v * 128
        slab_scr[...] = p[:, c0:c0 + 128]
        for c in range(tl // T):
            for s in range(T):
                u_ref[v, c * nb:(c + 1) * nb, s * 128:(s + 1) * 128] = slab_scr[pl.ds(c * T + s, nb, stride=tl), :]


def _inproj0(x, w_bf16, tl=64):
    B, L, D = x.shape
    tl = min(tl, L)
    nout = w_bf16.shape[1]
    T = SSM_CHUNK
    head = jax.ShapeDtypeStruct((B, SB_HEADS, L, HEAD_DIM), BF16)
    head_spec = pl.BlockSpec((B, SB_HEADS, tl, HEAD_DIM), lambda i: (0, 0, i, 0))
    return pl.pallas_call(
        _inproj0_body,
        grid=(L // tl,),
        in_specs=[pl.BlockSpec((B, tl, D), lambda i: (0, i, 0)),
                  pl.BlockSpec((D, nout), lambda i: (0, 0))],
        out_specs=[head_spec, head_spec, head_spec,
                   pl.BlockSpec((SSM_LANE_TILES, tl // T * B, T * 128), lambda i: (0, i, 0))],
        out_shape=[head, head, head,
                   jax.ShapeDtypeStruct((SSM_LANE_TILES, L // T * B, T * 128), F32)],
        scratch_shapes=[pltpu.VMEM((B * tl, 128), F32)],
        compiler_params=_params("parallel"),
        name="inproj0",
    )(x, w_bf16)


SB_BLOCK = 256
SB_HEADS_PER_STEP = 8
EXP2_UNDERFLOW = -151.0


def _sb_attn_body(q_ref, k_ref, v_ref, o_ref, acc_scr, cs_scr):
    i = pl.program_id(2)
    t = q_ref.shape[2]
    nh = q_ref.shape[1]
    row = lax.broadcasted_iota(jnp.int32, (t, t), 0)
    col = lax.broadcasted_iota(jnp.int32, (t, t), 1)
    suffix = jnp.where(row > col, 1.0, 0.0).astype(BF16)
    below = col < row

    def logits(hh, k0):
        return _dot_nt(q_ref[0, hh], k_ref[0, hh, pl.ds(k0, t), :])

    def alive(css):
        return (jnp.max(functools.reduce(jnp.maximum, css)) > EXP2_UNDERFLOW).astype(jnp.int32)

    def all_heads(k0, diagonal):
        zs = [logits(hh, k0) for hh in range(nh)]
        sps, l1s = [], []
        for z in zs:
            sp = jnp.maximum(z, 0.0) + jnp.log2(1.0 + jnp.exp2(-jnp.abs(z)))
            sps.append(sp)
            l1s.append(jnp.where(below, -sp, 0.0) if diagonal else -sp)
        within_all = _dot(jnp.concatenate([l1.astype(BF16) for l1 in l1s], axis=0), suffix)
        css = []
        for hh in range(nh):
            vb = v_ref[0, hh, pl.ds(k0, t), :]
            base = zs[hh] - sps[hh] + within_all[hh * t:(hh + 1) * t]
            if diagonal:
                w = jnp.where(below, jnp.exp2(base), 0.0)
                acc_scr[hh] = _dot(w.astype(BF16), vb)
                cs = jnp.sum(l1s[hh], axis=1, keepdims=True)
            else:
                w = jnp.exp2(base + cs_scr[hh])
                acc_scr[hh] += _dot(w.astype(BF16), vb)
                cs = cs_scr[hh] + jnp.sum(l1s[hh], axis=1, keepdims=True)
            cs_scr[hh] = cs
            css.append(cs)
        return alive(css)

    first = all_heads(pl.multiple_of(i * t, t), True)

    def cond(c):
        return jnp.logical_and(c[0] <= i, c[1] > 0)

    def body(c):
        return c[0] + 1, all_heads(pl.multiple_of((i - c[0]) * t, t), False)

    lax.while_loop(cond, body, (jnp.int32(1), first))
    o_ref[0] = jnp.concatenate([acc_scr[hh] for hh in range(nh)], axis=-1).astype(o_ref.dtype)


def _sb_attention(q, k, v):
    B, H, L, d = q.shape
    t = min(SB_BLOCK, L)
    nh = SB_HEADS_PER_STEP
    return pl.pallas_call(
        _sb_attn_body,
        grid=(B, H // nh, L // t),
        in_specs=[pl.BlockSpec((1, nh, t, d), lambda b, h, i: (b, h, i, 0)),
                  pl.BlockSpec((1, nh, L, d), lambda b, h, i: (b, h, 0, 0)),
                  pl.BlockSpec((1, nh, L, d), lambda b, h, i: (b, h, 0, 0))],
        out_specs=pl.BlockSpec((1, t, nh * d), lambda b, h, i: (b, i, h)),
        out_shape=jax.ShapeDtypeStruct((B, L, H * d), BF16),
        scratch_shapes=[pltpu.VMEM((nh, t, d), F32), pltpu.VMEM((nh, t, 1), F32)],
        compiler_params=_params("parallel", "parallel", "arbitrary"),
        name="sb_attention",
    )(q, k, v)


def _ssm_tables(lam_re, lam_im, log_dt, b_re, b_im, c_re, c_im, d_skip):
    T, G, P, H = SSM_CHUNK, SSM_GROUPS, SSM_STATE, SSM_GROUP
    hp = lax.Precision.HIGHEST
    dt = jnp.exp(log_dt.astype(F32))[:, None]
    lr = lam_re.astype(F32)
    li = lam_im.astype(F32)
    mag = jnp.exp(lr * dt)
    a_re = mag * jnp.cos(li * dt)
    a_im = mag * jnp.sin(li * dt)
    den = lr * lr + li * li
    nr = a_re - 1.0
    f_re = (nr * lr + a_im * li) / den
    f_im = (a_im * lr - nr * li) / den
    br = b_re.astype(F32)
    bi = b_im.astype(F32)
    bbt_re = (f_re[..., None] * br - f_im[..., None] * bi).transpose(0, 2, 1)[:, None]
    bbt_im = (f_re[..., None] * bi + f_im[..., None] * br).transpose(0, 2, 1)[:, None]
    cr = c_re.astype(F32)[:, None]
    ci = c_im.astype(F32)[:, None]

    def powers(k):
        k = jnp.asarray(k, F32)[None, :, None]
        pmag = jnp.exp(k * (lr * dt)[:, None, :])
        ang = k * (li * dt)[:, None, :]
        return (pmag * jnp.cos(ang))[:, :, None, :], (pmag * jnp.sin(ang))[:, :, None, :]

    def cmul(xr, xi, yr, yi, rows):
        return ((xr * yr - xi * yi).reshape(G, rows, P), (xr * yi + xi * yr).reshape(G, rows, P))

    steps = np.arange(T)
    bs_re, bs_im = cmul(*powers(-steps), bbt_re, bbt_im, T * H)
    ct_re, ct_im = cmul(*powers(steps), cr, ci, T * H)
    full = (jnp.einsum('gap,gbp->gab', bs_re, ct_re, precision=hp)
            - jnp.einsum('gap,gbp->gab', bs_im, ct_im, precision=hp))
    row_s = lax.broadcasted_iota(jnp.int32, (T * H, T * H), 0) // H
    col_t = lax.broadcasted_iota(jnp.int32, (T * H, T * H), 1) // H
    intra = jnp.where(row_s <= col_t, full, 0.0)
    st_re, st_im = cmul(*powers(T - 1 - steps), bbt_re, bbt_im, T * H)
    to_st = jnp.concatenate([st_re, st_im, st_im, st_re], axis=-1)
    c1_re, c1_im = cmul(*powers(steps + 1), cr, ci, T * H)
    from_st_t = jnp.concatenate([c1_re, -c1_im], axis=-1)
    at_re, at_im = powers([T])
    at_re, at_im = at_re[:, 0, 0], at_im[:, 0, 0]
    zeros = jnp.zeros_like(at_re)
    adv = jnp.stack([jnp.concatenate([at_re, at_re], -1),
                     jnp.concatenate([-at_im, at_im], -1),
                     jnp.concatenate([at_im, -at_im], -1)]
                    + [jnp.concatenate([zeros, zeros], -1)] * 5, axis=1)
    dvec = jnp.tile(d_skip.astype(F32).reshape(G, 1, H), (1, 1, T))
    return intra, to_st, from_st_t, adv, dvec


def _ssm_expand(tables):
    intra, to_st, from_st_t, adv, dvec = tables
    V, W, T, H, P = SSM_LANE_TILES, SSM_TILE_GROUPS, SSM_CHUNK, SSM_GROUP, SSM_STATE
    n = T * W * H
    k_in = lax.broadcasted_iota(jnp.int32, (T * H, n), 0)
    col = lax.broadcasted_iota(jnp.int32, (T * H, n), 1)
    spread_th = (k_in == (col // 128) * H + col % H).astype(F32)
    spread_q = (k_in[:2 * P, :W * 2 * P] == col[:2 * P, :W * 2 * P] % (2 * P)).astype(F32)
    row_g = (lax.broadcasted_iota(jnp.int32, (n, 1), 0) // H) % W
    st_row_g = lax.broadcasted_iota(jnp.int32, (W * 2 * P, 1), 0) // (2 * P)
    col_g_th = (lax.broadcasted_iota(jnp.int32, (1, n), 1) // H) % W
    col_g_q = lax.broadcasted_iota(jnp.int32, (1, W * 2 * P), 1) // (2 * P)

    def blockdiag(rows, spread, row_group, col_group):
        full = jnp.einsum('vrk,kc->vrc', rows, spread)
        return jnp.where(row_group == col_group, full, 0.0).astype(BF16)

    by_row = lambda t: t.reshape(V, W, T, H, T * H).transpose(0, 2, 1, 3, 4).reshape(V, n, T * H)
    m8 = blockdiag(by_row(intra), spread_th, row_g, col_g_th)
    ws8 = blockdiag(by_row(to_st)[:, :, :2 * P], spread_q, row_g, col_g_q)
    wi_full = jnp.einsum('vwkq,kc->vwqc', from_st_t.reshape(V, W, T * H, 2 * P), spread_th).reshape(V, W * 2 * P, n)
    wi8 = jnp.where(st_row_g == col_g_th, wi_full, 0.0).astype(BF16)
    adv8 = adv.reshape(V, W, 8, 2 * P).transpose(0, 2, 1, 3).reshape(V, 8, W * 2 * P)
    d8 = jnp.broadcast_to(dvec.reshape(V, 1, W, T, H)[:, :, :, 0:1], (V, 1, W, T, H))
    d8 = d8.transpose(0, 1, 3, 2, 4).reshape(V, 1, T * W * H)
    return m8, ws8, wi8, adv8, d8


def _ssm_body(u_ref, m8_ref, ws8_ref, wi8_ref, adv_ref, dvec_ref, y_ref, s_scr, xin_scr, x1_scr, x2_scr, *, bsz):
    half = x1_scr.shape[1]

    @pl.when(pl.program_id(1) == 0)
    def _():
        x1_scr[...] = jnp.zeros_like(x1_scr)
        x2_scr[...] = jnp.zeros_like(x2_scr)

    u = u_ref[0]
    ub = u.astype(BF16)
    s1 = _dot(ub, ws8_ref[0])
    lane = lax.broadcasted_iota(jnp.int32, s1.shape, 1)
    s_scr[:, :half] = s1
    s_scr[:, half:] = jnp.where(lane % (2 * SSM_STATE) < SSM_STATE,
                                pltpu.roll(s1, half - SSM_STATE, 1), pltpu.roll(s1, SSM_STATE, 1))
    a1 = adv_ref[0, 0:1, :]
    a2 = adv_ref[0, 1:2, :]
    a3 = adv_ref[0, 2:3, :]

    def step(c, carry):
        x1, x2 = carry
        r0 = pl.multiple_of(c * bsz, bsz)
        xin_scr[pl.ds(r0, bsz), :] = x1
        s = s_scr[pl.ds(r0, bsz), :]
        return (a1 * x1 + a2 * x2 + s[:, :half], a1 * x2 + a3 * x1 + s[:, half:])

    x1, x2 = lax.fori_loop(0, u.shape[0] // bsz, step, (x1_scr[...], x2_scr[...]))
    x1_scr[...] = x1
    x2_scr[...] = x2
    y_ref[0] = _dot(ub, m8_ref[0]) + _dot(xin_scr[...].astype(BF16), wi8_ref[0]) + dvec_ref[0] * u


def _ssm(us, tables, bsz, rows=256):
    m8, ws8, wi8, adv8, d8 = _ssm_expand(tables)
    V, CB, width = us.shape
    B = bsz
    half = SSM_TILE_GROUPS * 2 * SSM_STATE
    rows = min(rows, CB)
    once = dict(pipeline_mode=pl.Buffered(1))
    y = pl.pallas_call(
        functools.partial(_ssm_body, bsz=B),
        grid=(V, CB // rows),
        in_specs=[pl.BlockSpec((1, rows, width), lambda v, r: (v, r, 0)),
                  pl.BlockSpec((1, width, width), lambda v, r: (v, 0, 0), **once),
                  pl.BlockSpec((1, width, half), lambda v, r: (v, 0, 0), **once),
                  pl.BlockSpec((1, half, width), lambda v, r: (v, 0, 0), **once),
                  pl.BlockSpec((1, 8, half), lambda v, r: (v, 0, 0)),
                  pl.BlockSpec((1, 1, width), lambda v, r: (v, 0, 0))],
        out_specs=pl.BlockSpec((1, rows, width), lambda v, r: (v, r, 0)),
        out_shape=jax.ShapeDtypeStruct((V, CB, width), F32),
        scratch_shapes=[pltpu.VMEM((rows, 2 * half), F32), pltpu.VMEM((rows, half), F32),
                        pltpu.VMEM((B, half), F32), pltpu.VMEM((B, half), F32)],
        compiler_params=_params("parallel", "arbitrary"),
        name="s5_scan",
    )(us, m8, ws8, wi8, adv8, d8)
    return y


def _outproj0_body(x_ref, oa_ref, y_ref, wglu_ref, woa_ref, wob_ref, g_ref, b_ref, o_ref, slab_scr):
    nb, tl, D = x_ref.shape
    T = SSM_CHUNK
    ys = []
    for v in range(SSM_LANE_TILES):
        for c in range(tl // T):
            for s in range(T):
                slab_scr[pl.ds(c * T + s, nb, stride=tl), :] = y_ref[v, c * nb:(c + 1) * nb, s * 128:(s + 1) * 128]
        ys.append(slab_scr[...])
    h = _gelu(jnp.concatenate(ys, axis=-1))
    ob = h * _sigmoid(_dot(h.astype(BF16), wglu_ref[...]))
    m = _dot(oa_ref[...].reshape(nb * tl, SB_WIDTH), woa_ref[...]) + _dot(ob.astype(BF16), wob_ref[...])
    r = ALPHA * x_ref[...].reshape(nb * tl, D) + m
    o_ref[...] = _layer_norm(r, g_ref[...], b_ref[...]).reshape(nb, tl, D)


def _outproj0(x, oa, ys, wglu, woa, wob, g, b, tl=64):
    B, L, D = x.shape
    tl = min(tl, L)
    T = SSM_CHUNK
    row = lambda i: (0, i, 0)
    fix = lambda i: (0, 0)
    return pl.pallas_call(
        _outproj0_body,
        grid=(L // tl,),
        in_specs=[pl.BlockSpec((B, tl, D), row), pl.BlockSpec((B, tl, SB_WIDTH), row),
                  pl.BlockSpec((SSM_LANE_TILES, tl // T * B, T * 128), row),
                  pl.BlockSpec((SSM_WIDTH, SSM_WIDTH), fix), pl.BlockSpec((SB_WIDTH, D), fix),
                  pl.BlockSpec((SSM_WIDTH, D), fix), pl.BlockSpec((1, D), fix), pl.BlockSpec((1, D), fix)],
        out_specs=pl.BlockSpec((B, tl, D), row),
        out_shape=jax.ShapeDtypeStruct((B, L, D), F32),
        scratch_shapes=[pltpu.VMEM((B * tl, 128), F32)],
        compiler_params=_params("parallel"),
        name="outproj0",
    )(x, oa, ys, wglu, woa, wob, g, b)


def _outproj1_body(x_ref, o_ref_in, w_ref, g_ref, b_ref, o_ref):
    m = _dot(o_ref_in[...], w_ref[...])
    o_ref[...] = _layer_norm(ALPHA * x_ref[...] + m, g_ref[...], b_ref[...])


def _outproj1(x, o, w, g, b, tm=1024):
    N, D = x.shape
    tm = min(tm, N)
    row = lambda i: (i, 0)
    fix = lambda i: (0, 0)
    return pl.pallas_call(
        _outproj1_body,
        grid=(N // tm,),
        in_specs=[pl.BlockSpec((tm, D), row), pl.BlockSpec((tm, o.shape[1]), row),
                  pl.BlockSpec(w.shape, fix), pl.BlockSpec((1, D), fix), pl.BlockSpec((1, D), fix)],
        out_specs=pl.BlockSpec((tm, D), row),
        out_shape=jax.ShapeDtypeStruct((N, D), F32),
        compiler_params=_params("parallel"),
        name="outproj1",
    )(x, o, w, g, b)


def _router_body(x_ref, whl_ref, b_ref, g_ref, grp_ref):
    NE = N_EXPERTS
    xh, xl = _split_bf16(x_ref[...])
    whl = whl_ref[...]
    part = _dot(xh, whl)
    logits = (part[:, :NE] + part[:, NE:] + _dot(xl, whl[:, :NE])).T
    aff = _sigmoid(logits)
    sel = aff + b_ref[...]
    E, K = EXPERTS_PER_GROUP, N_GROUPS
    s = [sel[e:e + 1, :] for e in range(N_EXPERTS)]
    a = [aff[e:e + 1, :] for e in range(N_EXPERTS)]
    gscore = []
    for k in range(K):
        v = s[k * E:(k + 1) * E]
        best = None
        for x in range(E):
            for y in range(x + 1, E):
                pair = v[x] + v[y]
                best = pair if best is None else jnp.maximum(best, pair)
        gscore.append(best)
    top = functools.reduce(jnp.maximum, gscore)
    is_g = []
    taken = None
    for k in range(K):
        hit = gscore[k] == top
        if taken is None:
            is_g.append(hit)
            taken = hit
        else:
            is_g.append(jnp.logical_and(hit, jnp.logical_not(taken)))
            taken = jnp.logical_or(taken, hit)

    def pick(rows, j):
        out = rows[(K - 1) * E + j]
        for k in range(K - 2, -1, -1):
            out = jnp.where(is_g[k], rows[k * E + j], out)
        return out

    v = [pick(s, j) for j in range(E)]
    av = [pick(a, j) for j in range(E)]
    chosen = []
    for j in range(E):
        r = jnp.zeros_like(v[j])
        for j2 in range(E):
            if j2 == j:
                continue
            ahead = (v[j2] >= v[j]) if j2 < j else (v[j2] > v[j])
            r = r + jnp.where(ahead, 1.0, 0.0)
        chosen.append(r < 2.0)
    wj = [jnp.where(chosen[j], av[j], 0.0) for j in range(E)]
    tot = wj[0] + wj[1] + wj[2] + wj[3]
    gj = [w / tot for w in wj]
    rows = [jnp.where(is_g[e // E], gj[e % E], 0.0) for e in range(N_EXPERTS)]
    g_ref[...] = jnp.concatenate(rows, axis=0)
    grp_ref[...] = jnp.concatenate([jnp.where(is_g[k], 1.0, 0.0) for k in range(K)], axis=0)


def _router(x, w_router, b_router, tm=1024):
    N, D = x.shape
    tm = min(tm, N)
    whl = jnp.concatenate(_split_bf16(w_router.astype(F32)), axis=1)
    fix = lambda i: (0, 0)
    return pl.pallas_call(
        _router_body,
        grid=(N // tm,),
        in_specs=[pl.BlockSpec((tm, D), lambda i: (i, 0)), pl.BlockSpec((D, 2 * N_EXPERTS), fix),
                  pl.BlockSpec((N_EXPERTS, 1), fix)],
        out_specs=[pl.BlockSpec((N_EXPERTS, tm), lambda i: (0, i)), pl.BlockSpec((N_GROUPS, tm), lambda i: (0, i))],
        out_shape=[jax.ShapeDtypeStruct((N_EXPERTS, N), F32), jax.ShapeDtypeStruct((N_GROUPS, N), F32)],
        compiler_params=_params("parallel"),
        name="router",
    )(x, whl, b_router.astype(F32).reshape(N_EXPERTS, 1))


MOE_TILE = 1024
MOE_PASS_ROWS = 304
MOE_CHUNK = 256
MOE_EXPERTS_PER_STEP = 2
ROW_ALIGN = 16


def _moe_body(seg_ref, x_ref, gt_ref, grp_ref, w1_ref, w3_ref, w2_ref, g_ref, b_ref, o_ref,
              xs_scr, ys_scr, pt_scr, gs_scr):
    i = pl.program_id(0)
    e = pl.program_id(1)
    T = x_ref.shape[0]
    K, NE = N_GROUPS, N_EXPERTS
    cap = MOE_PASS_ROWS
    ch = min(MOE_CHUNK, T)

    @pl.when(e == 0)
    def _():
        grp = grp_ref[...]
        xb = x_ref[...].astype(BF16)
        gates = gt_ref[...]
        gh, gm = _split_bf16(gates)
        gl = (gates - gh.astype(F32) - gm.astype(F32)).astype(BF16)
        g3 = jnp.concatenate([gh, gm, gl], axis=0)
        cnt = jnp.sum(grp, axis=1, keepdims=True)
        off = [jnp.zeros((1, 1), F32)]
        for k in range(K - 1):
            off.append(off[-1] + cnt[k:k + 1])
        r_i = lax.broadcasted_iota(jnp.int32, (ch, T), 0)
        c_i = lax.broadcasted_iota(jnp.int32, (ch, T), 1)
        lane = lax.broadcasted_iota(jnp.int32, (2 * K, T), 1)
        run = jnp.concatenate([grp, jnp.zeros_like(grp)], axis=0)
        sh = 1
        while sh < T:
            run = run + jnp.where(lane >= sh, pltpu.roll(run, sh, 1), 0.0)
            sh *= 2
        pos_row = jnp.zeros((1, T), F32)
        for k in range(K):
            pos_row = pos_row + grp[k:k + 1, :] * (off[k] + run[k:k + 1, :] - 1.0)
        pos_t = jnp.broadcast_to(pos_row, (2 * K, T))
        pos_col = [pos_t[:, c * ch:(c + 1) * ch].T[:, 0:1] for c in range(T // ch)]
        for c in range(T // ch):
            c0 = c * ch
            perm = jnp.where((r_i + c0).astype(F32) == pos_row, 1.0, 0.0).astype(BF16)
            xs_scr[c0:c0 + ch, :] = _dot(perm, xb).astype(BF16)
            g3s = _dot_nt(perm, g3)
            gs_scr[c0:c0 + ch, :] = g3s[:, :NE] + g3s[:, NE:2 * NE] + g3s[:, 2 * NE:]
            pt_scr[c0:c0 + ch, :] = jnp.where(c_i.astype(F32) == pos_col[c], 1.0, 0.0).astype(BF16)
        xs_scr[T:, :] = jnp.zeros((cap, xs_scr.shape[1]), BF16)
        gs_scr[T:, :] = jnp.zeros((cap, gs_scr.shape[1]), F32)
        ys_scr[...] = jnp.zeros_like(ys_scr)

    ne = w1_ref.shape[0]
    k = (e * ne) // EXPERTS_PER_GROUP
    off = seg_ref[i, k]
    cnt = seg_ref[i, K + k]
    start0 = (off // ROW_ALIGN) * ROW_ALIGN
    n_pass = (off + cnt - start0 + cap - 1) // cap

    def one_pass(n, c):
        start = pl.multiple_of(start0 + n * cap, ROW_ALIGN)
        xc = xs_scr[pl.ds(start, cap), :]
        gsc = gs_scr[pl.ds(start, cap), :]
        lane = lax.broadcasted_iota(jnp.int32, gsc.shape, 1)
        y = None
        for x in range(ne):
            a = _dot(xc, w1_ref[x])
            hm = (a * _sigmoid(a)) * _dot(xc, w3_ref[x])
            gcol = jnp.sum(jnp.where(lane == e * ne + x, gsc, 0.0), axis=1, keepdims=True)
            yx = gcol * _dot(hm.astype(BF16), w2_ref[x])
            y = yx if y is None else y + yx
        ys_scr[pl.ds(start, cap), :] += y
        return c

    lax.fori_loop(0, n_pass, one_pass, 0)

    @pl.when(e == pl.num_programs(1) - 1)
    def _():
        y = _dot(pt_scr[...], ys_scr[0:T, :].astype(BF16))
        o_ref[...] = _layer_norm(ALPHA * x_ref[...] + y, g_ref[...], b_ref[...])


def _moe(x, gates_t, grp_t, w1, w3, w2, g, b):
    N, D = x.shape
    T = min(MOE_TILE, N)
    K = N_GROUPS
    cnt = grp_t.reshape(K, N // T, T).sum(-1).astype(jnp.int32).T
    seg = jnp.concatenate([jnp.cumsum(cnt, axis=1) - cnt, cnt], axis=1)
    fix = lambda i, e, s: (0, 0)
    ne = MOE_EXPERTS_PER_STEP
    rows = T + MOE_PASS_ROWS
    grid_spec = pltpu.PrefetchScalarGridSpec(
        num_scalar_prefetch=1,
        grid=(N // T, N_EXPERTS // ne),
        in_specs=[pl.BlockSpec((T, D), lambda i, e, s: (i, 0)),
                  pl.BlockSpec((N_EXPERTS, T), lambda i, e, s: (0, i)),
                  pl.BlockSpec((K, T), lambda i, e, s: (0, i)),
                  pl.BlockSpec((ne, D, D_EXPERT), lambda i, e, s: (e, 0, 0)),
                  pl.BlockSpec((ne, D, D_EXPERT), lambda i, e, s: (e, 0, 0)),
                  pl.BlockSpec((ne, D_EXPERT, D), lambda i, e, s: (e, 0, 0)),
                  pl.BlockSpec((1, D), fix), pl.BlockSpec((1, D), fix)],
        out_specs=pl.BlockSpec((T, D), lambda i, e, s: (i, 0)),
        scratch_shapes=[pltpu.VMEM((rows, D), BF16), pltpu.VMEM((rows, D), F32),
                        pltpu.VMEM((T, T), BF16), pltpu.VMEM((rows, N_EXPERTS), F32)])
    return pl.pallas_call(
        _moe_body,
        grid_spec=grid_spec,
        out_shape=jax.ShapeDtypeStruct((N, D), F32),
        compiler_params=_params("parallel", "arbitrary"),
        name="moe",
    )(seg, x, gates_t, grp_t, w1, w3, w2, g, b)


NSA_KVW = NSA_KV_HEADS * HEAD_DIM
NSA_ROPE_W = NSA_HEADS * HEAD_DIM + 2 * NSA_KVW


def _rot_cols(w):
    K, n = w.shape
    w3 = w.reshape(K, n // HEAD_DIM, 2, HEAD_DIM // 2)
    return jnp.stack([-w3[:, :, 1], w3[:, :, 0]], axis=2).reshape(K, n)


def _inproj1_weights(w_in):
    H, G, d = NSA_HEADS, NSA_KV_HEADS, HEAD_DIM
    cuts = [H * d + i * NSA_KVW for i in range(7)]
    q, kc, vc, ks, vs, kw, vw, gate = jnp.split(w_in.astype(F32), cuts, axis=1)
    w_all = jnp.concatenate([q, ks, kw, kc, vc], axis=1).astype(BF16)
    w_t = jnp.concatenate([vs, vw, gate], axis=1).T.astype(BF16)
    return w_all, w_t


def _rope_tables(pos):
    inv = ROPE_THETA ** (-jnp.arange(0, HEAD_DIM, 2, dtype=F32) / HEAD_DIM)
    ang = pos.astype(F32)[:, None] * inv[None, :]
    c, s = jnp.cos(ang), jnp.sin(ang)
    return jnp.concatenate([c, c], axis=-1), jnp.concatenate([s, s], axis=-1)


def _rope_roll_tables(pos):
    inv = ROPE_THETA ** (-jnp.arange(0, HEAD_DIM, 2, dtype=F32) / HEAD_DIM)
    ang = pos.astype(F32)[:, None] * inv[None, :]
    c, s, z = jnp.cos(ang), jnp.sin(ang), jnp.zeros_like(ang)
    two = lambda a, b: jnp.concatenate([a, b, a, b], axis=-1)
    return two(c, c), two(-s, z), two(z, s)


V_ROWS = HEAD_DIM + 16


def _inproj1_body(x_ref, w_ref, wt_ref, cos_ref, sinlo_ref, sinhi_ref, q_ref, ks_ref, kw_ref, kc_ref, vc_ref,
                  vst_ref, vwt_ref, gate_ref):
    xb = x_ref[0].astype(BF16)
    tl = xb.shape[0]
    d = HEAD_DIM
    p = _dot(xb, w_ref[...])
    pt = _dot_nt(wt_ref[...], xb)
    ones = jnp.ones((V_ROWS - d, tl), BF16)
    for n, ref in enumerate((vst_ref, vwt_ref)):
        for g in range(NSA_KV_HEADS):
            r0 = n * NSA_KVW + g * d
            ref[0, g, 0:d, :] = pt[r0:r0 + d, :].astype(BF16)
            ref[0, g, d:V_ROWS, :] = ones
    gate_ref[0] = pt[2 * NSA_KVW:, :]
    cos = cos_ref[...]
    sin_lo = sinlo_ref[...]
    sin_hi = sinhi_ref[...]
    roped = []
    for c in range(NSA_ROPE_W // 128):
        xc = p[:, c * 128:(c + 1) * 128]
        roped.append(xc * cos + pltpu.roll(xc, 96, 1) * sin_lo + pltpu.roll(xc, 32, 1) * sin_hi)

    def head(chunks, h):
        blk = chunks[h // 2]
        return blk[:, (h % 2) * d:(h % 2 + 1) * d]

    for h in range(NSA_HEADS):
        q_ref[0, h] = (head(roped, h) * LOG2_Q_SCALE).astype(BF16)
    nq = NSA_HEADS // 2
    pos = pl.program_id(1) * tl + lax.broadcasted_iota(jnp.int32, (tl, 128 - d), 0)
    block_onehot = jnp.where(pos // SEL_LEN == lax.broadcasted_iota(jnp.int32, (tl, 128 - d), 1), 1.0, 0.0)
    for g in range(NSA_KV_HEADS):
        ks_ref[0, g] = jnp.concatenate([head(roped[nq:], g), block_onehot], axis=1).astype(BF16)
        kw_ref[0, g] = head(roped[nq + 2:], g).astype(BF16)
    for n, ref in enumerate((kc_ref, vc_ref)):
        for g in range(NSA_KV_HEADS):
            c0 = NSA_ROPE_W + n * NSA_KVW + g * d
            ref[0, g] = p[:, c0:c0 + d]


def _inproj1(x, w_all, w_t, tl=512):
    B, L, D = x.shape
    tl = min(tl, L)
    G = NSA_KV_HEADS
    n_gate = w_t.shape[0] - 2 * NSA_KVW
    tables = _rope_roll_tables(jnp.arange(L))
    qh = jax.ShapeDtypeStruct((B, NSA_HEADS, L, HEAD_DIM), BF16)
    kvh = jax.ShapeDtypeStruct((B, G, L, HEAD_DIM), BF16)
    vth = jax.ShapeDtypeStruct((B, G, V_ROWS, L), BF16)
    q_spec = pl.BlockSpec((1, NSA_HEADS, tl, HEAD_DIM), lambda b, i: (b, 0, i, 0))
    kv_spec = pl.BlockSpec((1, G, tl, HEAD_DIM), lambda b, i: (b, 0, i, 0))
    ksel_spec = pl.BlockSpec((1, G, tl, 128), lambda b, i: (b, 0, i, 0))
    vt_spec = pl.BlockSpec((1, G, V_ROWS, tl), lambda b, i: (b, 0, 0, i))
    tab_spec = pl.BlockSpec((tl, 128), lambda b, i: (i, 0))
    return pl.pallas_call(
        _inproj1_body,
        grid=(B, L // tl),
        in_specs=[pl.BlockSpec((1, tl, D), lambda b, i: (b, i, 0)),
                  pl.BlockSpec(w_all.shape, lambda b, i: (0, 0)),
                  pl.BlockSpec(w_t.shape, lambda b, i: (0, 0)),
                  tab_spec, tab_spec, tab_spec],
        out_specs=([q_spec, ksel_spec] + [kv_spec] * 3 + [vt_spec] * 2
                   + [pl.BlockSpec((1, n_gate, tl), lambda b, i: (b, 0, i))]),
        out_shape=([qh, jax.ShapeDtypeStruct((B, G, L, 128), BF16), kvh]
                   + [jax.ShapeDtypeStruct((B, G, L, HEAD_DIM), F32)] * 2 + [vth] * 2
                   + [jax.ShapeDtypeStruct((B, n_gate, L), F32)]),
        compiler_params=_params("parallel", "parallel"),
        name="inproj1",
    )(x, w_all, w_t, *tables)


def _compress_body(kc_ref, vc_ref, posk_ref, posv_ref, w1k_ref, w1v_ref, w2k_ref, w2kr_ref, w2v_ref,
                   cos_ref, sin_ref, kcmp_ref, vcmp_ref):
    def hidden(a_ref, pos_ref, w1_ref):
        nrow = a_ref.shape[2] // CMP_STRIDE
        a = jnp.concatenate([a_ref[0, 0, pl.ds(l, nrow, stride=CMP_STRIDE), :] for l in range(CMP_STRIDE)],
                            axis=1).astype(F32)
        lo = _dot((a + pos_ref[0:1, :]).astype(BF16), w1_ref[0])
        hi = _dot((a + pos_ref[1:2, :]).astype(BF16), w1_ref[1])
        hi_next = pltpu.roll(hi, nrow - 1, 0)
        return _gelu(lo + hi_next).astype(BF16)

    hk = hidden(kc_ref, posk_ref, w1k_ref)
    kcmp = _dot(hk, w2k_ref[...]) * cos_ref[...] + _dot(hk, w2kr_ref[...]) * sin_ref[...]
    kcmp_ref[0, 0] = kcmp.astype(BF16)
    hv = hidden(vc_ref, posv_ref, w1v_ref)
    vcmp_ref[0, 0] = _dot_nt(w2v_ref[...], hv).astype(BF16)


def _compress(kc, vc, pos_k, w1_k, w2_k, pos_v, w1_v, w2_v):
    B, G, L, d = kc.shape
    half = CMP_STRIDE * d
    nb = L // CMP_STRIDE
    posk = pos_k.astype(F32).reshape(2, half)
    posv = pos_v.astype(F32).reshape(2, half)
    w1k = w1_k.astype(BF16).reshape(2, half, CMP_HIDDEN)
    w1v = w1_v.astype(BF16).reshape(2, half, CMP_HIDDEN)
    w2k = w2_k.astype(F32)
    cos, sin = _rope_tables(jnp.arange(nb) * CMP_STRIDE + CMP_LEN - 1)
    blk = pl.BlockSpec((1, 1, L, d), lambda b, g: (b, g, 0, 0))
    out = pl.BlockSpec((1, 1, nb, d), lambda b, g: (b, g, 0, 0))
    out_t = pl.BlockSpec((1, 1, d, nb), lambda b, g: (b, g, 0, 0))
    fix2 = lambda b, g: (0, 0)
    fix3 = lambda b, g: (0, 0, 0)
    return pl.pallas_call(
        _compress_body,
        grid=(B, G),
        in_specs=[blk, blk, pl.BlockSpec((2, half), fix2), pl.BlockSpec((2, half), fix2),
                  pl.BlockSpec((2, half, CMP_HIDDEN), fix3), pl.BlockSpec((2, half, CMP_HIDDEN), fix3),
                  pl.BlockSpec((CMP_HIDDEN, d), fix2), pl.BlockSpec((CMP_HIDDEN, d), fix2),
                  pl.BlockSpec((d, CMP_HIDDEN), fix2), pl.BlockSpec((nb, d), fix2), pl.BlockSpec((nb, d), fix2)],
        out_specs=[out, out_t],
        out_shape=[jax.ShapeDtypeStruct((B, G, nb, d), BF16), jax.ShapeDtypeStruct((B, G, d, nb), BF16)],
        compiler_params=_params("parallel", "parallel"),
        name="compress",
    )(kc, vc, posk, posv, w1k, w1v, w2k.astype(BF16), _rot_cols(w2k).astype(BF16), w2_v.T.astype(BF16), cos, sin)


NSA_KBLOCK = 256
NSA_GROUPS_PER_STEP = 4
NSA_HEADS_PER_CHAIN = 4
NSA_SCORE_LOOKAHEAD = 3


def _nsa_body(q_ref, kcmp_ref, vcmpt_ref, ks_ref, vst_ref, kw_ref, vwt_ref, gate_ref, o_ref, qsel_scr, *, seq):
    i = pl.program_id(2)
    tq = q_ref.shape[2]
    gb = kcmp_ref.shape[1]
    tk = min(NSA_KBLOCK, seq)
    R, d = NSA_REP, HEAD_DIM
    nq = R * tq
    nb = seq // SEL_LEN
    mc = seq // CMP_STRIDE
    t_row = i * tq + lax.broadcasted_iota(jnp.int32, (1, tq), 1)
    t_all = jnp.concatenate([t_row] * R, axis=1)
    m_col = lax.broadcasted_iota(jnp.int32, (mc, 1), 0)
    m_row = lax.broadcasted_iota(jnp.int32, (1, mc), 1)
    n_col = lax.broadcasted_iota(jnp.int32, (nb, 1), 0)

    def q_rows(g):
        return q_ref[0, g * R:(g + 1) * R].reshape(nq, d)

    valid_c = (m_col * CMP_STRIDE + (CMP_LEN - 1)) <= t_all
    ovl = jnp.logical_and(m_row * CMP_STRIDE < (n_col + 1) * SEL_LEN,
                          m_row * CMP_STRIDE + CMP_LEN > n_col * SEL_LEN)
    ovl = jnp.where(ovl, 1.0, 0.0).astype(BF16)
    cur = t_row // SEL_LEN
    forced = jnp.logical_or(n_col == 0, jnp.logical_or(n_col == cur, n_col == cur - 1))
    bonus = jnp.where(forced, FORCE_BONUS, 0.0)
    valid_s = n_col * SEL_LEN <= t_row
    o_c = []
    ties = [jnp.where(n_col > n2, 1.0, 0.0) for n2 in range(nb)]
    cmp_scores = [_dot_nt(kcmp_ref[0, g], q_rows(g)) for g in range(gb)]
    for g in range(gb):
        s = jnp.where(valid_c, cmp_scores[g], NEG)
        e = jnp.exp2(s - jnp.maximum(jnp.max(s, axis=0, keepdims=True), 0.5 * NEG))
        den = jnp.sum(e, axis=0, keepdims=True)
        p = e / jnp.where(den > 0.0, den, 1.0)
        o_c.append(_dot(vcmpt_ref[0, g], p.astype(BF16)))
        psum = p[:, 0:tq]
        for r in range(1, R):
            psum = psum + p[:, r * tq:(r + 1) * tq]
        ph, plo = _split_bf16(psum)
        score = jnp.where(valid_s, _dot(ovl, ph) + _dot(ovl, plo) + bonus, NEG)
        rank = jnp.zeros((nb, tq), F32)
        for n2 in range(nb):
            other = score[n2:n2 + 1, :]
            rank = rank + jnp.where(other > score, 1.0, jnp.where(other == score, ties[n2], 0.0))
        pen = jnp.where(rank < float(SEL_TOP), 0.0, NEG).T.astype(BF16)
        pen = jnp.concatenate([pen, jnp.zeros((tq, 128 - d - nb), BF16)], axis=1)
        qsel_scr[g] = jnp.concatenate([q_rows(g), jnp.concatenate([pen] * R, axis=0)], axis=1)

    k_col = lax.broadcasted_iota(jnp.int32, (tk, 1), 0)
    hp = NSA_HEADS_PER_CHAIN
    j_hi = (i * tq + tq - 1) // tk + 1

    def sweep(q_fn, k_ref, vt_ref, j_lo, bias_fn, bias_every_block):
        chains = [(g, h0) for g in range(gb) for h0 in range(0, R, hp)]

        def body(j, state, with_bias):
            k0 = pl.multiple_of(j * tk, tk)
            bias = jnp.concatenate([bias_fn(k0 + k_col)] * hp, axis=1) if with_bias else None

            def scores(c):
                g, h0 = chains[c]
                sc = _dot_nt(k_ref[0, g, pl.ds(k0, tk), :], q_fn(g, h0))
                return sc + bias if with_bias else sc

            new_state = []
            ahead = NSA_SCORE_LOOKAHEAD
            pending = [scores(c) for c in range(min(ahead, len(chains)))]
            for c, (m_run, acc) in enumerate(state):
                sc = pending.pop(0)
                if c + ahead < len(chains):
                    pending.append(scores(c + ahead))
                vtb = vt_ref[0, chains[c][0], :, pl.ds(k0, tk)]
                m_new = jnp.maximum(m_run, jnp.max(sc, axis=0, keepdims=True))
                pr = jnp.exp2(sc - m_new)
                new_state.append((m_new, jnp.exp2(m_run - m_new) * acc + _dot(vtb, pr.astype(BF16))))
            return tuple(new_state)

        state = tuple((jnp.full((1, hp * tq), NEG, F32), jnp.zeros((V_ROWS, hp * tq), F32)) for _ in chains)
        state = lax.fori_loop(j_lo, j_hi - 1, functools.partial(body, with_bias=bias_every_block), state)
        state = body(j_hi - 1, state, True)
        outs = []
        for g in range(gb):
            acc = jnp.concatenate([a for (cg, _), (_, a) in zip(chains, state) if cg == g], axis=1)
            outs.append(acc[0:d] / acc[d:d + 1])
        return outs

    def causal_bias(kpos):
        return jnp.where(kpos <= t_row, 0.0, NEG)

    o_s = sweep(lambda g, h0: qsel_scr[g, h0 * tq:(h0 + hp) * tq, :], ks_ref, vst_ref, 0, causal_bias, False)

    def win_bias(kpos):
        return jnp.where(jnp.logical_and(kpos <= t_row, kpos > t_row - WINDOW), 0.0, NEG)

    o_w = sweep(lambda g, h0: q_ref[0, g * R + h0:g * R + h0 + hp].reshape(hp * tq, d), kw_ref, vwt_ref,
                jnp.maximum(i * tq - (WINDOW - 1), 0) // tk, win_bias, True)

    outs = []
    for g in range(gb):
        row0 = (pl.program_id(1) * gb + g) * 3 * R

        def gate(branch):
            rows = [gate_ref[0, pl.ds(row0 + 3 * r + branch, 1), :] for r in range(R)]
            return _sigmoid(jnp.concatenate(rows, axis=1))

        o_t = gate(0) * o_c[g] + gate(1) * o_s[g] + gate(2) * o_w[g]
        for r in range(R):
            outs.append(o_t[:, r * tq:(r + 1) * tq].T)
    o_ref[0] = jnp.concatenate(outs, axis=-1).astype(o_ref.dtype)


def _nsa_attention(q, kcmp, vcmpt, ks, vst, kw, vwt, gate):
    B, H, L, d = q.shape
    G, R = NSA_KV_HEADS, NSA_REP
    gb = NSA_GROUPS_PER_STEP
    tq = min(ATT_BLOCK, L)
    mc = L // CMP_STRIDE
    k_spec = pl.BlockSpec((1, gb, L, d), lambda b, g, i: (b, g, 0, 0))
    ksel_spec = pl.BlockSpec((1, gb, L, 128), lambda b, g, i: (b, g, 0, 0))
    vt_spec = pl.BlockSpec((1, gb, V_ROWS, L), lambda b, g, i: (b, g, 0, 0))
    return pl.pallas_call(
        functools.partial(_nsa_body, seq=L),
        grid=(B, G // gb, L // tq),
        in_specs=[pl.BlockSpec((1, gb * R, tq, d), lambda b, g, i: (b, g, i, 0)),
                  pl.BlockSpec((1, gb, mc, d), lambda b, g, i: (b, g, 0, 0)),
                  pl.BlockSpec((1, gb, d, mc), lambda b, g, i: (b, g, 0, 0)),
                  ksel_spec, vt_spec, k_spec, vt_spec,
                  pl.BlockSpec((1, gate.shape[1], tq), lambda b, g, i: (b, 0, i))],
        out_specs=pl.BlockSpec((1, tq, gb * R * d), lambda b, g, i: (b, i, g)),
        out_shape=jax.ShapeDtypeStruct((B, L, H * d), BF16),
        scratch_shapes=[pltpu.VMEM((gb, R * tq, 128), BF16)],
        compiler_params=_params("parallel", "parallel", "arbitrary"),
        name="nsa_attention",
    )(q, kcmp, vcmpt, ks, vst, kw, vwt, gate)


def kernel(x, w_in_0, ssm_lam_re, ssm_lam_im, ssm_log_dt, ssm_b_re, ssm_b_im, ssm_c_re, ssm_c_im, ssm_d, w_glu, w_out_0, ln_mix_g_0, ln_mix_b_0, ln_ffn_g_0, ln_ffn_b_0, w1_0, w3_0, w2_0, w_in_1, cmp_pos_k, cmp_w1_k, cmp_w2_k, cmp_pos_v, cmp_w1_v, cmp_w2_v, w_out_1, ln_mix_g_1, ln_mix_b_1, ln_ffn_g_1, ln_ffn_b_1, w1_1, w3_1, w2_1, w_router, b_router):
    B, L, D = x.shape
    N = B * L
    vec = lambda a: a.astype(F32).reshape(1, D)

    def ffn(h, w1, w3, w2, g, b):
        gates_t, grp_t = _router(h, w_router, b_router)
        return _moe(h, gates_t, grp_t, w1.astype(BF16), w3.astype(BF16), w2.astype(BF16), vec(g), vec(b))

    q, k, v, u = _inproj0(x, w_in_0.astype(BF16))
    o_a = _sb_attention(q, k, v)
    y = _ssm(u, _ssm_tables(ssm_lam_re, ssm_lam_im, ssm_log_dt, ssm_b_re, ssm_b_im, ssm_c_re, ssm_c_im, ssm_d), B)
    w_out_0b = w_out_0.astype(BF16)
    h = _outproj0(x, o_a, y, w_glu.astype(BF16), w_out_0b[:SB_WIDTH], w_out_0b[SB_WIDTH:],
                  vec(ln_mix_g_0), vec(ln_mix_b_0)).reshape(N, D)
    h = ffn(h, w1_0, w3_0, w2_0, ln_ffn_g_0, ln_ffn_b_0)

    q, ks, kw, kc, vc, vst, vwt, gate = _inproj1(h.reshape(B, L, D), *_inproj1_weights(w_in_1))
    kcmp, vcmpt = _compress(kc, vc, cmp_pos_k, cmp_w1_k, cmp_w2_k, cmp_pos_v, cmp_w1_v, cmp_w2_v)
    o = _nsa_attention(q, kcmp, vcmpt, ks, vst, kw, vwt, gate).reshape(N, NSA_HEADS * HEAD_DIM)
    h = _outproj1(h, o, w_out_1.astype(BF16), vec(ln_mix_g_1), vec(ln_mix_b_1))
    h = ffn(h, w1_1, w3_1, w2_1, ln_ffn_g_1, ln_ffn_b_1)
    return h.reshape(B, L, D)
```

```python
import functools
import math

import numpy as np
import jax
import jax.numpy as jnp
from jax import lax
from jax.experimental import pallas as pl
from jax.experimental.pallas import tpu as pltpu

F32 = jnp.float32
BF16 = jnp.bfloat16

D_MODEL = 1024
DEPTH = 2
SB_HEADS = 8
HEAD_DIM = 64
SB_WIDTH = SB_HEADS * HEAD_DIM
SSM_WIDTH = D_MODEL - SB_WIDTH
SSM_GROUP = 16
SSM_GROUPS = SSM_WIDTH // SSM_GROUP
SSM_STATE = 64
SSM_CHUNK = 16
SSM_LANE_TILES = SSM_WIDTH // 128
SSM_TILE_GROUPS = 128 // SSM_GROUP
NSA_HEADS = 16
NSA_KV_HEADS = 4
NSA_REP = NSA_HEADS // NSA_KV_HEADS
CMP_LEN = 32
CMP_STRIDE = 16
CMP_HIDDEN = 256
SEL_LEN = 64
SEL_TOP = 8
WINDOW = 512
ROPE_THETA = 10000.0
FORCE_BONUS = 1e4
NEG = -1e30
N_EXPERTS = 16
N_GROUPS = 4
EXPERTS_PER_GROUP = N_EXPERTS // N_GROUPS
D_EXPERT = 512
ALPHA = (2 * DEPTH) ** 0.25
LN_EPS = 1e-5
ATTN_SCALE = HEAD_DIM ** -0.5
LOG2_Q_SCALE = ATTN_SCALE * math.log2(math.e)
ATT_BLOCK = 256
GELU_C = math.sqrt(2.0 / math.pi)


VMEM_BYTES_V7X = 64 * 1024 * 1024
VMEM_LIMIT_BYTES = VMEM_BYTES_V7X * 7 // 8


def _params(*sem):
    return pltpu.CompilerParams(dimension_semantics=sem, vmem_limit_bytes=VMEM_LIMIT_BYTES)


def _sigmoid(x):
    return 1.0 / (1.0 + jnp.exp(-x))


def _gelu(x):
    return 0.5 * x * (1.0 + jnp.tanh(GELU_C * (x + 0.044715 * (x * x * x))))


def _layer_norm(r, g, b):
    mu = jnp.mean(r, axis=-1, keepdims=True)
    d = r - mu
    var = jnp.mean(d * d, axis=-1, keepdims=True)
    return d * lax.rsqrt(var + LN_EPS) * g + b


def _dot(a, b):
    return jnp.dot(a, b, preferred_element_type=F32)


def _dot_nt(a, b):
    return lax.dot_general(a, b, (((1,), (1,)), ((), ())), preferred_element_type=F32)


def _split_bf16(x):
    hi = x.astype(BF16)
    lo = (x - hi.astype(F32)).astype(BF16)
    return hi, lo


def _inproj0_body(x_ref, w_ref, q_ref, k_ref, v_ref, u_ref, slab_scr):
    nb, tl, D = x_ref.shape
    p = _dot(x_ref[...].reshape(nb * tl, D).astype(BF16), w_ref[...])
    for h in range(SB_HEADS):
        c = h * HEAD_DIM
        q_ref[:, h] = (p[:, c:c + HEAD_DIM] * LOG2_Q_SCALE).astype(BF16).reshape(nb, tl, HEAD_DIM)
        k_ref[:, h] = p[:, SB_WIDTH + c:SB_WIDTH + c + HEAD_DIM].astype(BF16).reshape(nb, tl, HEAD_DIM)
        v_ref[:, h] = p[:, 2 * SB_WIDTH + c:2 * SB_WIDTH + c + HEAD_DIM].astype(BF16).reshape(nb, tl, HEAD_DIM)
    T = SSM_CHUNK
    for v in range(SSM_LANE_TILES):
        c0 = 3 * SB_WIDTH + v * 128
        slab_scr[...] = p[:, c0:c0 + 128]
        for c in range(tl // T):
            for s in range(T):
                u_ref[v, c * nb:(c + 1) * nb, s * 128:(s + 1) * 128] = slab_scr[pl.ds(c * T + s, nb, stride=tl), :]


def _inproj0(x, w_bf16, tl=64):
    B, L, D = x.shape
    tl = min(tl, L)
    nout = w_bf16.shape[1]
    T = SSM_CHUNK
    head = jax.ShapeDtypeStruct((B, SB_HEADS, L, HEAD_DIM), BF16)
    head_spec = pl.BlockSpec((B, SB_HEADS, tl, HEAD_DIM), lambda i: (0, 0, i, 0))
    return pl.pallas_call(
        _inproj0_body,
        grid=(L // tl,),
        in_specs=[pl.BlockSpec((B, tl, D), lambda i: (0, i, 0)),
                  pl.BlockSpec((D, nout), lambda i: (0, 0))],
        out_specs=[head_spec, head_spec, head_spec,
                   pl.BlockSpec((SSM_LANE_TILES, tl // T * B, T * 128), lambda i: (0, i, 0))],
        out_shape=[head, head, head,
                   jax.ShapeDtypeStruct((SSM_LANE_TILES, L // T * B, T * 128), F32)],
        scratch_shapes=[pltpu.VMEM((B * tl, 128), F32)],
        compiler_params=_params("parallel"),
        name="inproj0",
    )(x, w_bf16)


SB_BLOCK = 256
SB_HEADS_PER_STEP = 8
EXP2_UNDERFLOW = -151.0


def _sb_attn_body(q_ref, k_ref, v_ref, o_ref, acc_scr, cs_scr):
    i = pl.program_id(2)
    t = q_ref.shape[2]
    nh = q_ref.shape[1]
    row = lax.broadcasted_iota(jnp.int32, (t, t), 0)
    col = lax.broadcasted_iota(jnp.int32, (t, t), 1)
    suffix = jnp.where(row > col, 1.0, 0.0).astype(BF16)
    below = col < row

    def logits(hh, k0):
        return _dot_nt(q_ref[0, hh], k_ref[0, hh, pl.ds(k0, t), :])

    def alive(css):
        return (jnp.max(functools.reduce(jnp.maximum, css)) > EXP2_UNDERFLOW).astype(jnp.int32)

    def all_heads(k0, diagonal):
        zs = [logits(hh, k0) for hh in range(nh)]
        sps, l1s = [], []
        for z in zs:
            sp = jnp.maximum(z, 0.0) + jnp.log2(1.0 + jnp.exp2(-jnp.abs(z)))
            sps.append(sp)
            l1s.append(jnp.where(below, -sp, 0.0) if diagonal else -sp)
        within_all = _dot(jnp.concatenate([l1.astype(BF16) for l1 in l1s], axis=0), suffix)
        css = []
        for hh in range(nh):
            vb = v_ref[0, hh, pl.ds(k0, t), :]
            base = zs[hh] - sps[hh] + within_all[hh * t:(hh + 1) * t]
            if diagonal:
                w = jnp.where(below, jnp.exp2(base), 0.0)
                acc_scr[hh] = _dot(w.astype(BF16), vb)
                cs = jnp.sum(l1s[hh], axis=1, keepdims=True)
            else:
                w = jnp.exp2(base + cs_scr[hh])
                acc_scr[hh] += _dot(w.astype(BF16), vb)
                cs = cs_scr[hh] + jnp.sum(l1s[hh], axis=1, keepdims=True)
            cs_scr[hh] = cs
            css.append(cs)
        return alive(css)

    first = all_heads(pl.multiple_of(i * t, t), True)

    def cond(c):
        return jnp.logical_and(c[0] <= i, c[1] > 0)

    def body(c):
        return c[0] + 1, all_heads(pl.multiple_of((i - c[0]) * t, t), False)

    lax.while_loop(cond, body, (jnp.int32(1), first))
    o_ref[0] = jnp.concatenate([acc_scr[hh] for hh in range(nh)], axis=-1).astype(o_ref.dtype)


def _sb_attention(q, k, v):
    B, H, L, d = q.shape
    t = min(SB_BLOCK, L)
    nh = SB_HEADS_PER_STEP
    return pl.pallas_call(
        _sb_attn_body,
        grid=(B, H // nh, L // t),
        in_specs=[pl.BlockSpec((1, nh, t, d), lambda b, h, i: (b, h, i, 0)),
                  pl.BlockSpec((1, nh, L, d), lambda b, h, i: (b, h, 0, 0)),
                  pl.BlockSpec((1, nh, L, d), lambda b, h, i: (b, h, 0, 0))],
        out_specs=pl.BlockSpec((1, t, nh * d), lambda b, h, i: (b, i, h)),
        out_shape=jax.ShapeDtypeStruct((B, L, H * d), BF16),
        scratch_shapes=[pltpu.VMEM((nh, t, d), F32), pltpu.VMEM((nh, t, 1), F32)],
        compiler_params=_params("parallel", "parallel", "arbitrary"),
        name="sb_attention",
    )(q, k, v)


def _ssm_tables(lam_re, lam_im, log_dt, b_re, b_im, c_re, c_im, d_skip):
    T, G, P, H = SSM_CHUNK, SSM_GROUPS, SSM_STATE, SSM_GROUP
    hp = lax.Precision.HIGHEST
    dt = jnp.exp(log_dt.astype(F32))[:, None]
    lr = lam_re.astype(F32)
    li = lam_im.astype(F32)
    mag = jnp.exp(lr * dt)
    a_re = mag * jnp.cos(li * dt)
    a_im = mag * jnp.sin(li * dt)
    den = lr * lr + li * li
    nr = a_re - 1.0
    f_re = (nr * lr + a_im * li) / den
    f_im = (a_im * lr - nr * li) / den
    br = b_re.astype(F32)
    bi = b_im.astype(F32)
    bbt_re = (f_re[..., None] * br - f_im[..., None] * bi).transpose(0, 2, 1)[:, None]
    bbt_im = (f_re[..., None] * bi + f_im[..., None] * br).transpose(0, 2, 1)[:, None]
    cr = c_re.astype(F32)[:, None]
    ci = c_im.astype(F32)[:, None]

    def powers(k):
        k = jnp.asarray(k, F32)[None, :, None]
        pmag = jnp.exp(k * (lr * dt)[:, None, :])
        ang = k * (li * dt)[:, None, :]
        return (pmag * jnp.cos(ang))[:, :, None, :], (pmag * jnp.sin(ang))[:, :, None, :]

    def cmul(xr, xi, yr, yi, rows):
        return ((xr * yr - xi * yi).reshape(G, rows, P), (xr * yi + xi * yr).reshape(G, rows, P))

    steps = np.arange(T)
    bs_re, bs_im = cmul(*powers(-steps), bbt_re, bbt_im, T * H)
    ct_re, ct_im = cmul(*powers(steps), cr, ci, T * H)
    full = (jnp.einsum('gap,gbp->gab', bs_re, ct_re, precision=hp)
            - jnp.einsum('gap,gbp->gab', bs_im, ct_im, precision=hp))
    row_s = lax.broadcasted_iota(jnp.int32, (T * H, T * H), 0) // H
    col_t = lax.broadcasted_iota(jnp.int32, (T * H, T * H), 1) // H
    intra = jnp.where(row_s <= col_t, full, 0.0)
    st_re, st_im = cmul(*powers(T - 1 - steps), bbt_re, bbt_im, T * H)
    to_st = jnp.concatenate([st_re, st_im, st_im, st_re], axis=-1)
    c1_re, c1_im = cmul(*powers(steps + 1), cr, ci, T * H)
    from_st_t = jnp.concatenate([c1_re, -c1_im], axis=-1)
    at_re, at_im = powers([T])
    at_re, at_im = at_re[:, 0, 0], at_im[:, 0, 0]
    zeros = jnp.zeros_like(at_re)
    adv = jnp.stack([jnp.concatenate([at_re, at_re], -1),
                     jnp.concatenate([-at_im, at_im], -1),
                     jnp.concatenate([at_im, -at_im], -1)]
                    + [jnp.concatenate([zeros, zeros], -1)] * 5, axis=1)
    dvec = jnp.tile(d_skip.astype(F32).reshape(G, 1, H), (1, 1, T))
    return intra, to_st, from_st_t, adv, dvec


def _ssm_expand(tables):
    intra, to_st, from_st_t, adv, dvec = tables
    V, W, T, H, P = SSM_LANE_TILES, SSM_TILE_GROUPS, SSM_CHUNK, SSM_GROUP, SSM_STATE
    n = T * W * H
    k_in = lax.broadcasted_iota(jnp.int32, (T * H, n), 0)
    col = lax.broadcasted_iota(jnp.int32, (T * H, n), 1)
    spread_th = (k_in == (col // 128) * H + col % H).astype(F32)
    spread_q = (k_in[:2 * P, :W * 2 * P] == col[:2 * P, :W * 2 * P] % (2 * P)).astype(F32)
    row_g = (lax.broadcasted_iota(jnp.int32, (n, 1), 0) // H) % W
    st_row_g = lax.broadcasted_iota(jnp.int32, (W * 2 * P, 1), 0) // (2 * P)
    col_g_th = (lax.broadcasted_iota(jnp.int32, (1, n), 1) // H) % W
    col_g_q = lax.broadcasted_iota(jnp.int32, (1, W * 2 * P), 1) // (2 * P)

    def blockdiag(rows, spread, row_group, col_group):
        full = jnp.einsum('vrk,kc->vrc', rows, spread)
        return jnp.where(row_group == col_group, full, 0.0).astype(BF16)

    by_row = lambda t: t.reshape(V, W, T, H, T * H).transpose(0, 2, 1, 3, 4).reshape(V, n, T * H)
    m8 = blockdiag(by_row(intra), spread_th, row_g, col_g_th)
    ws8 = blockdiag(by_row(to_st)[:, :, :2 * P], spread_q, row_g, col_g_q)
    wi_full = jnp.einsum('vwkq,kc->vwqc', from_st_t.reshape(V, W, T * H, 2 * P), spread_th).reshape(V, W * 2 * P, n)
    wi8 = jnp.where(st_row_g == col_g_th, wi_full, 0.0).astype(BF16)
    adv8 = adv.reshape(V, W, 8, 2 * P).transpose(0, 2, 1, 3).reshape(V, 8, W * 2 * P)
    d8 = jnp.broadcast_to(dvec.reshape(V, 1, W, T, H)[:, :, :, 0:1], (V, 1, W, T, H))
    d8 = d8.transpose(0, 1, 3, 2, 4).reshape(V, 1, T * W * H)
    return m8, ws8, wi8, adv8, d8


def _ssm_body(u_ref, m8_ref, ws8_ref, wi8_ref, adv_ref, dvec_ref, y_ref, s_scr, xin_scr, x1_scr, x2_scr, *, bsz):
    half = x1_scr.shape[1]

    @pl.when(pl.program_id(1) == 0)
    def _():
        x1_scr[...] = jnp.zeros_like(x1_scr)
        x2_scr[...] = jnp.zeros_like(x2_scr)

    u = u_ref[0]
    ub = u.astype(BF16)
    s1 = _dot(ub, ws8_ref[0])
    lane = lax.broadcasted_iota(jnp.int32, s1.shape, 1)
    s_scr[:, :half] = s1
    s_scr[:, half:] = jnp.where(lane % (2 * SSM_STATE) < SSM_STATE,
                                pltpu.roll(s1, half - SSM_STATE, 1), pltpu.roll(s1, SSM_STATE, 1))
    a1 = adv_ref[0, 0:1, :]
    a2 = adv_ref[0, 1:2, :]
    a3 = adv_ref[0, 2:3, :]

    def step(c, carry):
        x1, x2 = carry
        r0 = pl.multiple_of(c * bsz, bsz)
        xin_scr[pl.ds(r0, bsz), :] = x1
        s = s_scr[pl.ds(r0, bsz), :]
        return (a1 * x1 + a2 * x2 + s[:, :half], a1 * x2 + a3 * x1 + s[:, half:])

    x1, x2 = lax.fori_loop(0, u.shape[0] // bsz, step, (x1_scr[...], x2_scr[...]))
    x1_scr[...] = x1
    x2_scr[...] = x2
    y_ref[0] = _dot(ub, m8_ref[0]) + _dot(xin_scr[...].astype(BF16), wi8_ref[0]) + dvec_ref[0] * u


def _ssm(us, tables, bsz, rows=512):
    m8, ws8, wi8, adv8, d8 = _ssm_expand(tables)
    V, CB, width = us.shape
    B = bsz
    half = SSM_TILE_GROUPS * 2 * SSM_STATE
    rows = min(rows, CB)
    once = dict(pipeline_mode=pl.Buffered(1))
    y = pl.pallas_call(
        functools.partial(_ssm_body, bsz=B),
        grid=(V, CB // rows),
        in_specs=[pl.BlockSpec((1, rows, width), lambda v, r: (v, r, 0)),
                  pl.BlockSpec((1, width, width), lambda v, r: (v, 0, 0), **once),
                  pl.BlockSpec((1, width, half), lambda v, r: (v, 0, 0), **once),
                  pl.BlockSpec((1, half, width), lambda v, r: (v, 0, 0), **once),
                  pl.BlockSpec((1, 8, half), lambda v, r: (v, 0, 0)),
                  pl.BlockSpec((1, 1, width), lambda v, r: (v, 0, 0))],
        out_specs=pl.BlockSpec((1, rows, width), lambda v, r: (v, r, 0)),
        out_shape=jax.ShapeDtypeStruct((V, CB, width), F32),
        scratch_shapes=[pltpu.VMEM((rows, 2 * half), F32), pltpu.VMEM((rows, half), F32),
                        pltpu.VMEM((B, half), F32), pltpu.VMEM((B, half), F32)],
        compiler_params=_params("parallel", "arbitrary"),
        name="s5_scan",
    )(us, m8, ws8, wi8, adv8, d8)
    return y


def _outproj0_body(x_ref, oa_ref, y_ref, wglu_ref, woa_ref, wob_ref, g_ref, b_ref, o_ref, slab_scr):
    nb, tl, D = x_ref.shape
    T = SSM_CHUNK
    ys = []
    for v in range(SSM_LANE_TILES):
        for c in range(tl // T):
            for s in range(T):
                slab_scr[pl.ds(c * T + s, nb, stride=tl), :] = y_ref[v, c * nb:(c + 1) * nb, s * 128:(s + 1) * 128]
        ys.append(slab_scr[...])
    h = _gelu(jnp.concatenate(ys, axis=-1))
    ob = h * _sigmoid(_dot(h.astype(BF16), wglu_ref[...]))
    m = _dot(oa_ref[...].reshape(nb * tl, SB_WIDTH), woa_ref[...]) + _dot(ob.astype(BF16), wob_ref[...])
    r = ALPHA * x_ref[...].reshape(nb * tl, D) + m
    o_ref[...] = _layer_norm(r, g_ref[...], b_ref[...]).reshape(nb, tl, D)


def _outproj0(x, oa, ys, wglu, woa, wob, g, b, tl=64):
    B, L, D = x.shape
    tl = min(tl, L)
    T = SSM_CHUNK
    row = lambda i: (0, i, 0)
    fix = lambda i: (0, 0)
    return pl.pallas_call(
        _outproj0_body,
        grid=(L // tl,),
        in_specs=[pl.BlockSpec((B, tl, D), row), pl.BlockSpec((B, tl, SB_WIDTH), row),
                  pl.BlockSpec((SSM_LANE_TILES, tl // T * B, T * 128), row),
                  pl.BlockSpec((SSM_WIDTH, SSM_WIDTH), fix), pl.BlockSpec((SB_WIDTH, D), fix),
                  pl.BlockSpec((SSM_WIDTH, D), fix), pl.BlockSpec((1, D), fix), pl.BlockSpec((1, D), fix)],
        out_specs=pl.BlockSpec((B, tl, D), row),
        out_shape=jax.ShapeDtypeStruct((B, L, D), F32),
        scratch_shapes=[pltpu.VMEM((B * tl, 128), F32)],
        compiler_params=_params("parallel"),
        name="outproj0",
    )(x, oa, ys, wglu, woa, wob, g, b)


def _outproj1_body(x_ref, o_ref_in, w_ref, g_ref, b_ref, o_ref):
    m = _dot(o_ref_in[...], w_ref[...])
    o_ref[...] = _layer_norm(ALPHA * x_ref[...] + m, g_ref[...], b_ref[...])


def _outproj1(x, o, w, g, b, tm=1024):
    N, D = x.shape
    tm = min(tm, N)
    row = lambda i: (i, 0)
    fix = lambda i: (0, 0)
    return pl.pallas_call(
        _outproj1_body,
        grid=(N // tm,),
        in_specs=[pl.BlockSpec((tm, D), row), pl.BlockSpec((tm, o.shape[1]), row),
                  pl.BlockSpec(w.shape, fix), pl.BlockSpec((1, D), fix), pl.BlockSpec((1, D), fix)],
        out_specs=pl.BlockSpec((tm, D), row),
        out_shape=jax.ShapeDtypeStruct((N, D), F32),
        compiler_params=_params("parallel"),
        name="outproj1",
    )(x, o, w, g, b)


def _router_body(x_ref, whl_ref, b_ref, g_ref, grp_ref):
    NE = N_EXPERTS
    xh, xl = _split_bf16(x_ref[...])
    whl = whl_ref[...]
    part = _dot(xh, whl)
    logits = (part[:, :NE] + part[:, NE:] + _dot(xl, whl[:, :NE])).T
    aff = _sigmoid(logits)
    sel = aff + b_ref[...]
    E, K = EXPERTS_PER_GROUP, N_GROUPS
    s = [sel[e:e + 1, :] for e in range(N_EXPERTS)]
    a = [aff[e:e + 1, :] for e in range(N_EXPERTS)]
    gscore = []
    for k in range(K):
        v = s[k * E:(k + 1) * E]
        best = None
        for x in range(E):
            for y in range(x + 1, E):
                pair = v[x] + v[y]
                best = pair if best is None else jnp.maximum(best, pair)
        gscore.append(best)
    top = functools.reduce(jnp.maximum, gscore)
    is_g = []
    taken = None
    for k in range(K):
        hit = gscore[k] == top
        if taken is None:
            is_g.append(hit)
            taken = hit
        else:
            is_g.append(jnp.logical_and(hit, jnp.logical_not(taken)))
            taken = jnp.logical_or(taken, hit)

    def pick(rows, j):
        out = rows[(K - 1) * E + j]
        for k in range(K - 2, -1, -1):
            out = jnp.where(is_g[k], rows[k * E + j], out)
        return out

    v = [pick(s, j) for j in range(E)]
    av = [pick(a, j) for j in range(E)]
    chosen = []
    for j in range(E):
        r = jnp.zeros_like(v[j])
        for j2 in range(E):
            if j2 == j:
                continue
            ahead = (v[j2] >= v[j]) if j2 < j else (v[j2] > v[j])
            r = r + jnp.where(ahead, 1.0, 0.0)
        chosen.append(r < 2.0)
    wj = [jnp.where(chosen[j], av[j], 0.0) for j in range(E)]
    tot = wj[0] + wj[1] + wj[2] + wj[3]
    gj = [w / tot for w in wj]
    rows = [jnp.where(is_g[e // E], gj[e % E], 0.0) for e in range(N_EXPERTS)]
    g_ref[...] = jnp.concatenate(rows, axis=0)
    grp_ref[...] = jnp.concatenate([jnp.where(is_g[k], 1.0, 0.0) for k in range(K)], axis=0)


def _router(x, w_router, b_router, tm=1024):
    N, D = x.shape
    tm = min(tm, N)
    whl = jnp.concatenate(_split_bf16(w_router.astype(F32)), axis=1)
    fix = lambda i: (0, 0)
    return pl.pallas_call(
        _router_body,
        grid=(N // tm,),
        in_specs=[pl.BlockSpec((tm, D), lambda i: (i, 0)), pl.BlockSpec((D, 2 * N_EXPERTS), fix),
                  pl.BlockSpec((N_EXPERTS, 1), fix)],
        out_specs=[pl.BlockSpec((N_EXPERTS, tm), lambda i: (0, i)), pl.BlockSpec((N_GROUPS, tm), lambda i: (0, i))],
        out_shape=[jax.ShapeDtypeStruct((N_EXPERTS, N), F32), jax.ShapeDtypeStruct((N_GROUPS, N), F32)],
        compiler_params=_params("parallel"),
        name="router",
    )(x, whl, b_router.astype(F32).reshape(N_EXPERTS, 1))


MOE_TILE = 1024
MOE_PASS_ROWS = 288
MOE_CHUNK = 256
MOE_EXPERTS_PER_STEP = 2
ROW_ALIGN = 16


def _moe_body(seg_ref, x_ref, gt_ref, grp_ref, w1_ref, w3_ref, w2_ref, g_ref, b_ref, o_ref,
              xs_scr, ys_scr, pt_scr, gs_scr):
    i = pl.program_id(0)
    e = pl.program_id(1)
    T = x_ref.shape[0]
    K, NE = N_GROUPS, N_EXPERTS
    cap = MOE_PASS_ROWS
    ch = min(MOE_CHUNK, T)

    @pl.when(e == 0)
    def _():
        grp = grp_ref[...]
        xb = x_ref[...].astype(BF16)
        gates = gt_ref[...]
        gh, gm = _split_bf16(gates)
        gl = (gates - gh.astype(F32) - gm.astype(F32)).astype(BF16)
        g3 = jnp.concatenate([gh, gm, gl], axis=0)
        cnt = jnp.sum(grp, axis=1, keepdims=True)
        off = [jnp.zeros((1, 1), F32)]
        for k in range(K - 1):
            off.append(off[-1] + cnt[k:k + 1])
        r_i = lax.broadcasted_iota(jnp.int32, (ch, T), 0)
        c_i = lax.broadcasted_iota(jnp.int32, (ch, T), 1)
        lane = lax.broadcasted_iota(jnp.int32, (2 * K, T), 1)
        run = jnp.concatenate([grp, jnp.zeros_like(grp)], axis=0)
        sh = 1
        while sh < T:
            run = run + jnp.where(lane >= sh, pltpu.roll(run, sh, 1), 0.0)
            sh *= 2
        pos_row = jnp.zeros((1, T), F32)
        for k in range(K):
            pos_row = pos_row + grp[k:k + 1, :] * (off[k] + run[k:k + 1, :] - 1.0)
        pos_t = jnp.broadcast_to(pos_row, (2 * K, T))
        pos_col = [pos_t[:, c * ch:(c + 1) * ch].T[:, 0:1] for c in range(T // ch)]
        for c in range(T // ch):
            c0 = c * ch
            perm = jnp.where((r_i + c0).astype(F32) == pos_row, 1.0, 0.0).astype(BF16)
            xs_scr[c0:c0 + ch, :] = _dot(perm, xb).astype(BF16)
            g3s = _dot_nt(perm, g3)
            gs_scr[c0:c0 + ch, :] = g3s[:, :NE] + g3s[:, NE:2 * NE] + g3s[:, 2 * NE:]
            pt_scr[c0:c0 + ch, :] = jnp.where(c_i.astype(F32) == pos_col[c], 1.0, 0.0).astype(BF16)
        xs_scr[T:, :] = jnp.zeros((cap, xs_scr.shape[1]), BF16)
        gs_scr[T:, :] = jnp.zeros((cap, gs_scr.shape[1]), F32)
        ys_scr[...] = jnp.zeros_like(ys_scr)

    ne = w1_ref.shape[0]
    k = (e * ne) // EXPERTS_PER_GROUP
    off = seg_ref[i, k]
    cnt = seg_ref[i, K + k]
    start0 = (off // ROW_ALIGN) * ROW_ALIGN
    n_pass = (off + cnt - start0 + cap - 1) // cap

    def one_pass(n, c):
        start = pl.multiple_of(start0 + n * cap, ROW_ALIGN)
        xc = xs_scr[pl.ds(start, cap), :]
        gsc = gs_scr[pl.ds(start, cap), :]
        lane = lax.broadcasted_iota(jnp.int32, gsc.shape, 1)
        y = None
        for x in range(ne):
            a = _dot(xc, w1_ref[x])
            hm = (a * _sigmoid(a)) * _dot(xc, w3_ref[x])
            gcol = jnp.sum(jnp.where(lane == e * ne + x, gsc, 0.0), axis=1, keepdims=True)
            yx = gcol * _dot(hm.astype(BF16), w2_ref[x])
            y = yx if y is None else y + yx
        ys_scr[pl.ds(start, cap), :] += y
        return c

    lax.fori_loop(0, n_pass, one_pass, 0)

    @pl.when(e == pl.num_programs(1) - 1)
    def _():
        y = _dot(pt_scr[...], ys_scr[0:T, :].astype(BF16))
        o_ref[...] = _layer_norm(ALPHA * x_ref[...] + y, g_ref[...], b_ref[...])


def _moe(x, gates_t, grp_t, w1, w3, w2, g, b):
    N, D = x.shape
    T = min(MOE_TILE, N)
    K = N_GROUPS
    cnt = grp_t.reshape(K, N // T, T).sum(-1).astype(jnp.int32).T
    seg = jnp.concatenate([jnp.cumsum(cnt, axis=1) - cnt, cnt], axis=1)
    fix = lambda i, e, s: (0, 0)
    ne = MOE_EXPERTS_PER_STEP
    rows = T + MOE_PASS_ROWS
    grid_spec = pltpu.PrefetchScalarGridSpec(
        num_scalar_prefetch=1,
        grid=(N // T, N_EXPERTS // ne),
        in_specs=[pl.BlockSpec((T, D), lambda i, e, s: (i, 0)),
                  pl.BlockSpec((N_EXPERTS, T), lambda i, e, s: (0, i)),
                  pl.BlockSpec((K, T), lambda i, e, s: (0, i)),
                  pl.BlockSpec((ne, D, D_EXPERT), lambda i, e, s: (e, 0, 0)),
                  pl.BlockSpec((ne, D, D_EXPERT), lambda i, e, s: (e, 0, 0)),
                  pl.BlockSpec((ne, D_EXPERT, D), lambda i, e, s: (e, 0, 0)),
                  pl.BlockSpec((1, D), fix), pl.BlockSpec((1, D), fix)],
        out_specs=pl.BlockSpec((T, D), lambda i, e, s: (i, 0)),
        scratch_shapes=[pltpu.VMEM((rows, D), BF16), pltpu.VMEM((rows, D), F32),
                        pltpu.VMEM((T, T), BF16), pltpu.VMEM((rows, N_EXPERTS), F32)])
    return pl.pallas_call(
        _moe_body,
        grid_spec=grid_spec,
        out_shape=jax.ShapeDtypeStruct((N, D), F32),
        compiler_params=_params("parallel", "arbitrary"),
        name="moe",
    )(seg, x, gates_t, grp_t, w1, w3, w2, g, b)


NSA_KVW = NSA_KV_HEADS * HEAD_DIM
NSA_ROPE_W = NSA_HEADS * HEAD_DIM + 2 * NSA_KVW


def _rot_cols(w):
    K, n = w.shape
    w3 = w.reshape(K, n // HEAD_DIM, 2, HEAD_DIM // 2)
    return jnp.stack([-w3[:, :, 1], w3[:, :, 0]], axis=2).reshape(K, n)


def _inproj1_weights(w_in):
    H, G, d = NSA_HEADS, NSA_KV_HEADS, HEAD_DIM
    cuts = [H * d + i * NSA_KVW for i in range(7)]
    q, kc, vc, ks, vs, kw, vw, gate = jnp.split(w_in.astype(F32), cuts, axis=1)
    w_all = jnp.concatenate([q, ks, kw, kc, vc], axis=1).astype(BF16)
    w_t = jnp.concatenate([vs, vw, gate], axis=1).T.astype(BF16)
    return w_all, w_t


def _rope_tables(pos):
    inv = ROPE_THETA ** (-jnp.arange(0, HEAD_DIM, 2, dtype=F32) / HEAD_DIM)
    ang = pos.astype(F32)[:, None] * inv[None, :]
    c, s = jnp.cos(ang), jnp.sin(ang)
    return jnp.concatenate([c, c], axis=-1), jnp.concatenate([s, s], axis=-1)


def _rope_roll_tables(pos):
    inv = ROPE_THETA ** (-jnp.arange(0, HEAD_DIM, 2, dtype=F32) / HEAD_DIM)
    ang = pos.astype(F32)[:, None] * inv[None, :]
    c, s, z = jnp.cos(ang), jnp.sin(ang), jnp.zeros_like(ang)
    two = lambda a, b: jnp.concatenate([a, b, a, b], axis=-1)
    return two(c, c), two(-s, z), two(z, s)


V_ROWS = HEAD_DIM + 16


def _inproj1_body(x_ref, w_ref, wt_ref, cos_ref, sinlo_ref, sinhi_ref, q_ref, ks_ref, kw_ref, kc_ref, vc_ref,
                  vst_ref, vwt_ref, gate_ref):
    xb = x_ref[0].astype(BF16)
    tl = xb.shape[0]
    d = HEAD_DIM
    p = _dot(xb, w_ref[...])
    pt = _dot_nt(wt_ref[...], xb)
    ones = jnp.ones((V_ROWS - d, tl), BF16)
    for n, ref in enumerate((vst_ref, vwt_ref)):
        for g in range(NSA_KV_HEADS):
            r0 = n * NSA_KVW + g * d
            ref[0, g, 0:d, :] = pt[r0:r0 + d, :].astype(BF16)
            ref[0, g, d:V_ROWS, :] = ones
    gate_ref[0] = pt[2 * NSA_KVW:, :]
    cos = cos_ref[...]
    sin_lo = sinlo_ref[...]
    sin_hi = sinhi_ref[...]
    roped = []
    for c in range(NSA_ROPE_W // 128):
        xc = p[:, c * 128:(c + 1) * 128]
        roped.append(xc * cos + pltpu.roll(xc, 96, 1) * sin_lo + pltpu.roll(xc, 32, 1) * sin_hi)

    def head(chunks, h):
        blk = chunks[h // 2]
        return blk[:, (h % 2) * d:(h % 2 + 1) * d]

    for h in range(NSA_HEADS):
        q_ref[0, h] = (head(roped, h) * LOG2_Q_SCALE).astype(BF16)
    nq = NSA_HEADS // 2
    pos = pl.program_id(1) * tl + lax.broadcasted_iota(jnp.int32, (tl, 128 - d), 0)
    block_onehot = jnp.where(pos // SEL_LEN == lax.broadcasted_iota(jnp.int32, (tl, 128 - d), 1), 1.0, 0.0)
    for g in range(NSA_KV_HEADS):
        ks_ref[0, g] = jnp.concatenate([head(roped[nq:], g), block_onehot], axis=1).astype(BF16)
        kw_ref[0, g] = head(roped[nq + 2:], g).astype(BF16)
    for n, ref in enumerate((kc_ref, vc_ref)):
        for g in range(NSA_KV_HEADS):
            c0 = NSA_ROPE_W + n * NSA_KVW + g * d
            ref[0, g] = p[:, c0:c0 + d]


def _inproj1(x, w_all, w_t, tl=512):
    B, L, D = x.shape
    tl = min(tl, L)
    G = NSA_KV_HEADS
    n_gate = w_t.shape[0] - 2 * NSA_KVW
    tables = _rope_roll_tables(jnp.arange(L))
    qh = jax.ShapeDtypeStruct((B, NSA_HEADS, L, HEAD_DIM), BF16)
    kvh = jax.ShapeDtypeStruct((B, G, L, HEAD_DIM), BF16)
    vth = jax.ShapeDtypeStruct((B, G, V_ROWS, L), BF16)
    q_spec = pl.BlockSpec((1, NSA_HEADS, tl, HEAD_DIM), lambda b, i: (b, 0, i, 0))
    kv_spec = pl.BlockSpec((1, G, tl, HEAD_DIM), lambda b, i: (b, 0, i, 0))
    ksel_spec = pl.BlockSpec((1, G, tl, 128), lambda b, i: (b, 0, i, 0))
    vt_spec = pl.BlockSpec((1, G, V_ROWS, tl), lambda b, i: (b, 0, 0, i))
    tab_spec = pl.BlockSpec((tl, 128), lambda b, i: (i, 0))
    return pl.pallas_call(
        _inproj1_body,
        grid=(B, L // tl),
        in_specs=[pl.BlockSpec((1, tl, D), lambda b, i: (b, i, 0)),
                  pl.BlockSpec(w_all.shape, lambda b, i: (0, 0)),
                  pl.BlockSpec(w_t.shape, lambda b, i: (0, 0)),
                  tab_spec, tab_spec, tab_spec],
        out_specs=([q_spec, ksel_spec] + [kv_spec] * 3 + [vt_spec] * 2
                   + [pl.BlockSpec((1, n_gate, tl), lambda b, i: (b, 0, i))]),
        out_shape=([qh, jax.ShapeDtypeStruct((B, G, L, 128), BF16), kvh]
                   + [jax.ShapeDtypeStruct((B, G, L, HEAD_DIM), F32)] * 2 + [vth] * 2
                   + [jax.ShapeDtypeStruct((B, n_gate, L), F32)]),
        compiler_params=_params("parallel", "parallel"),
        name="inproj1",
    )(x, w_all, w_t, *tables)


def _compress_body(kc_ref, vc_ref, posk_ref, posv_ref, w1k_ref, w1v_ref, w2k_ref, w2kr_ref, w2v_ref,
                   cos_ref, sin_ref, kcmp_ref, vcmp_ref):
    def hidden(a_ref, pos_ref, w1_ref):
        nrow = a_ref.shape[2] // CMP_STRIDE
        a = jnp.concatenate([a_ref[0, 0, pl.ds(l, nrow, stride=CMP_STRIDE), :] for l in range(CMP_STRIDE)],
                            axis=1).astype(F32)
        lo = _dot((a + pos_ref[0:1, :]).astype(BF16), w1_ref[0])
        hi = _dot((a + pos_ref[1:2, :]).astype(BF16), w1_ref[1])
        hi_next = pltpu.roll(hi, nrow - 1, 0)
        return _gelu(lo + hi_next).astype(BF16)

    hk = hidden(kc_ref, posk_ref, w1k_ref)
    kcmp = _dot(hk, w2k_ref[...]) * cos_ref[...] + _dot(hk, w2kr_ref[...]) * sin_ref[...]
    kcmp_ref[0, 0] = kcmp.astype(BF16)
    hv = hidden(vc_ref, posv_ref, w1v_ref)
    vcmp_ref[0, 0] = _dot_nt(w2v_ref[...], hv).astype(BF16)


def _compress(kc, vc, pos_k, w1_k, w2_k, pos_v, w1_v, w2_v):
    B, G, L, d = kc.shape
    half = CMP_STRIDE * d
    nb = L // CMP_STRIDE
    posk = pos_k.astype(F32).reshape(2, half)
    posv = pos_v.astype(F32).reshape(2, half)
    w1k = w1_k.astype(BF16).reshape(2, half, CMP_HIDDEN)
    w1v = w1_v.astype(BF16).reshape(2, half, CMP_HIDDEN)
    w2k = w2_k.astype(F32)
    cos, sin = _rope_tables(jnp.arange(nb) * CMP_STRIDE + CMP_LEN - 1)
    blk = pl.BlockSpec((1, 1, L, d), lambda b, g: (b, g, 0, 0))
    out = pl.BlockSpec((1, 1, nb, d), lambda b, g: (b, g, 0, 0))
    out_t = pl.BlockSpec((1, 1, d, nb), lambda b, g: (b, g, 0, 0))
    fix2 = lambda b, g: (0, 0)
    fix3 = lambda b, g: (0, 0, 0)
    return pl.pallas_call(
        _compress_body,
        grid=(B, G),
        in_specs=[blk, blk, pl.BlockSpec((2, half), fix2), pl.BlockSpec((2, half), fix2),
                  pl.BlockSpec((2, half, CMP_HIDDEN), fix3), pl.BlockSpec((2, half, CMP_HIDDEN), fix3),
                  pl.BlockSpec((CMP_HIDDEN, d), fix2), pl.BlockSpec((CMP_HIDDEN, d), fix2),
                  pl.BlockSpec((d, CMP_HIDDEN), fix2), pl.BlockSpec((nb, d), fix2), pl.BlockSpec((nb, d), fix2)],
        out_specs=[out, out_t],
        out_shape=[jax.ShapeDtypeStruct((B, G, nb, d), BF16), jax.ShapeDtypeStruct((B, G, d, nb), BF16)],
        compiler_params=_params("parallel", "parallel"),
        name="compress",
    )(kc, vc, posk, posv, w1k, w1v, w2k.astype(BF16), _rot_cols(w2k).astype(BF16), w2_v.T.astype(BF16), cos, sin)


NSA_KBLOCK = 256
NSA_GROUPS_PER_STEP = 4
NSA_HEADS_PER_CHAIN = 4
NSA_SCORE_LOOKAHEAD = 3


def _nsa_body(q_ref, kcmp_ref, vcmpt_ref, ks_ref, vst_ref, kw_ref, vwt_ref, gate_ref, o_ref, qsel_scr, *, seq):
    i = pl.program_id(2)
    tq = q_ref.shape[2]
    gb = kcmp_ref.shape[1]
    tk = min(NSA_KBLOCK, seq)
    R, d = NSA_REP, HEAD_DIM
    nq = R * tq
    nb = seq // SEL_LEN
    mc = seq // CMP_STRIDE
    t_row = i * tq + lax.broadcasted_iota(jnp.int32, (1, tq), 1)
    t_all = jnp.concatenate([t_row] * R, axis=1)
    m_col = lax.broadcasted_iota(jnp.int32, (mc, 1), 0)
    m_row = lax.broadcasted_iota(jnp.int32, (1, mc), 1)
    n_col = lax.broadcasted_iota(jnp.int32, (nb, 1), 0)

    def q_rows(g):
        return q_ref[0, g * R:(g + 1) * R].reshape(nq, d)

    valid_c = (m_col * CMP_STRIDE + (CMP_LEN - 1)) <= t_all
    ovl = jnp.logical_and(m_row * CMP_STRIDE < (n_col + 1) * SEL_LEN,
                          m_row * CMP_STRIDE + CMP_LEN > n_col * SEL_LEN)
    ovl = jnp.where(ovl, 1.0, 0.0).astype(BF16)
    cur = t_row // SEL_LEN
    forced = jnp.logical_or(n_col == 0, jnp.logical_or(n_col == cur, n_col == cur - 1))
    bonus = jnp.where(forced, FORCE_BONUS, 0.0)
    valid_s = n_col * SEL_LEN <= t_row
    o_c = []
    ties = [jnp.where(n_col > n2, 1.0, 0.0) for n2 in range(nb)]
    cmp_scores = [_dot_nt(kcmp_ref[0, g], q_rows(g)) for g in range(gb)]
    for g in range(gb):
        s = jnp.where(valid_c, cmp_scores[g], NEG)
        e = jnp.exp2(s - jnp.maximum(jnp.max(s, axis=0, keepdims=True), 0.5 * NEG))
        den = jnp.sum(e, axis=0, keepdims=True)
        p = e / jnp.where(den > 0.0, den, 1.0)
        o_c.append(_dot(vcmpt_ref[0, g], p.astype(BF16)))
        psum = p[:, 0:tq]
        for r in range(1, R):
            psum = psum + p[:, r * tq:(r + 1) * tq]
        ph, plo = _split_bf16(psum)
        score = jnp.where(valid_s, _dot(ovl, ph) + _dot(ovl, plo) + bonus, NEG)
        rank = jnp.zeros((nb, tq), F32)
        for n2 in range(nb):
            other = score[n2:n2 + 1, :]
            rank = rank + jnp.where(other > score, 1.0, jnp.where(other == score, ties[n2], 0.0))
        pen = jnp.where(rank < float(SEL_TOP), 0.0, NEG).T.astype(BF16)
        pen = jnp.concatenate([pen, jnp.zeros((tq, 128 - d - nb), BF16)], axis=1)
        qsel_scr[g] = jnp.concatenate([q_rows(g), jnp.concatenate([pen] * R, axis=0)], axis=1)

    k_col = lax.broadcasted_iota(jnp.int32, (tk, 1), 0)
    hp = NSA_HEADS_PER_CHAIN
    j_hi = (i * tq + tq - 1) // tk + 1

    def sweep(q_fn, k_ref, vt_ref, j_lo, bias_fn, bias_every_block):
        chains = [(g, h0) for g in range(gb) for h0 in range(0, R, hp)]

        def body(j, state, with_bias):
            k0 = pl.multiple_of(j * tk, tk)
            bias = jnp.concatenate([bias_fn(k0 + k_col)] * hp, axis=1) if with_bias else None

            def scores(c):
                g, h0 = chains[c]
                sc = _dot_nt(k_ref[0, g, pl.ds(k0, tk), :], q_fn(g, h0))
                return sc + bias if with_bias else sc

            new_state = []
            ahead = NSA_SCORE_LOOKAHEAD
            pending = [scores(c) for c in range(min(ahead, len(chains)))]
            for c, (m_run, acc) in enumerate(state):
                sc = pending.pop(0)
                if c + ahead < len(chains):
                    pending.append(scores(c + ahead))
                vtb = vt_ref[0, chains[c][0], :, pl.ds(k0, tk)]
                m_new = jnp.maximum(m_run, jnp.max(sc, axis=0, keepdims=True))
                pr = jnp.exp2(sc - m_new)
                new_state.append((m_new, jnp.exp2(m_run - m_new) * acc + _dot(vtb, pr.astype(BF16))))
            return tuple(new_state)

        state = tuple((jnp.full((1, hp * tq), NEG, F32), jnp.zeros((V_ROWS, hp * tq), F32)) for _ in chains)
        state = lax.fori_loop(j_lo, j_hi - 1, functools.partial(body, with_bias=bias_every_block), state)
        state = body(j_hi - 1, state, True)
        outs = []
        for g in range(gb):
            acc = jnp.concatenate([a for (cg, _), (_, a) in zip(chains, state) if cg == g], axis=1)
            outs.append(acc[0:d] / acc[d:d + 1])
        return outs

    def causal_bias(kpos):
        return jnp.where(kpos <= t_row, 0.0, NEG)

    o_s = sweep(lambda g, h0: qsel_scr[g, h0 * tq:(h0 + hp) * tq, :], ks_ref, vst_ref, 0, causal_bias, False)

    def win_bias(kpos):
        return jnp.where(jnp.logical_and(kpos <= t_row, kpos > t_row - WINDOW), 0.0, NEG)

    o_w = sweep(lambda g, h0: q_ref[0, g * R + h0:g * R + h0 + hp].reshape(hp * tq, d), kw_ref, vwt_ref,
                jnp.maximum(i * tq - (WINDOW - 1), 0) // tk, win_bias, True)

    outs = []
    for g in range(gb):
        row0 = (pl.program_id(1) * gb + g) * 3 * R

        def gate(branch):
            rows = [gate_ref[0, pl.ds(row0 + 3 * r + branch, 1), :] for r in range(R)]
            return _sigmoid(jnp.concatenate(rows, axis=1))

        o_t = gate(0) * o_c[g] + gate(1) * o_s[g] + gate(2) * o_w[g]
        for r in range(R):
            outs.append(o_t[:, r * tq:(r + 1) * tq].T)
    o_ref[0] = jnp.concatenate(outs, axis=-1).astype(o_ref.dtype)


def _nsa_attention(q, kcmp, vcmpt, ks, vst, kw, vwt, gate):
    B, H, L, d = q.shape
    G, R = NSA_KV_HEADS, NSA_REP
    gb = NSA_GROUPS_PER_STEP
    tq = min(ATT_BLOCK, L)
    mc = L // CMP_STRIDE
    k_spec = pl.BlockSpec((1, gb, L, d), lambda b, g, i: (b, g, 0, 0))
    ksel_spec = pl.BlockSpec((1, gb, L, 128), lambda b, g, i: (b, g, 0, 0))
    vt_spec = pl.BlockSpec((1, gb, V_ROWS, L), lambda b, g, i: (b, g, 0, 0))
    return pl.pallas_call(
        functools.partial(_nsa_body, seq=L),
        grid=(B, G // gb, L // tq),
        in_specs=[pl.BlockSpec((1, gb * R, tq, d), lambda b, g, i: (b, g, i, 0)),
                  pl.BlockSpec((1, gb, mc, d), lambda b, g, i: (b, g, 0, 0)),
                  pl.BlockSpec((1, gb, d, mc), lambda b, g, i: (b, g, 0, 0)),
                  ksel_spec, vt_spec, k_spec, vt_spec,
                  pl.BlockSpec((1, gate.shape[1], tq), lambda b, g, i: (b, 0, i))],
        out_specs=pl.BlockSpec((1, tq, gb * R * d), lambda b, g, i: (b, i, g)),
        out_shape=jax.ShapeDtypeStruct((B, L, H * d), BF16),
        scratch_shapes=[pltpu.VMEM((gb, R * tq, 128), BF16)],
        compiler_params=_params("parallel", "parallel", "arbitrary"),
        name="nsa_attention",
    )(q, kcmp, vcmpt, ks, vst, kw, vwt, gate)


def kernel(x, w_in_0, ssm_lam_re, ssm_lam_im, ssm_log_dt, ssm_b_re, ssm_b_im, ssm_c_re, ssm_c_im, ssm_d, w_glu, w_out_0, ln_mix_g_0, ln_mix_b_0, ln_ffn_g_0, ln_ffn_b_0, w1_0, w3_0, w2_0, w_in_1, cmp_pos_k, cmp_w1_k, cmp_w2_k, cmp_pos_v, cmp_w1_v, cmp_w2_v, w_out_1, ln_mix_g_1, ln_mix_b_1, ln_ffn_g_1, ln_ffn_b_1, w1_1, w3_1, w2_1, w_router, b_router):
    B, L, D = x.shape
    N = B * L
    vec = lambda a: a.astype(F32).reshape(1, D)

    def ffn(h, w1, w3, w2, g, b):
        gates_t, grp_t = _router(h, w_router, b_router)
        return _moe(h, gates_t, grp_t, w1.astype(BF16), w3.astype(BF16), w2.astype(BF16), vec(g), vec(b))

    q, k, v, u = _inproj0(x, w_in_0.astype(BF16))
    o_a = _sb_attention(q, k, v)
    y = _ssm(u, _ssm_tables(ssm_lam_re, ssm_lam_im, ssm_log_dt, ssm_b_re, ssm_b_im, ssm_c_re, ssm_c_im, ssm_d), B)
    w_out_0b = w_out_0.astype(BF16)
    h = _outproj0(x, o_a, y, w_glu.astype(BF16), w_out_0b[:SB_WIDTH], w_out_0b[SB_WIDTH:],
                  vec(ln_mix_g_0), vec(ln_mix_b_0)).reshape(N, D)
    h = ffn(h, w1_0, w3_0, w2_0, ln_ffn_g_0, ln_ffn_b_0)

    q, ks, kw, kc, vc, vst, vwt, gate = _inproj1(h.reshape(B, L, D), *_inproj1_weights(w_in_1))
    kcmp, vcmpt = _compress(kc, vc, cmp_pos_k, cmp_w1_k, cmp_w2_k, cmp_pos_v, cmp_w1_v, cmp_w2_v)
    o = _nsa_attention(q, kcmp, vcmpt, ks, vst, kw, vwt, gate).reshape(N, NSA_HEADS * HEAD_DIM)
    h = _outproj1(h, o, w_out_1.astype(BF16), vec(ln_mix_g_1), vec(ln_mix_b_1))
    h = ffn(h, w1_1, w3_1, w2_1, ln_ffn_g_1, ln_ffn_b_1)
    return h.reshape(B, L, D)
```

```python
import functools
import math

import numpy as np
import jax
import jax.numpy as jnp
from jax import lax
from jax.experimental import pallas as pl
from jax.experimental.pallas import tpu as pltpu

F32 = jnp.float32
BF16 = jnp.bfloat16

D_MODEL = 1024
DEPTH = 2
SB_HEADS = 8
HEAD_DIM = 64
SB_WIDTH = SB_HEADS * HEAD_DIM
SSM_WIDTH = D_MODEL - SB_WIDTH
SSM_GROUP = 16
SSM_GROUPS = SSM_WIDTH // SSM_GROUP
SSM_STATE = 64
SSM_CHUNK = 16
SSM_LANE_TILES = SSM_WIDTH // 128
SSM_TILE_GROUPS = 128 // SSM_GROUP
NSA_HEADS = 16
NSA_KV_HEADS = 4
NSA_REP = NSA_HEADS // NSA_KV_HEADS
CMP_LEN = 32
CMP_STRIDE = 16
CMP_HIDDEN = 256
SEL_LEN = 64
SEL_TOP = 8
WINDOW = 512
ROPE_THETA = 10000.0
FORCE_BONUS = 1e4
NEG = -1e30
N_EXPERTS = 16
N_GROUPS = 4
EXPERTS_PER_GROUP = N_EXPERTS // N_GROUPS
D_EXPERT = 512
ALPHA = (2 * DEPTH) ** 0.25
LN_EPS = 1e-5
ATTN_SCALE = HEAD_DIM ** -0.5
LOG2_Q_SCALE = ATTN_SCALE * math.log2(math.e)
ATT_BLOCK = 256
GELU_C = math.sqrt(2.0 / math.pi)


def _params(*sem):
    return pltpu.CompilerParams(dimension_semantics=sem, vmem_limit_bytes=56 * 1024 * 1024)


def _sigmoid(x):
    return 1.0 / (1.0 + jnp.exp(-x))


def _gelu(x):
    return 0.5 * x * (1.0 + jnp.tanh(GELU_C * (x + 0.044715 * (x * x * x))))


def _layer_norm(r, g, b):
    mu = jnp.mean(r, axis=-1, keepdims=True)
    d = r - mu
    var = jnp.mean(d * d, axis=-1, keepdims=True)
    return d * lax.rsqrt(var + LN_EPS) * g + b


def _dot(a, b):
    return jnp.dot(a, b, preferred_element_type=F32)


def _dot_nt(a, b):
    return lax.dot_general(a, b, (((1,), (1,)), ((), ())), preferred_element_type=F32)


def _split_bf16(x):
    hi = x.astype(BF16)
    lo = (x - hi.astype(F32)).astype(BF16)
    return hi, lo


def _inproj0_body(x_ref, w_ref, q_ref, k_ref, v_ref, u_ref, slab_scr):
    nb, tl, D = x_ref.shape
    p = _dot(x_ref[...].reshape(nb * tl, D).astype(BF16), w_ref[...])
    for h in range(SB_HEADS):
        c = h * HEAD_DIM
        q_ref[:, h] = (p[:, c:c + HEAD_DIM] * LOG2_Q_SCALE).astype(BF16).reshape(nb, tl, HEAD_DIM)
        k_ref[:, h] = p[:, SB_WIDTH + c:SB_WIDTH + c + HEAD_DIM].astype(BF16).reshape(nb, tl, HEAD_DIM)
        v_ref[:, h] = p[:, 2 * SB_WIDTH + c:2 * SB_WIDTH + c + HEAD_DIM].astype(BF16).reshape(nb, tl, HEAD_DIM)
    T = SSM_CHUNK
    for v in range(SSM_LANE_TILES):
        c0 = 3 * SB_WIDTH + v * 128
        slab_scr[...] = p[:, c0:c0 + 128]
        for c in range(tl // T):
            for s in range(T):
                u_ref[v, c * nb:(c + 1) * nb, s * 128:(s + 1) * 128] = slab_scr[pl.ds(c * T + s, nb, stride=tl), :]


def _inproj0(x, w_bf16, tl=64):
    B, L, D = x.shape
    tl = min(tl, L)
    nout = w_bf16.shape[1]
    T = SSM_CHUNK
    head = jax.ShapeDtypeStruct((B, SB_HEADS, L, HEAD_DIM), BF16)
    head_spec = pl.BlockSpec((B, SB_HEADS, tl, HEAD_DIM), lambda i: (0, 0, i, 0))
    return pl.pallas_call(
        _inproj0_body,
        grid=(L // tl,),
        in_specs=[pl.BlockSpec((B, tl, D), lambda i: (0, i, 0)),
                  pl.BlockSpec((D, nout), lambda i: (0, 0))],
        out_specs=[head_spec, head_spec, head_spec,
                   pl.BlockSpec((SSM_LANE_TILES, tl // T * B, T * 128), lambda i: (0, i, 0))],
        out_shape=[head, head, head,
                   jax.ShapeDtypeStruct((SSM_LANE_TILES, L // T * B, T * 128), F32)],
        scratch_shapes=[pltpu.VMEM((B * tl, 128), F32)],
        compiler_params=_params("parallel"),
        name="inproj0",
    )(x, w_bf16)


SB_BLOCK = 256
SB_HEADS_PER_STEP = 8
EXP2_UNDERFLOW = -151.0


def _sb_attn_body(q_ref, k_ref, v_ref, o_ref, acc_scr, cs_scr):
    i = pl.program_id(2)
    t = q_ref.shape[2]
    nh = q_ref.shape[1]
    row = lax.broadcasted_iota(jnp.int32, (t, t), 0)
    col = lax.broadcasted_iota(jnp.int32, (t, t), 1)
    suffix = jnp.where(row > col, 1.0, 0.0).astype(BF16)
    below = col < row

    def logits(hh, k0):
        return _dot_nt(q_ref[0, hh], k_ref[0, hh, pl.ds(k0, t), :])

    def alive(css):
        return (jnp.max(functools.reduce(jnp.maximum, css)) > EXP2_UNDERFLOW).astype(jnp.int32)

    def all_heads(k0, diagonal):
        zs = [logits(hh, k0) for hh in range(nh)]
        sps, l1s = [], []
        for z in zs:
            sp = jnp.maximum(z, 0.0) + jnp.log2(1.0 + jnp.exp2(-jnp.abs(z)))
            sps.append(sp)
            l1s.append(jnp.where(below, sp, 0.0) if diagonal else sp)
        within_all = _dot(jnp.concatenate([l1.astype(BF16) for l1 in l1s], axis=0), suffix)
        css = []
        for hh in range(nh):
            vb = v_ref[0, hh, pl.ds(k0, t), :]
            base = zs[hh] - sps[hh] - within_all[hh * t:(hh + 1) * t]
            if diagonal:
                w = jnp.where(below, jnp.exp2(base), 0.0)
                acc_scr[hh] = _dot(w.astype(BF16), vb)
                cs = -jnp.sum(l1s[hh], axis=1, keepdims=True)
            else:
                w = jnp.exp2(base + cs_scr[hh])
                acc_scr[hh] += _dot(w.astype(BF16), vb)
                cs = cs_scr[hh] - jnp.sum(l1s[hh], axis=1, keepdims=True)
            cs_scr[hh] = cs
            css.append(cs)
        return alive(css)

    first = all_heads(pl.multiple_of(i * t, t), True)

    def cond(c):
        return jnp.logical_and(c[0] <= i, c[1] > 0)

    def body(c):
        return c[0] + 1, all_heads(pl.multiple_of((i - c[0]) * t, t), False)

    lax.while_loop(cond, body, (jnp.int32(1), first))
    o_ref[0] = jnp.concatenate([acc_scr[hh] for hh in range(nh)], axis=-1).astype(o_ref.dtype)


def _sb_attention(q, k, v):
    B, H, L, d = q.shape
    t = min(SB_BLOCK, L)
    nh = SB_HEADS_PER_STEP
    return pl.pallas_call(
        _sb_attn_body,
        grid=(B, H // nh, L // t),
        in_specs=[pl.BlockSpec((1, nh, t, d), lambda b, h, i: (b, h, i, 0)),
                  pl.BlockSpec((1, nh, L, d), lambda b, h, i: (b, h, 0, 0)),
                  pl.BlockSpec((1, nh, L, d), lambda b, h, i: (b, h, 0, 0))],
        out_specs=pl.BlockSpec((1, t, nh * d), lambda b, h, i: (b, i, h)),
        out_shape=jax.ShapeDtypeStruct((B, L, H * d), BF16),
        scratch_shapes=[pltpu.VMEM((nh, t, d), F32), pltpu.VMEM((nh, t, 1), F32)],
        compiler_params=_params("parallel", "parallel", "arbitrary"),
        name="sb_attention",
    )(q, k, v)


def _ssm_tables(lam_re, lam_im, log_dt, b_re, b_im, c_re, c_im, d_skip):
    T, G, P, H = SSM_CHUNK, SSM_GROUPS, SSM_STATE, SSM_GROUP
    hp = lax.Precision.HIGHEST
    dt = jnp.exp(log_dt.astype(F32))[:, None]
    lr = lam_re.astype(F32)
    li = lam_im.astype(F32)
    mag = jnp.exp(lr * dt)
    a_re = mag * jnp.cos(li * dt)
    a_im = mag * jnp.sin(li * dt)
    den = lr * lr + li * li
    nr = a_re - 1.0
    f_re = (nr * lr + a_im * li) / den
    f_im = (a_im * lr - nr * li) / den
    br = b_re.astype(F32)
    bi = b_im.astype(F32)
    bbt_re = (f_re[..., None] * br - f_im[..., None] * bi).transpose(0, 2, 1)[:, None]
    bbt_im = (f_re[..., None] * bi + f_im[..., None] * br).transpose(0, 2, 1)[:, None]
    cr = c_re.astype(F32)[:, None]
    ci = c_im.astype(F32)[:, None]

    def powers(k):
        k = jnp.asarray(k, F32)[None, :, None]
        pmag = jnp.exp(k * (lr * dt)[:, None, :])
        ang = k * (li * dt)[:, None, :]
        return (pmag * jnp.cos(ang))[:, :, None, :], (pmag * jnp.sin(ang))[:, :, None, :]

    def cmul(xr, xi, yr, yi, rows):
        return ((xr * yr - xi * yi).reshape(G, rows, P), (xr * yi + xi * yr).reshape(G, rows, P))

    steps = np.arange(T)
    bs_re, bs_im = cmul(*powers(-steps), bbt_re, bbt_im, T * H)
    ct_re, ct_im = cmul(*powers(steps), cr, ci, T * H)
    full = (jnp.einsum('gap,gbp->gab', bs_re, ct_re, precision=hp)
            - jnp.einsum('gap,gbp->gab', bs_im, ct_im, precision=hp))
    row_s = lax.broadcasted_iota(jnp.int32, (T * H, T * H), 0) // H
    col_t = lax.broadcasted_iota(jnp.int32, (T * H, T * H), 1) // H
    intra = jnp.where(row_s <= col_t, full, 0.0)
    st_re, st_im = cmul(*powers(T - 1 - steps), bbt_re, bbt_im, T * H)
    to_st = jnp.concatenate([st_re, st_im, st_im, st_re], axis=-1)
    c1_re, c1_im = cmul(*powers(steps + 1), cr, ci, T * H)
    from_st_t = jnp.concatenate([c1_re, -c1_im], axis=-1)
    at_re, at_im = powers([T])
    at_re, at_im = at_re[:, 0, 0], at_im[:, 0, 0]
    zeros = jnp.zeros_like(at_re)
    adv = jnp.stack([jnp.concatenate([at_re, at_re], -1),
                     jnp.concatenate([-at_im, at_im], -1),
                     jnp.concatenate([at_im, -at_im], -1)]
                    + [jnp.concatenate([zeros, zeros], -1)] * 5, axis=1)
    dvec = jnp.tile(d_skip.astype(F32).reshape(G, 1, H), (1, 1, T))
    return intra, to_st, from_st_t, adv, dvec


def _ssm_expand(tables):
    intra, to_st, from_st_t, adv, dvec = tables
    V, W, T, H, P = SSM_LANE_TILES, SSM_TILE_GROUPS, SSM_CHUNK, SSM_GROUP, SSM_STATE
    n = T * W * H
    k_in = lax.broadcasted_iota(jnp.int32, (T * H, n), 0)
    col = lax.broadcasted_iota(jnp.int32, (T * H, n), 1)
    spread_th = (k_in == (col // 128) * H + col % H).astype(F32)
    spread_q = (k_in[:2 * P, :W * 2 * P] == col[:2 * P, :W * 2 * P] % (2 * P)).astype(F32)
    row_g = (lax.broadcasted_iota(jnp.int32, (n, 1), 0) // H) % W
    st_row_g = lax.broadcasted_iota(jnp.int32, (W * 2 * P, 1), 0) // (2 * P)
    col_g_th = (lax.broadcasted_iota(jnp.int32, (1, n), 1) // H) % W
    col_g_q = lax.broadcasted_iota(jnp.int32, (1, W * 2 * P), 1) // (2 * P)

    def blockdiag(rows, spread, row_group, col_group):
        full = jnp.einsum('vrk,kc->vrc', rows, spread)
        return jnp.where(row_group == col_group, full, 0.0).astype(BF16)

    by_row = lambda t: t.reshape(V, W, T, H, T * H).transpose(0, 2, 1, 3, 4).reshape(V, n, T * H)
    m8 = blockdiag(by_row(intra), spread_th, row_g, col_g_th)
    ws8 = blockdiag(by_row(to_st)[:, :, :2 * P], spread_q, row_g, col_g_q)
    wi_full = jnp.einsum('vwkq,kc->vwqc', from_st_t.reshape(V, W, T * H, 2 * P), spread_th).reshape(V, W * 2 * P, n)
    wi8 = jnp.where(st_row_g == col_g_th, wi_full, 0.0).astype(BF16)
    adv8 = adv.reshape(V, W, 8, 2 * P).transpose(0, 2, 1, 3).reshape(V, 8, W * 2 * P)
    d8 = jnp.broadcast_to(dvec.reshape(V, 1, W, T, H)[:, :, :, 0:1], (V, 1, W, T, H))
    d8 = d8.transpose(0, 1, 3, 2, 4).reshape(V, 1, T * W * H)
    return m8, ws8, wi8, adv8, d8


def _ssm_body(u_ref, m8_ref, ws8_ref, wi8_ref, adv_ref, dvec_ref, y_ref, s_scr, xin_scr, x1_scr, x2_scr, *, bsz):
    half = x1_scr.shape[1]

    @pl.when(pl.program_id(1) == 0)
    def _():
        x1_scr[...] = jnp.zeros_like(x1_scr)
        x2_scr[...] = jnp.zeros_like(x2_scr)

    u = u_ref[0]
    ub = u.astype(BF16)
    s1 = _dot(ub, ws8_ref[0])
    lane = lax.broadcasted_iota(jnp.int32, s1.shape, 1)
    s_scr[:, :half] = s1
    s_scr[:, half:] = jnp.where(lane % (2 * SSM_STATE) < SSM_STATE,
                                pltpu.roll(s1, half - SSM_STATE, 1), pltpu.roll(s1, SSM_STATE, 1))
    a1 = adv_ref[0, 0:1, :]
    a2 = adv_ref[0, 1:2, :]
    a3 = adv_ref[0, 2:3, :]

    def step(c, carry):
        x1, x2 = carry
        r0 = pl.multiple_of(c * bsz, bsz)
        xin_scr[pl.ds(r0, bsz), :] = x1
        s = s_scr[pl.ds(r0, bsz), :]
        return (a1 * x1 + a2 * x2 + s[:, :half], a1 * x2 + a3 * x1 + s[:, half:])

    x1, x2 = lax.fori_loop(0, u.shape[0] // bsz, step, (x1_scr[...], x2_scr[...]))
    x1_scr[...] = x1
    x2_scr[...] = x2
    y_ref[0] = _dot(ub, m8_ref[0]) + _dot(xin_scr[...].astype(BF16), wi8_ref[0]) + dvec_ref[0] * u


def _ssm(us, tables, bsz, rows=256):
    m8, ws8, wi8, adv8, d8 = _ssm_expand(tables)
    V, CB, width = us.shape
    B = bsz
    half = SSM_TILE_GROUPS * 2 * SSM_STATE
    rows = min(rows, CB)
    once = dict(pipeline_mode=pl.Buffered(1))
    y = pl.pallas_call(
        functools.partial(_ssm_body, bsz=B),
        grid=(V, CB // rows),
        in_specs=[pl.BlockSpec((1, rows, width), lambda v, r: (v, r, 0)),
                  pl.BlockSpec((1, width, width), lambda v, r: (v, 0, 0), **once),
                  pl.BlockSpec((1, width, half), lambda v, r: (v, 0, 0), **once),
                  pl.BlockSpec((1, half, width), lambda v, r: (v, 0, 0), **once),
                  pl.BlockSpec((1, 8, half), lambda v, r: (v, 0, 0)),
                  pl.BlockSpec((1, 1, width), lambda v, r: (v, 0, 0))],
        out_specs=pl.BlockSpec((1, rows, width), lambda v, r: (v, r, 0)),
        out_shape=jax.ShapeDtypeStruct((V, CB, width), F32),
        scratch_shapes=[pltpu.VMEM((rows, 2 * half), F32), pltpu.VMEM((rows, half), F32),
                        pltpu.VMEM((B, half), F32), pltpu.VMEM((B, half), F32)],
        compiler_params=_params("parallel", "arbitrary"),
        name="s5_scan",
    )(us, m8, ws8, wi8, adv8, d8)
    return y


def _outproj0_body(x_ref, oa_ref, y_ref, wglu_ref, woa_ref, wob_ref, g_ref, b_ref, o_ref, slab_scr):
    nb, tl, D = x_ref.shape
    T = SSM_CHUNK
    ys = []
    for v in range(SSM_LANE_TILES):
        for c in range(tl // T):
            for s in range(T):
                slab_scr[pl.ds(c * T + s, nb, stride=tl), :] = y_ref[v, c * nb:(c + 1) * nb, s * 128:(s + 1) * 128]
        ys.append(slab_scr[...])
    h = _gelu(jnp.concatenate(ys, axis=-1))
    ob = h * _sigmoid(_dot(h.astype(BF16), wglu_ref[...]))
    m = _dot(oa_ref[...].reshape(nb * tl, SB_WIDTH), woa_ref[...]) + _dot(ob.astype(BF16), wob_ref[...])
    r = ALPHA * x_ref[...].reshape(nb * tl, D) + m
    o_ref[...] = _layer_norm(r, g_ref[...], b_ref[...]).reshape(nb, tl, D)


def _outproj0(x, oa, ys, wglu, woa, wob, g, b, tl=64):
    B, L, D = x.shape
    tl = min(tl, L)
    T = SSM_CHUNK
    row = lambda i: (0, i, 0)
    fix = lambda i: (0, 0)
    return pl.pallas_call(
        _outproj0_body,
        grid=(L // tl,),
        in_specs=[pl.BlockSpec((B, tl, D), row), pl.BlockSpec((B, tl, SB_WIDTH), row),
                  pl.BlockSpec((SSM_LANE_TILES, tl // T * B, T * 128), row),
                  pl.BlockSpec((SSM_WIDTH, SSM_WIDTH), fix), pl.BlockSpec((SB_WIDTH, D), fix),
                  pl.BlockSpec((SSM_WIDTH, D), fix), pl.BlockSpec((1, D), fix), pl.BlockSpec((1, D), fix)],
        out_specs=pl.BlockSpec((B, tl, D), row),
        out_shape=jax.ShapeDtypeStruct((B, L, D), F32),
        scratch_shapes=[pltpu.VMEM((B * tl, 128), F32)],
        compiler_params=_params("parallel"),
        name="outproj0",
    )(x, oa, ys, wglu, woa, wob, g, b)


def _outproj1_body(x_ref, o_ref_in, w_ref, g_ref, b_ref, o_ref):
    m = _dot(o_ref_in[...], w_ref[...])
    o_ref[...] = _layer_norm(ALPHA * x_ref[...] + m, g_ref[...], b_ref[...])


def _outproj1(x, o, w, g, b, tm=1024):
    N, D = x.shape
    tm = min(tm, N)
    row = lambda i: (i, 0)
    fix = lambda i: (0, 0)
    return pl.pallas_call(
        _outproj1_body,
        grid=(N // tm,),
        in_specs=[pl.BlockSpec((tm, D), row), pl.BlockSpec((tm, o.shape[1]), row),
                  pl.BlockSpec(w.shape, fix), pl.BlockSpec((1, D), fix), pl.BlockSpec((1, D), fix)],
        out_specs=pl.BlockSpec((tm, D), row),
        out_shape=jax.ShapeDtypeStruct((N, D), F32),
        compiler_params=_params("parallel"),
        name="outproj1",
    )(x, o, w, g, b)


def _router_body(x_ref, whl_ref, b_ref, g_ref, grp_ref):
    NE = N_EXPERTS
    xh, xl = _split_bf16(x_ref[...])
    whl = whl_ref[...]
    part = _dot(xh, whl)
    logits = (part[:, :NE] + part[:, NE:] + _dot(xl, whl[:, :NE])).T
    aff = _sigmoid(logits)
    sel = aff + b_ref[...]
    E, K = EXPERTS_PER_GROUP, N_GROUPS
    s = [sel[e:e + 1, :] for e in range(N_EXPERTS)]
    a = [aff[e:e + 1, :] for e in range(N_EXPERTS)]
    gscore = []
    for k in range(K):
        v = s[k * E:(k + 1) * E]
        best = None
        for x in range(E):
            for y in range(x + 1, E):
                pair = v[x] + v[y]
                best = pair if best is None else jnp.maximum(best, pair)
        gscore.append(best)
    top = functools.reduce(jnp.maximum, gscore)
    is_g = []
    taken = None
    for k in range(K):
        hit = gscore[k] == top
        if taken is None:
            is_g.append(hit)
            taken = hit
        else:
            is_g.append(jnp.logical_and(hit, jnp.logical_not(taken)))
            taken = jnp.logical_or(taken, hit)

    def pick(rows, j):
        out = rows[(K - 1) * E + j]
        for k in range(K - 2, -1, -1):
            out = jnp.where(is_g[k], rows[k * E + j], out)
        return out

    v = [pick(s, j) for j in range(E)]
    av = [pick(a, j) for j in range(E)]
    chosen = []
    for j in range(E):
        r = jnp.zeros_like(v[j])
        for j2 in range(E):
            if j2 == j:
                continue
            ahead = (v[j2] >= v[j]) if j2 < j else (v[j2] > v[j])
            r = r + jnp.where(ahead, 1.0, 0.0)
        chosen.append(r < 2.0)
    wj = [jnp.where(chosen[j], av[j], 0.0) for j in range(E)]
    tot = wj[0] + wj[1] + wj[2] + wj[3]
    gj = [w / tot for w in wj]
    rows = [jnp.where(is_g[e // E], gj[e % E], 0.0) for e in range(N_EXPERTS)]
    g_ref[...] = jnp.concatenate(rows, axis=0)
    grp_ref[...] = jnp.concatenate([jnp.where(is_g[k], 1.0, 0.0) for k in range(K)], axis=0)


def _router(x, w_router, b_router, tm=1024):
    N, D = x.shape
    tm = min(tm, N)
    whl = jnp.concatenate(_split_bf16(w_router.astype(F32)), axis=1)
    fix = lambda i: (0, 0)
    return pl.pallas_call(
        _router_body,
        grid=(N // tm,),
        in_specs=[pl.BlockSpec((tm, D), lambda i: (i, 0)), pl.BlockSpec((D, 2 * N_EXPERTS), fix),
                  pl.BlockSpec((N_EXPERTS, 1), fix)],
        out_specs=[pl.BlockSpec((N_EXPERTS, tm), lambda i: (0, i)), pl.BlockSpec((N_GROUPS, tm), lambda i: (0, i))],
        out_shape=[jax.ShapeDtypeStruct((N_EXPERTS, N), F32), jax.ShapeDtypeStruct((N_GROUPS, N), F32)],
        compiler_params=_params("parallel"),
        name="router",
    )(x, whl, b_router.astype(F32).reshape(N_EXPERTS, 1))


MOE_TILE = 1024
MOE_PASS_ROWS = 304
MOE_CHUNK = 256
MOE_EXPERTS_PER_STEP = 2
ROW_ALIGN = 16


def _moe_body(seg_ref, x_ref, gt_ref, grp_ref, w1_ref, w3_ref, w2_ref, g_ref, b_ref, o_ref,
              xs_scr, ys_scr, pt_scr, gs_scr):
    i = pl.program_id(0)
    e = pl.program_id(1)
    T = x_ref.shape[0]
    K, NE = N_GROUPS, N_EXPERTS
    cap = MOE_PASS_ROWS
    ch = min(MOE_CHUNK, T)

    @pl.when(e == 0)
    def _():
        grp = grp_ref[...]
        xb = x_ref[...].astype(BF16)
        gates = gt_ref[...]
        gh, gm = _split_bf16(gates)
        gl = (gates - gh.astype(F32) - gm.astype(F32)).astype(BF16)
        g3 = jnp.concatenate([gh, gm, gl], axis=0)
        cnt = jnp.sum(grp, axis=1, keepdims=True)
        off = [jnp.zeros((1, 1), F32)]
        for k in range(K - 1):
            off.append(off[-1] + cnt[k:k + 1])
        r_i = lax.broadcasted_iota(jnp.int32, (ch, T), 0)
        c_i = lax.broadcasted_iota(jnp.int32, (ch, T), 1)
        lane = lax.broadcasted_iota(jnp.int32, (2 * K, T), 1)
        run = jnp.concatenate([grp, jnp.zeros_like(grp)], axis=0)
        sh = 1
        while sh < T:
            run = run + jnp.where(lane >= sh, pltpu.roll(run, sh, 1), 0.0)
            sh *= 2
        pos_row = jnp.zeros((1, T), F32)
        for k in range(K):
            pos_row = pos_row + grp[k:k + 1, :] * (off[k] + run[k:k + 1, :] - 1.0)
        pos_t = jnp.broadcast_to(pos_row, (2 * K, T))
        pos_col = [pos_t[:, c * ch:(c + 1) * ch].T[:, 0:1] for c in range(T // ch)]
        for c in range(T // ch):
            c0 = c * ch
            perm = jnp.where((r_i + c0).astype(F32) == pos_row, 1.0, 0.0).astype(BF16)
            xs_scr[c0:c0 + ch, :] = _dot(perm, xb).astype(BF16)
            g3s = _dot_nt(perm, g3)
            gs_scr[c0:c0 + ch, :] = g3s[:, :NE] + g3s[:, NE:2 * NE] + g3s[:, 2 * NE:]
            pt_scr[c0:c0 + ch, :] = jnp.where(c_i.astype(F32) == pos_col[c], 1.0, 0.0).astype(BF16)
        xs_scr[T:, :] = jnp.zeros((cap, xs_scr.shape[1]), BF16)
        gs_scr[T:, :] = jnp.zeros((cap, gs_scr.shape[1]), F32)
        ys_scr[...] = jnp.zeros_like(ys_scr)

    ne = w1_ref.shape[0]
    k = (e * ne) // EXPERTS_PER_GROUP
    off = seg_ref[i, k]
    cnt = seg_ref[i, K + k]
    start0 = (off // ROW_ALIGN) * ROW_ALIGN
    n_pass = (off + cnt - start0 + cap - 1) // cap

    def one_pass(n, c):
        start = pl.multiple_of(start0 + n * cap, ROW_ALIGN)
        xc = xs_scr[pl.ds(start, cap), :]
        gsc = gs_scr[pl.ds(start, cap), :]
        lane = lax.broadcasted_iota(jnp.int32, gsc.shape, 1)
        y = None
        for x in range(ne):
            a = _dot(xc, w1_ref[x])
            hm = (a * _sigmoid(a)) * _dot(xc, w3_ref[x])
            gcol = jnp.sum(jnp.where(lane == e * ne + x, gsc, 0.0), axis=1, keepdims=True)
            yx = gcol * _dot(hm.astype(BF16), w2_ref[x])
            y = yx if y is None else y + yx
        ys_scr[pl.ds(start, cap), :] += y
        return c

    lax.fori_loop(0, n_pass, one_pass, 0)

    @pl.when(e == pl.num_programs(1) - 1)
    def _():
        y = _dot(pt_scr[...], ys_scr[0:T, :].astype(BF16))
        o_ref[...] = _layer_norm(ALPHA * x_ref[...] + y, g_ref[...], b_ref[...])


def _moe(x, gates_t, grp_t, w1, w3, w2, g, b):
    N, D = x.shape
    T = min(MOE_TILE, N)
    K = N_GROUPS
    cnt = grp_t.reshape(K, N // T, T).sum(-1).astype(jnp.int32).T
    seg = jnp.concatenate([jnp.cumsum(cnt, axis=1) - cnt, cnt], axis=1)
    fix = lambda i, e, s: (0, 0)
    ne = MOE_EXPERTS_PER_STEP
    rows = T + MOE_PASS_ROWS
    grid_spec = pltpu.PrefetchScalarGridSpec(
        num_scalar_prefetch=1,
        grid=(N // T, N_EXPERTS // ne),
        in_specs=[pl.BlockSpec((T, D), lambda i, e, s: (i, 0)),
                  pl.BlockSpec((N_EXPERTS, T), lambda i, e, s: (0, i)),
                  pl.BlockSpec((K, T), lambda i, e, s: (0, i)),
                  pl.BlockSpec((ne, D, D_EXPERT), lambda i, e, s: (e, 0, 0)),
                  pl.BlockSpec((ne, D, D_EXPERT), lambda i, e, s: (e, 0, 0)),
                  pl.BlockSpec((ne, D_EXPERT, D), lambda i, e, s: (e, 0, 0)),
                  pl.BlockSpec((1, D), fix), pl.BlockSpec((1, D), fix)],
        out_specs=pl.BlockSpec((T, D), lambda i, e, s: (i, 0)),
        scratch_shapes=[pltpu.VMEM((rows, D), BF16), pltpu.VMEM((rows, D), F32),
                        pltpu.VMEM((T, T), BF16), pltpu.VMEM((rows, N_EXPERTS), F32)])
    return pl.pallas_call(
        _moe_body,
        grid_spec=grid_spec,
        out_shape=jax.ShapeDtypeStruct((N, D), F32),
        compiler_params=_params("parallel", "arbitrary"),
        name="moe",
    )(seg, x, gates_t, grp_t, w1, w3, w2, g, b)


NSA_KVW = NSA_KV_HEADS * HEAD_DIM
NSA_ROPE_W = NSA_HEADS * HEAD_DIM + 2 * NSA_KVW


def _rot_cols(w):
    K, n = w.shape
    w3 = w.reshape(K, n // HEAD_DIM, 2, HEAD_DIM // 2)
    return jnp.stack([-w3[:, :, 1], w3[:, :, 0]], axis=2).reshape(K, n)


def _inproj1_weights(w_in):
    H, G, d = NSA_HEADS, NSA_KV_HEADS, HEAD_DIM
    cuts = [H * d + i * NSA_KVW for i in range(7)]
    q, kc, vc, ks, vs, kw, vw, gate = jnp.split(w_in.astype(F32), cuts, axis=1)
    w_all = jnp.concatenate([q, ks, kw, kc, vc], axis=1).astype(BF16)
    w_t = jnp.concatenate([vs, vw, gate], axis=1).T.astype(BF16)
    return w_all, w_t


def _rope_tables(pos):
    inv = ROPE_THETA ** (-jnp.arange(0, HEAD_DIM, 2, dtype=F32) / HEAD_DIM)
    ang = pos.astype(F32)[:, None] * inv[None, :]
    c, s = jnp.cos(ang), jnp.sin(ang)
    return jnp.concatenate([c, c], axis=-1), jnp.concatenate([s, s], axis=-1)


def _rope_roll_tables(pos):
    inv = ROPE_THETA ** (-jnp.arange(0, HEAD_DIM, 2, dtype=F32) / HEAD_DIM)
    ang = pos.astype(F32)[:, None] * inv[None, :]
    c, s, z = jnp.cos(ang), jnp.sin(ang), jnp.zeros_like(ang)
    two = lambda a, b: jnp.concatenate([a, b, a, b], axis=-1)
    return two(c, c), two(-s, z), two(z, s)


V_ROWS = HEAD_DIM + 16


def _inproj1_body(x_ref, w_ref, wt_ref, cos_ref, sinlo_ref, sinhi_ref, q_ref, ks_ref, kw_ref, kc_ref, vc_ref,
                  vst_ref, vwt_ref, gate_ref):
    xb = x_ref[0].astype(BF16)
    tl = xb.shape[0]
    d = HEAD_DIM
    p = _dot(xb, w_ref[...])
    pt = _dot_nt(wt_ref[...], xb)
    ones = jnp.ones((V_ROWS - d, tl), BF16)
    for n, ref in enumerate((vst_ref, vwt_ref)):
        for g in range(NSA_KV_HEADS):
            r0 = n * NSA_KVW + g * d
            ref[0, g, 0:d, :] = pt[r0:r0 + d, :].astype(BF16)
            ref[0, g, d:V_ROWS, :] = ones
    gate_ref[0] = pt[2 * NSA_KVW:, :]
    cos = cos_ref[...]
    sin_lo = sinlo_ref[...]
    sin_hi = sinhi_ref[...]
    roped = []
    for c in range(NSA_ROPE_W // 128):
        xc = p[:, c * 128:(c + 1) * 128]
        roped.append(xc * cos + pltpu.roll(xc, 96, 1) * sin_lo + pltpu.roll(xc, 32, 1) * sin_hi)

    def head(chunks, h):
        blk = chunks[h // 2]
        return blk[:, (h % 2) * d:(h % 2 + 1) * d]

    for h in range(NSA_HEADS):
        q_ref[0, h] = (head(roped, h) * LOG2_Q_SCALE).astype(BF16)
    nq = NSA_HEADS // 2
    pos = pl.program_id(1) * tl + lax.broadcasted_iota(jnp.int32, (tl, 128 - d), 0)
    block_onehot = jnp.where(pos // SEL_LEN == lax.broadcasted_iota(jnp.int32, (tl, 128 - d), 1), 1.0, 0.0)
    for g in range(NSA_KV_HEADS):
        ks_ref[0, g] = jnp.concatenate([head(roped[nq:], g), block_onehot], axis=1).astype(BF16)
        kw_ref[0, g] = head(roped[nq + 2:], g).astype(BF16)
    for n, ref in enumerate((kc_ref, vc_ref)):
        for g in range(NSA_KV_HEADS):
            c0 = NSA_ROPE_W + n * NSA_KVW + g * d
            ref[0, g] = p[:, c0:c0 + d]


def _inproj1(x, w_all, w_t, tl=512):
    B, L, D = x.shape
    tl = min(tl, L)
    G = NSA_KV_HEADS
    n_gate = w_t.shape[0] - 2 * NSA_KVW
    tables = _rope_roll_tables(jnp.arange(L))
    qh = jax.ShapeDtypeStruct((B, NSA_HEADS, L, HEAD_DIM), BF16)
    kvh = jax.ShapeDtypeStruct((B, G, L, HEAD_DIM), BF16)
    vth = jax.ShapeDtypeStruct((B, G, V_ROWS, L), BF16)
    q_spec = pl.BlockSpec((1, NSA_HEADS, tl, HEAD_DIM), lambda b, i: (b, 0, i, 0))
    kv_spec = pl.BlockSpec((1, G, tl, HEAD_DIM), lambda b, i: (b, 0, i, 0))
    ksel_spec = pl.BlockSpec((1, G, tl, 128), lambda b, i: (b, 0, i, 0))
    vt_spec = pl.BlockSpec((1, G, V_ROWS, tl), lambda b, i: (b, 0, 0, i))
    tab_spec = pl.BlockSpec((tl, 128), lambda b, i: (i, 0))
    return pl.pallas_call(
        _inproj1_body,
        grid=(B, L // tl),
        in_specs=[pl.BlockSpec((1, tl, D), lambda b, i: (b, i, 0)),
                  pl.BlockSpec(w_all.shape, lambda b, i: (0, 0)),
                  pl.BlockSpec(w_t.shape, lambda b, i: (0, 0)),
                  tab_spec, tab_spec, tab_spec],
        out_specs=([q_spec, ksel_spec] + [kv_spec] * 3 + [vt_spec] * 2
                   + [pl.BlockSpec((1, n_gate, tl), lambda b, i: (b, 0, i))]),
        out_shape=([qh, jax.ShapeDtypeStruct((B, G, L, 128), BF16), kvh]
                   + [jax.ShapeDtypeStruct((B, G, L, HEAD_DIM), F32)] * 2 + [vth] * 2
                   + [jax.ShapeDtypeStruct((B, n_gate, L), F32)]),
        compiler_params=_params("parallel", "parallel"),
        name="inproj1",
    )(x, w_all, w_t, *tables)


def _compress_body(kc_ref, vc_ref, posk_ref, posv_ref, w1k_ref, w1v_ref, w2k_ref, w2kr_ref, w2v_ref,
                   cos_ref, sin_ref, kcmp_ref, vcmp_ref):
    def hidden(a_ref, pos_ref, w1_ref):
        nrow = a_ref.shape[2] // CMP_STRIDE
        a = jnp.concatenate([a_ref[0, 0, pl.ds(l, nrow, stride=CMP_STRIDE), :] for l in range(CMP_STRIDE)],
                            axis=1).astype(F32)
        lo = _dot((a + pos_ref[0:1, :]).astype(BF16), w1_ref[0])
        hi = _dot((a + pos_ref[1:2, :]).astype(BF16), w1_ref[1])
        hi_next = pltpu.roll(hi, nrow - 1, 0)
        return _gelu(lo + hi_next).astype(BF16)

    hk = hidden(kc_ref, posk_ref, w1k_ref)
    kcmp = _dot(hk, w2k_ref[...]) * cos_ref[...] + _dot(hk, w2kr_ref[...]) * sin_ref[...]
    kcmp_ref[0, 0] = kcmp.astype(BF16)
    hv = hidden(vc_ref, posv_ref, w1v_ref)
    vcmp_ref[0, 0] = _dot_nt(w2v_ref[...], hv).astype(BF16)


def _compress(kc, vc, pos_k, w1_k, w2_k, pos_v, w1_v, w2_v):
    B, G, L, d = kc.shape
    half = CMP_STRIDE * d
    nb = L // CMP_STRIDE
    posk = pos_k.astype(F32).reshape(2, half)
    posv = pos_v.astype(F32).reshape(2, half)
    w1k = w1_k.astype(BF16).reshape(2, half, CMP_HIDDEN)
    w1v = w1_v.astype(BF16).reshape(2, half, CMP_HIDDEN)
    w2k = w2_k.astype(F32)
    cos, sin = _rope_tables(jnp.arange(nb) * CMP_STRIDE + CMP_LEN - 1)
    blk = pl.BlockSpec((1, 1, L, d), lambda b, g: (b, g, 0, 0))
    out = pl.BlockSpec((1, 1, nb, d), lambda b, g: (b, g, 0, 0))
    out_t = pl.BlockSpec((1, 1, d, nb), lambda b, g: (b, g, 0, 0))
    fix2 = lambda b, g: (0, 0)
    fix3 = lambda b, g: (0, 0, 0)
    return pl.pallas_call(
        _compress_body,
        grid=(B, G),
        in_specs=[blk, blk, pl.BlockSpec((2, half), fix2), pl.BlockSpec((2, half), fix2),
                  pl.BlockSpec((2, half, CMP_HIDDEN), fix3), pl.BlockSpec((2, half, CMP_HIDDEN), fix3),
                  pl.BlockSpec((CMP_HIDDEN, d), fix2), pl.BlockSpec((CMP_HIDDEN, d), fix2),
                  pl.BlockSpec((d, CMP_HIDDEN), fix2), pl.BlockSpec((nb, d), fix2), pl.BlockSpec((nb, d), fix2)],
        out_specs=[out, out_t],
        out_shape=[jax.ShapeDtypeStruct((B, G, nb, d), BF16), jax.ShapeDtypeStruct((B, G, d, nb), BF16)],
        compiler_params=_params("parallel", "parallel"),
        name="compress",
    )(kc, vc, posk, posv, w1k, w1v, w2k.astype(BF16), _rot_cols(w2k).astype(BF16), w2_v.T.astype(BF16), cos, sin)


NSA_KBLOCK = 256
NSA_GROUPS_PER_STEP = 4
NSA_HEADS_PER_CHAIN = 4
NSA_SCORE_LOOKAHEAD = 3


def _nsa_body(q_ref, kcmp_ref, vcmpt_ref, ks_ref, vst_ref, kw_ref, vwt_ref, gate_ref, o_ref, qsel_scr, *, seq):
    i = pl.program_id(2)
    tq = q_ref.shape[2]
    gb = kcmp_ref.shape[1]
    tk = min(NSA_KBLOCK, seq)
    R, d = NSA_REP, HEAD_DIM
    nq = R * tq
    nb = seq // SEL_LEN
    mc = seq // CMP_STRIDE
    t_row = i * tq + lax.broadcasted_iota(jnp.int32, (1, tq), 1)
    t_all = jnp.concatenate([t_row] * R, axis=1)
    m_col = lax.broadcasted_iota(jnp.int32, (mc, 1), 0)
    m_row = lax.broadcasted_iota(jnp.int32, (1, mc), 1)
    n_col = lax.broadcasted_iota(jnp.int32, (nb, 1), 0)

    def q_rows(g):
        return q_ref[0, g * R:(g + 1) * R].reshape(nq, d)

    valid_c = (m_col * CMP_STRIDE + (CMP_LEN - 1)) <= t_all
    ovl = jnp.logical_and(m_row * CMP_STRIDE < (n_col + 1) * SEL_LEN,
                          m_row * CMP_STRIDE + CMP_LEN > n_col * SEL_LEN)
    ovl = jnp.where(ovl, 1.0, 0.0).astype(BF16)
    cur = t_row // SEL_LEN
    forced = jnp.logical_or(n_col == 0, jnp.logical_or(n_col == cur, n_col == cur - 1))
    bonus = jnp.where(forced, FORCE_BONUS, 0.0)
    valid_s = n_col * SEL_LEN <= t_row
    o_c = []
    ties = [jnp.where(n_col > n2, 1.0, 0.0) for n2 in range(nb)]
    cmp_scores = [_dot_nt(kcmp_ref[0, g], q_rows(g)) for g in range(gb)]
    for g in range(gb):
        s = jnp.where(valid_c, cmp_scores[g], NEG)
        e = jnp.exp2(s - jnp.maximum(jnp.max(s, axis=0, keepdims=True), 0.5 * NEG))
        den = jnp.sum(e, axis=0, keepdims=True)
        p = e / jnp.where(den > 0.0, den, 1.0)
        o_c.append(_dot(vcmpt_ref[0, g], p.astype(BF16)))
        psum = p[:, 0:tq]
        for r in range(1, R):
            psum = psum + p[:, r * tq:(r + 1) * tq]
        ph, plo = _split_bf16(psum)
        score = jnp.where(valid_s, _dot(ovl, ph) + _dot(ovl, plo) + bonus, NEG)
        rank = jnp.zeros((nb, tq), F32)
        for n2 in range(nb):
            other = score[n2:n2 + 1, :]
            rank = rank + jnp.where(other > score, 1.0, jnp.where(other == score, ties[n2], 0.0))
        pen = jnp.where(rank < float(SEL_TOP), 0.0, NEG).T.astype(BF16)
        pen = jnp.concatenate([pen, jnp.zeros((tq, 128 - d - nb), BF16)], axis=1)
        qsel_scr[g] = jnp.concatenate([q_rows(g), jnp.concatenate([pen] * R, axis=0)], axis=1)

    k_col = lax.broadcasted_iota(jnp.int32, (tk, 1), 0)
    hp = NSA_HEADS_PER_CHAIN
    j_hi = (i * tq + tq - 1) // tk + 1

    def sweep(q_fn, k_ref, vt_ref, j_lo, bias_fn, bias_every_block):
        chains = [(g, h0) for g in range(gb) for h0 in range(0, R, hp)]

        def body(j, state, with_bias):
            k0 = pl.multiple_of(j * tk, tk)
            bias = jnp.concatenate([bias_fn(k0 + k_col)] * hp, axis=1) if with_bias else None

            def scores(c):
                g, h0 = chains[c]
                sc = _dot_nt(k_ref[0, g, pl.ds(k0, tk), :], q_fn(g, h0))
                return sc + bias if with_bias else sc

            new_state = []
            ahead = NSA_SCORE_LOOKAHEAD
            pending = [scores(c) for c in range(min(ahead, len(chains)))]
            for c, (m_run, acc) in enumerate(state):
                sc = pending.pop(0)
                if c + ahead < len(chains):
                    pending.append(scores(c + ahead))
                vtb = vt_ref[0, chains[c][0], :, pl.ds(k0, tk)]
                m_new = jnp.maximum(m_run, jnp.max(sc, axis=0, keepdims=True))
                pr = jnp.exp2(sc - m_new)
                new_state.append((m_new, jnp.exp2(m_run - m_new) * acc + _dot(vtb, pr.astype(BF16))))
            return tuple(new_state)

        state = tuple((jnp.full((1, hp * tq), NEG, F32), jnp.zeros((V_ROWS, hp * tq), F32)) for _ in chains)
        state = lax.fori_loop(j_lo, j_hi - 1, functools.partial(body, with_bias=bias_every_block), state)
        state = body(j_hi - 1, state, True)
        outs = []
        for g in range(gb):
            acc = jnp.concatenate([a for (cg, _), (_, a) in zip(chains, state) if cg == g], axis=1)
            outs.append(acc[0:d] / acc[d:d + 1])
        return outs

    def causal_bias(kpos):
        return jnp.where(kpos <= t_row, 0.0, NEG)

    o_s = sweep(lambda g, h0: qsel_scr[g, h0 * tq:(h0 + hp) * tq, :], ks_ref, vst_ref, 0, causal_bias, False)

    def win_bias(kpos):
        return jnp.where(jnp.logical_and(kpos <= t_row, kpos > t_row - WINDOW), 0.0, NEG)

    o_w = sweep(lambda g, h0: q_ref[0, g * R + h0:g * R + h0 + hp].reshape(hp * tq, d), kw_ref, vwt_ref,
                jnp.maximum(i * tq - (WINDOW - 1), 0) // tk, win_bias, True)

    outs = []
    for g in range(gb):
        row0 = (pl.program_id(1) * gb + g) * 3 * R

        def gate(branch):
            rows = [gate_ref[0, pl.ds(row0 + 3 * r + branch, 1), :] for r in range(R)]
            return _sigmoid(jnp.concatenate(rows, axis=1))

        o_t = gate(0) * o_c[g] + gate(1) * o_s[g] + gate(2) * o_w[g]
        for r in range(R):
            outs.append(o_t[:, r * tq:(r + 1) * tq].T)
    o_ref[0] = jnp.concatenate(outs, axis=-1).astype(o_ref.dtype)


def _nsa_attention(q, kcmp, vcmpt, ks, vst, kw, vwt, gate):
    B, H, L, d = q.shape
    G, R = NSA_KV_HEADS, NSA_REP
    gb = NSA_GROUPS_PER_STEP
    tq = min(ATT_BLOCK, L)
    mc = L // CMP_STRIDE
    k_spec = pl.BlockSpec((1, gb, L, d), lambda b, g, i: (b, g, 0, 0))
    ksel_spec = pl.BlockSpec((1, gb, L, 128), lambda b, g, i: (b, g, 0, 0))
    vt_spec = pl.BlockSpec((1, gb, V_ROWS, L), lambda b, g, i: (b, g, 0, 0))
    return pl.pallas_call(
        functools.partial(_nsa_body, seq=L),
        grid=(B, G // gb, L // tq),
        in_specs=[pl.BlockSpec((1, gb * R, tq, d), lambda b, g, i: (b, g, i, 0)),
                  pl.BlockSpec((1, gb, mc, d), lambda b, g, i: (b, g, 0, 0)),
                  pl.BlockSpec((1, gb, d, mc), lambda b, g, i: (b, g, 0, 0)),
                  ksel_spec, vt_spec, k_spec, vt_spec,
                  pl.BlockSpec((1, gate.shape[1], tq), lambda b, g, i: (b, 0, i))],
        out_specs=pl.BlockSpec((1, tq, gb * R * d), lambda b, g, i: (b, i, g)),
        out_shape=jax.ShapeDtypeStruct((B, L, H * d), BF16),
        scratch_shapes=[pltpu.VMEM((gb, R * tq, 128), BF16)],
        compiler_params=_params("parallel", "parallel", "arbitrary"),
        name="nsa_attention",
    )(q, kcmp, vcmpt, ks, vst, kw, vwt, gate)


def kernel(x, w_in_0, ssm_lam_re, ssm_lam_im, ssm_log_dt, ssm_b_re, ssm_b_im, ssm_c_re, ssm_c_im, ssm_d, w_glu, w_out_0, ln_mix_g_0, ln_mix_b_0, ln_ffn_g_0, ln_ffn_b_0, w1_0, w3_0, w2_0, w_in_1, cmp_pos_k, cmp_w1_k, cmp_w2_k, cmp_pos_v, cmp_w1_v, cmp_w2_v, w_out_1, ln_mix_g_1, ln_mix_b_1, ln_ffn_g_1, ln_ffn_b_1, w1_1, w3_1, w2_1, w_router, b_router):
    B, L, D = x.shape
    N = B * L
    vec = lambda a: a.astype(F32).reshape(1, D)

    def ffn(h, w1, w3, w2, g, b):
        gates_t, grp_t = _router(h, w_router, b_router)
        return _moe(h, gates_t, grp_t, w1.astype(BF16), w3.astype(BF16), w2.astype(BF16), vec(g), vec(b))

    q, k, v, u = _inproj0(x, w_in_0.astype(BF16))
    o_a = _sb_attention(q, k, v)
    y = _ssm(u, _ssm_tables(ssm_lam_re, ssm_lam_im, ssm_log_dt, ssm_b_re, ssm_b_im, ssm_c_re, ssm_c_im, ssm_d), B)
    w_out_0b = w_out_0.astype(BF16)
    h = _outproj0(x, o_a, y, w_glu.astype(BF16), w_out_0b[:SB_WIDTH], w_out_0b[SB_WIDTH:],
                  vec(ln_mix_g_0), vec(ln_mix_b_0)).reshape(N, D)
    h = ffn(h, w1_0, w3_0, w2_0, ln_ffn_g_0, ln_ffn_b_0)

    q, ks, kw, kc, vc, vst, vwt, gate = _inproj1(h.reshape(B, L, D), *_inproj1_weights(w_in_1))
    kcmp, vcmpt = _compress(kc, vc, cmp_pos_k, cmp_w1_k, cmp_w2_k, cmp_pos_v, cmp_w1_v, cmp_w2_v)
    o = _nsa_attention(q, kcmp, vcmpt, ks, vst, kw, vwt, gate).reshape(N, NSA_HEADS * HEAD_DIM)
    h = _outproj1(h, o, w_out_1.astype(BF16), vec(ln_mix_g_1), vec(ln_mix_b_1))
    h = ffn(h, w1_1, w3_1, w2_1, ln_ffn_g_1, ln_ffn_b_1)
    return h.reshape(B, L, D)
```

```python
import functools
import math

import numpy as np
import jax
import jax.numpy as jnp
from jax import lax
from jax.experimental import pallas as pl
from jax.experimental.pallas import tpu as pltpu

F32 = jnp.float32
BF16 = jnp.bfloat16

D_MODEL = 1024
DEPTH = 2
SB_HEADS = 8
HEAD_DIM = 64
SB_WIDTH = SB_HEADS * HEAD_DIM
SSM_WIDTH = D_MODEL - SB_WIDTH
SSM_GROUP = 16
SSM_GROUPS = SSM_WIDTH // SSM_GROUP
SSM_STATE = 64
SSM_CHUNK = 16
SSM_LANE_TILES = SSM_WIDTH // 128
SSM_TILE_GROUPS = 128 // SSM_GROUP
NSA_HEADS = 16
NSA_KV_HEADS = 4
NSA_REP = NSA_HEADS // NSA_KV_HEADS
CMP_LEN = 32
CMP_STRIDE = 16
CMP_HIDDEN = 256
SEL_LEN = 64
SEL_TOP = 8
WINDOW = 512
ROPE_THETA = 10000.0
FORCE_BONUS = 1e4
NEG = -1e30
N_EXPERTS = 16
N_GROUPS = 4
EXPERTS_PER_GROUP = N_EXPERTS // N_GROUPS
D_EXPERT = 512
ALPHA = (2 * DEPTH) ** 0.25
LN_EPS = 1e-5
ATTN_SCALE = HEAD_DIM ** -0.5
LOG2_Q_SCALE = ATTN_SCALE * math.log2(math.e)
ATT_BLOCK = 256
GELU_C = math.sqrt(2.0 / math.pi)


def _params(*sem):
    return pltpu.CompilerParams(dimension_semantics=sem, vmem_limit_bytes=56 * 1024 * 1024)


def _sigmoid(x):
    return 1.0 / (1.0 + jnp.exp(-x))


def _gelu(x):
    return 0.5 * x * (1.0 + jnp.tanh(GELU_C * (x + 0.044715 * (x * x * x))))


def _layer_norm(r, g, b):
    mu = jnp.mean(r, axis=-1, keepdims=True)
    d = r - mu
    var = jnp.mean(d * d, axis=-1, keepdims=True)
    return d * lax.rsqrt(var + LN_EPS) * g + b


def _dot(a, b):
    return jnp.dot(a, b, preferred_element_type=F32)


def _dot_nt(a, b):
    return lax.dot_general(a, b, (((1,), (1,)), ((), ())), preferred_element_type=F32)


def _split_bf16(x):
    hi = x.astype(BF16)
    lo = (x - hi.astype(F32)).astype(BF16)
    return hi, lo


def _inproj0_body(x_ref, w_ref, q_ref, k_ref, v_ref, u_ref, slab_scr):
    nb, tl, D = x_ref.shape
    p = _dot(x_ref[...].reshape(nb * tl, D).astype(BF16), w_ref[...])
    for h in range(SB_HEADS):
        c = h * HEAD_DIM
        q_ref[:, h] = (p[:, c:c + HEAD_DIM] * LOG2_Q_SCALE).astype(BF16).reshape(nb, tl, HEAD_DIM)
        k_ref[:, h] = p[:, SB_WIDTH + c:SB_WIDTH + c + HEAD_DIM].astype(BF16).reshape(nb, tl, HEAD_DIM)
        v_ref[:, h] = p[:, 2 * SB_WIDTH + c:2 * SB_WIDTH + c + HEAD_DIM].astype(BF16).reshape(nb, tl, HEAD_DIM)
    T = SSM_CHUNK
    for v in range(SSM_LANE_TILES):
        c0 = 3 * SB_WIDTH + v * 128
        slab_scr[...] = p[:, c0:c0 + 128]
        for c in range(tl // T):
            for s in range(T):
                u_ref[v, c * nb:(c + 1) * nb, s * 128:(s + 1) * 128] = slab_scr[pl.ds(c * T + s, nb, stride=tl), :]


def _inproj0(x, w_bf16, tl=64):
    B, L, D = x.shape
    tl = min(tl, L)
    nout = w_bf16.shape[1]
    T = SSM_CHUNK
    head = jax.ShapeDtypeStruct((B, SB_HEADS, L, HEAD_DIM), BF16)
    head_spec = pl.BlockSpec((B, SB_HEADS, tl, HEAD_DIM), lambda i: (0, 0, i, 0))
    return pl.pallas_call(
        _inproj0_body,
        grid=(L // tl,),
        in_specs=[pl.BlockSpec((B, tl, D), lambda i: (0, i, 0)),
                  pl.BlockSpec((D, nout), lambda i: (0, 0))],
        out_specs=[head_spec, head_spec, head_spec,
                   pl.BlockSpec((SSM_LANE_TILES, tl // T * B, T * 128), lambda i: (0, i, 0))],
        out_shape=[head, head, head,
                   jax.ShapeDtypeStruct((SSM_LANE_TILES, L // T * B, T * 128), F32)],
        scratch_shapes=[pltpu.VMEM((B * tl, 128), F32)],
        compiler_params=_params("parallel"),
        name="inproj0",
    )(x, w_bf16)


SB_BLOCK = 256
SB_HEADS_PER_STEP = 8
EXP2_UNDERFLOW = -151.0


def _sb_attn_body(q_ref, k_ref, v_ref, o_ref, acc_scr, cs_scr):
    i = pl.program_id(2)
    t = q_ref.shape[2]
    nh = q_ref.shape[1]
    row = lax.broadcasted_iota(jnp.int32, (t, t), 0)
    col = lax.broadcasted_iota(jnp.int32, (t, t), 1)
    suffix = jnp.where(row > col, 1.0, 0.0).astype(BF16)
    below = col < row

    def logits(hh, k0):
        return _dot_nt(q_ref[0, hh], k_ref[0, hh, pl.ds(k0, t), :])

    def alive(css):
        return (jnp.max(functools.reduce(jnp.maximum, css)) > EXP2_UNDERFLOW).astype(jnp.int32)

    def all_heads(k0, diagonal):
        zs = [logits(hh, k0) for hh in range(nh)]
        sps, l1s = [], []
        for z in zs:
            sp = jnp.maximum(z, 0.0) + jnp.log2(1.0 + jnp.exp2(-jnp.abs(z)))
            sps.append(sp)
            l1s.append(jnp.where(below, sp, 0.0) if diagonal else sp)
        within_all = _dot(jnp.concatenate([l1.astype(BF16) for l1 in l1s], axis=0), suffix)
        css = []
        for hh in range(nh):
            vb = v_ref[0, hh, pl.ds(k0, t), :]
            base = zs[hh] - sps[hh] - within_all[hh * t:(hh + 1) * t]
            if diagonal:
                w = jnp.where(below, jnp.exp2(base), 0.0)
                acc_scr[hh] = _dot(w.astype(BF16), vb)
                cs = -jnp.sum(l1s[hh], axis=1, keepdims=True)
            else:
                w = jnp.exp2(base + cs_scr[hh])
                acc_scr[hh] += _dot(w.astype(BF16), vb)
                cs = cs_scr[hh] - jnp.sum(l1s[hh], axis=1, keepdims=True)
            cs_scr[hh] = cs
            css.append(cs)
        return alive(css)

    first = all_heads(pl.multiple_of(i * t, t), True)

    def cond(c):
        return jnp.logical_and(c[0] <= i, c[1] > 0)

    def body(c):
        return c[0] + 1, all_heads(pl.multiple_of((i - c[0]) * t, t), False)

    lax.while_loop(cond, body, (jnp.int32(1), first))
    o_ref[0] = jnp.concatenate([acc_scr[hh] for hh in range(nh)], axis=-1).astype(o_ref.dtype)


def _sb_attention(q, k, v):
    B, H, L, d = q.shape
    t = min(SB_BLOCK, L)
    nh = SB_HEADS_PER_STEP
    return pl.pallas_call(
        _sb_attn_body,
        grid=(B, H // nh, L // t),
        in_specs=[pl.BlockSpec((1, nh, t, d), lambda b, h, i: (b, h, i, 0)),
                  pl.BlockSpec((1, nh, L, d), lambda b, h, i: (b, h, 0, 0)),
                  pl.BlockSpec((1, nh, L, d), lambda b, h, i: (b, h, 0, 0))],
        out_specs=pl.BlockSpec((1, t, nh * d), lambda b, h, i: (b, i, h)),
        out_shape=jax.ShapeDtypeStruct((B, L, H * d), BF16),
        scratch_shapes=[pltpu.VMEM((nh, t, d), F32), pltpu.VMEM((nh, t, 1), F32)],
        compiler_params=_params("parallel", "parallel", "arbitrary"),
        name="sb_attention",
    )(q, k, v)


def _ssm_tables(lam_re, lam_im, log_dt, b_re, b_im, c_re, c_im, d_skip):
    T, G, P, H = SSM_CHUNK, SSM_GROUPS, SSM_STATE, SSM_GROUP
    hp = lax.Precision.HIGHEST
    dt = jnp.exp(log_dt.astype(F32))[:, None]
    lr = lam_re.astype(F32)
    li = lam_im.astype(F32)
    mag = jnp.exp(lr * dt)
    a_re = mag * jnp.cos(li * dt)
    a_im = mag * jnp.sin(li * dt)
    den = lr * lr + li * li
    nr = a_re - 1.0
    f_re = (nr * lr + a_im * li) / den
    f_im = (a_im * lr - nr * li) / den
    br = b_re.astype(F32)
    bi = b_im.astype(F32)
    bbt_re = (f_re[..., None] * br - f_im[..., None] * bi).transpose(0, 2, 1)[:, None]
    bbt_im = (f_re[..., None] * bi + f_im[..., None] * br).transpose(0, 2, 1)[:, None]
    cr = c_re.astype(F32)[:, None]
    ci = c_im.astype(F32)[:, None]

    def powers(k):
        k = jnp.asarray(k, F32)[None, :, None]
        pmag = jnp.exp(k * (lr * dt)[:, None, :])
        ang = k * (li * dt)[:, None, :]
        return (pmag * jnp.cos(ang))[:, :, None, :], (pmag * jnp.sin(ang))[:, :, None, :]

    def cmul(xr, xi, yr, yi, rows):
        return ((xr * yr - xi * yi).reshape(G, rows, P), (xr * yi + xi * yr).reshape(G, rows, P))

    steps = np.arange(T)
    bs_re, bs_im = cmul(*powers(-steps), bbt_re, bbt_im, T * H)
    ct_re, ct_im = cmul(*powers(steps), cr, ci, T * H)
    full = (jnp.einsum('gap,gbp->gab', bs_re, ct_re, precision=hp)
            - jnp.einsum('gap,gbp->gab', bs_im, ct_im, precision=hp))
    row_s = lax.broadcasted_iota(jnp.int32, (T * H, T * H), 0) // H
    col_t = lax.broadcasted_iota(jnp.int32, (T * H, T * H), 1) // H
    intra = jnp.where(row_s <= col_t, full, 0.0)
    st_re, st_im = cmul(*powers(T - 1 - steps), bbt_re, bbt_im, T * H)
    to_st = jnp.concatenate([st_re, st_im, st_im, st_re], axis=-1)
    c1_re, c1_im = cmul(*powers(steps + 1), cr, ci, T * H)
    from_st_t = jnp.concatenate([c1_re, -c1_im], axis=-1)
    at_re, at_im = powers([T])
    at_re, at_im = at_re[:, 0, 0], at_im[:, 0, 0]
    zeros = jnp.zeros_like(at_re)
    adv = jnp.stack([jnp.concatenate([at_re, at_re], -1),
                     jnp.concatenate([-at_im, at_im], -1),
                     jnp.concatenate([at_im, -at_im], -1)]
                    + [jnp.concatenate([zeros, zeros], -1)] * 5, axis=1)
    dvec = jnp.tile(d_skip.astype(F32).reshape(G, 1, H), (1, 1, T))
    return intra, to_st, from_st_t, adv, dvec


def _ssm_expand(tables):
    intra, to_st, from_st_t, adv, dvec = tables
    V, W, T, H, P = SSM_LANE_TILES, SSM_TILE_GROUPS, SSM_CHUNK, SSM_GROUP, SSM_STATE
    n = T * W * H
    k_in = lax.broadcasted_iota(jnp.int32, (T * H, n), 0)
    col = lax.broadcasted_iota(jnp.int32, (T * H, n), 1)
    spread_th = (k_in == (col // 128) * H + col % H).astype(F32)
    spread_q = (k_in[:2 * P, :W * 2 * P] == col[:2 * P, :W * 2 * P] % (2 * P)).astype(F32)
    row_g = (lax.broadcasted_iota(jnp.int32, (n, 1), 0) // H) % W
    st_row_g = lax.broadcasted_iota(jnp.int32, (W * 2 * P, 1), 0) // (2 * P)
    col_g_th = (lax.broadcasted_iota(jnp.int32, (1, n), 1) // H) % W
    col_g_q = lax.broadcasted_iota(jnp.int32, (1, W * 2 * P), 1) // (2 * P)

    def blockdiag(rows, spread, row_group, col_group):
        full = jnp.einsum('vrk,kc->vrc', rows, spread)
        return jnp.where(row_group == col_group, full, 0.0).astype(BF16)

    by_row = lambda t: t.reshape(V, W, T, H, T * H).transpose(0, 2, 1, 3, 4).reshape(V, n, T * H)
    m8 = blockdiag(by_row(intra), spread_th, row_g, col_g_th)
    ws8 = blockdiag(by_row(to_st)[:, :, :2 * P], spread_q, row_g, col_g_q)
    wi_full = jnp.einsum('vwkq,kc->vwqc', from_st_t.reshape(V, W, T * H, 2 * P), spread_th).reshape(V, W * 2 * P, n)
    wi8 = jnp.where(st_row_g == col_g_th, wi_full, 0.0).astype(BF16)
    adv8 = adv.reshape(V, W, 8, 2 * P).transpose(0, 2, 1, 3).reshape(V, 8, W * 2 * P)
    d8 = jnp.broadcast_to(dvec.reshape(V, 1, W, T, H)[:, :, :, 0:1], (V, 1, W, T, H))
    d8 = d8.transpose(0, 1, 3, 2, 4).reshape(V, 1, T * W * H)
    return m8, ws8, wi8, adv8, d8


def _ssm_body(u_ref, m8_ref, ws8_ref, wi8_ref, adv_ref, dvec_ref, y_ref, s_scr, xin_scr, x1_scr, x2_scr, *, bsz):
    half = x1_scr.shape[1]

    @pl.when(pl.program_id(1) == 0)
    def _():
        x1_scr[...] = jnp.zeros_like(x1_scr)
        x2_scr[...] = jnp.zeros_like(x2_scr)

    u = u_ref[0]
    ub = u.astype(BF16)
    s1 = _dot(ub, ws8_ref[0])
    lane = lax.broadcasted_iota(jnp.int32, s1.shape, 1)
    s_scr[:, :half] = s1
    s_scr[:, half:] = jnp.where(lane % (2 * SSM_STATE) < SSM_STATE,
                                pltpu.roll(s1, half - SSM_STATE, 1), pltpu.roll(s1, SSM_STATE, 1))
    a1 = adv_ref[0, 0:1, :]
    a2 = adv_ref[0, 1:2, :]
    a3 = adv_ref[0, 2:3, :]

    def step(c, carry):
        x1, x2 = carry
        r0 = pl.multiple_of(c * bsz, bsz)
        xin_scr[pl.ds(r0, bsz), :] = x1
        s = s_scr[pl.ds(r0, bsz), :]
        return (a1 * x1 + a2 * x2 + s[:, :half], a1 * x2 + a3 * x1 + s[:, half:])

    x1, x2 = lax.fori_loop(0, u.shape[0] // bsz, step, (x1_scr[...], x2_scr[...]))
    x1_scr[...] = x1
    x2_scr[...] = x2
    y_ref[0] = _dot(ub, m8_ref[0]) + _dot(xin_scr[...].astype(BF16), wi8_ref[0]) + dvec_ref[0] * u


def _ssm(us, tables, bsz, rows=256):
    m8, ws8, wi8, adv8, d8 = _ssm_expand(tables)
    V, CB, width = us.shape
    B = bsz
    half = SSM_TILE_GROUPS * 2 * SSM_STATE
    rows = min(rows, CB)
    once = dict(pipeline_mode=pl.Buffered(1))
    y = pl.pallas_call(
        functools.partial(_ssm_body, bsz=B),
        grid=(V, CB // rows),
        in_specs=[pl.BlockSpec((1, rows, width), lambda v, r: (v, r, 0)),
                  pl.BlockSpec((1, width, width), lambda v, r: (v, 0, 0), **once),
                  pl.BlockSpec((1, width, half), lambda v, r: (v, 0, 0), **once),
                  pl.BlockSpec((1, half, width), lambda v, r: (v, 0, 0), **once),
                  pl.BlockSpec((1, 8, half), lambda v, r: (v, 0, 0)),
                  pl.BlockSpec((1, 1, width), lambda v, r: (v, 0, 0))],
        out_specs=pl.BlockSpec((1, rows, width), lambda v, r: (v, r, 0)),
        out_shape=jax.ShapeDtypeStruct((V, CB, width), F32),
        scratch_shapes=[pltpu.VMEM((rows, 2 * half), F32), pltpu.VMEM((rows, half), F32),
                        pltpu.VMEM((B, half), F32), pltpu.VMEM((B, half), F32)],
        compiler_params=_params("parallel", "arbitrary"),
        name="s5_scan",
    )(us, m8, ws8, wi8, adv8, d8)
    return y


def _outproj0_body(x_ref, oa_ref, y_ref, wglu_ref, woa_ref, wob_ref, g_ref, b_ref, o_ref, slab_scr):
    nb, tl, D = x_ref.shape
    T = SSM_CHUNK
    ys = []
    for v in range(SSM_LANE_TILES):
        for c in range(tl // T):
            for s in range(T):
                slab_scr[pl.ds(c * T + s, nb, stride=tl), :] = y_ref[v, c * nb:(c + 1) * nb, s * 128:(s + 1) * 128]
        ys.append(slab_scr[...])
    h = _gelu(jnp.concatenate(ys, axis=-1))
    ob = h * _sigmoid(_dot(h.astype(BF16), wglu_ref[...]))
    m = _dot(oa_ref[...].reshape(nb * tl, SB_WIDTH), woa_ref[...]) + _dot(ob.astype(BF16), wob_ref[...])
    r = ALPHA * x_ref[...].reshape(nb * tl, D) + m
    o_ref[...] = _layer_norm(r, g_ref[...], b_ref[...]).reshape(nb, tl, D)


def _outproj0(x, oa, ys, wglu, woa, wob, g, b, tl=64):
    B, L, D = x.shape
    tl = min(tl, L)
    T = SSM_CHUNK
    row = lambda i: (0, i, 0)
    fix = lambda i: (0, 0)
    return pl.pallas_call(
        _outproj0_body,
        grid=(L // tl,),
        in_specs=[pl.BlockSpec((B, tl, D), row), pl.BlockSpec((B, tl, SB_WIDTH), row),
                  pl.BlockSpec((SSM_LANE_TILES, tl // T * B, T * 128), row),
                  pl.BlockSpec((SSM_WIDTH, SSM_WIDTH), fix), pl.BlockSpec((SB_WIDTH, D), fix),
                  pl.BlockSpec((SSM_WIDTH, D), fix), pl.BlockSpec((1, D), fix), pl.BlockSpec((1, D), fix)],
        out_specs=pl.BlockSpec((B, tl, D), row),
        out_shape=jax.ShapeDtypeStruct((B, L, D), F32),
        scratch_shapes=[pltpu.VMEM((B * tl, 128), F32)],
        compiler_params=_params("parallel"),
        name="outproj0",
    )(x, oa, ys, wglu, woa, wob, g, b)


def _outproj1_body(x_ref, o_ref_in, w_ref, g_ref, b_ref, o_ref):
    m = _dot(o_ref_in[...], w_ref[...])
    o_ref[...] = _layer_norm(ALPHA * x_ref[...] + m, g_ref[...], b_ref[...])


def _outproj1(x, o, w, g, b, tm=1024):
    N, D = x.shape
    tm = min(tm, N)
    row = lambda i: (i, 0)
    fix = lambda i: (0, 0)
    return pl.pallas_call(
        _outproj1_body,
        grid=(N // tm,),
        in_specs=[pl.BlockSpec((tm, D), row), pl.BlockSpec((tm, o.shape[1]), row),
                  pl.BlockSpec(w.shape, fix), pl.BlockSpec((1, D), fix), pl.BlockSpec((1, D), fix)],
        out_specs=pl.BlockSpec((tm, D), row),
        out_shape=jax.ShapeDtypeStruct((N, D), F32),
        compiler_params=_params("parallel"),
        name="outproj1",
    )(x, o, w, g, b)


def _router_body(x_ref, whl_ref, b_ref, g_ref, grp_ref):
    NE = N_EXPERTS
    xh, xl = _split_bf16(x_ref[...])
    whl = whl_ref[...]
    part = _dot(xh, whl)
    logits = (part[:, :NE] + part[:, NE:] + _dot(xl, whl[:, :NE])).T
    aff = _sigmoid(logits)
    sel = aff + b_ref[...]
    E, K = EXPERTS_PER_GROUP, N_GROUPS
    s = [sel[e:e + 1, :] for e in range(N_EXPERTS)]
    a = [aff[e:e + 1, :] for e in range(N_EXPERTS)]
    gscore = []
    for k in range(K):
        v = s[k * E:(k + 1) * E]
        best = None
        for x in range(E):
            for y in range(x + 1, E):
                pair = v[x] + v[y]
                best = pair if best is None else jnp.maximum(best, pair)
        gscore.append(best)
    top = functools.reduce(jnp.maximum, gscore)
    is_g = []
    taken = None
    for k in range(K):
        hit = gscore[k] == top
        if taken is None:
            is_g.append(hit)
            taken = hit
        else:
            is_g.append(jnp.logical_and(hit, jnp.logical_not(taken)))
            taken = jnp.logical_or(taken, hit)

    def pick(rows, j):
        out = rows[(K - 1) * E + j]
        for k in range(K - 2, -1, -1):
            out = jnp.where(is_g[k], rows[k * E + j], out)
        return out

    v = [pick(s, j) for j in range(E)]
    av = [pick(a, j) for j in range(E)]
    chosen = []
    for j in range(E):
        r = jnp.zeros_like(v[j])
        for j2 in range(E):
            if j2 == j:
                continue
            ahead = (v[j2] >= v[j]) if j2 < j else (v[j2] > v[j])
            r = r + jnp.where(ahead, 1.0, 0.0)
        chosen.append(r < 2.0)
    wj = [jnp.where(chosen[j], av[j], 0.0) for j in range(E)]
    tot = wj[0] + wj[1] + wj[2] + wj[3]
    gj = [w / tot for w in wj]
    rows = [jnp.where(is_g[e // E], gj[e % E], 0.0) for e in range(N_EXPERTS)]
    g_ref[...] = jnp.concatenate(rows, axis=0)
    grp_ref[...] = jnp.concatenate([jnp.where(is_g[k], 1.0, 0.0) for k in range(K)], axis=0)


def _router(x, w_router, b_router, tm=1024):
    N, D = x.shape
    tm = min(tm, N)
    whl = jnp.concatenate(_split_bf16(w_router.astype(F32)), axis=1)
    fix = lambda i: (0, 0)
    return pl.pallas_call(
        _router_body,
        grid=(N // tm,),
        in_specs=[pl.BlockSpec((tm, D), lambda i: (i, 0)), pl.BlockSpec((D, 2 * N_EXPERTS), fix),
                  pl.BlockSpec((N_EXPERTS, 1), fix)],
        out_specs=[pl.BlockSpec((N_EXPERTS, tm), lambda i: (0, i)), pl.BlockSpec((N_GROUPS, tm), lambda i: (0, i))],
        out_shape=[jax.ShapeDtypeStruct((N_EXPERTS, N), F32), jax.ShapeDtypeStruct((N_GROUPS, N), F32)],
        compiler_params=_params("parallel"),
        name="router",
    )(x, whl, b_router.astype(F32).reshape(N_EXPERTS, 1))


MOE_TILE = 1024
MOE_PASS_ROWS = 304
MOE_CHUNK = 256
MOE_EXPERTS_PER_STEP = 2
ROW_ALIGN = 16


def _moe_body(seg_ref, x_ref, gt_ref, grp_ref, w1_ref, w3_ref, w2_ref, g_ref, b_ref, o_ref,
              xs_scr, ys_scr, pt_scr, gs_scr):
    i = pl.program_id(0)
    e = pl.program_id(1)
    T = x_ref.shape[0]
    K, NE = N_GROUPS, N_EXPERTS
    cap = MOE_PASS_ROWS
    ch = min(MOE_CHUNK, T)

    @pl.when(e == 0)
    def _():
        grp = grp_ref[...]
        xb = x_ref[...].astype(BF16)
        gates = gt_ref[...]
        gh, gm = _split_bf16(gates)
        gl = (gates - gh.astype(F32) - gm.astype(F32)).astype(BF16)
        g3 = jnp.concatenate([gh, gm, gl], axis=0)
        cnt = jnp.sum(grp, axis=1, keepdims=True)
        off = [jnp.zeros((1, 1), F32)]
        for k in range(K - 1):
            off.append(off[-1] + cnt[k:k + 1])
        r_i = lax.broadcasted_iota(jnp.int32, (ch, T), 0)
        c_i = lax.broadcasted_iota(jnp.int32, (ch, T), 1)
        lane = lax.broadcasted_iota(jnp.int32, (2 * K, T), 1)
        run = jnp.concatenate([grp, jnp.zeros_like(grp)], axis=0)
        sh = 1
        while sh < T:
            run = run + jnp.where(lane >= sh, pltpu.roll(run, sh, 1), 0.0)
            sh *= 2
        pos_row = jnp.zeros((1, T), F32)
        for k in range(K):
            pos_row = pos_row + grp[k:k + 1, :] * (off[k] + run[k:k + 1, :] - 1.0)
        pos_t = jnp.broadcast_to(pos_row, (2 * K, T))
        pos_col = [pos_t[:, c * ch:(c + 1) * ch].T[:, 0:1] for c in range(T // ch)]
        for c in range(T // ch):
            c0 = c * ch
            perm = jnp.where((r_i + c0).astype(F32) == pos_row, 1.0, 0.0).astype(BF16)
            xs_scr[c0:c0 + ch, :] = _dot(perm, xb).astype(BF16)
            g3s = _dot_nt(perm, g3)
            gs_scr[c0:c0 + ch, :] = g3s[:, :NE] + g3s[:, NE:2 * NE] + g3s[:, 2 * NE:]
            pt_scr[c0:c0 + ch, :] = jnp.where(c_i.astype(F32) == pos_col[c], 1.0, 0.0).astype(BF16)
        xs_scr[T:, :] = jnp.zeros((cap, xs_scr.shape[1]), BF16)
        gs_scr[T:, :] = jnp.zeros((cap, gs_scr.shape[1]), F32)
        ys_scr[...] = jnp.zeros_like(ys_scr)

    ne = w1_ref.shape[0]
    k = (e * ne) // EXPERTS_PER_GROUP
    off = seg_ref[i, k]
    cnt = seg_ref[i, K + k]
    start0 = (off // ROW_ALIGN) * ROW_ALIGN
    n_pass = (off + cnt - start0 + cap - 1) // cap

    def one_pass(n, c):
        start = pl.multiple_of(start0 + n * cap, ROW_ALIGN)
        xc = xs_scr[pl.ds(start, cap), :]
        gsc = gs_scr[pl.ds(start, cap), :]
        lane = lax.broadcasted_iota(jnp.int32, gsc.shape, 1)
        y = None
        for x in range(ne):
            a = _dot(xc, w1_ref[x])
            hm = (a * _sigmoid(a)) * _dot(xc, w3_ref[x])
            gcol = jnp.sum(jnp.where(lane == e * ne + x, gsc, 0.0), axis=1, keepdims=True)
            yx = gcol * _dot(hm.astype(BF16), w2_ref[x])
            y = yx if y is None else y + yx
        ys_scr[pl.ds(start, cap), :] += y
        return c

    lax.fori_loop(0, n_pass, one_pass, 0)

    @pl.when(e == pl.num_programs(1) - 1)
    def _():
        y = _dot(pt_scr[...], ys_scr[0:T, :].astype(BF16))
        o_ref[...] = _layer_norm(ALPHA * x_ref[...] + y, g_ref[...], b_ref[...])


def _moe(x, gates_t, grp_t, w1, w3, w2, g, b):
    N, D = x.shape
    T = min(MOE_TILE, N)
    K = N_GROUPS
    cnt = grp_t.reshape(K, N // T, T).sum(-1).astype(jnp.int32).T
    seg = jnp.concatenate([jnp.cumsum(cnt, axis=1) - cnt, cnt], axis=1)
    fix = lambda i, e, s: (0, 0)
    ne = MOE_EXPERTS_PER_STEP
    rows = T + MOE_PASS_ROWS
    grid_spec = pltpu.PrefetchScalarGridSpec(
        num_scalar_prefetch=1,
        grid=(N // T, N_EXPERTS // ne),
        in_specs=[pl.BlockSpec((T, D), lambda i, e, s: (i, 0)),
                  pl.BlockSpec((N_EXPERTS, T), lambda i, e, s: (0, i)),
                  pl.BlockSpec((K, T), lambda i, e, s: (0, i)),
                  pl.BlockSpec((ne, D, D_EXPERT), lambda i, e, s: (e, 0, 0)),
                  pl.BlockSpec((ne, D, D_EXPERT), lambda i, e, s: (e, 0, 0)),
                  pl.BlockSpec((ne, D_EXPERT, D), lambda i, e, s: (e, 0, 0)),
                  pl.BlockSpec((1, D), fix), pl.BlockSpec((1, D), fix)],
        out_specs=pl.BlockSpec((T, D), lambda i, e, s: (i, 0)),
        scratch_shapes=[pltpu.VMEM((rows, D), BF16), pltpu.VMEM((rows, D), F32),
                        pltpu.VMEM((T, T), BF16), pltpu.VMEM((rows, N_EXPERTS), F32)])
    return pl.pallas_call(
        _moe_body,
        grid_spec=grid_spec,
        out_shape=jax.ShapeDtypeStruct((N, D), F32),
        compiler_params=_params("parallel", "arbitrary"),
        name="moe",
    )(seg, x, gates_t, grp_t, w1, w3, w2, g, b)


NSA_KVW = NSA_KV_HEADS * HEAD_DIM
NSA_ROPE_W = NSA_HEADS * HEAD_DIM + 2 * NSA_KVW


def _rot_cols(w):
    K, n = w.shape
    w3 = w.reshape(K, n // HEAD_DIM, 2, HEAD_DIM // 2)
    return jnp.stack([-w3[:, :, 1], w3[:, :, 0]], axis=2).reshape(K, n)


def _inproj1_weights(w_in):
    H, G, d = NSA_HEADS, NSA_KV_HEADS, HEAD_DIM
    cuts = [H * d + i * NSA_KVW for i in range(7)]
    q, kc, vc, ks, vs, kw, vw, gate = jnp.split(w_in.astype(F32), cuts, axis=1)
    w_all = jnp.concatenate([q, ks, kw, kc, vc], axis=1).astype(BF16)
    w_t = jnp.concatenate([vs, vw, gate], axis=1).T.astype(BF16)
    return w_all, w_t


def _rope_tables(pos):
    inv = ROPE_THETA ** (-jnp.arange(0, HEAD_DIM, 2, dtype=F32) / HEAD_DIM)
    ang = pos.astype(F32)[:, None] * inv[None, :]
    c, s = jnp.cos(ang), jnp.sin(ang)
    return jnp.concatenate([c, c], axis=-1), jnp.concatenate([s, s], axis=-1)


def _rope_roll_tables(pos):
    inv = ROPE_THETA ** (-jnp.arange(0, HEAD_DIM, 2, dtype=F32) / HEAD_DIM)
    ang = pos.astype(F32)[:, None] * inv[None, :]
    c, s, z = jnp.cos(ang), jnp.sin(ang), jnp.zeros_like(ang)
    two = lambda a, b: jnp.concatenate([a, b, a, b], axis=-1)
    return two(c, c), two(-s, z), two(z, s)


V_ROWS = HEAD_DIM + 16


def _inproj1_body(x_ref, w_ref, wt_ref, cos_ref, sinlo_ref, sinhi_ref, q_ref, ks_ref, kw_ref, kc_ref, vc_ref,
                  vst_ref, vwt_ref, gate_ref):
    xb = x_ref[0].astype(BF16)
    tl = xb.shape[0]
    d = HEAD_DIM
    p = _dot(xb, w_ref[...])
    pt = _dot_nt(wt_ref[...], xb)
    ones = jnp.ones((V_ROWS - d, tl), BF16)
    for n, ref in enumerate((vst_ref, vwt_ref)):
        for g in range(NSA_KV_HEADS):
            r0 = n * NSA_KVW + g * d
            ref[0, g, 0:d, :] = pt[r0:r0 + d, :].astype(BF16)
            ref[0, g, d:V_ROWS, :] = ones
    gate_ref[0] = pt[2 * NSA_KVW:, :]
    cos = cos_ref[...]
    sin_lo = sinlo_ref[...]
    sin_hi = sinhi_ref[...]
    roped = []
    for c in range(NSA_ROPE_W // 128):
        xc = p[:, c * 128:(c + 1) * 128]
        roped.append(xc * cos + pltpu.roll(xc, 96, 1) * sin_lo + pltpu.roll(xc, 32, 1) * sin_hi)

    def head(chunks, h):
        blk = chunks[h // 2]
        return blk[:, (h % 2) * d:(h % 2 + 1) * d]

    for h in range(NSA_HEADS):
        q_ref[0, h] = (head(roped, h) * LOG2_Q_SCALE).astype(BF16)
    nq = NSA_HEADS // 2
    pos = pl.program_id(1) * tl + lax.broadcasted_iota(jnp.int32, (tl, 128 - d), 0)
    block_onehot = jnp.where(pos // SEL_LEN == lax.broadcasted_iota(jnp.int32, (tl, 128 - d), 1), 1.0, 0.0)
    for g in range(NSA_KV_HEADS):
        ks_ref[0, g] = jnp.concatenate([head(roped[nq:], g), block_onehot], axis=1).astype(BF16)
        kw_ref[0, g] = head(roped[nq + 2:], g).astype(BF16)
    for n, ref in enumerate((kc_ref, vc_ref)):
        for g in range(NSA_KV_HEADS):
            c0 = NSA_ROPE_W + n * NSA_KVW + g * d
            ref[0, g] = p[:, c0:c0 + d]


def _inproj1(x, w_all, w_t, tl=512):
    B, L, D = x.shape
    tl = min(tl, L)
    G = NSA_KV_HEADS
    n_gate = w_t.shape[0] - 2 * NSA_KVW
    tables = _rope_roll_tables(jnp.arange(L))
    qh = jax.ShapeDtypeStruct((B, NSA_HEADS, L, HEAD_DIM), BF16)
    kvh = jax.ShapeDtypeStruct((B, G, L, HEAD_DIM), BF16)
    vth = jax.ShapeDtypeStruct((B, G, V_ROWS, L), BF16)
    q_spec = pl.BlockSpec((1, NSA_HEADS, tl, HEAD_DIM), lambda b, i: (b, 0, i, 0))
    kv_spec = pl.BlockSpec((1, G, tl, HEAD_DIM), lambda b, i: (b, 0, i, 0))
    ksel_spec = pl.BlockSpec((1, G, tl, 128), lambda b, i: (b, 0, i, 0))
    vt_spec = pl.BlockSpec((1, G, V_ROWS, tl), lambda b, i: (b, 0, 0, i))
    tab_spec = pl.BlockSpec((tl, 128), lambda b, i: (i, 0))
    return pl.pallas_call(
        _inproj1_body,
        grid=(B, L // tl),
        in_specs=[pl.BlockSpec((1, tl, D), lambda b, i: (b, i, 0)),
                  pl.BlockSpec(w_all.shape, lambda b, i: (0, 0)),
                  pl.BlockSpec(w_t.shape, lambda b, i: (0, 0)),
                  tab_spec, tab_spec, tab_spec],
        out_specs=([q_spec, ksel_spec] + [kv_spec] * 3 + [vt_spec] * 2
                   + [pl.BlockSpec((1, n_gate, tl), lambda b, i: (b, 0, i))]),
        out_shape=([qh, jax.ShapeDtypeStruct((B, G, L, 128), BF16), kvh]
                   + [jax.ShapeDtypeStruct((B, G, L, HEAD_DIM), F32)] * 2 + [vth] * 2
                   + [jax.ShapeDtypeStruct((B, n_gate, L), F32)]),
        compiler_params=_params("parallel", "parallel"),
        name="inproj1",
    )(x, w_all, w_t, *tables)


def _compress_body(kc_ref, vc_ref, posk_ref, posv_ref, w1k_ref, w1v_ref, w2k_ref, w2kr_ref, w2v_ref,
                   cos_ref, sin_ref, kcmp_ref, vcmp_ref):
    G = kc_ref.shape[1]
    nrow = kc_ref.shape[2] // CMP_STRIDE

    def hidden(a_ref, pos_ref, w1_ref):
        a = jnp.concatenate(
            [jnp.concatenate([a_ref[0, g, pl.ds(l, nrow, stride=CMP_STRIDE), :] for l in range(CMP_STRIDE)], axis=1)
             for g in range(G)], axis=0).astype(F32)
        lo = _dot((a + pos_ref[0:1, :]).astype(BF16), w1_ref[0])
        hi = _dot((a + pos_ref[1:2, :]).astype(BF16), w1_ref[1])
        hi_next = pltpu.roll(hi, G * nrow - 1, 0)
        return _gelu(lo + hi_next).astype(BF16)

    hk = hidden(kc_ref, posk_ref, w1k_ref)
    cos = jnp.concatenate([cos_ref[...]] * G, axis=0)
    sin = jnp.concatenate([sin_ref[...]] * G, axis=0)
    kcmp = (_dot(hk, w2k_ref[...]) * cos + _dot(hk, w2kr_ref[...]) * sin).astype(BF16)
    hv = hidden(vc_ref, posv_ref, w1v_ref)
    vcmp = _dot_nt(w2v_ref[...], hv).astype(BF16)
    for g in range(G):
        kcmp_ref[0, g] = kcmp[g * nrow:(g + 1) * nrow]
        vcmp_ref[0, g] = vcmp[:, g * nrow:(g + 1) * nrow]


def _compress(kc, vc, pos_k, w1_k, w2_k, pos_v, w1_v, w2_v):
    B, G, L, d = kc.shape
    half = CMP_STRIDE * d
    nb = L // CMP_STRIDE
    posk = pos_k.astype(F32).reshape(2, half)
    posv = pos_v.astype(F32).reshape(2, half)
    w1k = w1_k.astype(BF16).reshape(2, half, CMP_HIDDEN)
    w1v = w1_v.astype(BF16).reshape(2, half, CMP_HIDDEN)
    w2k = w2_k.astype(F32)
    cos, sin = _rope_tables(jnp.arange(nb) * CMP_STRIDE + CMP_LEN - 1)
    blk = pl.BlockSpec((1, G, L, d), lambda b: (b, 0, 0, 0))
    out = pl.BlockSpec((1, G, nb, d), lambda b: (b, 0, 0, 0))
    out_t = pl.BlockSpec((1, G, d, nb), lambda b: (b, 0, 0, 0))
    fix2 = lambda b: (0, 0)
    fix3 = lambda b: (0, 0, 0)
    return pl.pallas_call(
        _compress_body,
        grid=(B,),
        in_specs=[blk, blk, pl.BlockSpec((2, half), fix2), pl.BlockSpec((2, half), fix2),
                  pl.BlockSpec((2, half, CMP_HIDDEN), fix3), pl.BlockSpec((2, half, CMP_HIDDEN), fix3),
                  pl.BlockSpec((CMP_HIDDEN, d), fix2), pl.BlockSpec((CMP_HIDDEN, d), fix2),
                  pl.BlockSpec((d, CMP_HIDDEN), fix2), pl.BlockSpec((nb, d), fix2), pl.BlockSpec((nb, d), fix2)],
        out_specs=[out, out_t],
        out_shape=[jax.ShapeDtypeStruct((B, G, nb, d), BF16), jax.ShapeDtypeStruct((B, G, d, nb), BF16)],
        compiler_params=_params("parallel"),
        name="compress",
    )(kc, vc, posk, posv, w1k, w1v, w2k.astype(BF16), _rot_cols(w2k).astype(BF16), w2_v.T.astype(BF16), cos, sin)


NSA_KBLOCK = 256
NSA_GROUPS_PER_STEP = 4
NSA_HEADS_PER_CHAIN = 4
NSA_SCORE_LOOKAHEAD = 3


def _nsa_body(q_ref, kcmp_ref, vcmpt_ref, ks_ref, vst_ref, kw_ref, vwt_ref, gate_ref, o_ref, qsel_scr, *, seq):
    i = pl.program_id(2)
    tq = q_ref.shape[2]
    gb = kcmp_ref.shape[1]
    tk = min(NSA_KBLOCK, seq)
    R, d = NSA_REP, HEAD_DIM
    nq = R * tq
    nb = seq // SEL_LEN
    mc = seq // CMP_STRIDE
    t_row = i * tq + lax.broadcasted_iota(jnp.int32, (1, tq), 1)
    t_all = jnp.concatenate([t_row] * R, axis=1)
    m_col = lax.broadcasted_iota(jnp.int32, (mc, 1), 0)
    m_row = lax.broadcasted_iota(jnp.int32, (1, mc), 1)
    n_col = lax.broadcasted_iota(jnp.int32, (nb, 1), 0)

    def q_rows(g):
        return q_ref[0, g * R:(g + 1) * R].reshape(nq, d)

    valid_c = (m_col * CMP_STRIDE + (CMP_LEN - 1)) <= t_all
    ovl = jnp.logical_and(m_row * CMP_STRIDE < (n_col + 1) * SEL_LEN,
                          m_row * CMP_STRIDE + CMP_LEN > n_col * SEL_LEN)
    ovl = jnp.where(ovl, 1.0, 0.0).astype(BF16)
    cur = t_row // SEL_LEN
    forced = jnp.logical_or(n_col == 0, jnp.logical_or(n_col == cur, n_col == cur - 1))
    bonus = jnp.where(forced, FORCE_BONUS, 0.0)
    valid_s = n_col * SEL_LEN <= t_row
    o_c = []
    ties = [jnp.where(n_col > n2, 1.0, 0.0) for n2 in range(nb)]
    cmp_scores = [_dot_nt(kcmp_ref[0, g], q_rows(g)) for g in range(gb)]
    for g in range(gb):
        s = jnp.where(valid_c, cmp_scores[g], NEG)
        e = jnp.exp2(s - jnp.maximum(jnp.max(s, axis=0, keepdims=True), 0.5 * NEG))
        den = jnp.sum(e, axis=0, keepdims=True)
        p = e / jnp.where(den > 0.0, den, 1.0)
        o_c.append(_dot(vcmpt_ref[0, g], p.astype(BF16)))
        psum = p[:, 0:tq]
        for r in range(1, R):
            psum = psum + p[:, r * tq:(r + 1) * tq]
        ph, plo = _split_bf16(psum)
        score = jnp.where(valid_s, _dot(ovl, ph) + _dot(ovl, plo) + bonus, NEG)
        rank = jnp.zeros((nb, tq), F32)
        for n2 in range(nb):
            other = score[n2:n2 + 1, :]
            rank = rank + jnp.where(other > score, 1.0, jnp.where(other == score, ties[n2], 0.0))
        pen = jnp.where(rank < float(SEL_TOP), 0.0, NEG).T.astype(BF16)
        pen = jnp.concatenate([pen, jnp.zeros((tq, 128 - d - nb), BF16)], axis=1)
        qsel_scr[g] = jnp.concatenate([q_rows(g), jnp.concatenate([pen] * R, axis=0)], axis=1)

    k_col = lax.broadcasted_iota(jnp.int32, (tk, 1), 0)
    hp = NSA_HEADS_PER_CHAIN
    j_hi = (i * tq + tq - 1) // tk + 1

    def sweep(q_fn, k_ref, vt_ref, j_lo, bias_fn, bias_every_block):
        chains = [(g, h0) for g in range(gb) for h0 in range(0, R, hp)]

        def body(j, state, with_bias):
            k0 = pl.multiple_of(j * tk, tk)
            bias = jnp.concatenate([bias_fn(k0 + k_col)] * hp, axis=1) if with_bias else None

            def scores(c):
                g, h0 = chains[c]
                sc = _dot_nt(k_ref[0, g, pl.ds(k0, tk), :], q_fn(g, h0))
                return sc + bias if with_bias else sc

            new_state = []
            ahead = NSA_SCORE_LOOKAHEAD
            pending = [scores(c) for c in range(min(ahead, len(chains)))]
            for c, (m_run, acc) in enumerate(state):
                sc = pending.pop(0)
                if c + ahead < len(chains):
                    pending.append(scores(c + ahead))
                vtb = vt_ref[0, chains[c][0], :, pl.ds(k0, tk)]
                m_new = jnp.maximum(m_run, jnp.max(sc, axis=0, keepdims=True))
                pr = jnp.exp2(sc - m_new)
                new_state.append((m_new, jnp.exp2(m_run - m_new) * acc + _dot(vtb, pr.astype(BF16))))
            return tuple(new_state)

        state = tuple((jnp.full((1, hp * tq), NEG, F32), jnp.zeros((V_ROWS, hp * tq), F32)) for _ in chains)
        state = lax.fori_loop(j_lo, j_hi - 1, functools.partial(body, with_bias=bias_every_block), state)
        state = body(j_hi - 1, state, True)
        outs = []
        for g in range(gb):
            acc = jnp.concatenate([a for (cg, _), (_, a) in zip(chains, state) if cg == g], axis=1)
            outs.append(acc[0:d] / acc[d:d + 1])
        return outs

    def causal_bias(kpos):
        return jnp.where(kpos <= t_row, 0.0, NEG)

    o_s = sweep(lambda g, h0: qsel_scr[g, h0 * tq:(h0 + hp) * tq, :], ks_ref, vst_ref, 0, causal_bias, False)

    def win_bias(kpos):
        return jnp.where(jnp.logical_and(kpos <= t_row, kpos > t_row - WINDOW), 0.0, NEG)

    o_w = sweep(lambda g, h0: q_ref[0, g * R + h0:g * R + h0 + hp].reshape(hp * tq, d), kw_ref, vwt_ref,
                jnp.maximum(i * tq - (WINDOW - 1), 0) // tk, win_bias, True)

    outs = []
    for g in range(gb):
        row0 = (pl.program_id(1) * gb + g) * 3 * R

        def gate(branch):
            rows = [gate_ref[0, pl.ds(row0 + 3 * r + branch, 1), :] for r in range(R)]
            return _sigmoid(jnp.concatenate(rows, axis=1))

        o_t = gate(0) * o_c[g] + gate(1) * o_s[g] + gate(2) * o_w[g]
        for r in range(R):
            outs.append(o_t[:, r * tq:(r + 1) * tq].T)
    o_ref[0] = jnp.concatenate(outs, axis=-1).astype(o_ref.dtype)


def _nsa_attention(q, kcmp, vcmpt, ks, vst, kw, vwt, gate):
    B, H, L, d = q.shape
    G, R = NSA_KV_HEADS, NSA_REP
    gb = NSA_GROUPS_PER_STEP
    tq = min(ATT_BLOCK, L)
    mc = L // CMP_STRIDE
    k_spec = pl.BlockSpec((1, gb, L, d), lambda b, g, i: (b, g, 0, 0))
    ksel_spec = pl.BlockSpec((1, gb, L, 128), lambda b, g, i: (b, g, 0, 0))
    vt_spec = pl.BlockSpec((1, gb, V_ROWS, L), lambda b, g, i: (b, g, 0, 0))
    return pl.pallas_call(
        functools.partial(_nsa_body, seq=L),
        grid=(B, G // gb, L // tq),
        in_specs=[pl.BlockSpec((1, gb * R, tq, d), lambda b, g, i: (b, g, i, 0)),
                  pl.BlockSpec((1, gb, mc, d), lambda b, g, i: (b, g, 0, 0)),
                  pl.BlockSpec((1, gb, d, mc), lambda b, g, i: (b, g, 0, 0)),
                  ksel_spec, vt_spec, k_spec, vt_spec,
                  pl.BlockSpec((1, gate.shape[1], tq), lambda b, g, i: (b, 0, i))],
        out_specs=pl.BlockSpec((1, tq, gb * R * d), lambda b, g, i: (b, i, g)),
        out_shape=jax.ShapeDtypeStruct((B, L, H * d), BF16),
        scratch_shapes=[pltpu.VMEM((gb, R * tq, 128), BF16)],
        compiler_params=_params("parallel", "parallel", "arbitrary"),
        name="nsa_attention",
    )(q, kcmp, vcmpt, ks, vst, kw, vwt, gate)


def kernel(x, w_in_0, ssm_lam_re, ssm_lam_im, ssm_log_dt, ssm_b_re, ssm_b_im, ssm_c_re, ssm_c_im, ssm_d, w_glu, w_out_0, ln_mix_g_0, ln_mix_b_0, ln_ffn_g_0, ln_ffn_b_0, w1_0, w3_0, w2_0, w_in_1, cmp_pos_k, cmp_w1_k, cmp_w2_k, cmp_pos_v, cmp_w1_v, cmp_w2_v, w_out_1, ln_mix_g_1, ln_mix_b_1, ln_ffn_g_1, ln_ffn_b_1, w1_1, w3_1, w2_1, w_router, b_router):
    B, L, D = x.shape
    N = B * L
    vec = lambda a: a.astype(F32).reshape(1, D)

    def ffn(h, w1, w3, w2, g, b):
        gates_t, grp_t = _router(h, w_router, b_router)
        return _moe(h, gates_t, grp_t, w1.astype(BF16), w3.astype(BF16), w2.astype(BF16), vec(g), vec(b))

    q, k, v, u = _inproj0(x, w_in_0.astype(BF16))
    o_a = _sb_attention(q, k, v)
    y = _ssm(u, _ssm_tables(ssm_lam_re, ssm_lam_im, ssm_log_dt, ssm_b_re, ssm_b_im, ssm_c_re, ssm_c_im, ssm_d), B)
    w_out_0b = w_out_0.astype(BF16)
    h = _outproj0(x, o_a, y, w_glu.astype(BF16), w_out_0b[:SB_WIDTH], w_out_0b[SB_WIDTH:],
                  vec(ln_mix_g_0), vec(ln_mix_b_0)).reshape(N, D)
    h = ffn(h, w1_0, w3_0, w2_0, ln_ffn_g_0, ln_ffn_b_0)

    q, ks, kw, kc, vc, vst, vwt, gate = _inproj1(h.reshape(B, L, D), *_inproj1_weights(w_in_1))
    kcmp, vcmpt = _compress(kc, vc, cmp_pos_k, cmp_w1_k, cmp_w2_k, cmp_pos_v, cmp_w1_v, cmp_w2_v)
    o = _nsa_attention(q, kcmp, vcmpt, ks, vst, kw, vwt, gate).reshape(N, NSA_HEADS * HEAD_DIM)
    h = _outproj1(h, o, w_out_1.astype(BF16), vec(ln_mix_g_1), vec(ln_mix_b_1))
    h = ffn(h, w1_1, w3_1, w2_1, ln_ffn_g_1, ln_ffn_b_1)
    return h.reshape(B, L, D)
```

```python
import functools
import math

import numpy as np
import jax
import jax.numpy as jnp
from jax import lax
from jax.experimental import pallas as pl
from jax.experimental.pallas import tpu as pltpu

F32 = jnp.float32
BF16 = jnp.bfloat16

D_MODEL = 1024
DEPTH = 2
SB_HEADS = 8
HEAD_DIM = 64
SB_WIDTH = SB_HEADS * HEAD_DIM
SSM_WIDTH = D_MODEL - SB_WIDTH
SSM_GROUP = 16
SSM_GROUPS = SSM_WIDTH // SSM_GROUP
SSM_STATE = 64
SSM_CHUNK = 16
SSM_LANE_TILES = SSM_WIDTH // 128
SSM_TILE_GROUPS = 128 // SSM_GROUP
NSA_HEADS = 16
NSA_KV_HEADS = 4
NSA_REP = NSA_HEADS // NSA_KV_HEADS
CMP_LEN = 32
CMP_STRIDE = 16
CMP_HIDDEN = 256
SEL_LEN = 64
SEL_TOP = 8
WINDOW = 512
ROPE_THETA = 10000.0
FORCE_BONUS = 1e4
NEG = -1e30
N_EXPERTS = 16
N_GROUPS = 4
EXPERTS_PER_GROUP = N_EXPERTS // N_GROUPS
D_EXPERT = 512
ALPHA = (2 * DEPTH) ** 0.25
LN_EPS = 1e-5
ATTN_SCALE = HEAD_DIM ** -0.5
LOG2_Q_SCALE = ATTN_SCALE * math.log2(math.e)
ATT_BLOCK = 256
GELU_C = math.sqrt(2.0 / math.pi)


def _params(*sem):
    return pltpu.CompilerParams(dimension_semantics=sem, vmem_limit_bytes=56 * 1024 * 1024)


def _sigmoid(x):
    return 1.0 / (1.0 + jnp.exp(-x))


def _gelu(x):
    return 0.5 * x * (1.0 + jnp.tanh(GELU_C * (x + 0.044715 * (x * x * x))))


def _layer_norm(r, g, b):
    mu = jnp.mean(r, axis=-1, keepdims=True)
    d = r - mu
    var = jnp.mean(d * d, axis=-1, keepdims=True)
    return d * lax.rsqrt(var + LN_EPS) * g + b


def _dot(a, b):
    return jnp.dot(a, b, preferred_element_type=F32)


def _dot_nt(a, b):
    return lax.dot_general(a, b, (((1,), (1,)), ((), ())), preferred_element_type=F32)


def _split_bf16(x):
    hi = x.astype(BF16)
    lo = (x - hi.astype(F32)).astype(BF16)
    return hi, lo


def _inproj0_body(x_ref, w_ref, q_ref, k_ref, v_ref, u_ref, slab_scr):
    nb, tl, D = x_ref.shape
    p = _dot(x_ref[...].reshape(nb * tl, D).astype(BF16), w_ref[...])
    for h in range(SB_HEADS):
        c = h * HEAD_DIM
        q_ref[:, h] = (p[:, c:c + HEAD_DIM] * LOG2_Q_SCALE).astype(BF16).reshape(nb, tl, HEAD_DIM)
        k_ref[:, h] = p[:, SB_WIDTH + c:SB_WIDTH + c + HEAD_DIM].astype(BF16).reshape(nb, tl, HEAD_DIM)
        v_ref[:, h] = p[:, 2 * SB_WIDTH + c:2 * SB_WIDTH + c + HEAD_DIM].astype(BF16).reshape(nb, tl, HEAD_DIM)
    T = SSM_CHUNK
    for v in range(SSM_LANE_TILES):
        c0 = 3 * SB_WIDTH + v * 128
        slab_scr[...] = p[:, c0:c0 + 128]
        for c in range(tl // T):
            for s in range(T):
                u_ref[v, c * nb:(c + 1) * nb, s * 128:(s + 1) * 128] = slab_scr[pl.ds(c * T + s, nb, stride=tl), :]


def _inproj0(x, w_bf16, tl=64):
    B, L, D = x.shape
    tl = min(tl, L)
    nout = w_bf16.shape[1]
    T = SSM_CHUNK
    head = jax.ShapeDtypeStruct((B, SB_HEADS, L, HEAD_DIM), BF16)
    head_spec = pl.BlockSpec((B, SB_HEADS, tl, HEAD_DIM), lambda i: (0, 0, i, 0))
    return pl.pallas_call(
        _inproj0_body,
        grid=(L // tl,),
        in_specs=[pl.BlockSpec((B, tl, D), lambda i: (0, i, 0)),
                  pl.BlockSpec((D, nout), lambda i: (0, 0))],
        out_specs=[head_spec, head_spec, head_spec,
                   pl.BlockSpec((SSM_LANE_TILES, tl // T * B, T * 128), lambda i: (0, i, 0))],
        out_shape=[head, head, head,
                   jax.ShapeDtypeStruct((SSM_LANE_TILES, L // T * B, T * 128), F32)],
        scratch_shapes=[pltpu.VMEM((B * tl, 128), F32)],
        compiler_params=_params("parallel"),
        name="inproj0",
    )(x, w_bf16)


SB_BLOCK = 256
SB_HEADS_PER_STEP = 8
EXP2_UNDERFLOW = -151.0


def _sb_attn_body(q_ref, k_ref, v_ref, o_ref, acc_scr, cs_scr):
    i = pl.program_id(2)
    t = q_ref.shape[2]
    nh = q_ref.shape[1]
    row = lax.broadcasted_iota(jnp.int32, (t, t), 0)
    col = lax.broadcasted_iota(jnp.int32, (t, t), 1)
    suffix = jnp.where(row > col, 1.0, 0.0).astype(BF16)
    below = col < row

    def logits(hh, k0):
        return _dot_nt(q_ref[0, hh], k_ref[0, hh, pl.ds(k0, t), :])

    def alive(css):
        return (jnp.max(functools.reduce(jnp.maximum, css)) > EXP2_UNDERFLOW).astype(jnp.int32)

    def all_heads(k0, diagonal):
        zs = [logits(hh, k0) for hh in range(nh)]
        sps, l1s = [], []
        for z in zs:
            sp = jnp.maximum(z, 0.0) + jnp.log2(1.0 + jnp.exp2(-jnp.abs(z)))
            sps.append(sp)
            l1s.append(jnp.where(below, sp, 0.0) if diagonal else sp)
        within_all = _dot(jnp.concatenate([l1.astype(BF16) for l1 in l1s], axis=0), suffix)
        css = []
        for hh in range(nh):
            vb = v_ref[0, hh, pl.ds(k0, t), :]
            base = zs[hh] - sps[hh] - within_all[hh * t:(hh + 1) * t]
            if diagonal:
                w = jnp.where(below, jnp.exp2(base), 0.0)
                acc_scr[hh] = _dot(w.astype(BF16), vb)
                cs = -jnp.sum(l1s[hh], axis=1, keepdims=True)
            else:
                w = jnp.exp2(base + cs_scr[hh])
                acc_scr[hh] += _dot(w.astype(BF16), vb)
                cs = cs_scr[hh] - jnp.sum(l1s[hh], axis=1, keepdims=True)
            cs_scr[hh] = cs
            css.append(cs)
        return alive(css)

    first = all_heads(pl.multiple_of(i * t, t), True)

    def cond(c):
        return jnp.logical_and(c[0] <= i, c[1] > 0)

    def body(c):
        return c[0] + 1, all_heads(pl.multiple_of((i - c[0]) * t, t), False)

    lax.while_loop(cond, body, (jnp.int32(1), first))
    o_ref[0] = jnp.concatenate([acc_scr[hh] for hh in range(nh)], axis=-1).astype(o_ref.dtype)


def _sb_attention(q, k, v):
    B, H, L, d = q.shape
    t = min(SB_BLOCK, L)
    nh = SB_HEADS_PER_STEP
    return pl.pallas_call(
        _sb_attn_body,
        grid=(B, H // nh, L // t),
        in_specs=[pl.BlockSpec((1, nh, t, d), lambda b, h, i: (b, h, i, 0)),
                  pl.BlockSpec((1, nh, L, d), lambda b, h, i: (b, h, 0, 0)),
                  pl.BlockSpec((1, nh, L, d), lambda b, h, i: (b, h, 0, 0))],
        out_specs=pl.BlockSpec((1, t, nh * d), lambda b, h, i: (b, i, h)),
        out_shape=jax.ShapeDtypeStruct((B, L, H * d), BF16),
        scratch_shapes=[pltpu.VMEM((nh, t, d), F32), pltpu.VMEM((nh, t, 1), F32)],
        compiler_params=_params("parallel", "parallel", "arbitrary"),
        name="sb_attention",
    )(q, k, v)


def _ssm_tables(lam_re, lam_im, log_dt, b_re, b_im, c_re, c_im, d_skip):
    T, G, P, H = SSM_CHUNK, SSM_GROUPS, SSM_STATE, SSM_GROUP
    hp = lax.Precision.HIGHEST
    dt = jnp.exp(log_dt.astype(F32))[:, None]
    lr = lam_re.astype(F32)
    li = lam_im.astype(F32)
    mag = jnp.exp(lr * dt)
    a_re = mag * jnp.cos(li * dt)
    a_im = mag * jnp.sin(li * dt)
    den = lr * lr + li * li
    nr = a_re - 1.0
    f_re = (nr * lr + a_im * li) / den
    f_im = (a_im * lr - nr * li) / den
    br = b_re.astype(F32)
    bi = b_im.astype(F32)
    bbt_re = (f_re[..., None] * br - f_im[..., None] * bi).transpose(0, 2, 1)[:, None]
    bbt_im = (f_re[..., None] * bi + f_im[..., None] * br).transpose(0, 2, 1)[:, None]
    cr = c_re.astype(F32)[:, None]
    ci = c_im.astype(F32)[:, None]

    def powers(k):
        k = jnp.asarray(k, F32)[None, :, None]
        pmag = jnp.exp(k * (lr * dt)[:, None, :])
        ang = k * (li * dt)[:, None, :]
        return (pmag * jnp.cos(ang))[:, :, None, :], (pmag * jnp.sin(ang))[:, :, None, :]

    def cmul(xr, xi, yr, yi, rows):
        return ((xr * yr - xi * yi).reshape(G, rows, P), (xr * yi + xi * yr).reshape(G, rows, P))

    steps = np.arange(T)
    bs_re, bs_im = cmul(*powers(-steps), bbt_re, bbt_im, T * H)
    ct_re, ct_im = cmul(*powers(steps), cr, ci, T * H)
    full = (jnp.einsum('gap,gbp->gab', bs_re, ct_re, precision=hp)
            - jnp.einsum('gap,gbp->gab', bs_im, ct_im, precision=hp))
    row_s = lax.broadcasted_iota(jnp.int32, (T * H, T * H), 0) // H
    col_t = lax.broadcasted_iota(jnp.int32, (T * H, T * H), 1) // H
    intra = jnp.where(row_s <= col_t, full, 0.0)
    st_re, st_im = cmul(*powers(T - 1 - steps), bbt_re, bbt_im, T * H)
    to_st = jnp.concatenate([st_re, st_im, st_im, st_re], axis=-1)
    c1_re, c1_im = cmul(*powers(steps + 1), cr, ci, T * H)
    from_st_t = jnp.concatenate([c1_re, -c1_im], axis=-1)
    at_re, at_im = powers([T])
    at_re, at_im = at_re[:, 0, 0], at_im[:, 0, 0]
    zeros = jnp.zeros_like(at_re)
    adv = jnp.stack([jnp.concatenate([at_re, at_re], -1),
                     jnp.concatenate([-at_im, at_im], -1),
                     jnp.concatenate([at_im, -at_im], -1)]
                    + [jnp.concatenate([zeros, zeros], -1)] * 5, axis=1)
    dvec = jnp.tile(d_skip.astype(F32).reshape(G, 1, H), (1, 1, T))
    return intra, to_st, from_st_t, adv, dvec


def _ssm_expand(tables):
    intra, to_st, from_st_t, adv, dvec = tables
    V, W, T, H, P = SSM_LANE_TILES, SSM_TILE_GROUPS, SSM_CHUNK, SSM_GROUP, SSM_STATE
    n = T * W * H
    k_in = lax.broadcasted_iota(jnp.int32, (T * H, n), 0)
    col = lax.broadcasted_iota(jnp.int32, (T * H, n), 1)
    spread_th = (k_in == (col // 128) * H + col % H).astype(F32)
    spread_q = (k_in[:2 * P, :W * 2 * P] == col[:2 * P, :W * 2 * P] % (2 * P)).astype(F32)
    row_g = (lax.broadcasted_iota(jnp.int32, (n, 1), 0) // H) % W
    st_row_g = lax.broadcasted_iota(jnp.int32, (W * 2 * P, 1), 0) // (2 * P)
    col_g_th = (lax.broadcasted_iota(jnp.int32, (1, n), 1) // H) % W
    col_g_q = lax.broadcasted_iota(jnp.int32, (1, W * 2 * P), 1) // (2 * P)

    def blockdiag(rows, spread, row_group, col_group):
        full = jnp.einsum('vrk,kc->vrc', rows, spread)
        return jnp.where(row_group == col_group, full, 0.0).astype(BF16)

    by_row = lambda t: t.reshape(V, W, T, H, T * H).transpose(0, 2, 1, 3, 4).reshape(V, n, T * H)
    m8 = blockdiag(by_row(intra), spread_th, row_g, col_g_th)
    ws8 = blockdiag(by_row(to_st)[:, :, :2 * P], spread_q, row_g, col_g_q)
    wi_full = jnp.einsum('vwkq,kc->vwqc', from_st_t.reshape(V, W, T * H, 2 * P), spread_th).reshape(V, W * 2 * P, n)
    wi8 = jnp.where(st_row_g == col_g_th, wi_full, 0.0).astype(BF16)
    adv8 = adv.reshape(V, W, 8, 2 * P).transpose(0, 2, 1, 3).reshape(V, 8, W * 2 * P)
    d8 = jnp.broadcast_to(dvec.reshape(V, 1, W, T, H)[:, :, :, 0:1], (V, 1, W, T, H))
    d8 = d8.transpose(0, 1, 3, 2, 4).reshape(V, 1, T * W * H)
    return m8, ws8, wi8, adv8, d8


def _ssm_body(u_ref, m8_ref, ws8_ref, wi8_ref, adv_ref, dvec_ref, y_ref, s_scr, xin_scr, x1_scr, x2_scr, *, bsz):
    half = x1_scr.shape[1]

    @pl.when(pl.program_id(1) == 0)
    def _():
        x1_scr[...] = jnp.zeros_like(x1_scr)
        x2_scr[...] = jnp.zeros_like(x2_scr)

    u = u_ref[0]
    ub = u.astype(BF16)
    s1 = _dot(ub, ws8_ref[0])
    lane = lax.broadcasted_iota(jnp.int32, s1.shape, 1)
    s_scr[:, :half] = s1
    s_scr[:, half:] = jnp.where(lane % (2 * SSM_STATE) < SSM_STATE,
                                pltpu.roll(s1, half - SSM_STATE, 1), pltpu.roll(s1, SSM_STATE, 1))
    a1 = adv_ref[0, 0:1, :]
    a2 = adv_ref[0, 1:2, :]
    a3 = adv_ref[0, 2:3, :]

    def step(c, carry):
        x1, x2 = carry
        r0 = pl.multiple_of(c * bsz, bsz)
        xin_scr[pl.ds(r0, bsz), :] = x1
        s = s_scr[pl.ds(r0, bsz), :]
        return (a1 * x1 + a2 * x2 + s[:, :half], a1 * x2 + a3 * x1 + s[:, half:])

    x1, x2 = lax.fori_loop(0, u.shape[0] // bsz, step, (x1_scr[...], x2_scr[...]))
    x1_scr[...] = x1
    x2_scr[...] = x2
    y_ref[0] = _dot(ub, m8_ref[0]) + _dot(xin_scr[...].astype(BF16), wi8_ref[0]) + dvec_ref[0] * u


def _ssm(us, tables, bsz, rows=256):
    m8, ws8, wi8, adv8, d8 = _ssm_expand(tables)
    V, CB, width = us.shape
    B = bsz
    half = SSM_TILE_GROUPS * 2 * SSM_STATE
    rows = min(rows, CB)
    once = dict(pipeline_mode=pl.Buffered(1))
    y = pl.pallas_call(
        functools.partial(_ssm_body, bsz=B),
        grid=(V, CB // rows),
        in_specs=[pl.BlockSpec((1, rows, width), lambda v, r: (v, r, 0)),
                  pl.BlockSpec((1, width, width), lambda v, r: (v, 0, 0), **once),
                  pl.BlockSpec((1, width, half), lambda v, r: (v, 0, 0), **once),
                  pl.BlockSpec((1, half, width), lambda v, r: (v, 0, 0), **once),
                  pl.BlockSpec((1, 8, half), lambda v, r: (v, 0, 0)),
                  pl.BlockSpec((1, 1, width), lambda v, r: (v, 0, 0))],
        out_specs=pl.BlockSpec((1, rows, width), lambda v, r: (v, r, 0)),
        out_shape=jax.ShapeDtypeStruct((V, CB, width), F32),
        scratch_shapes=[pltpu.VMEM((rows, 2 * half), F32), pltpu.VMEM((rows, half), F32),
                        pltpu.VMEM((B, half), F32), pltpu.VMEM((B, half), F32)],
        compiler_params=_params("parallel", "arbitrary"),
        name="s5_scan",
    )(us, m8, ws8, wi8, adv8, d8)
    return y


def _outproj0_body(x_ref, oa_ref, y_ref, wglu_ref, woa_ref, wob_ref, g_ref, b_ref, o_ref, slab_scr):
    nb, tl, D = x_ref.shape
    T = SSM_CHUNK
    ys = []
    for v in range(SSM_LANE_TILES):
        for c in range(tl // T):
            for s in range(T):
                slab_scr[pl.ds(c * T + s, nb, stride=tl), :] = y_ref[v, c * nb:(c + 1) * nb, s * 128:(s + 1) * 128]
        ys.append(slab_scr[...])
    h = _gelu(jnp.concatenate(ys, axis=-1))
    ob = h * _sigmoid(_dot(h.astype(BF16), wglu_ref[...]))
    m = _dot(oa_ref[...].reshape(nb * tl, SB_WIDTH), woa_ref[...]) + _dot(ob.astype(BF16), wob_ref[...])
    r = ALPHA * x_ref[...].reshape(nb * tl, D) + m
    o_ref[...] = _layer_norm(r, g_ref[...], b_ref[...]).reshape(nb, tl, D)


def _outproj0(x, oa, ys, wglu, woa, wob, g, b, tl=64):
    B, L, D = x.shape
    tl = min(tl, L)
    T = SSM_CHUNK
    row = lambda i: (0, i, 0)
    fix = lambda i: (0, 0)
    return pl.pallas_call(
        _outproj0_body,
        grid=(L // tl,),
        in_specs=[pl.BlockSpec((B, tl, D), row), pl.BlockSpec((B, tl, SB_WIDTH), row),
                  pl.BlockSpec((SSM_LANE_TILES, tl // T * B, T * 128), row),
                  pl.BlockSpec((SSM_WIDTH, SSM_WIDTH), fix), pl.BlockSpec((SB_WIDTH, D), fix),
                  pl.BlockSpec((SSM_WIDTH, D), fix), pl.BlockSpec((1, D), fix), pl.BlockSpec((1, D), fix)],
        out_specs=pl.BlockSpec((B, tl, D), row),
        out_shape=jax.ShapeDtypeStruct((B, L, D), F32),
        scratch_shapes=[pltpu.VMEM((B * tl, 128), F32)],
        compiler_params=_params("parallel"),
        name="outproj0",
    )(x, oa, ys, wglu, woa, wob, g, b)


def _outproj1_body(x_ref, o_ref_in, w_ref, g_ref, b_ref, whl_ref, rb_ref, o_ref, gates_ref, grp_ref):
    m = _dot(o_ref_in[...], w_ref[...])
    h = _layer_norm(ALPHA * x_ref[...] + m, g_ref[...], b_ref[...])
    o_ref[...] = h
    gates_ref[...], grp_ref[...] = _route(h, whl_ref[...], rb_ref[...])


def _outproj1(x, o, w, g, b, w_router, b_router, tm=1024):
    N, D = x.shape
    tm = min(tm, N)
    whl, rb = _router_weights(w_router, b_router)
    row = lambda i: (i, 0)
    col = lambda i: (0, i)
    fix = lambda i: (0, 0)
    return pl.pallas_call(
        _outproj1_body,
        grid=(N // tm,),
        in_specs=[pl.BlockSpec((tm, D), row), pl.BlockSpec((tm, o.shape[1]), row),
                  pl.BlockSpec(w.shape, fix), pl.BlockSpec((1, D), fix), pl.BlockSpec((1, D), fix),
                  pl.BlockSpec(whl.shape, fix), pl.BlockSpec(rb.shape, fix)],
        out_specs=[pl.BlockSpec((tm, D), row), pl.BlockSpec((N_EXPERTS, tm), col), pl.BlockSpec((N_GROUPS, tm), col)],
        out_shape=[jax.ShapeDtypeStruct((N, D), F32), jax.ShapeDtypeStruct((N_EXPERTS, N), F32),
                   jax.ShapeDtypeStruct((N_GROUPS, N), F32)],
        compiler_params=_params("parallel"),
        name="outproj1",
    )(x, o, w, g, b, whl, rb)


def _route(x, whl, bias):
    NE = N_EXPERTS
    xh, xl = _split_bf16(x)
    part = _dot(xh, whl)
    logits = (part[:, :NE] + part[:, NE:] + _dot(xl, whl[:, :NE])).T
    aff = _sigmoid(logits)
    sel = aff + bias
    E, K = EXPERTS_PER_GROUP, N_GROUPS
    s = [sel[e:e + 1, :] for e in range(N_EXPERTS)]
    a = [aff[e:e + 1, :] for e in range(N_EXPERTS)]
    gscore = []
    for k in range(K):
        v = s[k * E:(k + 1) * E]
        best = None
        for x in range(E):
            for y in range(x + 1, E):
                pair = v[x] + v[y]
                best = pair if best is None else jnp.maximum(best, pair)
        gscore.append(best)
    top = functools.reduce(jnp.maximum, gscore)
    is_g = []
    taken = None
    for k in range(K):
        hit = gscore[k] == top
        if taken is None:
            is_g.append(hit)
            taken = hit
        else:
            is_g.append(jnp.logical_and(hit, jnp.logical_not(taken)))
            taken = jnp.logical_or(taken, hit)

    def pick(rows, j):
        out = rows[(K - 1) * E + j]
        for k in range(K - 2, -1, -1):
            out = jnp.where(is_g[k], rows[k * E + j], out)
        return out

    v = [pick(s, j) for j in range(E)]
    av = [pick(a, j) for j in range(E)]
    chosen = []
    for j in range(E):
        r = jnp.zeros_like(v[j])
        for j2 in range(E):
            if j2 == j:
                continue
            ahead = (v[j2] >= v[j]) if j2 < j else (v[j2] > v[j])
            r = r + jnp.where(ahead, 1.0, 0.0)
        chosen.append(r < 2.0)
    wj = [jnp.where(chosen[j], av[j], 0.0) for j in range(E)]
    tot = wj[0] + wj[1] + wj[2] + wj[3]
    gj = [w / tot for w in wj]
    rows = [jnp.where(is_g[e // E], gj[e % E], 0.0) for e in range(N_EXPERTS)]
    return (jnp.concatenate(rows, axis=0),
            jnp.concatenate([jnp.where(is_g[k], 1.0, 0.0) for k in range(K)], axis=0))


def _router_body(x_ref, whl_ref, b_ref, g_ref, grp_ref):
    g_ref[...], grp_ref[...] = _route(x_ref[...], whl_ref[...], b_ref[...])


def _router_weights(w_router, b_router):
    whl = jnp.concatenate(_split_bf16(w_router.astype(F32)), axis=1)
    return whl, b_router.astype(F32).reshape(N_EXPERTS, 1)


def _router(x, w_router, b_router, tm=1024):
    N, D = x.shape
    tm = min(tm, N)
    whl, rb = _router_weights(w_router, b_router)
    fix = lambda i: (0, 0)
    return pl.pallas_call(
        _router_body,
        grid=(N // tm,),
        in_specs=[pl.BlockSpec((tm, D), lambda i: (i, 0)), pl.BlockSpec((D, 2 * N_EXPERTS), fix),
                  pl.BlockSpec((N_EXPERTS, 1), fix)],
        out_specs=[pl.BlockSpec((N_EXPERTS, tm), lambda i: (0, i)), pl.BlockSpec((N_GROUPS, tm), lambda i: (0, i))],
        out_shape=[jax.ShapeDtypeStruct((N_EXPERTS, N), F32), jax.ShapeDtypeStruct((N_GROUPS, N), F32)],
        compiler_params=_params("parallel"),
        name="router",
    )(x, whl, rb)


MOE_TILE = 1024
MOE_PASS_ROWS = 304
MOE_CHUNK = 256
MOE_EXPERTS_PER_STEP = 2
ROW_ALIGN = 16


def _moe_body(seg_ref, x_ref, gt_ref, grp_ref, w1_ref, w3_ref, w2_ref, g_ref, b_ref, o_ref,
              xs_scr, ys_scr, pt_scr, gs_scr):
    i = pl.program_id(0)
    e = pl.program_id(1)
    T = x_ref.shape[0]
    K, NE = N_GROUPS, N_EXPERTS
    cap = MOE_PASS_ROWS
    ch = min(MOE_CHUNK, T)

    @pl.when(e == 0)
    def _():
        grp = grp_ref[...]
        xb = x_ref[...].astype(BF16)
        gates = gt_ref[...]
        gh, gm = _split_bf16(gates)
        gl = (gates - gh.astype(F32) - gm.astype(F32)).astype(BF16)
        g3 = jnp.concatenate([gh, gm, gl], axis=0)
        cnt = jnp.sum(grp, axis=1, keepdims=True)
        off = [jnp.zeros((1, 1), F32)]
        for k in range(K - 1):
            off.append(off[-1] + cnt[k:k + 1])
        r_i = lax.broadcasted_iota(jnp.int32, (ch, T), 0)
        c_i = lax.broadcasted_iota(jnp.int32, (ch, T), 1)
        lane = lax.broadcasted_iota(jnp.int32, (2 * K, T), 1)
        run = jnp.concatenate([grp, jnp.zeros_like(grp)], axis=0)
        sh = 1
        while sh < T:
            run = run + jnp.where(lane >= sh, pltpu.roll(run, sh, 1), 0.0)
            sh *= 2
        pos_row = jnp.zeros((1, T), F32)
        for k in range(K):
            pos_row = pos_row + grp[k:k + 1, :] * (off[k] + run[k:k + 1, :] - 1.0)
        pos_t = jnp.broadcast_to(pos_row, (2 * K, T))
        pos_col = [pos_t[:, c * ch:(c + 1) * ch].T[:, 0:1] for c in range(T // ch)]
        for c in range(T // ch):
            c0 = c * ch
            perm = jnp.where((r_i + c0).astype(F32) == pos_row, 1.0, 0.0).astype(BF16)
            xs_scr[c0:c0 + ch, :] = _dot(perm, xb).astype(BF16)
            g3s = _dot_nt(perm, g3)
            gs_scr[c0:c0 + ch, :] = g3s[:, :NE] + g3s[:, NE:2 * NE] + g3s[:, 2 * NE:]
            pt_scr[c0:c0 + ch, :] = jnp.where(c_i.astype(F32) == pos_col[c], 1.0, 0.0).astype(BF16)
        xs_scr[T:, :] = jnp.zeros((cap, xs_scr.shape[1]), BF16)
        gs_scr[T:, :] = jnp.zeros((cap, gs_scr.shape[1]), F32)
        ys_scr[...] = jnp.zeros_like(ys_scr)

    ne = w1_ref.shape[0]
    k = (e * ne) // EXPERTS_PER_GROUP
    off = seg_ref[i, k]
    cnt = seg_ref[i, K + k]
    start0 = (off // ROW_ALIGN) * ROW_ALIGN
    n_pass = (off + cnt - start0 + cap - 1) // cap

    def one_pass(n, c):
        start = pl.multiple_of(start0 + n * cap, ROW_ALIGN)
        xc = xs_scr[pl.ds(start, cap), :]
        gsc = gs_scr[pl.ds(start, cap), :]
        lane = lax.broadcasted_iota(jnp.int32, gsc.shape, 1)
        y = None
        for x in range(ne):
            a = _dot(xc, w1_ref[x])
            hm = (a * _sigmoid(a)) * _dot(xc, w3_ref[x])
            gcol = jnp.sum(jnp.where(lane == e * ne + x, gsc, 0.0), axis=1, keepdims=True)
            yx = gcol * _dot(hm.astype(BF16), w2_ref[x])
            y = yx if y is None else y + yx
        ys_scr[pl.ds(start, cap), :] += y
        return c

    lax.fori_loop(0, n_pass, one_pass, 0)

    @pl.when(e == pl.num_programs(1) - 1)
    def _():
        y = _dot(pt_scr[...], ys_scr[0:T, :].astype(BF16))
        o_ref[...] = _layer_norm(ALPHA * x_ref[...] + y, g_ref[...], b_ref[...])


def _moe(x, gates_t, grp_t, w1, w3, w2, g, b):
    N, D = x.shape
    T = min(MOE_TILE, N)
    K = N_GROUPS
    cnt = grp_t.reshape(K, N // T, T).sum(-1).astype(jnp.int32).T
    seg = jnp.concatenate([jnp.cumsum(cnt, axis=1) - cnt, cnt], axis=1)
    fix = lambda i, e, s: (0, 0)
    ne = MOE_EXPERTS_PER_STEP
    rows = T + MOE_PASS_ROWS
    grid_spec = pltpu.PrefetchScalarGridSpec(
        num_scalar_prefetch=1,
        grid=(N // T, N_EXPERTS // ne),
        in_specs=[pl.BlockSpec((T, D), lambda i, e, s: (i, 0)),
                  pl.BlockSpec((N_EXPERTS, T), lambda i, e, s: (0, i)),
                  pl.BlockSpec((K, T), lambda i, e, s: (0, i)),
                  pl.BlockSpec((ne, D, D_EXPERT), lambda i, e, s: (e, 0, 0)),
                  pl.BlockSpec((ne, D, D_EXPERT), lambda i, e, s: (e, 0, 0)),
                  pl.BlockSpec((ne, D_EXPERT, D), lambda i, e, s: (e, 0, 0)),
                  pl.BlockSpec((1, D), fix), pl.BlockSpec((1, D), fix)],
        out_specs=pl.BlockSpec((T, D), lambda i, e, s: (i, 0)),
        scratch_shapes=[pltpu.VMEM((rows, D), BF16), pltpu.VMEM((rows, D), F32),
                        pltpu.VMEM((T, T), BF16), pltpu.VMEM((rows, N_EXPERTS), F32)])
    return pl.pallas_call(
        _moe_body,
        grid_spec=grid_spec,
        out_shape=jax.ShapeDtypeStruct((N, D), F32),
        compiler_params=_params("parallel", "arbitrary"),
        name="moe",
    )(seg, x, gates_t, grp_t, w1, w3, w2, g, b)


NSA_KVW = NSA_KV_HEADS * HEAD_DIM
NSA_ROPE_W = NSA_HEADS * HEAD_DIM + 2 * NSA_KVW


def _rot_cols(w):
    K, n = w.shape
    w3 = w.reshape(K, n // HEAD_DIM, 2, HEAD_DIM // 2)
    return jnp.stack([-w3[:, :, 1], w3[:, :, 0]], axis=2).reshape(K, n)


def _inproj1_weights(w_in):
    H, G, d = NSA_HEADS, NSA_KV_HEADS, HEAD_DIM
    cuts = [H * d + i * NSA_KVW for i in range(7)]
    q, kc, vc, ks, vs, kw, vw, gate = jnp.split(w_in.astype(F32), cuts, axis=1)
    w_all = jnp.concatenate([q, ks, kw, kc, vc], axis=1).astype(BF16)
    w_t = jnp.concatenate([vs, vw, gate], axis=1).T.astype(BF16)
    return w_all, w_t


def _rope_tables(pos):
    inv = ROPE_THETA ** (-jnp.arange(0, HEAD_DIM, 2, dtype=F32) / HEAD_DIM)
    ang = pos.astype(F32)[:, None] * inv[None, :]
    c, s = jnp.cos(ang), jnp.sin(ang)
    return jnp.concatenate([c, c], axis=-1), jnp.concatenate([s, s], axis=-1)


def _rope_roll_tables(pos):
    inv = ROPE_THETA ** (-jnp.arange(0, HEAD_DIM, 2, dtype=F32) / HEAD_DIM)
    ang = pos.astype(F32)[:, None] * inv[None, :]
    c, s, z = jnp.cos(ang), jnp.sin(ang), jnp.zeros_like(ang)
    two = lambda a, b: jnp.concatenate([a, b, a, b], axis=-1)
    return two(c, c), two(-s, z), two(z, s)


V_ROWS = HEAD_DIM + 16


def _inproj1_body(x_ref, w_ref, wt_ref, cos_ref, sinlo_ref, sinhi_ref, q_ref, ks_ref, kw_ref, kc_ref, vc_ref,
                  vst_ref, vwt_ref, gate_ref):
    xb = x_ref[0].astype(BF16)
    tl = xb.shape[0]
    d = HEAD_DIM
    p = _dot(xb, w_ref[...])
    pt = _dot_nt(wt_ref[...], xb)
    ones = jnp.ones((V_ROWS - d, tl), BF16)
    for n, ref in enumerate((vst_ref, vwt_ref)):
        for g in range(NSA_KV_HEADS):
            r0 = n * NSA_KVW + g * d
            ref[0, g, 0:d, :] = pt[r0:r0 + d, :].astype(BF16)
            ref[0, g, d:V_ROWS, :] = ones
    gate_ref[0] = pt[2 * NSA_KVW:, :]
    cos = cos_ref[...]
    sin_lo = sinlo_ref[...]
    sin_hi = sinhi_ref[...]
    roped = []
    for c in range(NSA_ROPE_W // 128):
        xc = p[:, c * 128:(c + 1) * 128]
        roped.append(xc * cos + pltpu.roll(xc, 96, 1) * sin_lo + pltpu.roll(xc, 32, 1) * sin_hi)

    def head(chunks, h):
        blk = chunks[h // 2]
        return blk[:, (h % 2) * d:(h % 2 + 1) * d]

    for h in range(NSA_HEADS):
        q_ref[0, h] = (head(roped, h) * LOG2_Q_SCALE).astype(BF16)
    nq = NSA_HEADS // 2
    pos = pl.program_id(1) * tl + lax.broadcasted_iota(jnp.int32, (tl, 128 - d), 0)
    block_onehot = jnp.where(pos // SEL_LEN == lax.broadcasted_iota(jnp.int32, (tl, 128 - d), 1), 1.0, 0.0)
    for g in range(NSA_KV_HEADS):
        ks_ref[0, g] = jnp.concatenate([head(roped[nq:], g), block_onehot], axis=1).astype(BF16)
        kw_ref[0, g] = head(roped[nq + 2:], g).astype(BF16)
    for n, ref in enumerate((kc_ref, vc_ref)):
        for g in range(NSA_KV_HEADS):
            c0 = NSA_ROPE_W + n * NSA_KVW + g * d
            ref[0, g] = p[:, c0:c0 + d]


def _inproj1(x, w_all, w_t, tl=512):
    B, L, D = x.shape
    tl = min(tl, L)
    G = NSA_KV_HEADS
    n_gate = w_t.shape[0] - 2 * NSA_KVW
    tables = _rope_roll_tables(jnp.arange(L))
    qh = jax.ShapeDtypeStruct((B, NSA_HEADS, L, HEAD_DIM), BF16)
    kvh = jax.ShapeDtypeStruct((B, G, L, HEAD_DIM), BF16)
    vth = jax.ShapeDtypeStruct((B, G, V_ROWS, L), BF16)
    q_spec = pl.BlockSpec((1, NSA_HEADS, tl, HEAD_DIM), lambda b, i: (b, 0, i, 0))
    kv_spec = pl.BlockSpec((1, G, tl, HEAD_DIM), lambda b, i: (b, 0, i, 0))
    ksel_spec = pl.BlockSpec((1, G, tl, 128), lambda b, i: (b, 0, i, 0))
    vt_spec = pl.BlockSpec((1, G, V_ROWS, tl), lambda b, i: (b, 0, 0, i))
    tab_spec = pl.BlockSpec((tl, 128), lambda b, i: (i, 0))
    return pl.pallas_call(
        _inproj1_body,
        grid=(B, L // tl),
        in_specs=[pl.BlockSpec((1, tl, D), lambda b, i: (b, i, 0)),
                  pl.BlockSpec(w_all.shape, lambda b, i: (0, 0)),
                  pl.BlockSpec(w_t.shape, lambda b, i: (0, 0)),
                  tab_spec, tab_spec, tab_spec],
        out_specs=([q_spec, ksel_spec] + [kv_spec] * 3 + [vt_spec] * 2
                   + [pl.BlockSpec((1, n_gate, tl), lambda b, i: (b, 0, i))]),
        out_shape=([qh, jax.ShapeDtypeStruct((B, G, L, 128), BF16), kvh]
                   + [jax.ShapeDtypeStruct((B, G, L, HEAD_DIM), F32)] * 2 + [vth] * 2
                   + [jax.ShapeDtypeStruct((B, n_gate, L), F32)]),
        compiler_params=_params("parallel", "parallel"),
        name="inproj1",
    )(x, w_all, w_t, *tables)


def _compress_body(kc_ref, vc_ref, posk_ref, posv_ref, w1k_ref, w1v_ref, w2k_ref, w2kr_ref, w2v_ref,
                   cos_ref, sin_ref, kcmp_ref, vcmp_ref):
    G = kc_ref.shape[1]
    nrow = kc_ref.shape[2] // CMP_STRIDE

    def hidden(a_ref, pos_ref, w1_ref):
        a = jnp.concatenate(
            [jnp.concatenate([a_ref[0, g, pl.ds(l, nrow, stride=CMP_STRIDE), :] for l in range(CMP_STRIDE)], axis=1)
             for g in range(G)], axis=0).astype(F32)
        lo = _dot((a + pos_ref[0:1, :]).astype(BF16), w1_ref[0])
        hi = _dot((a + pos_ref[1:2, :]).astype(BF16), w1_ref[1])
        hi_next = pltpu.roll(hi, G * nrow - 1, 0)
        return _gelu(lo + hi_next).astype(BF16)

    hk = hidden(kc_ref, posk_ref, w1k_ref)
    cos = jnp.concatenate([cos_ref[...]] * G, axis=0)
    sin = jnp.concatenate([sin_ref[...]] * G, axis=0)
    kcmp = (_dot(hk, w2k_ref[...]) * cos + _dot(hk, w2kr_ref[...]) * sin).astype(BF16)
    hv = hidden(vc_ref, posv_ref, w1v_ref)
    vcmp = _dot_nt(w2v_ref[...], hv).astype(BF16)
    for g in range(G):
        kcmp_ref[0, g] = kcmp[g * nrow:(g + 1) * nrow]
        vcmp_ref[0, g] = vcmp[:, g * nrow:(g + 1) * nrow]


def _compress(kc, vc, pos_k, w1_k, w2_k, pos_v, w1_v, w2_v):
    B, G, L, d = kc.shape
    half = CMP_STRIDE * d
    nb = L // CMP_STRIDE
    posk = pos_k.astype(F32).reshape(2, half)
    posv = pos_v.astype(F32).reshape(2, half)
    w1k = w1_k.astype(BF16).reshape(2, half, CMP_HIDDEN)
    w1v = w1_v.astype(BF16).reshape(2, half, CMP_HIDDEN)
    w2k = w2_k.astype(F32)
    cos, sin = _rope_tables(jnp.arange(nb) * CMP_STRIDE + CMP_LEN - 1)
    blk = pl.BlockSpec((1, G, L, d), lambda b: (b, 0, 0, 0))
    out = pl.BlockSpec((1, G, nb, d), lambda b: (b, 0, 0, 0))
    out_t = pl.BlockSpec((1, G, d, nb), lambda b: (b, 0, 0, 0))
    fix2 = lambda b: (0, 0)
    fix3 = lambda b: (0, 0, 0)
    return pl.pallas_call(
        _compress_body,
        grid=(B,),
        in_specs=[blk, blk, pl.BlockSpec((2, half), fix2), pl.BlockSpec((2, half), fix2),
                  pl.BlockSpec((2, half, CMP_HIDDEN), fix3), pl.BlockSpec((2, half, CMP_HIDDEN), fix3),
                  pl.BlockSpec((CMP_HIDDEN, d), fix2), pl.BlockSpec((CMP_HIDDEN, d), fix2),
                  pl.BlockSpec((d, CMP_HIDDEN), fix2), pl.BlockSpec((nb, d), fix2), pl.BlockSpec((nb, d), fix2)],
        out_specs=[out, out_t],
        out_shape=[jax.ShapeDtypeStruct((B, G, nb, d), BF16), jax.ShapeDtypeStruct((B, G, d, nb), BF16)],
        compiler_params=_params("parallel"),
        name="compress",
    )(kc, vc, posk, posv, w1k, w1v, w2k.astype(BF16), _rot_cols(w2k).astype(BF16), w2_v.T.astype(BF16), cos, sin)


NSA_KBLOCK = 256
NSA_GROUPS_PER_STEP = 4
NSA_HEADS_PER_CHAIN = 4
NSA_SCORE_LOOKAHEAD = 3


def _nsa_body(q_ref, kcmp_ref, vcmpt_ref, ks_ref, vst_ref, kw_ref, vwt_ref, gate_ref, o_ref, qsel_scr, *, seq):
    i = pl.program_id(2)
    tq = q_ref.shape[2]
    gb = kcmp_ref.shape[1]
    tk = min(NSA_KBLOCK, seq)
    R, d = NSA_REP, HEAD_DIM
    nq = R * tq
    nb = seq // SEL_LEN
    mc = seq // CMP_STRIDE
    t_row = i * tq + lax.broadcasted_iota(jnp.int32, (1, tq), 1)
    t_all = jnp.concatenate([t_row] * R, axis=1)
    m_col = lax.broadcasted_iota(jnp.int32, (mc, 1), 0)
    m_row = lax.broadcasted_iota(jnp.int32, (1, mc), 1)
    n_col = lax.broadcasted_iota(jnp.int32, (nb, 1), 0)

    def q_rows(g):
        return q_ref[0, g * R:(g + 1) * R].reshape(nq, d)

    valid_c = (m_col * CMP_STRIDE + (CMP_LEN - 1)) <= t_all
    ovl = jnp.logical_and(m_row * CMP_STRIDE < (n_col + 1) * SEL_LEN,
                          m_row * CMP_STRIDE + CMP_LEN > n_col * SEL_LEN)
    ovl = jnp.where(ovl, 1.0, 0.0).astype(BF16)
    cur = t_row // SEL_LEN
    forced = jnp.logical_or(n_col == 0, jnp.logical_or(n_col == cur, n_col == cur - 1))
    bonus = jnp.where(forced, FORCE_BONUS, 0.0)
    valid_s = n_col * SEL_LEN <= t_row
    o_c = []
    ties = [jnp.where(n_col > n2, 1.0, 0.0) for n2 in range(nb)]
    cmp_scores = [_dot_nt(kcmp_ref[0, g], q_rows(g)) for g in range(gb)]
    for g in range(gb):
        s = jnp.where(valid_c, cmp_scores[g], NEG)
        e = jnp.exp2(s - jnp.maximum(jnp.max(s, axis=0, keepdims=True), 0.5 * NEG))
        den = jnp.sum(e, axis=0, keepdims=True)
        p = e / jnp.where(den > 0.0, den, 1.0)
        o_c.append(_dot(vcmpt_ref[0, g], p.astype(BF16)))
        psum = p[:, 0:tq]
        for r in range(1, R):
            psum = psum + p[:, r * tq:(r + 1) * tq]
        ph, plo = _split_bf16(psum)
        score = jnp.where(valid_s, _dot(ovl, ph) + _dot(ovl, plo) + bonus, NEG)
        rank = jnp.zeros((nb, tq), F32)
        for n2 in range(nb):
            other = score[n2:n2 + 1, :]
            rank = rank + jnp.where(other > score, 1.0, jnp.where(other == score, ties[n2], 0.0))
        pen = jnp.where(rank < float(SEL_TOP), 0.0, NEG).T.astype(BF16)
        pen = jnp.concatenate([pen, jnp.zeros((tq, 128 - d - nb), BF16)], axis=1)
        qsel_scr[g] = jnp.concatenate([q_rows(g), jnp.concatenate([pen] * R, axis=0)], axis=1)

    k_col = lax.broadcasted_iota(jnp.int32, (tk, 1), 0)
    hp = NSA_HEADS_PER_CHAIN
    j_hi = (i * tq + tq - 1) // tk + 1

    def sweep(q_fn, k_ref, vt_ref, j_lo, bias_fn, bias_every_block):
        chains = [(g, h0) for g in range(gb) for h0 in range(0, R, hp)]

        def body(j, state, with_bias):
            k0 = pl.multiple_of(j * tk, tk)
            bias = jnp.concatenate([bias_fn(k0 + k_col)] * hp, axis=1) if with_bias else None

            def scores(c):
                g, h0 = chains[c]
                sc = _dot_nt(k_ref[0, g, pl.ds(k0, tk), :], q_fn(g, h0))
                return sc + bias if with_bias else sc

            new_state = []
            ahead = NSA_SCORE_LOOKAHEAD
            pending = [scores(c) for c in range(min(ahead, len(chains)))]
            for c, (m_run, acc) in enumerate(state):
                sc = pending.pop(0)
                if c + ahead < len(chains):
                    pending.append(scores(c + ahead))
                vtb = vt_ref[0, chains[c][0], :, pl.ds(k0, tk)]
                m_new = jnp.maximum(m_run, jnp.max(sc, axis=0, keepdims=True))
                pr = jnp.exp2(sc - m_new)
                new_state.append((m_new, jnp.exp2(m_run - m_new) * acc + _dot(vtb, pr.astype(BF16))))
            return tuple(new_state)

        state = tuple((jnp.full((1, hp * tq), NEG, F32), jnp.zeros((V_ROWS, hp * tq), F32)) for _ in chains)
        state = lax.fori_loop(j_lo, j_hi - 1, functools.partial(body, with_bias=bias_every_block), state)
        state = body(j_hi - 1, state, True)
        outs = []
        for g in range(gb):
            acc = jnp.concatenate([a for (cg, _), (_, a) in zip(chains, state) if cg == g], axis=1)
            outs.append(acc[0:d] / acc[d:d + 1])
        return outs

    def causal_bias(kpos):
        return jnp.where(kpos <= t_row, 0.0, NEG)

    o_s = sweep(lambda g, h0: qsel_scr[g, h0 * tq:(h0 + hp) * tq, :], ks_ref, vst_ref, 0, causal_bias, False)

    def win_bias(kpos):
        return jnp.where(jnp.logical_and(kpos <= t_row, kpos > t_row - WINDOW), 0.0, NEG)

    o_w = sweep(lambda g, h0: q_ref[0, g * R + h0:g * R + h0 + hp].reshape(hp * tq, d), kw_ref, vwt_ref,
                jnp.maximum(i * tq - (WINDOW - 1), 0) // tk, win_bias, True)

    outs = []
    for g in range(gb):
        row0 = (pl.program_id(1) * gb + g) * 3 * R

        def gate(branch):
            rows = [gate_ref[0, pl.ds(row0 + 3 * r + branch, 1), :] for r in range(R)]
            return _sigmoid(jnp.concatenate(rows, axis=1))

        o_t = gate(0) * o_c[g] + gate(1) * o_s[g] + gate(2) * o_w[g]
        for r in range(R):
            outs.append(o_t[:, r * tq:(r + 1) * tq].T)
    o_ref[0] = jnp.concatenate(outs, axis=-1).astype(o_ref.dtype)


def _nsa_attention(q, kcmp, vcmpt, ks, vst, kw, vwt, gate):
    B, H, L, d = q.shape
    G, R = NSA_KV_HEADS, NSA_REP
    gb = NSA_GROUPS_PER_STEP
    tq = min(ATT_BLOCK, L)
    mc = L // CMP_STRIDE
    k_spec = pl.BlockSpec((1, gb, L, d), lambda b, g, i: (b, g, 0, 0))
    ksel_spec = pl.BlockSpec((1, gb, L, 128), lambda b, g, i: (b, g, 0, 0))
    vt_spec = pl.BlockSpec((1, gb, V_ROWS, L), lambda b, g, i: (b, g, 0, 0))
    return pl.pallas_call(
        functools.partial(_nsa_body, seq=L),
        grid=(B, G // gb, L // tq),
        in_specs=[pl.BlockSpec((1, gb * R, tq, d), lambda b, g, i: (b, g, i, 0)),
                  pl.BlockSpec((1, gb, mc, d), lambda b, g, i: (b, g, 0, 0)),
                  pl.BlockSpec((1, gb, d, mc), lambda b, g, i: (b, g, 0, 0)),
                  ksel_spec, vt_spec, k_spec, vt_spec,
                  pl.BlockSpec((1, gate.shape[1], tq), lambda b, g, i: (b, 0, i))],
        out_specs=pl.BlockSpec((1, tq, gb * R * d), lambda b, g, i: (b, i, g)),
        out_shape=jax.ShapeDtypeStruct((B, L, H * d), BF16),
        scratch_shapes=[pltpu.VMEM((gb, R * tq, 128), BF16)],
        compiler_params=_params("parallel", "parallel", "arbitrary"),
        name="nsa_attention",
    )(q, kcmp, vcmpt, ks, vst, kw, vwt, gate)


def kernel(x, w_in_0, ssm_lam_re, ssm_lam_im, ssm_log_dt, ssm_b_re, ssm_b_im, ssm_c_re, ssm_c_im, ssm_d, w_glu, w_out_0, ln_mix_g_0, ln_mix_b_0, ln_ffn_g_0, ln_ffn_b_0, w1_0, w3_0, w2_0, w_in_1, cmp_pos_k, cmp_w1_k, cmp_w2_k, cmp_pos_v, cmp_w1_v, cmp_w2_v, w_out_1, ln_mix_g_1, ln_mix_b_1, ln_ffn_g_1, ln_ffn_b_1, w1_1, w3_1, w2_1, w_router, b_router):
    B, L, D = x.shape
    N = B * L
    vec = lambda a: a.astype(F32).reshape(1, D)

    def ffn(h, routing, w1, w3, w2, g, b):
        gates_t, grp_t = routing
        return _moe(h, gates_t, grp_t, w1.astype(BF16), w3.astype(BF16), w2.astype(BF16), vec(g), vec(b))

    q, k, v, u = _inproj0(x, w_in_0.astype(BF16))
    o_a = _sb_attention(q, k, v)
    y = _ssm(u, _ssm_tables(ssm_lam_re, ssm_lam_im, ssm_log_dt, ssm_b_re, ssm_b_im, ssm_c_re, ssm_c_im, ssm_d), B)
    w_out_0b = w_out_0.astype(BF16)
    h = _outproj0(x, o_a, y, w_glu.astype(BF16), w_out_0b[:SB_WIDTH], w_out_0b[SB_WIDTH:],
                  vec(ln_mix_g_0), vec(ln_mix_b_0)).reshape(N, D)
    h = ffn(h, _router(h, w_router, b_router), w1_0, w3_0, w2_0, ln_ffn_g_0, ln_ffn_b_0)

    q, ks, kw, kc, vc, vst, vwt, gate = _inproj1(h.reshape(B, L, D), *_inproj1_weights(w_in_1))
    kcmp, vcmpt = _compress(kc, vc, cmp_pos_k, cmp_w1_k, cmp_w2_k, cmp_pos_v, cmp_w1_v, cmp_w2_v)
    o = _nsa_attention(q, kcmp, vcmpt, ks, vst, kw, vwt, gate).reshape(N, NSA_HEADS * HEAD_DIM)
    h, gates_t, grp_t = _outproj1(h, o, w_out_1.astype(BF16), vec(ln_mix_g_1), vec(ln_mix_b_1), w_router, b_router)
    h = ffn(h, (gates_t, grp_t), w1_1, w3_1, w2_1, ln_ffn_g_1, ln_ffn_b_1)
    return h.reshape(B, L, D)
```
